```python
import math
import jax, jax.numpy as jnp
from jax import lax
import numpy as np

D_MODEL = 2048
BATCH = 8
SEQ = 4096
DEPTH = 4

D_MIX = D_MODEL
D_CONV = D_MIX // 4
CONV_GROUPS = 4
CONV_WIDTH = 3
HEAD_DIM = 128
D_ATT = D_MIX // 2
N_ATT_HEADS = D_ATT // HEAD_DIM
D_LRU = D_MIX - D_CONV - D_ATT
LRU_BLOCKS = 4
LRU_BLOCK = D_LRU // LRU_BLOCKS
LRU_CONV_WIDTH = 4
LRU_C = 8.0
D_IN = 3 * D_CONV + 3 * D_ATT + N_ATT_HEADS + 2 * D_LRU
D_FF = 256 * int(math.ceil(8 * D_MODEL / 3 / 256))
Q_BLOCK = 128
EPS = 1e-6

kernel_name = "hymba_parallel_conv_fox_rglru_macaron"


def rmsnorm(x, g):
    xf = x.astype(jnp.float32)
    y = xf * lax.rsqrt(jnp.mean(xf * xf, axis=-1, keepdims=True) + EPS)
    return (y * g.astype(jnp.float32)).astype(x.dtype)


def swiglu(h, w_in, w_out):
    g, u = jnp.split(h @ w_in, 2, axis=-1)
    return (jax.nn.silu(g) * u) @ w_out


def causal_depthwise_conv(x, w):
    k_width, ch = w.shape
    return lax.conv_general_dilated(
        x, w[:, None, :].astype(x.dtype), window_strides=(1,),
        padding=[(k_width - 1, 0)], dimension_numbers=("NWC", "WIO", "NWC"),
        feature_group_count=ch)


def forgetting_attention(q, k, v, f_logit, b_f):
    bsz, seq, _ = q.shape
    def heads(t):
        return t.reshape(bsz, seq, N_ATT_HEADS, HEAD_DIM).transpose(0, 2, 1, 3)
    q, k, v = heads(q), heads(k), heads(v)
    log_f = jax.nn.log_sigmoid(f_logit.astype(jnp.float32) + b_f.astype(jnp.float32))
    cum = jnp.cumsum(log_f, axis=1).transpose(0, 2, 1)
    scale = HEAD_DIM ** -0.5
    k_pos = jnp.arange(seq)

    def block(i):
        start = i * Q_BLOCK
        qb = lax.dynamic_slice_in_dim(q, start, Q_BLOCK, axis=2)
        cb = lax.dynamic_slice_in_dim(cum, start, Q_BLOCK, axis=2)
        s = (jnp.einsum("bhqd,bhkd->bhqk", qb, k).astype(jnp.float32) * scale
             + cb[..., None] - cum[:, :, None, :])
        q_pos = start + jnp.arange(Q_BLOCK)
        s = jnp.where(k_pos[None, :] <= q_pos[:, None], s, -jnp.inf)
        p = jax.nn.softmax(s, axis=-1)
        return jnp.einsum("bhqk,bhkd->bhqd", p.astype(v.dtype), v)

    o = lax.map(block, jnp.arange(seq // Q_BLOCK))
    return o.transpose(1, 0, 3, 2, 4).reshape(bsz, seq, D_ATT)


def rg_lru(x, w_a, b_a, w_x, b_x, lam):
    bsz, seq, _ = x.shape
    xb = x.reshape(bsz, seq, LRU_BLOCKS, LRU_BLOCK)
    r = jax.nn.sigmoid(jnp.einsum("btgi,gij->btgj", xb, w_a).reshape(bsz, seq, D_LRU) + b_a)
    i = jax.nn.sigmoid(jnp.einsum("btgi,gij->btgj", xb, w_x).reshape(bsz, seq, D_LRU) + b_x)
    log_a = -LRU_C * r.astype(jnp.float32) * jax.nn.softplus(-lam.astype(jnp.float32))
    a = jnp.exp(log_a)
    u = jnp.sqrt(-jnp.expm1(2.0 * log_a)) * (i * x).astype(jnp.float32)

    def combine(left, right):
        a1, b1 = left
        a2, b2 = right
        return a1 * a2, a2 * b1 + b2

    _, h = lax.associative_scan(combine, (a, u), axis=1)
    return h.astype(x.dtype)


def parallel_mixer(h, w_in, conv_w, fgate_b, lru_conv_w, lru_conv_b, lru_w_a, lru_b_a,
                   lru_w_x, lru_b_x, lru_lambda, out_norm, w_out):
    sizes = [D_CONV] * 3 + [D_ATT] * 3 + [N_ATT_HEADS] + [D_LRU] * 2
    splits = [int(s) for s in np.cumsum(sizes)[:-1]]
    c_b, c_c, c_v, q, k, v, f_logit, lru_gate, lru_x = jnp.split(h @ w_in, splits, axis=-1)
    y_conv = c_b * causal_depthwise_conv(c_c * c_v, conv_w)
    y_att = forgetting_attention(q, k, v, f_logit, fgate_b)
    xr = causal_depthwise_conv(lru_x, lru_conv_w) + lru_conv_b
    y_lru = jax.nn.gelu(lru_gate) * rg_lru(xr, lru_w_a, lru_b_a, lru_w_x, lru_b_x, lru_lambda)
    g_c, g_a, g_l = jnp.split(out_norm, [D_CONV, D_CONV + D_ATT])
    y = jnp.concatenate([rmsnorm(y_conv, g_c), rmsnorm(y_att, g_a), rmsnorm(y_lru, g_l)], axis=-1)
    return y @ w_out


def _fwd_setup_inputs(seed: int = 0) -> dict:
    key = jax.random.key(seed)
    ks = iter(jax.random.split(key, 32))
    f32 = jnp.float32
    res_scale = (2.0 * DEPTH) ** -0.5

    def normal(shape, std):
        return jax.random.normal(next(ks), shape, f32) * std

    def gain(shape):
        return 1.0 + normal(shape, 0.02)

    x = jax.random.normal(next(ks), (BATCH, SEQ, D_MODEL), f32)
    a_c = jax.random.uniform(next(ks), (DEPTH, D_LRU), f32, 0.9, 0.999)
    s = a_c ** (1.0 / LRU_C)
    lru_lambda = jnp.log(s) - jnp.log1p(-s)
    return {
        "x": x,
        "norm_ffn1": gain((DEPTH, D_MODEL)),
        "ffn1_w_in": normal((DEPTH, D_MODEL, 2 * D_FF), D_MODEL ** -0.5),
        "ffn1_w_out": normal((DEPTH, D_FF, D_MODEL), D_FF ** -0.5 * res_scale),
        "norm_mix": gain((DEPTH, D_MODEL)),
        "mix_w_in": normal((DEPTH, D_MODEL, D_IN), D_MODEL ** -0.5),
        "conv_w": normal((DEPTH, CONV_WIDTH, D_CONV), CONV_WIDTH ** -0.5),
        "fgate_b": jax.random.uniform(next(ks), (DEPTH, N_ATT_HEADS), f32, 1.0, 4.0),
        "lru_conv_w": normal((DEPTH, LRU_CONV_WIDTH, D_LRU), LRU_CONV_WIDTH ** -0.5),
        "lru_conv_b": normal((DEPTH, D_LRU), 0.02),
        "lru_w_a": normal((DEPTH, LRU_BLOCKS, LRU_BLOCK, LRU_BLOCK), LRU_BLOCK ** -0.5),
        "lru_b_a": normal((DEPTH, D_LRU), 0.02),
        "lru_w_x": normal((DEPTH, LRU_BLOCKS, LRU_BLOCK, LRU_BLOCK), LRU_BLOCK ** -0.5),
        "lru_b_x": normal((DEPTH, D_LRU), 0.02),
        "lru_lambda": lru_lambda,
        "mix_out_norm": gain((DEPTH, D_MIX)),
        "mix_w_out": normal((DEPTH, D_MIX, D_MODEL), D_MIX ** -0.5 * res_scale),
        "norm_ffn2": gain((DEPTH, D_MODEL)),
        "ffn2_w_in": normal((DEPTH, D_MODEL, 2 * D_FF), D_MODEL ** -0.5),
        "ffn2_w_out": normal((DEPTH, D_FF, D_MODEL), D_FF ** -0.5 * res_scale),
        "final_norm": gain((D_MODEL,)),
    }


def _fwd_reference(x, norm_ffn1, ffn1_w_in, ffn1_w_out, norm_mix, mix_w_in, conv_w, fgate_b,
              lru_conv_w, lru_conv_b, lru_w_a, lru_b_a, lru_w_x, lru_b_x, lru_lambda,
              mix_out_norm, mix_w_out, norm_ffn2, ffn2_w_in, ffn2_w_out, final_norm):
    for l in range(DEPTH):
        x = x + 0.5 * swiglu(rmsnorm(x, norm_ffn1[l]), ffn1_w_in[l], ffn1_w_out[l])
        x = x + parallel_mixer(rmsnorm(x, norm_mix[l]), mix_w_in[l], conv_w[l], fgate_b[l],
                               lru_conv_w[l], lru_conv_b[l], lru_w_a[l], lru_b_a[l],
                               lru_w_x[l], lru_b_x[l], lru_lambda[l], mix_out_norm[l],
                               mix_w_out[l])
        x = x + 0.5 * swiglu(rmsnorm(x, norm_ffn2[l]), ffn2_w_in[l], ffn2_w_out[l])
    return rmsnorm(x, final_norm)


import jax as _jax
import jax.numpy as _jnp

TWIN_FORMAT = 'train_step'
FWD_PARAMS = ['x', 'norm_ffn1', 'ffn1_w_in', 'ffn1_w_out', 'norm_mix', 'mix_w_in', 'conv_w', 'fgate_b', 'lru_conv_w', 'lru_conv_b', 'lru_w_a', 'lru_b_a', 'lru_w_x', 'lru_b_x', 'lru_lambda', 'mix_out_norm', 'mix_w_out', 'norm_ffn2', 'ffn2_w_in', 'ffn2_w_out', 'final_norm']
TWIN_WEIGHTS = ['norm_ffn1', 'ffn1_w_in', 'ffn1_w_out', 'norm_mix', 'mix_w_in', 'conv_w', 'fgate_b', 'lru_conv_w', 'lru_conv_b', 'lru_w_a', 'lru_b_a', 'lru_w_x', 'lru_b_x', 'lru_lambda', 'mix_out_norm', 'mix_w_out', 'norm_ffn2', 'ffn2_w_in', 'ffn2_w_out', 'final_norm']
TWIN_DIFF_INPUT = 'x'
TWIN_INPUTS = ['x', 'norm_ffn1', 'ffn1_w_in', 'ffn1_w_out', 'norm_mix', 'mix_w_in', 'conv_w', 'fgate_b', 'lru_conv_w', 'lru_conv_b', 'lru_w_a', 'lru_b_a', 'lru_w_x', 'lru_b_x', 'lru_lambda', 'mix_out_norm', 'mix_w_out', 'norm_ffn2', 'ffn2_w_in', 'ffn2_w_out', 'final_norm', 'loss_target', 'm_norm_ffn1', 'm_ffn1_w_in', 'm_ffn1_w_out', 'm_norm_mix', 'm_mix_w_in', 'm_conv_w', 'm_fgate_b', 'm_lru_conv_w', 'm_lru_conv_b', 'm_lru_w_a', 'm_lru_b_a', 'm_lru_w_x', 'm_lru_b_x', 'm_lru_lambda', 'm_mix_out_norm', 'm_mix_w_out', 'm_norm_ffn2', 'm_ffn2_w_in', 'm_ffn2_w_out', 'm_final_norm', 'v_norm_ffn1', 'v_ffn1_w_in', 'v_ffn1_w_out', 'v_norm_mix', 'v_mix_w_in', 'v_conv_w', 'v_fgate_b', 'v_lru_conv_w', 'v_lru_conv_b', 'v_lru_w_a', 'v_lru_b_a', 'v_lru_w_x', 'v_lru_b_x', 'v_lru_lambda', 'v_mix_out_norm', 'v_mix_w_out', 'v_norm_ffn2', 'v_ffn2_w_in', 'v_ffn2_w_out', 'v_final_norm']
TWIN_OUTPUTS = ['loss', 'grad_x', 'grad_norm_ffn1', 'grad_ffn1_w_in', 'grad_ffn1_w_out', 'grad_norm_mix', 'grad_mix_w_in', 'grad_conv_w', 'grad_fgate_b', 'grad_lru_conv_w', 'grad_lru_conv_b', 'grad_lru_w_a', 'grad_lru_b_a', 'grad_lru_w_x', 'grad_lru_b_x', 'grad_lru_lambda', 'grad_mix_out_norm', 'grad_mix_w_out', 'grad_norm_ffn2', 'grad_ffn2_w_in', 'grad_ffn2_w_out', 'grad_final_norm', 'delta_norm_ffn1', 'delta_ffn1_w_in', 'delta_ffn1_w_out', 'delta_norm_mix', 'delta_mix_w_in', 'delta_conv_w', 'delta_fgate_b', 'delta_lru_conv_w', 'delta_lru_conv_b', 'delta_lru_w_a', 'delta_lru_b_a', 'delta_lru_w_x', 'delta_lru_b_x', 'delta_lru_lambda', 'delta_mix_out_norm', 'delta_mix_w_out', 'delta_norm_ffn2', 'delta_ffn2_w_in', 'delta_ffn2_w_out', 'delta_final_norm', 'new_m_norm_ffn1', 'new_m_ffn1_w_in', 'new_m_ffn1_w_out', 'new_m_norm_mix', 'new_m_mix_w_in', 'new_m_conv_w', 'new_m_fgate_b', 'new_m_lru_conv_w', 'new_m_lru_conv_b', 'new_m_lru_w_a', 'new_m_lru_b_a', 'new_m_lru_w_x', 'new_m_lru_b_x', 'new_m_lru_lambda', 'new_m_mix_out_norm', 'new_m_mix_w_out', 'new_m_norm_ffn2', 'new_m_ffn2_w_in', 'new_m_ffn2_w_out', 'new_m_final_norm', 'new_v_norm_ffn1', 'new_v_ffn1_w_in', 'new_v_ffn1_w_out', 'new_v_norm_mix', 'new_v_mix_w_in', 'new_v_conv_w', 'new_v_fgate_b', 'new_v_lru_conv_w', 'new_v_lru_conv_b', 'new_v_lru_w_a', 'new_v_lru_b_a', 'new_v_lru_w_x', 'new_v_lru_b_x', 'new_v_lru_lambda', 'new_v_mix_out_norm', 'new_v_mix_w_out', 'new_v_norm_ffn2', 'new_v_ffn2_w_in', 'new_v_ffn2_w_out', 'new_v_final_norm']
TWIN_LEAF_KINDS = {'loss': 'loss', 'grad_x': 'grad_x', 'grad_norm_ffn1': 'grad_w', 'grad_ffn1_w_in': 'grad_w', 'grad_ffn1_w_out': 'grad_w', 'grad_norm_mix': 'grad_w', 'grad_mix_w_in': 'grad_w', 'grad_conv_w': 'grad_w', 'grad_fgate_b': 'grad_w', 'grad_lru_conv_w': 'grad_w', 'grad_lru_conv_b': 'grad_w', 'grad_lru_w_a': 'grad_w', 'grad_lru_b_a': 'grad_w', 'grad_lru_w_x': 'grad_w', 'grad_lru_b_x': 'grad_w', 'grad_lru_lambda': 'grad_w', 'grad_mix_out_norm': 'grad_w', 'grad_mix_w_out': 'grad_w', 'grad_norm_ffn2': 'grad_w', 'grad_ffn2_w_in': 'grad_w', 'grad_ffn2_w_out': 'grad_w', 'grad_final_norm': 'grad_w', 'delta_norm_ffn1': 'delta_w', 'delta_ffn1_w_in': 'delta_w', 'delta_ffn1_w_out': 'delta_w', 'delta_norm_mix': 'delta_w', 'delta_mix_w_in': 'delta_w', 'delta_conv_w': 'delta_w', 'delta_fgate_b': 'delta_w', 'delta_lru_conv_w': 'delta_w', 'delta_lru_conv_b': 'delta_w', 'delta_lru_w_a': 'delta_w', 'delta_lru_b_a': 'delta_w', 'delta_lru_w_x': 'delta_w', 'delta_lru_b_x': 'delta_w', 'delta_lru_lambda': 'delta_w', 'delta_mix_out_norm': 'delta_w', 'delta_mix_w_out': 'delta_w', 'delta_norm_ffn2': 'delta_w', 'delta_ffn2_w_in': 'delta_w', 'delta_ffn2_w_out': 'delta_w', 'delta_final_norm': 'delta_w', 'new_m_norm_ffn1': 'new_m', 'new_m_ffn1_w_in': 'new_m', 'new_m_ffn1_w_out': 'new_m', 'new_m_norm_mix': 'new_m', 'new_m_mix_w_in': 'new_m', 'new_m_conv_w': 'new_m', 'new_m_fgate_b': 'new_m', 'new_m_lru_conv_w': 'new_m', 'new_m_lru_conv_b': 'new_m', 'new_m_lru_w_a': 'new_m', 'new_m_lru_b_a': 'new_m', 'new_m_lru_w_x': 'new_m', 'new_m_lru_b_x': 'new_m', 'new_m_lru_lambda': 'new_m', 'new_m_mix_out_norm': 'new_m', 'new_m_mix_w_out': 'new_m', 'new_m_norm_ffn2': 'new_m', 'new_m_ffn2_w_in': 'new_m', 'new_m_ffn2_w_out': 'new_m', 'new_m_final_norm': 'new_m', 'new_v_norm_ffn1': 'new_v', 'new_v_ffn1_w_in': 'new_v', 'new_v_ffn1_w_out': 'new_v', 'new_v_norm_mix': 'new_v', 'new_v_mix_w_in': 'new_v', 'new_v_conv_w': 'new_v', 'new_v_fgate_b': 'new_v', 'new_v_lru_conv_w': 'new_v', 'new_v_lru_conv_b': 'new_v', 'new_v_lru_w_a': 'new_v', 'new_v_lru_b_a': 'new_v', 'new_v_lru_w_x': 'new_v', 'new_v_lru_b_x': 'new_v', 'new_v_lru_lambda': 'new_v', 'new_v_mix_out_norm': 'new_v', 'new_v_mix_w_out': 'new_v', 'new_v_norm_ffn2': 'new_v', 'new_v_ffn2_w_in': 'new_v', 'new_v_ffn2_w_out': 'new_v', 'new_v_final_norm': 'new_v'}


def _forward(args):
    return _fwd_reference(*[args[k] for k in FWD_PARAMS])


def _output_shape():
    def fwd():
        inp = _fwd_setup_inputs(0)
        return _fwd_reference(*[inp[k] for k in FWD_PARAMS])
    out = _jax.eval_shape(fwd)
    return out.shape, out.dtype

N_MICROBATCH = 1
ADAM_LR = 0.001
ADAM_B1 = 0.9
ADAM_B2 = 0.999
ADAM_EPS = 1e-08
ADAM_WD = 0.01
ADAM_STEP = 10
PER_EXAMPLE_BATCH_AXIS = {'x': 0, 'loss_target': 0}
SHARED_INPUTS = []
_WEIGHT_DTYPES = {'norm_ffn1': _jnp.float32, 'ffn1_w_in': _jnp.float32, 'ffn1_w_out': _jnp.float32, 'norm_mix': _jnp.float32, 'mix_w_in': _jnp.float32, 'conv_w': _jnp.float32, 'fgate_b': _jnp.float32, 'lru_conv_w': _jnp.float32, 'lru_conv_b': _jnp.float32, 'lru_w_a': _jnp.float32, 'lru_b_a': _jnp.float32, 'lru_w_x': _jnp.float32, 'lru_b_x': _jnp.float32, 'lru_lambda': _jnp.float32, 'mix_out_norm': _jnp.float32, 'mix_w_out': _jnp.float32, 'norm_ffn2': _jnp.float32, 'ffn2_w_in': _jnp.float32, 'ffn2_w_out': _jnp.float32, 'final_norm': _jnp.float32}
MOMENT_SCALE = {'norm_ffn1': 1.434589e-02, 'ffn1_w_in': 6.128378e-03, 'ffn1_w_out': 2.829652e-02, 'norm_mix': 4.744357e-02, 'mix_w_in': 2.830434e-02, 'conv_w': 3.075757e-02, 'fgate_b': 1.912412e-01, 'lru_conv_w': 3.355757e-02, 'lru_conv_b': 3.509435e-01, 'lru_w_a': 9.137878e-03, 'lru_b_a': 7.529211e-03, 'lru_w_x': 1.637001e-02, 'lru_b_x': 1.224642e-02, 'lru_lambda': 1.563363e-02, 'mix_out_norm': 3.222851e-02, 'mix_w_out': 9.035761e-02, 'norm_ffn2': 1.276218e-02, 'ffn2_w_in': 5.425657e-03, 'ffn2_w_out': 2.505026e-02, 'final_norm': 1.600536e+01}


def _to_microbatches(a, axis):
    t = _jnp.moveaxis(a, axis, 0)
    t = t.reshape((N_MICROBATCH, t.shape[0] // N_MICROBATCH) + t.shape[1:])
    return _jnp.moveaxis(t, 1, axis + 1)


def setup_inputs(seed: int = 0) -> dict:
    inp = _fwd_setup_inputs(seed)
    key = _jax.random.fold_in(_jax.random.key(seed), 7919)
    shape, _ = _output_shape()
    out = dict(inp)
    out["loss_target"] = _jax.random.normal(_jax.random.fold_in(key, 0), shape, _jnp.float32)
    for i, name in enumerate(TWIN_WEIGHTS):
        w = inp[name].astype(_jnp.float32)
        if MOMENT_SCALE is None:
            s = _jnp.sqrt(_jnp.mean(_jnp.square(w)) + 1e-30)
        else:
            s = MOMENT_SCALE[name]
        km, kv = _jax.random.split(_jax.random.fold_in(key, i + 1))
        out[name] = w
        out["m_" + name] = s * _jax.random.normal(km, w.shape, _jnp.float32)
        out["v_" + name] = (s * s) * _jax.random.uniform(kv, w.shape, _jnp.float32, 0.5, 1.5)
    if N_MICROBATCH > 1:
        for name, axis in PER_EXAMPLE_BATCH_AXIS.items():
            out[name] = _to_microbatches(out[name], axis)
    return {'x': out['x'], 'norm_ffn1': out['norm_ffn1'], 'ffn1_w_in': out['ffn1_w_in'], 'ffn1_w_out': out['ffn1_w_out'], 'norm_mix': out['norm_mix'], 'mix_w_in': out['mix_w_in'], 'conv_w': out['conv_w'], 'fgate_b': out['fgate_b'], 'lru_conv_w': out['lru_conv_w'], 'lru_conv_b': out['lru_conv_b'], 'lru_w_a': out['lru_w_a'], 'lru_b_a': out['lru_b_a'], 'lru_w_x': out['lru_w_x'], 'lru_b_x': out['lru_b_x'], 'lru_lambda': out['lru_lambda'], 'mix_out_norm': out['mix_out_norm'], 'mix_w_out': out['mix_w_out'], 'norm_ffn2': out['norm_ffn2'], 'ffn2_w_in': out['ffn2_w_in'], 'ffn2_w_out': out['ffn2_w_out'], 'final_norm': out['final_norm'], 'loss_target': out['loss_target'], 'm_norm_ffn1': out['m_norm_ffn1'], 'm_ffn1_w_in': out['m_ffn1_w_in'], 'm_ffn1_w_out': out['m_ffn1_w_out'], 'm_norm_mix': out['m_norm_mix'], 'm_mix_w_in': out['m_mix_w_in'], 'm_conv_w': out['m_conv_w'], 'm_fgate_b': out['m_fgate_b'], 'm_lru_conv_w': out['m_lru_conv_w'], 'm_lru_conv_b': out['m_lru_conv_b'], 'm_lru_w_a': out['m_lru_w_a'], 'm_lru_b_a': out['m_lru_b_a'], 'm_lru_w_x': out['m_lru_w_x'], 'm_lru_b_x': out['m_lru_b_x'], 'm_lru_lambda': out['m_lru_lambda'], 'm_mix_out_norm': out['m_mix_out_norm'], 'm_mix_w_out': out['m_mix_w_out'], 'm_norm_ffn2': out['m_norm_ffn2'], 'm_ffn2_w_in': out['m_ffn2_w_in'], 'm_ffn2_w_out': out['m_ffn2_w_out'], 'm_final_norm': out['m_final_norm'], 'v_norm_ffn1': out['v_norm_ffn1'], 'v_ffn1_w_in': out['v_ffn1_w_in'], 'v_ffn1_w_out': out['v_ffn1_w_out'], 'v_norm_mix': out['v_norm_mix'], 'v_mix_w_in': out['v_mix_w_in'], 'v_conv_w': out['v_conv_w'], 'v_fgate_b': out['v_fgate_b'], 'v_lru_conv_w': out['v_lru_conv_w'], 'v_lru_conv_b': out['v_lru_conv_b'], 'v_lru_w_a': out['v_lru_w_a'], 'v_lru_b_a': out['v_lru_b_a'], 'v_lru_w_x': out['v_lru_w_x'], 'v_lru_b_x': out['v_lru_b_x'], 'v_lru_lambda': out['v_lru_lambda'], 'v_mix_out_norm': out['v_mix_out_norm'], 'v_mix_w_out': out['v_mix_w_out'], 'v_norm_ffn2': out['v_norm_ffn2'], 'v_ffn2_w_in': out['v_ffn2_w_in'], 'v_ffn2_w_out': out['v_ffn2_w_out'], 'v_final_norm': out['v_final_norm']}


def _loss(weights, diff, rest, loss_target):
    with _jax.named_scope("forward"):
        args = {**rest, TWIN_DIFF_INPUT: diff, **{k: w.astype(_WEIGHT_DTYPES[k]) for k, w in weights.items()}}
        y = _forward(args)
    with _jax.named_scope("loss_head"):
        err = _jnp.square(y.astype(_jnp.float32) - loss_target)
        return 0.5 * _jnp.sum(_jnp.mean(err, axis=-1)) if err.ndim else 0.5 * err


def _adamw(w, g, m, v):
    m = ADAM_B1 * m + (1.0 - ADAM_B1) * g
    v = ADAM_B2 * v + (1.0 - ADAM_B2) * _jnp.square(g)
    m_hat = m / (1.0 - ADAM_B1 ** ADAM_STEP)
    v_hat = v / (1.0 - ADAM_B2 ** ADAM_STEP)
    delta = -ADAM_LR * (m_hat / (_jnp.sqrt(v_hat) + ADAM_EPS) + ADAM_WD * w)
    return delta, m, v


def reference(x, norm_ffn1, ffn1_w_in, ffn1_w_out, norm_mix, mix_w_in, conv_w, fgate_b, lru_conv_w, lru_conv_b, lru_w_a, lru_b_a, lru_w_x, lru_b_x, lru_lambda, mix_out_norm, mix_w_out, norm_ffn2, ffn2_w_in, ffn2_w_out, final_norm, loss_target, m_norm_ffn1, m_ffn1_w_in, m_ffn1_w_out, m_norm_mix, m_mix_w_in, m_conv_w, m_fgate_b, m_lru_conv_w, m_lru_conv_b, m_lru_w_a, m_lru_b_a, m_lru_w_x, m_lru_b_x, m_lru_lambda, m_mix_out_norm, m_mix_w_out, m_norm_ffn2, m_ffn2_w_in, m_ffn2_w_out, m_final_norm, v_norm_ffn1, v_ffn1_w_in, v_ffn1_w_out, v_norm_mix, v_mix_w_in, v_conv_w, v_fgate_b, v_lru_conv_w, v_lru_conv_b, v_lru_w_a, v_lru_b_a, v_lru_w_x, v_lru_b_x, v_lru_lambda, v_mix_out_norm, v_mix_w_out, v_norm_ffn2, v_ffn2_w_in, v_ffn2_w_out, v_final_norm):
    given = dict(x=x, norm_ffn1=norm_ffn1, ffn1_w_in=ffn1_w_in, ffn1_w_out=ffn1_w_out, norm_mix=norm_mix, mix_w_in=mix_w_in, conv_w=conv_w, fgate_b=fgate_b, lru_conv_w=lru_conv_w, lru_conv_b=lru_conv_b, lru_w_a=lru_w_a, lru_b_a=lru_b_a, lru_w_x=lru_w_x, lru_b_x=lru_b_x, lru_lambda=lru_lambda, mix_out_norm=mix_out_norm, mix_w_out=mix_w_out, norm_ffn2=norm_ffn2, ffn2_w_in=ffn2_w_in, ffn2_w_out=ffn2_w_out, final_norm=final_norm, loss_target=loss_target, m_norm_ffn1=m_norm_ffn1, m_ffn1_w_in=m_ffn1_w_in, m_ffn1_w_out=m_ffn1_w_out, m_norm_mix=m_norm_mix, m_mix_w_in=m_mix_w_in, m_conv_w=m_conv_w, m_fgate_b=m_fgate_b, m_lru_conv_w=m_lru_conv_w, m_lru_conv_b=m_lru_conv_b, m_lru_w_a=m_lru_w_a, m_lru_b_a=m_lru_b_a, m_lru_w_x=m_lru_w_x, m_lru_b_x=m_lru_b_x, m_lru_lambda=m_lru_lambda, m_mix_out_norm=m_mix_out_norm, m_mix_w_out=m_mix_w_out, m_norm_ffn2=m_norm_ffn2, m_ffn2_w_in=m_ffn2_w_in, m_ffn2_w_out=m_ffn2_w_out, m_final_norm=m_final_norm, v_norm_ffn1=v_norm_ffn1, v_ffn1_w_in=v_ffn1_w_in, v_ffn1_w_out=v_ffn1_w_out, v_norm_mix=v_norm_mix, v_mix_w_in=v_mix_w_in, v_conv_w=v_conv_w, v_fgate_b=v_fgate_b, v_lru_conv_w=v_lru_conv_w, v_lru_conv_b=v_lru_conv_b, v_lru_w_a=v_lru_w_a, v_lru_b_a=v_lru_b_a, v_lru_w_x=v_lru_w_x, v_lru_b_x=v_lru_b_x, v_lru_lambda=v_lru_lambda, v_mix_out_norm=v_mix_out_norm, v_mix_w_out=v_mix_w_out, v_norm_ffn2=v_norm_ffn2, v_ffn2_w_in=v_ffn2_w_in, v_ffn2_w_out=v_ffn2_w_out, v_final_norm=v_final_norm)
    weights = {n: given[n] for n in TWIN_WEIGHTS}
    shared = {n: given[n] for n in SHARED_INPUTS}
    per_example = {n: given[n] for n in ['x']}
    grad_fn = _jax.value_and_grad(_loss, argnums=(0, 1))

    def one_microbatch(ex, loss_target):
        ex = dict(ex)
        diff = ex.pop(TWIN_DIFF_INPUT)
        return grad_fn(weights, diff, {**shared, **ex}, loss_target)

    if N_MICROBATCH == 1:
        loss, (grad_w, grad_x) = one_microbatch(per_example, given["loss_target"])
    else:
        def body(carry, xs):
            loss_sum, grad_sum = carry
            l_k, (gw_k, gx_k) = one_microbatch(xs[0], xs[1])
            with _jax.named_scope("update"):
                return (loss_sum + l_k, _jax.tree.map(_jnp.add, grad_sum, gw_k)), gx_k

        init = (_jnp.zeros((), _jnp.float32), _jax.tree.map(_jnp.zeros_like, weights))
        (loss, grad_w), grad_x = _jax.lax.scan(body, init, (per_example, given["loss_target"]))
    with _jax.named_scope("update"):
        delta_w, new_m, new_v = {}, {}, {}
        for n in TWIN_WEIGHTS:
            delta_w[n], new_m[n], new_v[n] = _adamw(weights[n], grad_w[n], given["m_" + n], given["v_" + n])
    return (loss, grad_x, *[grad_w[n] for n in TWIN_WEIGHTS], *[delta_w[n] for n in TWIN_WEIGHTS],
            *[new_m[n] for n in TWIN_WEIGHTS], *[new_v[n] for n in TWIN_WEIGHTS])
```

```python
import functools
import math

import jax
import jax.numpy as jnp
from jax import lax
from jax.experimental import pallas as pl
from jax.experimental.pallas import tpu as pltpu

F32 = jnp.float32
BF16 = jnp.bfloat16
S = jax.ShapeDtypeStruct
MESH = pl.DeviceIdType.MESH

D = 2048
DC = 512
DA = 1024
NH = 8
HD = 128
DL = 512
LB = 128
DIN = 5640
PW = 5760
C_Q, C_K, C_V = 12, 20, 28
C_GATE, C_LX, C_F = 36, 40, 44
EPS = 1e-6
LRU_C = 8.0
ATT_SCALE = HD ** -0.5
LR, B1, B2, AEPS, WD, STEP = 0.001, 0.9, 0.999, 1e-08, 0.01, 10
VMEM_LIMIT = 56 * 1024 * 1024

NT = (((1,), (1,)), ((), ()))
TN = (((0,), (0,)), ((), ()))
NN = (((1,), (0,)), ((), ()))


def _cp():
    return pltpu.CompilerParams(vmem_limit_bytes=VMEM_LIMIT)


def _bs(shape, fn):
    return pl.BlockSpec(shape, fn)


def _mm(name, a, b, a_spec, b_spec, o_spec, o_shape, grid, dims, nk, acc_tile, scale=1.0, res=None, r_spec=None):
    has_res = res is not None

    def body(*refs):
        if has_res:
            a_ref, b_ref, r_ref, o_ref = refs[:4]
            rest = refs[4:]
        else:
            a_ref, b_ref, o_ref = refs[:3]
            rest = refs[3:]
        prod = lax.dot_general(a_ref[...].astype(BF16), b_ref[...].astype(BF16), dims, preferred_element_type=F32)

        def finish(acc):
            if scale != 1.0:
                acc = acc * scale
            if has_res:
                acc = r_ref[...] + acc
            o_ref[...] = acc.astype(o_ref.dtype)

        if nk == 1:
            finish(prod)
        else:
            acc_ref = rest[0]
            k = pl.program_id(2)

            @pl.when(k == 0)
            def _():
                acc_ref[...] = prod

            @pl.when(k > 0)
            def _():
                acc_ref[...] += prod

            @pl.when(k == nk - 1)
            def _():
                finish(acc_ref[...])

    in_specs = [a_spec, b_spec] + ([r_spec] if has_res else [])
    args = (a, b) + ((res,) if has_res else ())
    scratch = [pltpu.VMEM(acc_tile, F32)] if nk > 1 else []
    return pl.pallas_call(body, grid=grid, in_specs=in_specs, out_specs=o_spec, out_shape=o_shape,
                          scratch_shapes=scratch, compiler_params=_cp(), name=name)(*args)


def _tile(n, pref):
    for t in pref:
        if n % t == 0:
            return t
    return n


def _rms_fwd(x, gain):
    T = x.shape[0]
    tb = _tile(T, (512,))

    def body(x_ref, g_ref, h_ref, r_ref):
        xv = x_ref[...]
        r = lax.rsqrt(jnp.mean(xv * xv, axis=1, keepdims=True) + EPS)
        h_ref[...] = (xv * r * g_ref[...]).astype(BF16)
        r_ref[...] = r

    return pl.pallas_call(
        body, grid=(T // tb,),
        in_specs=[_bs((tb, D), lambda i: (i, 0)), _bs((1, D), lambda i: (0, 0))],
        out_specs=[_bs((tb, D), lambda i: (i, 0)), _bs((tb, 1), lambda i: (i, 0))],
        out_shape=[S((T, D), BF16), S((T, 1), F32)], compiler_params=_cp(), name="rms_fwd")(x, gain)


def _rms_bwd(dh, x, rstd, gain, dres):
    T = x.shape[0]
    tb = _tile(T, (512,))

    def body(dh_ref, x_ref, r_ref, g_ref, dres_ref, dx_ref, dg_ref):
        i = pl.program_id(0)
        r = r_ref[...]
        xhat = x_ref[...] * r
        dh = dh_ref[...]
        dxh = dh * g_ref[...]
        m = jnp.mean(dxh * xhat, axis=1, keepdims=True)
        dx_ref[...] = dres_ref[...] + r * (dxh - xhat * m)
        part = jnp.sum(dh * xhat, axis=0, keepdims=True)

        @pl.when(i == 0)
        def _():
            dg_ref[...] = part

        @pl.when(i > 0)
        def _():
            dg_ref[...] += part

    row = _bs((tb, D), lambda i: (i, 0))
    return pl.pallas_call(
        body, grid=(T // tb,),
        in_specs=[row, row, _bs((tb, 1), lambda i: (i, 0)), _bs((1, D), lambda i: (0, 0)), row],
        out_specs=[row, _bs((1, D), lambda i: (0, 0))],
        out_shape=[S((T, D), F32), S((1, D), F32)], compiler_params=_cp(), name="rms_bwd")(dh, x, rstd, gain, dres)


def _loss_head(x, gain, tgt):
    T = x.shape[0]
    tb = _tile(T, (512,))

    def body(x_ref, g_ref, t_ref, l_ref, dx_ref, dg_ref):
        i = pl.program_id(0)
        xv = x_ref[...]
        g = g_ref[...]
        r = lax.rsqrt(jnp.mean(xv * xv, axis=1, keepdims=True) + EPS)
        xhat = xv * r
        e = xhat * g - t_ref[...]
        lpart = 0.5 * jnp.sum(jnp.sum(e * e, axis=1, keepdims=True), axis=0, keepdims=True) * (1.0 / D)
        dy = e * (1.0 / D)
        dxh = dy * g
        m = jnp.mean(dxh * xhat, axis=1, keepdims=True)
        dx_ref[...] = r * (dxh - xhat * m)
        gpart = jnp.sum(dy * xhat, axis=0, keepdims=True)
        lrow = jnp.broadcast_to(lpart, (1, 128))

        @pl.when(i == 0)
        def _():
            dg_ref[...] = gpart
            l_ref[...] = lrow

        @pl.when(i > 0)
        def _():
            dg_ref[...] += gpart
            l_ref[...] += lrow

    row = _bs((tb, D), lambda i: (i, 0))
    return pl.pallas_call(
        body, grid=(T // tb,),
        in_specs=[row, _bs((1, D), lambda i: (0, 0)), row],
        out_specs=[_bs((1, 128), lambda i: (0, 0)), row, _bs((1, D), lambda i: (0, 0))],
        out_shape=[S((1, 128), F32), S((T, D), F32), S((1, D), F32)], compiler_params=_cp(), name="loss_head")(x, gain, tgt)


def _ffn_in(h, win):
    T = h.shape[0]
    Fs = win.shape[2]
    F = 2 * Fs
    tn = _tile(Fs, (256, 128))
    nb = Fs // tn
    tm = _tile(T, (1024, 512))

    def body(h_ref, wg_ref, wu_ref, zg_ref, zu_ref, a_ref):
        hv = h_ref[...]
        zg = jnp.dot(hv, wg_ref[...], preferred_element_type=F32)
        zu = jnp.dot(hv, wu_ref[...], preferred_element_type=F32)
        zg_ref[...] = zg.astype(BF16)
        zu_ref[...] = zu.astype(BF16)
        a_ref[...] = (zg * jax.nn.sigmoid(zg) * zu).astype(BF16)

    col = _bs((tm, tn), lambda i, j: (i, j))
    return pl.pallas_call(
        body, grid=(T // tm, F // tn),
        in_specs=[_bs((tm, D), lambda i, j: (i, 0)),
                  _bs((None, D, tn), lambda i, j: (j // nb, 0, j % nb)),
                  _bs((None, D, tn), lambda i, j: (2 + j // nb, 0, j % nb))],
        out_specs=[col, col, col],
        out_shape=[S((T, F), BF16)] * 3, compiler_params=_cp(), name="ffn_in")(h, win, win)


def _ffn_out(act, wout, x):
    T, F = act.shape
    tm = _tile(T, (1024, 512))
    tn = 1024
    tk = _tile(F, (1408, 512, 256))
    nk = F // tk
    return _mm("ffn_out", act, wout,
               _bs((tm, tk), lambda i, j, k: (i, k)), _bs((tk, tn), lambda i, j, k: (k, j)),
               _bs((tm, tn), lambda i, j, k: (i, j)), S((T, D), F32), (T // tm, D // tn, nk), NN, nk, (tm, tn),
               scale=0.5, res=x, r_spec=_bs((tm, tn), lambda i, j, k: (i, j)))


def _ffn_bwd_dz(dyb, wout, zg, zu):
    T, F = zg.shape
    tm = _tile(T, (1024, 512))
    tn = _tile(F, (512, 256))

    def body(dy_ref, w_ref, zg_ref, zu_ref, dzg_ref, dzu_ref):
        da = 0.5 * lax.dot_general(dy_ref[...], w_ref[...], NT, preferred_element_type=F32)
        zg = zg_ref[...].astype(F32)
        zu = zu_ref[...].astype(F32)
        s = jax.nn.sigmoid(zg)
        dzg_ref[...] = (da * zu * (s * (1.0 + zg * (1.0 - s)))).astype(BF16)
        dzu_ref[...] = (da * (zg * s)).astype(BF16)

    col = _bs((tm, tn), lambda i, j: (i, j))
    nj = F // tn
    dzg, dzu = pl.pallas_call(
        body, grid=(T // tm, nj),
        in_specs=[_bs((tm, D), lambda i, j: (i, 0)), _bs((tn, D), lambda i, j: (j, 0)), col, col],
        out_specs=[col, col], out_shape=[S((T, F), BF16)] * 2, compiler_params=_cp(), name="ffn_bwd_dz")(dyb, wout, zg, zu)
    return dzg, dzu


def _ffn_bwd_dh(dzg, dzu, win):
    T, F = dzg.shape
    Fs = win.shape[2]
    tk = _tile(Fs, (1408, 256, 128))
    nkb = Fs // tk
    tm = _tile(T, (1024, 512))
    tn = 1024
    nk = 2 * nkb

    def body(dzg_ref, dzu_ref, wg_ref, wu_ref, o_ref, acc_ref):
        k = pl.program_id(2)
        prod = (lax.dot_general(dzg_ref[...], wg_ref[...], NT, preferred_element_type=F32)
                + lax.dot_general(dzu_ref[...], wu_ref[...], NT, preferred_element_type=F32))

        @pl.when(k == 0)
        def _():
            acc_ref[...] = prod

        @pl.when(k > 0)
        def _():
            acc_ref[...] += prod

        @pl.when(k == nk - 1)
        def _():
            o_ref[...] = acc_ref[...]

    a_spec = _bs((tm, tk), lambda i, j, k: (i, k))
    return pl.pallas_call(
        body, grid=(T // tm, D // tn, nk),
        in_specs=[a_spec, a_spec,
                  _bs((None, tn, tk), lambda i, j, k: (k // nkb, j, k % nkb)),
                  _bs((None, tn, tk), lambda i, j, k: (2 + k // nkb, j, k % nkb))],
        out_specs=_bs((tm, tn), lambda i, j, k: (i, j)), out_shape=S((T, D), F32),
        scratch_shapes=[pltpu.VMEM((tm, tn), F32)], compiler_params=_cp(), name="ffn_bwd_dh")(dzg, dzu, win, win)


def _ffn_bwd_dwin(h, dzg, dzu):
    T, F = dzg.shape
    Fs = F // 2
    tn = _tile(Fs, (1408, 256, 128))
    nb = Fs // tn
    tm = 1024
    tk = _tile(T, (1024, 512))
    nk = T // tk

    def one(name, dz):
        return _mm(name, h, dz,
                   _bs((tk, tm), lambda i, j, k: (k, i)), _bs((tk, tn), lambda i, j, k: (k, j)),
                   _bs((None, tm, tn), lambda i, j, k: (j // nb, i, j % nb)), S((2, D, Fs), BF16),
                   (D // tm, 2 * nb, nk), TN, nk, (tm, tn))

    return jnp.concatenate([one("ffn_bwd_dwg", dzg), one("ffn_bwd_dwu", dzu)], axis=0)


def _mm_tn(name, a, b, scale=1.0, tm=512, tn=1024):
    T, M = a.shape
    N = b.shape[1]
    tm = _tile(M, (tm, 512, 256, 128))
    tn = _tile(N, (tn, 1152, 1024, 512, 128))
    tk = _tile(T, (1024, 512))
    nk = T // tk
    return _mm(name, a, b,
               _bs((tk, tm), lambda i, j, k: (k, i)), _bs((tk, tn), lambda i, j, k: (k, j)),
               _bs((tm, tn), lambda i, j, k: (i, j)), S((M, N), BF16), (M // tm, N // tn, nk), TN, nk, (tm, tn), scale=scale)


def _mm_nt_full(name, a, b, tn):
    T, K = a.shape
    N = b.shape[0]
    tm = _tile(T, (1024, 512))
    return _mm(name, a, b,
               _bs((tm, K), lambda i, j, k: (i, 0)), _bs((tn, K), lambda i, j, k: (j, 0)),
               _bs((tm, tn), lambda i, j, k: (i, j)), S((T, N), F32), (T // tm, N // tn, 1), NT, 1, (tm, tn))


def _bt(T):
    return _tile(T, (512,))


def _down(ext, s):
    return pltpu.roll(ext, s, 0)[8:, :]


def _up(ext, s):
    n = ext.shape[0]
    return pltpu.roll(ext, n - s, 0)[: n - 8, :]


def _halo_prev(ref, start, b):
    lo = pl.multiple_of(jnp.maximum(start - 8, 0), 8)
    return ref[pl.ds(lo, 8), :] * (b > 0).astype(F32)


def _halo_next(ref, start, bt, b, nb):
    lo = pl.multiple_of(jnp.minimum(start + bt, (nb - 1) * bt), 8)
    return ref[pl.ds(lo, 8), :] * (b < nb - 1).astype(F32)


def _scan_fwd(A, U):
    n = U.shape[0]
    row = lax.broadcasted_iota(jnp.int32, U.shape, 0)
    d = 1
    while d < n:
        keep = row >= d
        Us = jnp.where(keep, pltpu.roll(U, d, 0), 0.0)
        if A is None:
            U = U + Us
        else:
            As = jnp.where(keep, pltpu.roll(A, d, 0), 1.0)
            U = A * Us + U
            A = A * As
        d *= 2
    return A, U


def _scan_bwd(A, U):
    n = U.shape[0]
    row = lax.broadcasted_iota(jnp.int32, U.shape, 0)
    d = 1
    while d < n:
        keep = row < n - d
        Us = jnp.where(keep, pltpu.roll(U, n - d, 0), 0.0)
        if A is None:
            U = U + Us
        else:
            As = jnp.where(keep, pltpu.roll(A, n - d, 0), 1.0)
            U = A * Us + U
            A = A * As
        d *= 2
    return A, U


def _softplus(z):
    return jnp.maximum(z, 0.0) + jnp.log(1.0 + jnp.exp(-jnp.abs(z)))


def _gelu_parts(g):
    k0 = math.sqrt(2.0 / math.pi)
    t = jnp.tanh(k0 * (g + 0.044715 * g * g * g))
    gel = 0.5 * g * (1.0 + t)
    dgel = 0.5 * (1.0 + t) + 0.5 * g * (1.0 - t * t) * k0 * (1.0 + 3.0 * 0.044715 * g * g)
    return gel, dgel


def _conv_fwd(P, cw):
    T = P.shape[0]
    bt = _bt(T)
    nb = T // bt

    def body(b_ref, c_ref, v_ref, w_ref, y_ref):
        w = w_ref[...]

        def step(b, carry):
            start = pl.multiple_of(b * bt, bt)
            rows = pl.ds(start, bt)
            m = c_ref[rows, :] * v_ref[rows, :]
            ext = jnp.concatenate([_halo_prev(c_ref, start, b) * _halo_prev(v_ref, start, b), m], axis=0)
            z = w[2:3, :] * m + w[1:2, :] * _down(ext, 1) + w[0:1, :] * _down(ext, 2)
            y_ref[rows, :] = b_ref[rows, :] * z
            return carry

        lax.fori_loop(0, nb, step, 0)

    def colspec(off):
        return _bs((T, 128), lambda c: (0, off + c))

    return pl.pallas_call(
        body, grid=(DC // 128,),
        in_specs=[colspec(0), colspec(4), colspec(8), _bs((8, 128), lambda c: (0, c))],
        out_specs=_bs((T, 128), lambda c: (0, c)), out_shape=S((T, DC), F32),
        compiler_params=_cp(), name="conv_fwd")(P, P, P, cw)


def _conv_bwd(P, cw, dy):
    T = P.shape[0]
    bt = _bt(T)
    nb = T // bt

    def body(b_ref, c_ref, v_ref, w_ref, dy_ref, db_ref, dc_ref, dv_ref, dw_ref):
        w = w_ref[...]

        def step(b, carry):
            a0, a1, a2 = carry
            start = pl.multiple_of(b * bt, bt)
            rows = pl.ds(start, bt)
            cb, cc, cv, dy = b_ref[rows, :], c_ref[rows, :], v_ref[rows, :], dy_ref[rows, :]
            m = cc * cv
            ext = jnp.concatenate([_halo_prev(c_ref, start, b) * _halo_prev(v_ref, start, b), m], axis=0)
            m1, m2 = _down(ext, 1), _down(ext, 2)
            z = w[2:3, :] * m + w[1:2, :] * m1 + w[0:1, :] * m2
            db_ref[rows, :] = dy * z
            dz = dy * cb
            extn = jnp.concatenate([dz, _halo_next(dy_ref, start, bt, b, nb) * _halo_next(b_ref, start, bt, b, nb)], axis=0)
            dm = w[2:3, :] * dz + w[1:2, :] * _up(extn, 1) + w[0:1, :] * _up(extn, 2)
            dc_ref[rows, :] = dm * cv
            dv_ref[rows, :] = dm * cc
            return (a0 + jnp.sum(dz * m2, axis=0, keepdims=True),
                    a1 + jnp.sum(dz * m1, axis=0, keepdims=True),
                    a2 + jnp.sum(dz * m, axis=0, keepdims=True))

        zero = jnp.zeros((1, 128), F32)
        a0, a1, a2 = lax.fori_loop(0, nb, step, (zero, zero, zero))
        dw_ref[...] = jnp.zeros((8, 128), F32)
        dw_ref[0:1, :] = a0
        dw_ref[1:2, :] = a1
        dw_ref[2:3, :] = a2

    def colspec(off):
        return _bs((T, 128), lambda c: (0, off + c))

    own = _bs((T, 128), lambda c: (0, c))
    return pl.pallas_call(
        body, grid=(DC // 128,),
        in_specs=[colspec(0), colspec(4), colspec(8), _bs((8, 128), lambda c: (0, c)), own],
        out_specs=[own, own, own, _bs((8, 128), lambda c: (0, c))],
        out_shape=[S((T, DC), F32)] * 3 + [S((8, DC), F32)], compiler_params=_cp(), name="conv_bwd")(P, P, P, cw, dy)


def _fgate_fwd(P, fb):
    T = P.shape[0]
    bt = _bt(T)
    nb = T // bt

    def body(f_ref, b_ref, c_ref):
        bias = b_ref[...]

        def step(b, carry):
            rows = pl.ds(pl.multiple_of(b * bt, bt), bt)
            logf = -_softplus(-(f_ref[rows, :] + bias))
            _, cs = _scan_fwd(None, logf)
            cs = cs + carry
            c_ref[rows, :] = cs
            return cs[bt - 1:bt, :]

        lax.fori_loop(0, nb, step, jnp.zeros((1, 128), F32))

    return pl.pallas_call(
        body, grid=(1,),
        in_specs=[_bs((T, 128), lambda i: (0, C_F)), _bs((1, 128), lambda i: (0, 0))],
        out_specs=_bs((T, 128), lambda i: (0, 0)), out_shape=S((T, 128), F32),
        compiler_params=_cp(), name="fgate_fwd")(P, fb)


def _fgate_bwd(P, fb, dcum):
    T = P.shape[0]
    bt = _bt(T)
    nb = T // bt

    def body(f_ref, b_ref, dc_ref, df_ref, db_ref):
        bias = b_ref[...]

        def step(i, carry):
            run, acc = carry
            b = nb - 1 - i
            rows = pl.ds(pl.multiple_of(b * bt, bt), bt)
            _, rs = _scan_bwd(None, dc_ref[rows, :])
            rs = rs + run
            df = rs * jax.nn.sigmoid(-(f_ref[rows, :] + bias))
            df_ref[rows, :] = df
            return rs[0:1, :], acc + jnp.sum(df, axis=0, keepdims=True)

        zero = jnp.zeros((1, 128), F32)
        _, acc = lax.fori_loop(0, nb, step, (zero, zero))
        db_ref[...] = acc

    return pl.pallas_call(
        body, grid=(1,),
        in_specs=[_bs((T, 128), lambda i: (0, C_F)), _bs((1, 128), lambda i: (0, 0)), _bs((T, 128), lambda i: (0, 0))],
        out_specs=[_bs((T, 128), lambda i: (0, 0)), _bs((1, 128), lambda i: (0, 0))],
        out_shape=[S((T, 128), F32), S((1, 128), F32)], compiler_params=_cp(), name="fgate_bwd")(P, fb, dcum)


def _att_tile(T):
    return _tile(T, (512,))


def _attn_fwd(P, cumq, cumk):
    T = P.shape[0]
    tq = _att_tile(T)
    nq = T // tq

    def body(q_ref, k_ref, v_ref, cq_ref, ck_ref, o_ref, lse_ref, m_s, l_s, acc_s):
        i, j = pl.program_id(1), pl.program_id(2)

        @pl.when(j == 0)
        def _():
            m_s[...] = jnp.full((tq, 1), -jnp.inf, F32)
            l_s[...] = jnp.zeros((tq, 1), F32)
            acc_s[...] = jnp.zeros((tq, HD), F32)

        @pl.when(j <= i)
        def _():
            s = lax.dot_general(q_ref[...].astype(BF16), k_ref[...].astype(BF16), NT, preferred_element_type=F32)
            s = s * ATT_SCALE + cq_ref[...] - ck_ref[...]
            rowp = i * tq + lax.broadcasted_iota(jnp.int32, (tq, tq), 0)
            colp = j * tq + lax.broadcasted_iota(jnp.int32, (tq, tq), 1)
            s = jnp.where(colp <= rowp, s, -jnp.inf)
            m_old = m_s[...]
            m_new = jnp.maximum(m_old, jnp.max(s, axis=1, keepdims=True))
            p = jnp.exp(s - m_new)
            alpha = jnp.exp(m_old - m_new)
            l_s[...] = alpha * l_s[...] + jnp.sum(p, axis=1, keepdims=True)
            acc_s[...] = alpha * acc_s[...] + jnp.dot(p.astype(BF16), v_ref[...].astype(BF16), preferred_element_type=F32)
            m_s[...] = m_new

        @pl.when(j == nq - 1)
        def _():
            o_ref[...] = acc_s[...] / l_s[...]
            lse_ref[...] = m_s[...] + jnp.log(l_s[...])

    return pl.pallas_call(
        body, grid=(NH, nq, nq),
        in_specs=[_bs((tq, HD), lambda h, i, j: (i, C_Q + h)),
                  _bs((tq, HD), lambda h, i, j: (jnp.minimum(j, i), C_K + h)),
                  _bs((tq, HD), lambda h, i, j: (jnp.minimum(j, i), C_V + h)),
                  _bs((None, tq, 1), lambda h, i, j: (h, i, 0)),
                  _bs((None, 1, tq), lambda h, i, j: (h, 0, jnp.minimum(j, i)))],
        out_specs=[_bs((tq, HD), lambda h, i, j: (i, h)), _bs((None, tq, 1), lambda h, i, j: (h, i, 0))],
        out_shape=[S((T, DA), F32), S((NH, T, 1), F32)],
        scratch_shapes=[pltpu.VMEM((tq, 1), F32), pltpu.VMEM((tq, 1), F32), pltpu.VMEM((tq, HD), F32)],
        compiler_params=_cp(), name="attn_fwd")(P, P, P, cumq, cumk)


def _attn_bwd(P, cumq, cumk, lse, o, do):
    T = P.shape[0]
    tq = _att_tile(T)
    nq = T // tq

    def body(q_ref, k_ref, v_ref, cq_ref, ck_ref, lse_ref, o_ref, do_ref,
             dq_ref, dk_ref, dv_ref, dc_ref, dr_ref, dk_s, dv_s, dc_s):
        j, i = pl.program_id(1), pl.program_id(2)

        @pl.when((j == 0) & (i == 0))
        def _():
            dq_ref[...] = jnp.zeros((T, HD), F32)
            dr_ref[...] = jnp.zeros((T, 1), F32)

        @pl.when(i == 0)
        def _():
            dk_s[...] = jnp.zeros((tq, HD), F32)
            dv_s[...] = jnp.zeros((tq, HD), F32)
            dc_s[...] = jnp.zeros((1, tq), F32)

        @pl.when(i >= j)
        def _():
            q = q_ref[...].astype(BF16)
            k = k_ref[...].astype(BF16)
            v = v_ref[...].astype(BF16)
            do_f = do_ref[...]
            dob = do_f.astype(BF16)
            s = lax.dot_general(q, k, NT, preferred_element_type=F32)
            s = s * ATT_SCALE + cq_ref[...] - ck_ref[...]
            rowp = i * tq + lax.broadcasted_iota(jnp.int32, (tq, tq), 0)
            colp = j * tq + lax.broadcasted_iota(jnp.int32, (tq, tq), 1)
            p = jnp.where(colp <= rowp, jnp.exp(s - lse_ref[...]), 0.0)
            delta = jnp.sum(do_f * o_ref[...], axis=1, keepdims=True)
            dp = lax.dot_general(dob, v, NT, preferred_element_type=F32)
            ds = p * (dp - delta)
            dsb = (ds * ATT_SCALE).astype(BF16)
            dv_s[...] += lax.dot_general(p.astype(BF16), dob, TN, preferred_element_type=F32)
            dk_s[...] += lax.dot_general(dsb, q, TN, preferred_element_type=F32)
            rows = pl.ds(pl.multiple_of(i * tq, tq), tq)
            dq_ref[rows, :] += jnp.dot(dsb, k, preferred_element_type=F32)
            dc_s[...] -= jnp.sum(ds, axis=0, keepdims=True)
            dr_ref[rows, :] += jnp.sum(ds, axis=1, keepdims=True)

        @pl.when(i == nq - 1)
        def _():
            dk_ref[...] = dk_s[...]
            dv_ref[...] = dv_s[...]
            dc_ref[...] = dc_s[...]

    def qside(col):
        return _bs((tq, HD), lambda h, j, i: (jnp.maximum(i, j), col + h))

    qvec = _bs((None, tq, 1), lambda h, j, i: (h, jnp.maximum(i, j), 0))
    kv_out = _bs((tq, HD), lambda h, j, i: (j, h))
    return pl.pallas_call(
        body, grid=(NH, nq, nq),
        in_specs=[qside(C_Q), _bs((tq, HD), lambda h, j, i: (j, C_K + h)), _bs((tq, HD), lambda h, j, i: (j, C_V + h)),
                  qvec, _bs((None, 1, tq), lambda h, j, i: (h, 0, j)), qvec, qside(0), qside(0)],
        out_specs=[_bs((T, HD), lambda h, j, i: (0, h)), kv_out, kv_out, _bs((None, 1, tq), lambda h, j, i: (h, 0, j)),
                   _bs((None, T, 1), lambda h, j, i: (h, 0, 0))],
        out_shape=[S((T, DA), F32)] * 3 + [S((NH, 1, T), F32), S((NH, T, 1), F32)],
        scratch_shapes=[pltpu.VMEM((tq, HD), F32), pltpu.VMEM((tq, HD), F32), pltpu.VMEM((1, tq), F32)],
        compiler_params=_cp(), name="attn_bwd")(P, P, P, cumq, cumk, lse, o, do)


def _lru_gates(xr, wa, wx, ba, bx, sp):
    xb = xr.astype(BF16)
    r = jax.nn.sigmoid(jnp.dot(xb, wa, preferred_element_type=F32) + ba)
    ig = jax.nn.sigmoid(jnp.dot(xb, wx, preferred_element_type=F32) + bx)
    log_a = -LRU_C * r * sp
    a = jnp.exp(log_a)
    th = jnp.tanh(log_a)
    om = -2.0 * th / (1.0 - th)
    mult = jnp.sqrt(om)
    return xb, r, ig, a, om, mult


def _lru_xr(lx_ref, cw, cb, start, b, rows):
    lx = lx_ref[rows, :]
    ext = jnp.concatenate([_halo_prev(lx_ref, start, b), lx], axis=0)
    return cw[3:4, :] * lx + cw[2:3, :] * _down(ext, 1) + cw[1:2, :] * _down(ext, 2) + cw[0:1, :] * _down(ext, 3) + cb


def _lru_fwd(P, lcw, vec, wa, wx):
    T = P.shape[0]
    bt = _bt(T)
    nb = T // bt

    def body(g_ref, lx_ref, cw_ref, vec_ref, wa_ref, wx_ref, y_ref, h_ref):
        cw = cw_ref[...]
        vec = vec_ref[...]
        wa = wa_ref[...].astype(BF16)
        wx = wx_ref[...].astype(BF16)
        sp = _softplus(-vec[3:4, :])

        def step(b, carry):
            start = pl.multiple_of(b * bt, bt)
            rows = pl.ds(start, bt)
            xr = _lru_xr(lx_ref, cw, vec[0:1, :], start, b, rows)
            _, _, ig, a, _, mult = _lru_gates(xr, wa, wx, vec[1:2, :], vec[2:3, :], sp)
            u = mult * (ig * xr)
            ac, hc = _scan_fwd(a, u)
            hb = hc + ac * carry
            h_ref[rows, :] = hb
            gel, _ = _gelu_parts(g_ref[rows, :])
            y_ref[rows, :] = gel * hb
            return hb[bt - 1:bt, :]

        lax.fori_loop(0, nb, step, jnp.zeros((1, 128), F32))

    own = _bs((T, 128), lambda c: (0, c))
    return pl.pallas_call(
        body, grid=(DL // 128,),
        in_specs=[_bs((T, 128), lambda c: (0, C_GATE + c)), _bs((T, 128), lambda c: (0, C_LX + c)),
                  _bs((8, 128), lambda c: (0, c)), _bs((8, 128), lambda c: (0, c)),
                  _bs((None, LB, LB), lambda c: (c, 0, 0)), _bs((None, LB, LB), lambda c: (c, 0, 0))],
        out_specs=[own, own], out_shape=[S((T, DL), F32)] * 2, compiler_params=_cp(), name="lru_fwd")(P, P, lcw, vec, wa, wx)


def _lru_bwd(P, lcw, vec, wa, wx, hst, dy):
    T = P.shape[0]
    bt = _bt(T)
    nb = T // bt

    def body(g_ref, lx_ref, cw_ref, vec_ref, wa_ref, wx_ref, h_ref, dy_ref,
             dg_ref, dlx_ref, sm_ref, dwa_ref, dwx_ref, dxr_s):
        cw = cw_ref[...]
        vec = vec_ref[...]
        wa = wa_ref[...].astype(BF16)
        wx = wx_ref[...].astype(BF16)
        lam = vec[3:4, :]
        sp = _softplus(-lam)
        dwa_ref[...] = jnp.zeros((LB, LB), F32)
        dwx_ref[...] = jnp.zeros((LB, LB), F32)
        zero = jnp.zeros((1, 128), F32)

        def step1(i, carry):
            wc, s_cb, s_ba, s_bx, s_sp = carry
            b = nb - 1 - i
            start = pl.multiple_of(b * bt, bt)
            rows = pl.ds(start, bt)
            xr = _lru_xr(lx_ref, cw, vec[0:1, :], start, b, rows)
            xb, r, ig, a, om, mult = _lru_gates(xr, wa, wx, vec[1:2, :], vec[2:3, :], sp)
            hb = h_ref[rows, :]
            dy = dy_ref[rows, :]
            gel, dgel = _gelu_parts(g_ref[rows, :])
            dg_ref[rows, :] = dy * hb * dgel
            dh = dy * gel
            ac, wcum = _scan_bwd(a, a * dh)
            w = wcum + ac * wc
            g = dh + _up(jnp.concatenate([w, jnp.broadcast_to(wc, (8, 128))], axis=0), 1)
            hprev = _down(jnp.concatenate([_halo_prev(h_ref, start, b), hb], axis=0), 1)
            da = g * hprev
            dmult = g * (ig * xr)
            dix = g * mult
            di = dix * xr
            dlog_a = da * a - dmult * ((1.0 - om) / mult)
            dr = dlog_a * (-LRU_C * sp)
            dpr = dr * r * (1.0 - r)
            dpi = di * ig * (1.0 - ig)
            dprb, dpib = dpr.astype(BF16), dpi.astype(BF16)
            dwa_ref[...] += lax.dot_general(xb, dprb, TN, preferred_element_type=F32)
            dwx_ref[...] += lax.dot_general(xb, dpib, TN, preferred_element_type=F32)
            dxr = (dix * ig + lax.dot_general(dprb, wa, NT, preferred_element_type=F32)
                   + lax.dot_general(dpib, wx, NT, preferred_element_type=F32))
            dxr_s[rows, :] = dxr
            return (w[0:1, :], s_cb + jnp.sum(dxr, axis=0, keepdims=True), s_ba + jnp.sum(dpr, axis=0, keepdims=True),
                    s_bx + jnp.sum(dpi, axis=0, keepdims=True), s_sp + jnp.sum(dlog_a * (-LRU_C * r), axis=0, keepdims=True))

        _, s_cb, s_ba, s_bx, s_sp = lax.fori_loop(0, nb, step1, (zero, zero, zero, zero, zero))

        def step2(b, carry):
            t0, t1, t2, t3 = carry
            start = pl.multiple_of(b * bt, bt)
            rows = pl.ds(start, bt)
            dxr = dxr_s[rows, :]
            extn = jnp.concatenate([dxr, _halo_next(dxr_s, start, bt, b, nb)], axis=0)
            dlx_ref[rows, :] = (cw[3:4, :] * dxr + cw[2:3, :] * _up(extn, 1) + cw[1:2, :] * _up(extn, 2)
                                + cw[0:1, :] * _up(extn, 3))
            lx = lx_ref[rows, :]
            ext = jnp.concatenate([_halo_prev(lx_ref, start, b), lx], axis=0)
            return (t0 + jnp.sum(dxr * _down(ext, 3), axis=0, keepdims=True),
                    t1 + jnp.sum(dxr * _down(ext, 2), axis=0, keepdims=True),
                    t2 + jnp.sum(dxr * _down(ext, 1), axis=0, keepdims=True),
                    t3 + jnp.sum(dxr * lx, axis=0, keepdims=True))

        t0, t1, t2, t3 = lax.fori_loop(0, nb, step2, (zero, zero, zero, zero))
        sm_ref[...] = jnp.zeros((16, 128), F32)
        for k, val in enumerate((t0, t1, t2, t3, s_cb, s_ba, s_bx, -s_sp * jax.nn.sigmoid(-lam))):
            sm_ref[k:k + 1, :] = val

    own = _bs((T, 128), lambda c: (0, c))
    wspec = _bs((None, LB, LB), lambda c: (c, 0, 0))
    return pl.pallas_call(
        body, grid=(DL // 128,),
        in_specs=[_bs((T, 128), lambda c: (0, C_GATE + c)), _bs((T, 128), lambda c: (0, C_LX + c)),
                  _bs((8, 128), lambda c: (0, c)), _bs((8, 128), lambda c: (0, c)), wspec, wspec, own, own],
        out_specs=[own, own, _bs((16, 128), lambda c: (0, c)), wspec, wspec],
        out_shape=[S((T, DL), F32)] * 2 + [S((16, DL), F32), S((4, LB, LB), F32), S((4, LB, LB), F32)],
        scratch_shapes=[pltpu.VMEM((T, 128), F32)], compiler_params=_cp(), name="lru_bwd")(P, P, lcw, vec, wa, wx, hst, dy)


_GROUPS = ((0, DC), (DC, DC + DA), (DC + DA, D))


def _gnorm_fwd(yc, ya, yl, gain):
    T = yc.shape[0]
    tb = _tile(T, (512,))

    def body(c_ref, a_ref, l_ref, g_ref, yn_ref, r0_ref, r1_ref, r2_ref):
        for (lo, hi), src, r_ref in zip(_GROUPS, (c_ref, a_ref, l_ref), (r0_ref, r1_ref, r2_ref)):
            yv = src[...]
            r = lax.rsqrt(jnp.mean(yv * yv, axis=1, keepdims=True) + EPS)
            yn_ref[:, lo:hi] = (yv * r * g_ref[:, lo:hi]).astype(BF16)
            r_ref[...] = r

    rs = _bs((tb, 1), lambda i: (i, 0))
    return pl.pallas_call(
        body, grid=(T // tb,),
        in_specs=[_bs((tb, DC), lambda i: (i, 0)), _bs((tb, DA), lambda i: (i, 0)), _bs((tb, DL), lambda i: (i, 0)),
                  _bs((1, D), lambda i: (0, 0))],
        out_specs=[_bs((tb, D), lambda i: (i, 0)), rs, rs, rs],
        out_shape=[S((T, D), BF16)] + [S((T, 1), F32)] * 3, compiler_params=_cp(), name="gnorm_fwd")(yc, ya, yl, gain)


def _gnorm_bwd(dyn, yc, ya, yl, r0, r1, r2, gain):
    T = yc.shape[0]
    tb = _tile(T, (512,))

    def body(d_ref, c_ref, a_ref, l_ref, r0_ref, r1_ref, r2_ref, g_ref, dc_ref, da_ref, dl_ref, dg_ref):
        i = pl.program_id(0)
        for (lo, hi), src, r_ref, dst in zip(_GROUPS, (c_ref, a_ref, l_ref), (r0_ref, r1_ref, r2_ref), (dc_ref, da_ref, dl_ref)):
            r = r_ref[...]
            yhat = src[...] * r
            dy = d_ref[:, lo:hi]
            dyh = dy * g_ref[:, lo:hi]
            m = jnp.mean(dyh * yhat, axis=1, keepdims=True)
            dst[...] = r * (dyh - yhat * m)
            part = jnp.sum(dy * yhat, axis=0, keepdims=True)

            @pl.when(i == 0)
            def _():
                dg_ref[:, lo:hi] = part

            @pl.when(i > 0)
            def _():
                dg_ref[:, lo:hi] += part

    rs = _bs((tb, 1), lambda i: (i, 0))
    specs = [_bs((tb, DC), lambda i: (i, 0)), _bs((tb, DA), lambda i: (i, 0)), _bs((tb, DL), lambda i: (i, 0))]
    return pl.pallas_call(
        body, grid=(T // tb,),
        in_specs=[_bs((tb, D), lambda i: (i, 0))] + specs + [rs, rs, rs, _bs((1, D), lambda i: (0, 0))],
        out_specs=specs + [_bs((1, D), lambda i: (0, 0))],
        out_shape=[S((T, DC), F32), S((T, DA), F32), S((T, DL), F32), S((1, D), F32)],
        compiler_params=_cp(), name="gnorm_bwd")(dyn, yc, ya, yl, r0, r1, r2, gain)


HBM = pl.BlockSpec(memory_space=pltpu.HBM)
N_BIG = 6


def _place():
    x, y, c = lax.axis_index("x"), lax.axis_index("y"), lax.axis_index("c")
    return x, y, c, 2 * x + y


def _peer(x, y, j):
    return x ^ ((j + 1) >> 1), y ^ ((j + 1) & 1)


def _all_gather_weights(shards, small):
    def body(*refs):
        src = refs[:N_BIG]
        sm = refs[N_BIG]
        out = refs[N_BIG + 1:2 * N_BIG + 1]
        osm = refs[2 * N_BIG + 1]
        isend, irecv, dsend, drecv, ssend, srecv, lsem = refs[2 * N_BIG + 2:]
        x, y, c, me = _place()

        def ici(t, j, peer_chip):
            rh = src[t].shape[0] // 2
            half = pl.ds(c * rh, rh)
            px, py = _peer(x, y, j)
            return pltpu.make_async_remote_copy(
                src_ref=src[t].at[half], dst_ref=out[t].at[peer_chip, half],
                send_sem=isend.at[t, j], recv_sem=irecv.at[t, j], device_id=(px, py, c), device_id_type=MESH)

        def d2d(t, j, chip, cc):
            rh = src[t].shape[0] // 2
            half = pl.ds(cc * rh, rh)
            return pltpu.make_async_remote_copy(
                src_ref=out[t].at[chip, half], dst_ref=out[t].at[chip, half],
                send_sem=dsend.at[t, j], recv_sem=drecv.at[t, j], device_id=(x, y, 1 - c), device_id_type=MESH)

        def small_copy(j):
            px, py = _peer(x, y, j)
            return pltpu.make_async_remote_copy(
                src_ref=sm, dst_ref=osm.at[me], send_sem=ssend.at[j], recv_sem=srecv.at[j],
                device_id=(px, py, c), device_id_type=MESH)

        local = [pltpu.make_async_copy(src[t], out[t].at[me], lsem.at[t]) for t in range(N_BIG)]
        local.append(pltpu.make_async_copy(sm, osm.at[me], lsem.at[N_BIG]))
        for cp in local:
            cp.start()
        for j in range(3):
            small_copy(j).start()
        for t in range(N_BIG):
            for j in range(3):
                ici(t, j, me).start()
        for t in range(N_BIG):
            for j in range(3):
                px, py = _peer(x, y, j)
                chip = 2 * px + py
                ici(t, j, chip).wait_recv()
                d2d(t, j, chip, c).start()
        for t in range(N_BIG):
            for j in range(3):
                px, py = _peer(x, y, j)
                d2d(t, j, 2 * px + py, 1 - c).wait_recv()
        for j in range(3):
            px, py = _peer(x, y, j)
            pltpu.make_async_remote_copy(
                src_ref=sm, dst_ref=osm.at[2 * px + py], send_sem=ssend.at[j], recv_sem=srecv.at[j],
                device_id=(px, py, c), device_id_type=MESH).wait_recv()
        for t in range(N_BIG):
            for j in range(3):
                ici(t, j, me).wait_send()
                d2d(t, j, me, c).wait_send()
        for j in range(3):
            small_copy(j).wait_send()
        for cp in local:
            cp.wait()

    dma = pltpu.SemaphoreType.DMA
    outs = pl.pallas_call(
        body, in_specs=[HBM] * (N_BIG + 1), out_specs=[HBM] * (N_BIG + 1),
        out_shape=[S((4,) + s.shape, s.dtype) for s in shards] + [S((4,) + small.shape, small.dtype)],
        scratch_shapes=[dma((N_BIG, 3)), dma((N_BIG, 3)), dma((N_BIG, 3)), dma((N_BIG, 3)), dma((3,)), dma((3,)), dma((N_BIG + 1,))],
        name="ag_weights")(*shards, small)
    return outs[:N_BIG], outs[N_BIG]


def _rs_pair(grads):
    def body(*refs):
        g = refs[:N_BIG]
        out = refs[N_BIG:2 * N_BIG]
        ssem, rsem = refs[2 * N_BIG:]
        x, y, c, _ = _place()

        def cp(t):
            rh = g[t].shape[1] // 2
            return pltpu.make_async_remote_copy(
                src_ref=g[t].at[:, pl.ds((1 - c) * rh, rh), :], dst_ref=out[t],
                send_sem=ssem.at[t], recv_sem=rsem.at[t], device_id=(x, y, 1 - c), device_id_type=MESH)

        for t in range(N_BIG):
            cp(t).start()
        for t in range(N_BIG):
            cp(t).wait()

    dma = pltpu.SemaphoreType.DMA
    return pl.pallas_call(
        body, in_specs=[HBM] * N_BIG, out_specs=[HBM] * N_BIG,
        out_shape=[S((4, g.shape[1] // 2, g.shape[2]), g.dtype) for g in grads],
        scratch_shapes=[dma((N_BIG,)), dma((N_BIG,))], name="rs_pair")(*grads)


def _rs_chips(sums):
    def body(*refs):
        s = refs[:N_BIG]
        out = refs[N_BIG:2 * N_BIG]
        ssem, rsem = refs[2 * N_BIG:]
        x, y, c, _ = _place()

        def cp(t, j):
            px, py = _peer(x, y, j)
            return pltpu.make_async_remote_copy(
                src_ref=s[t].at[2 * px + py], dst_ref=out[t].at[j],
                send_sem=ssem.at[t, j], recv_sem=rsem.at[t, j], device_id=(px, py, c), device_id_type=MESH)

        for t in range(N_BIG):
            for j in range(3):
                cp(t, j).start()
        for t in range(N_BIG):
            for j in range(3):
                cp(t, j).wait()

    dma = pltpu.SemaphoreType.DMA
    return pl.pallas_call(
        body, in_specs=[HBM] * N_BIG, out_specs=[HBM] * N_BIG,
        out_shape=[S((3,) + g.shape[1:], g.dtype) for g in sums],
        scratch_shapes=[dma((N_BIG, 3)), dma((N_BIG, 3))], name="rs_chips")(*sums)


def _rs_join(halves):
    def body(*refs):
        h = refs[:N_BIG]
        out = refs[N_BIG:2 * N_BIG]
        ssem, rsem, lsem = refs[2 * N_BIG:]
        x, y, c, _ = _place()

        def cp(t):
            return pltpu.make_async_remote_copy(
                src_ref=h[t], dst_ref=out[t].at[c], send_sem=ssem.at[t], recv_sem=rsem.at[t],
                device_id=(x, y, 1 - c), device_id_type=MESH)

        def got(t):
            return pltpu.make_async_remote_copy(
                src_ref=h[t], dst_ref=out[t].at[1 - c], send_sem=ssem.at[t], recv_sem=rsem.at[t],
                device_id=(x, y, 1 - c), device_id_type=MESH)

        local = [pltpu.make_async_copy(h[t], out[t].at[c], lsem.at[t]) for t in range(N_BIG)]
        for t in range(N_BIG):
            local[t].start()
            cp(t).start()
        for t in range(N_BIG):
            got(t).wait_recv()
            cp(t).wait_send()
            local[t].wait()

    dma = pltpu.SemaphoreType.DMA
    return pl.pallas_call(
        body, in_specs=[HBM] * N_BIG, out_specs=[HBM] * N_BIG,
        out_shape=[S((2,) + h.shape, h.dtype) for h in halves],
        scratch_shapes=[dma((N_BIG,)), dma((N_BIG,)), dma((N_BIG,))], name="rs_join")(*halves)


def _all_reduce_small(pack):
    R = pack.shape[0]
    rb = _tile(R, (512, 256, 128, 8))

    def body(x_ref, all_ref, sum_ref, send_sems, recv_sems, local_sem):
        x, y, c = lax.axis_index("x"), lax.axis_index("y"), lax.axis_index("c")
        me, sibling = (x, y, c), (x, y, 1 - c)
        chips = [(1 - x, y), (x, 1 - y), (1 - x, 1 - y)]

        def rows(px, py, pc):
            return all_ref.at[pl.ds((4 * px + 2 * py + pc) * R, R), :]

        def copy(k, block, to, src=None):
            return pltpu.make_async_remote_copy(
                src_ref=rows(*block) if src is None else src, dst_ref=rows(*block),
                send_sem=send_sems.at[k], recv_sem=recv_sems.at[k], device_id=to, device_id_type=MESH)

        mine = pltpu.make_async_copy(x_ref, rows(*me), local_sem)
        mine.start()
        first = [copy(0, me, sibling, src=x_ref)]
        first += [copy(1 + j, me, (*chip, c), src=x_ref) for j, chip in enumerate(chips)]
        for cp in first:
            cp.start()
        passed = [copy(4 + j, (*chip, c), sibling) for j, chip in enumerate(chips)]
        for j, chip in enumerate(chips):
            copy(1 + j, (*chip, c), me).wait_recv()
            passed[j].start()
        copy(0, sibling, me).wait_recv()
        for j, chip in enumerate(chips):
            copy(4 + j, (*chip, 1 - c), me).wait_recv()
        for cp in first + passed:
            cp.wait_send()
        mine.wait()

        def step(b, carry):
            off = pl.multiple_of(b * rb, rb)
            acc = all_ref[pl.ds(off, rb), :]
            for k in range(1, 8):
                acc = acc + all_ref[pl.ds(pl.multiple_of(k * R + off, 8), rb), :]
            sum_ref[pl.ds(off, rb), :] = acc
            return carry

        lax.fori_loop(0, R // rb, step, 0)

    vm = pl.BlockSpec(memory_space=pltpu.VMEM)
    dma = pltpu.SemaphoreType.DMA
    _, total = pl.pallas_call(
        body, in_specs=[vm], out_specs=[vm, vm],
        out_shape=[S((8 * R, 128), F32), S((R, 128), F32)],
        scratch_shapes=[dma((7,)), dma((7,)), dma],
        compiler_params=_cp(), name="allreduce_small")(pack)
    return total


def _row_tile(rh, cc, tile_bytes=3 * 1024 * 1024 // 2):
    for t in (512, 256, 128, 64, 32, 16):
        if rh % t == 0 and t * cc * 4 <= tile_bytes:
            return t
    return 16


def _my_chip():
    return 2 * lax.axis_index("x") + lax.axis_index("y")


def _pair_sum(g, recv):
    _, r, cc = g.shape
    rh = r // 2
    tb = _row_tile(rh, cc)
    nbh = rh // tb

    def body(g_ref, r_ref, o_ref):
        o_ref[...] = (g_ref[...].astype(F32) + r_ref[...].astype(F32)).astype(BF16)

    mine = _bs((None, tb, cc), lambda k, i: (k, lax.axis_index("c") * nbh + i, 0))
    plain = _bs((None, tb, cc), lambda k, i: (k, i, 0))
    return pl.pallas_call(body, grid=(4, nbh), in_specs=[mine, plain], out_specs=plain,
                          out_shape=S((4, rh, cc), BF16), compiler_params=_cp(), name="rs_pair_sum")(g, recv)


def _owner_sum(g, recv, ici, acc, l):
    _, r, cc = g.shape
    rh = r // 2
    tb = _row_tile(rh, cc)
    nbh = rh // tb

    def body(g_ref, r_ref, i0_ref, i1_ref, i2_ref, acc_ref, o_ref):
        s = g_ref[...].astype(F32) + r_ref[...].astype(F32)
        o_ref[...] = s + i0_ref[...].astype(F32) + i1_ref[...].astype(F32) + i2_ref[...].astype(F32)

    def slot(j):
        return _bs((None, tb, cc), lambda i: (j, i, 0))

    return pl.pallas_call(
        body, grid=(nbh,),
        in_specs=[_bs((None, tb, cc), lambda i: (_my_chip(), lax.axis_index("c") * nbh + i, 0)),
                  _bs((None, tb, cc), lambda i: (_my_chip(), i, 0)),
                  slot(0), slot(1), slot(2), pl.BlockSpec(memory_space=pl.ANY)],
        out_specs=_bs((None, tb, cc), lambda i: (l, i, 0)),
        out_shape=S(acc.shape, F32), input_output_aliases={5: 0},
        compiler_params=_cp(), name="rs_owner_sum")(g, recv, ici, ici, ici, acc)


def _adam_math(w, g, m, v):
    m = B1 * m + (1.0 - B1) * g
    v = B2 * v + (1.0 - B2) * (g * g)
    m_hat = m / (1.0 - B1 ** STEP)
    v_hat = v / (1.0 - B2 ** STEP)
    delta = -LR * (m_hat / (jnp.sqrt(v_hat) + AEPS) + WD * w)
    return delta, m, v


def _adamw_big(w, gj, m, v):
    L, r, cc = w.shape
    rh = r // 2
    tb = _row_tile(rh, cc)
    nbh = rh // tb

    def body(w_ref, g_ref, m_ref, v_ref, go_ref, d_ref, mo_ref, vo_ref):
        g = g_ref[...]
        d, m, v = _adam_math(w_ref[...], g, m_ref[...], v_ref[...])
        go_ref[...] = g
        d_ref[...] = d
        mo_ref[...] = m
        vo_ref[...] = v

    full = _bs((None, tb, cc), lambda l, hf, i: (l, hf * nbh + i, 0))
    return pl.pallas_call(
        body, grid=(L, 2, nbh),
        in_specs=[full, _bs((None, None, tb, cc), lambda l, hf, i: (hf, l, i, 0)), full, full],
        out_specs=[full] * 4, out_shape=[S(w.shape, F32)] * 4, compiler_params=_cp(), name="adamw_big")(w, gj, m, v)


def _adamw_small(w, g, m, v):
    R = w.shape[0]
    tb = _tile(R, (512, 256, 128, 8))

    def body(w_ref, g_ref, m_ref, v_ref, d_ref, mo_ref, vo_ref):
        d, m, v = _adam_math(w_ref[...], g_ref[...], m_ref[...], v_ref[...])
        d_ref[...] = d
        mo_ref[...] = m
        vo_ref[...] = v

    spec = _bs((tb, 128), lambda i: (i, 0))
    return pl.pallas_call(body, grid=(R // tb,), in_specs=[spec] * 4, out_specs=[spec] * 3,
                          out_shape=[S((R, 128), F32)] * 3, compiler_params=_cp(), name="adamw_small")(w, g, m, v)


def _mix_pad(w):
    return jnp.concatenate([w[:, :4608], w[:, 4616:DIN], w[:, 4608:4616], jnp.zeros((D, PW - DIN), w.dtype)], axis=1)


def _mix_unpad(g):
    return jnp.concatenate([g[:, :4608], g[:, 5632:5640], g[:, 4608:5632]], axis=1)


def _pack(parts):
    flat = jnp.concatenate([p.reshape(-1).astype(F32) for p in parts])
    n = flat.shape[0]
    total = -(-n // (512 * 128)) * (512 * 128)
    return jnp.pad(flat, (0, total - n)).reshape(total // 128, 128)


def _unpack(pack, shapes):
    flat = pack.reshape(-1)
    out, off = [], 0
    for s in shapes:
        n = math.prod(s)
        out.append(flat[off:off + n].reshape(s))
        off += n
    return out


def _ffn_forward(x, gain, win, wout):
    h, rstd = _rms_fwd(x, gain)
    zg, zu, act = _ffn_in(h, win)
    y = _ffn_out(act, wout, x)
    return y, (x, h, rstd, zg, zu, act)


def _ffn_backward(dy, saved, gain, win, wout):
    x, h, rstd, zg, zu, act = saved
    dyb = dy.astype(BF16)
    dzg, dzu = _ffn_bwd_dz(dyb, wout, zg, zu)
    dwout = _mm_tn("ffn_bwd_dwout", act, dyb, scale=0.5, tm=512, tn=1024)
    dwin = _ffn_bwd_dwin(h, dzg, dzu)
    dh = _ffn_bwd_dh(dzg, dzu, win)
    dx, dgain = _rms_bwd(dh, x, rstd, gain, dy)
    return dx, dgain, dwin, dwout


def _mixer_forward(x, p):
    T = x.shape[0]
    h, rstd = _rms_fwd(x, p["norm_mix"])
    tm = _tile(T, (1024, 512))
    tn = 1152
    P = _mm("mix_in", h, p["wmix"],
            _bs((tm, D), lambda i, j, k: (i, 0)), _bs((D, tn), lambda i, j, k: (0, j)),
            _bs((tm, tn), lambda i, j, k: (i, j)), S((T, PW), F32), (T // tm, PW // tn, 1), NN, 1, (tm, tn))
    yc = _conv_fwd(P, p["cw"])
    cum = _fgate_fwd(P, p["fb"])
    cumt = cum[:, :NH].T
    cumq, cumk = cumt.reshape(NH, T, 1), cumt.reshape(NH, 1, T)
    ya, lse = _attn_fwd(P, cumq, cumk)
    yl, hst = _lru_fwd(P, p["lcw"], p["lvec"], p["lru_w_a"], p["lru_w_x"])
    yn, r0, r1, r2 = _gnorm_fwd(yc, ya, yl, p["mix_out_norm"])
    tk = 512
    y = _mm("mix_out", yn, p["wo"],
            _bs((tm, tk), lambda i, j, k: (i, k)), _bs((tk, 1024), lambda i, j, k: (k, j)),
            _bs((tm, 1024), lambda i, j, k: (i, j)), S((T, D), F32), (T // tm, D // 1024, D // tk), NN, D // tk, (tm, 1024),
            res=x, r_spec=_bs((tm, 1024), lambda i, j, k: (i, j)))
    return y, (x, h, rstd, P, cumq, cumk, lse, yc, ya, yl, hst, yn, r0, r1, r2)


def _mixer_backward(dy, saved, p):
    x, h, rstd, P, cumq, cumk, lse, yc, ya, yl, hst, yn, r0, r1, r2 = saved
    T = x.shape[0]
    dyb = dy.astype(BF16)
    dyn = _mm_nt_full("mix_bwd_dyn", dyb, p["wo"], 512)
    dwo = _mm_tn("mix_bwd_dwo", yn, dyb, tm=512, tn=1024)
    dyc, dya, dyl, dgn = _gnorm_bwd(dyn, yc, ya, yl, r0, r1, r2, p["mix_out_norm"])
    dcb, dcc, dcv, dcw = _conv_bwd(P, p["cw"], dyc)
    dq, dk, dv, dck, dcq = _attn_bwd(P, cumq, cumk, lse, ya, dya)
    dcum = jnp.pad((dck.reshape(NH, T) + dcq.reshape(NH, T)).T, ((0, 0), (0, 128 - NH)))
    df, dfb = _fgate_bwd(P, p["fb"], dcum)
    dgate, dlx, lsm, dwa, dwx = _lru_bwd(P, p["lcw"], p["lvec"], p["lru_w_a"], p["lru_w_x"], hst, dyl)
    dP = jnp.concatenate([dcb, dcc, dcv, dq, dk, dv, dgate, dlx, df], axis=1).astype(BF16)
    tm = _tile(T, (1024, 512))
    tk = 1152
    nk = PW // tk
    dh = _mm("mix_bwd_dh", dP, p["wmix"],
             _bs((tm, tk), lambda i, j, k: (i, k)), _bs((1024, tk), lambda i, j, k: (j, k)),
             _bs((tm, 1024), lambda i, j, k: (i, j)), S((T, D), F32), (T // tm, D // 1024, nk), NT, nk, (tm, 1024))
    dwmix = _mm_tn("mix_bwd_dwmix", h, dP, tm=1024, tn=1152)
    dx, dgm = _rms_bwd(dh, x, rstd, p["norm_mix"], dy)
    small = dict(norm_mix=dgm[0], mix_out_norm=dgn[0], conv_w=dcw[:3], fgate_b=dfb[0, :NH], lru_conv_w=lsm[:4],
                 lru_conv_b=lsm[4], lru_b_a=lsm[5], lru_b_x=lsm[6], lru_lambda=lsm[7], lru_w_a=dwa, lru_w_x=dwx)
    return dx, small, dwmix, dwo


BIG = ("ffn1_w_in", "ffn1_w_out", "mix_w_in", "mix_w_out", "ffn2_w_in", "ffn2_w_out")
SMALL = ("norm_ffn1", "norm_mix", "conv_w", "fgate_b", "lru_conv_w", "lru_conv_b", "lru_w_a", "lru_b_a", "lru_w_x",
         "lru_b_x", "lru_lambda", "mix_out_norm", "norm_ffn2", "final_norm")
WEIGHTS = ("norm_ffn1", "ffn1_w_in", "ffn1_w_out", "norm_mix", "mix_w_in", "conv_w", "fgate_b", "lru_conv_w", "lru_conv_b",
           "lru_w_a", "lru_b_a", "lru_w_x", "lru_b_x", "lru_lambda", "mix_out_norm", "mix_w_out", "norm_ffn2", "ffn2_w_in",
           "ffn2_w_out", "final_norm")


def _step(args):
    xx, yy, cc_ = lax.axis_index("x"), lax.axis_index("y"), lax.axis_index("c")
    me = 2 * xx + yy
    x0 = args["x"][0]
    tgt = args["loss_target"][0]
    T = x0.shape[0]
    L = args["norm_ffn1"].shape[0]

    layers = []
    for l in range(L):
        shards = [args[n][l].astype(BF16) for n in BIG]
        small = jnp.concatenate([args["conv_w"][l], args["lru_conv_w"][l], jnp.zeros((1, 128), F32)], axis=0)
        gat, gsm = _all_gather_weights(shards, small)
        w1i, w1o, wmx, wo, w2i, w2o = gat
        cwl = gsm.transpose(1, 0, 2).reshape(8, 4 * 128)
        p = dict(
            w1i=w1i, w1o=w1o.reshape(-1, D), w2i=w2i, w2o=w2o.reshape(-1, D), wo=wo.reshape(D, D),
            wmix=_mix_pad(wmx.transpose(1, 0, 2).reshape(D, DIN)),
            cw=jnp.concatenate([cwl[:3], jnp.zeros((5, DC), F32)], axis=0),
            lcw=jnp.concatenate([cwl[3:7], jnp.zeros((4, DL), F32)], axis=0),
            fb=jnp.pad(args["fgate_b"][l], (0, 128 - NH)).reshape(1, 128),
            lvec=jnp.concatenate([args["lru_conv_b"][l][None], args["lru_b_a"][l][None], args["lru_b_x"][l][None],
                                  args["lru_lambda"][l][None], jnp.zeros((4, DL), F32)], axis=0),
            lru_w_a=args["lru_w_a"][l], lru_w_x=args["lru_w_x"][l],
            norm_ffn1=args["norm_ffn1"][l][None], norm_mix=args["norm_mix"][l][None],
            mix_out_norm=args["mix_out_norm"][l][None], norm_ffn2=args["norm_ffn2"][l][None])
        layers.append(p)

    xs = x0
    saved = []
    for p in layers:
        x1, s1 = _ffn_forward(xs, p["norm_ffn1"], p["w1i"], p["w1o"])
        x2, s2 = _mixer_forward(x1, p)
        x3, s3 = _ffn_forward(x2, p["norm_ffn2"], p["w2i"], p["w2o"])
        saved.append((s1, s2, s3))
        xs = x3
    lpart, dx, dfinal = _loss_head(xs, args["final_norm"][None], tgt)
    loss = lax.psum(lpart[0, 0], ("x", "y", "c"))

    acc = None
    small_grads = [None] * L
    for l in reversed(range(L)):
        p = layers[l]
        s1, s2, s3 = saved[l]
        dx, dg2, dw2i, dw2o = _ffn_backward(dx, s3, p["norm_ffn2"], p["w2i"], p["w2o"])
        dx, sm, dwmix, dwo = _mixer_backward(dx, s2, p)
        dx, dg1, dw1i, dw1o = _ffn_backward(dx, s1, p["norm_ffn1"], p["w1i"], p["w1o"])
        sm["norm_ffn1"] = dg1[0]
        sm["norm_ffn2"] = dg2[0]
        small_grads[l] = sm
        F4 = dw1o.shape[0] // 4
        grads = [dw1i, dw1o.reshape(4, F4, D), _mix_unpad(dwmix).reshape(D, 4, DIN // 4).transpose(1, 0, 2),
                 dwo.reshape(4, D // 4, D), dw2i, dw2o.reshape(4, F4, D)]
        recv = _rs_pair(grads)
        sums = [_pair_sum(g, r) for g, r in zip(grads, recv)]
        ici = _rs_chips(sums)
        if acc is None:
            acc = [jnp.zeros((L, g.shape[1] // 2, g.shape[2]), F32) for g in grads]
        acc = [_owner_sum(g, r, i3, a, l) for g, r, i3, a in zip(grads, recv, ici, acc)]
    joined = _rs_join(acc)

    out = {"loss": loss, "grad_x": dx[None]}
    for n, gj in zip(BIG, joined):
        g, d, m, v = _adamw_big(args[n], gj, args["m_" + n], args["v_" + n])
        out["grad_" + n], out["delta_" + n], out["new_m_" + n], out["new_v_" + n] = g, d, m, v

    full = {n: (dfinal[0] if n == "final_norm" else jnp.stack([small_grads[l][n] for l in range(L)])) for n in SMALL}
    shapes = [full[n].shape for n in SMALL]
    red = dict(zip(SMALL, _unpack(_all_reduce_small(_pack([full[n] for n in SMALL])), shapes)))
    for n in ("conv_w", "lru_conv_w"):
        red[n] = lax.dynamic_slice_in_dim(red[n], me * 128, 128, axis=2)
    oshapes = [args[n].shape for n in SMALL]
    d, m, v = _adamw_small(_pack([args[n] for n in SMALL]), _pack([red[n] for n in SMALL]),
                           _pack([args["m_" + n] for n in SMALL]), _pack([args["v_" + n] for n in SMALL]))
    for n, gg, dd, mm, vv in zip(SMALL, [red[n] for n in SMALL], _unpack(d, oshapes), _unpack(m, oshapes), _unpack(v, oshapes)):
        out["grad_" + n], out["delta_" + n], out["new_m_" + n], out["new_v_" + n] = gg, dd, mm, vv
    return out


def kernel(x, norm_ffn1, ffn1_w_in, ffn1_w_out, norm_mix, mix_w_in, conv_w, fgate_b, lru_conv_w, lru_conv_b, lru_w_a, lru_b_a, lru_w_x, lru_b_x, lru_lambda, mix_out_norm, mix_w_out, norm_ffn2, ffn2_w_in, ffn2_w_out, final_norm, loss_target, m_norm_ffn1, m_ffn1_w_in, m_ffn1_w_out, m_norm_mix, m_mix_w_in, m_conv_w, m_fgate_b, m_lru_conv_w, m_lru_conv_b, m_lru_w_a, m_lru_b_a, m_lru_w_x, m_lru_b_x, m_lru_lambda, m_mix_out_norm, m_mix_w_out, m_norm_ffn2, m_ffn2_w_in, m_ffn2_w_out, m_final_norm, v_norm_ffn1, v_ffn1_w_in, v_ffn1_w_out, v_norm_mix, v_mix_w_in, v_conv_w, v_fgate_b, v_lru_conv_w, v_lru_conv_b, v_lru_w_a, v_lru_b_a, v_lru_w_x, v_lru_b_x, v_lru_lambda, v_mix_out_norm, v_mix_w_out, v_norm_ffn2, v_ffn2_w_in, v_ffn2_w_out, v_final_norm):
    args = dict(locals())
    out = _step(args)
    res = [out["loss"], out["grad_x"]]
    for prefix in ("grad_", "delta_", "new_m_", "new_v_"):
        res += [out[prefix + n] for n in WEIGHTS]
    return tuple(res)
```

```python
import functools
import math

import jax
import jax.numpy as jnp
from jax import lax
from jax.experimental import pallas as pl
from jax.experimental.pallas import tpu as pltpu

F32 = jnp.float32
BF16 = jnp.bfloat16
S = jax.ShapeDtypeStruct
MESH = pl.DeviceIdType.MESH

D = 2048
DC = 512
DA = 1024
NH = 8
HD = 128
DL = 512
LB = 128
DIN = 5640
PW = 5760
C_Q, C_K, C_V = 12, 20, 28
C_GATE, C_LX, C_F = 36, 40, 44
EPS = 1e-6
LRU_C = 8.0
ATT_SCALE = HD ** -0.5
LR, B1, B2, AEPS, WD, STEP = 0.001, 0.9, 0.999, 1e-08, 0.01, 10
VMEM_LIMIT = 56 * 1024 * 1024

NT = (((1,), (1,)), ((), ()))
TN = (((0,), (0,)), ((), ()))
NN = (((1,), (0,)), ((), ()))


def _cp():
    return pltpu.CompilerParams(vmem_limit_bytes=VMEM_LIMIT)


def _bs(shape, fn):
    return pl.BlockSpec(shape, fn)


def _mm(name, a, b, a_spec, b_spec, o_spec, o_shape, grid, dims, nk, acc_tile, scale=1.0, res=None, r_spec=None):
    has_res = res is not None

    def body(*refs):
        if has_res:
            a_ref, b_ref, r_ref, o_ref = refs[:4]
            rest = refs[4:]
        else:
            a_ref, b_ref, o_ref = refs[:3]
            rest = refs[3:]
        prod = lax.dot_general(a_ref[...].astype(BF16), b_ref[...].astype(BF16), dims, preferred_element_type=F32)

        def finish(acc):
            if scale != 1.0:
                acc = acc * scale
            if has_res:
                acc = r_ref[...] + acc
            o_ref[...] = acc.astype(o_ref.dtype)

        if nk == 1:
            finish(prod)
        else:
            acc_ref = rest[0]
            k = pl.program_id(2)

            @pl.when(k == 0)
            def _():
                acc_ref[...] = prod

            @pl.when(k > 0)
            def _():
                acc_ref[...] += prod

            @pl.when(k == nk - 1)
            def _():
                finish(acc_ref[...])

    in_specs = [a_spec, b_spec] + ([r_spec] if has_res else [])
    args = (a, b) + ((res,) if has_res else ())
    scratch = [pltpu.VMEM(acc_tile, F32)] if nk > 1 else []
    return pl.pallas_call(body, grid=grid, in_specs=in_specs, out_specs=o_spec, out_shape=o_shape,
                          scratch_shapes=scratch, compiler_params=_cp(), name=name)(*args)


def _tile(n, pref):
    for t in pref:
        if n % t == 0:
            return t
    return n


def _rms_fwd(x, gain):
    T = x.shape[0]
    tb = _tile(T, (512,))

    def body(x_ref, g_ref, h_ref, r_ref):
        xv = x_ref[...]
        r = lax.rsqrt(jnp.mean(xv * xv, axis=1, keepdims=True) + EPS)
        h_ref[...] = (xv * r * g_ref[...]).astype(BF16)
        r_ref[...] = r

    return pl.pallas_call(
        body, grid=(T // tb,),
        in_specs=[_bs((tb, D), lambda i: (i, 0)), _bs((1, D), lambda i: (0, 0))],
        out_specs=[_bs((tb, D), lambda i: (i, 0)), _bs((tb, 1), lambda i: (i, 0))],
        out_shape=[S((T, D), BF16), S((T, 1), F32)], compiler_params=_cp(), name="rms_fwd")(x, gain)


def _rms_bwd(dh, x, rstd, gain, dres):
    T = x.shape[0]
    tb = _tile(T, (512,))

    def body(dh_ref, x_ref, r_ref, g_ref, dres_ref, dx_ref, dg_ref):
        i = pl.program_id(0)
        r = r_ref[...]
        xhat = x_ref[...] * r
        dh = dh_ref[...]
        dxh = dh * g_ref[...]
        m = jnp.mean(dxh * xhat, axis=1, keepdims=True)
        dx_ref[...] = dres_ref[...] + r * (dxh - xhat * m)
        part = jnp.sum(dh * xhat, axis=0, keepdims=True)

        @pl.when(i == 0)
        def _():
            dg_ref[...] = part

        @pl.when(i > 0)
        def _():
            dg_ref[...] += part

    row = _bs((tb, D), lambda i: (i, 0))
    return pl.pallas_call(
        body, grid=(T // tb,),
        in_specs=[row, row, _bs((tb, 1), lambda i: (i, 0)), _bs((1, D), lambda i: (0, 0)), row],
        out_specs=[row, _bs((1, D), lambda i: (0, 0))],
        out_shape=[S((T, D), F32), S((1, D), F32)], compiler_params=_cp(), name="rms_bwd")(dh, x, rstd, gain, dres)


def _loss_head(x, gain, tgt):
    T = x.shape[0]
    tb = _tile(T, (512,))

    def body(x_ref, g_ref, t_ref, l_ref, dx_ref, dg_ref):
        i = pl.program_id(0)
        xv = x_ref[...]
        g = g_ref[...]
        r = lax.rsqrt(jnp.mean(xv * xv, axis=1, keepdims=True) + EPS)
        xhat = xv * r
        e = xhat * g - t_ref[...]
        lpart = 0.5 * jnp.sum(jnp.sum(e * e, axis=1, keepdims=True), axis=0, keepdims=True) * (1.0 / D)
        dy = e * (1.0 / D)
        dxh = dy * g
        m = jnp.mean(dxh * xhat, axis=1, keepdims=True)
        dx_ref[...] = r * (dxh - xhat * m)
        gpart = jnp.sum(dy * xhat, axis=0, keepdims=True)
        lrow = jnp.broadcast_to(lpart, (1, 128))

        @pl.when(i == 0)
        def _():
            dg_ref[...] = gpart
            l_ref[...] = lrow

        @pl.when(i > 0)
        def _():
            dg_ref[...] += gpart
            l_ref[...] += lrow

    row = _bs((tb, D), lambda i: (i, 0))
    return pl.pallas_call(
        body, grid=(T // tb,),
        in_specs=[row, _bs((1, D), lambda i: (0, 0)), row],
        out_specs=[_bs((1, 128), lambda i: (0, 0)), row, _bs((1, D), lambda i: (0, 0))],
        out_shape=[S((1, 128), F32), S((T, D), F32), S((1, D), F32)], compiler_params=_cp(), name="loss_head")(x, gain, tgt)


def _ffn_in(h, win):
    T = h.shape[0]
    Fs = win.shape[2]
    F = 2 * Fs
    tn = _tile(Fs, (256, 128))
    nb = Fs // tn
    tm = _tile(T, (1024, 512))

    def body(h_ref, wg_ref, wu_ref, zg_ref, zu_ref, a_ref):
        hv = h_ref[...]
        zg = jnp.dot(hv, wg_ref[...], preferred_element_type=F32)
        zu = jnp.dot(hv, wu_ref[...], preferred_element_type=F32)
        zg_ref[...] = zg.astype(BF16)
        zu_ref[...] = zu.astype(BF16)
        a_ref[...] = (zg * jax.nn.sigmoid(zg) * zu).astype(BF16)

    col = _bs((tm, tn), lambda i, j: (i, j))
    return pl.pallas_call(
        body, grid=(T // tm, F // tn),
        in_specs=[_bs((tm, D), lambda i, j: (i, 0)),
                  _bs((None, D, tn), lambda i, j: (j // nb, 0, j % nb)),
                  _bs((None, D, tn), lambda i, j: (2 + j // nb, 0, j % nb))],
        out_specs=[col, col, col],
        out_shape=[S((T, F), BF16)] * 3, compiler_params=_cp(), name="ffn_in")(h, win, win)


def _ffn_out(act, wout, x):
    T, F = act.shape
    tm = _tile(T, (1024, 512))
    tn = 1024
    tk = _tile(F, (1408, 512, 256))
    nk = F // tk
    return _mm("ffn_out", act, wout,
               _bs((tm, tk), lambda i, j, k: (i, k)), _bs((tk, tn), lambda i, j, k: (k, j)),
               _bs((tm, tn), lambda i, j, k: (i, j)), S((T, D), F32), (T // tm, D // tn, nk), NN, nk, (tm, tn),
               scale=0.5, res=x, r_spec=_bs((tm, tn), lambda i, j, k: (i, j)))


def _ffn_bwd_dz(dyb, wout, zg, zu, dep):
    T, F = zg.shape
    tm = _tile(T, (1024, 512))
    tn = _tile(F, (512, 256))

    def body(dy_ref, w_ref, zg_ref, zu_ref, dep_ref, dzg_ref, dzu_ref):
        da = 0.5 * lax.dot_general(dy_ref[...], w_ref[...], NT, preferred_element_type=F32)
        zg = zg_ref[...].astype(F32)
        zu = zu_ref[...].astype(F32)
        s = jax.nn.sigmoid(zg)
        dzg_ref[...] = (da * zu * (s * (1.0 + zg * (1.0 - s)))).astype(BF16)
        dzu_ref[...] = (da * (zg * s)).astype(BF16)

    col = _bs((tm, tn), lambda i, j: (i, j))
    nj = F // tn
    dzg, dzu = pl.pallas_call(
        body, grid=(T // tm, nj),
        in_specs=[_bs((tm, D), lambda i, j: (i, 0)), _bs((tn, D), lambda i, j: (j, 0)), col, col,
                  pl.BlockSpec(memory_space=pl.ANY)],
        out_specs=[col, col], out_shape=[S((T, F), BF16)] * 2, compiler_params=_cp(), name="ffn_bwd_dz")(dyb, wout, zg, zu, dep)
    return dzg, dzu


def _ffn_bwd_dh(dzg, dzu, win):
    T, F = dzg.shape
    Fs = win.shape[2]
    tk = _tile(Fs, (1408, 256, 128))
    nkb = Fs // tk
    tm = _tile(T, (1024, 512))
    tn = 1024
    nk = 2 * nkb

    def body(dzg_ref, dzu_ref, wg_ref, wu_ref, o_ref, acc_ref):
        k = pl.program_id(2)
        prod = (lax.dot_general(dzg_ref[...], wg_ref[...], NT, preferred_element_type=F32)
                + lax.dot_general(dzu_ref[...], wu_ref[...], NT, preferred_element_type=F32))

        @pl.when(k == 0)
        def _():
            acc_ref[...] = prod

        @pl.when(k > 0)
        def _():
            acc_ref[...] += prod

        @pl.when(k == nk - 1)
        def _():
            o_ref[...] = acc_ref[...]

    a_spec = _bs((tm, tk), lambda i, j, k: (i, k))
    return pl.pallas_call(
        body, grid=(T // tm, D // tn, nk),
        in_specs=[a_spec, a_spec,
                  _bs((None, tn, tk), lambda i, j, k: (k // nkb, j, k % nkb)),
                  _bs((None, tn, tk), lambda i, j, k: (2 + k // nkb, j, k % nkb))],
        out_specs=_bs((tm, tn), lambda i, j, k: (i, j)), out_shape=S((T, D), F32),
        scratch_shapes=[pltpu.VMEM((tm, tn), F32)], compiler_params=_cp(), name="ffn_bwd_dh")(dzg, dzu, win, win)


def _ffn_bwd_dwin(h, dzg, dzu):
    T, F = dzg.shape
    Fs = F // 2
    tn = _tile(Fs, (1408, 256, 128))
    nb = Fs // tn
    tm = 1024
    tk = _tile(T, (1024, 512))
    nk = T // tk

    ng = 2 * nb

    def body(h_ref, zg_ref, zu_ref, o_ref, acc_ref):
        j, k = pl.program_id(1), pl.program_id(2)

        def accumulate(dz_ref):
            prod = lax.dot_general(h_ref[...], dz_ref[...], TN, preferred_element_type=F32)

            @pl.when(k == 0)
            def _():
                acc_ref[...] = prod

            @pl.when(k > 0)
            def _():
                acc_ref[...] += prod

        @pl.when(j < ng)
        def _():
            accumulate(zg_ref)

        @pl.when(j >= ng)
        def _():
            accumulate(zu_ref)

        @pl.when(k == nk - 1)
        def _():
            o_ref[...] = acc_ref[...].astype(BF16)

    return pl.pallas_call(
        body, grid=(D // tm, 2 * ng, nk),
        in_specs=[_bs((tk, tm), lambda i, j, k: (k, i)),
                  _bs((tk, tn), lambda i, j, k: (jnp.where(j < ng, k, nk - 1), jnp.minimum(j, ng - 1))),
                  _bs((tk, tn), lambda i, j, k: (jnp.where(j >= ng, k, 0), jnp.maximum(j - ng, 0)))],
        out_specs=_bs((None, tm, tn), lambda i, j, k: (j // nb, i, j % nb)), out_shape=S((4, D, Fs), BF16),
        scratch_shapes=[pltpu.VMEM((tm, tn), F32)], compiler_params=_cp(), name="ffn_bwd_dwin")(h, dzg, dzu)


def _mm_tn(name, a, b, scale=1.0, tm=512, tn=1024):
    T, M = a.shape
    N = b.shape[1]
    tm = _tile(M, (tm, 512, 256, 128))
    tn = _tile(N, (tn, 1152, 1024, 512, 128))
    tk = _tile(T, (1024, 512))
    nk = T // tk
    return _mm(name, a, b,
               _bs((tk, tm), lambda i, j, k: (k, i)), _bs((tk, tn), lambda i, j, k: (k, j)),
               _bs((tm, tn), lambda i, j, k: (i, j)), S((M, N), BF16), (M // tm, N // tn, nk), TN, nk, (tm, tn), scale=scale)


def _mm_nt_full(name, a, b, tn):
    T, K = a.shape
    N = b.shape[0]
    tm = _tile(T, (1024, 512))
    return _mm(name, a, b,
               _bs((tm, K), lambda i, j, k: (i, 0)), _bs((tn, K), lambda i, j, k: (j, 0)),
               _bs((tm, tn), lambda i, j, k: (i, j)), S((T, N), F32), (T // tm, N // tn, 1), NT, 1, (tm, tn))


def _bt(T):
    return _tile(T, (512,))


def _down(ext, s):
    return pltpu.roll(ext, s, 0)[8:, :]


def _up(ext, s):
    n = ext.shape[0]
    return pltpu.roll(ext, n - s, 0)[: n - 8, :]


def _halo_prev(ref, start, b):
    lo = pl.multiple_of(jnp.maximum(start - 8, 0), 8)
    return ref[pl.ds(lo, 8), :] * (b > 0).astype(F32)


def _halo_next(ref, start, bt, b, nb):
    lo = pl.multiple_of(jnp.minimum(start + bt, (nb - 1) * bt), 8)
    return ref[pl.ds(lo, 8), :] * (b < nb - 1).astype(F32)


def _scan_fwd(A, U):
    n = U.shape[0]
    row = lax.broadcasted_iota(jnp.int32, U.shape, 0)
    d = 1
    while d < n:
        keep = row >= d
        Us = jnp.where(keep, pltpu.roll(U, d, 0), 0.0)
        if A is None:
            U = U + Us
        else:
            As = jnp.where(keep, pltpu.roll(A, d, 0), 1.0)
            U = A * Us + U
            A = A * As
        d *= 2
    return A, U


def _scan_bwd(A, U):
    n = U.shape[0]
    row = lax.broadcasted_iota(jnp.int32, U.shape, 0)
    d = 1
    while d < n:
        keep = row < n - d
        Us = jnp.where(keep, pltpu.roll(U, n - d, 0), 0.0)
        if A is None:
            U = U + Us
        else:
            As = jnp.where(keep, pltpu.roll(A, n - d, 0), 1.0)
            U = A * Us + U
            A = A * As
        d *= 2
    return A, U


def _softplus(z):
    return jnp.maximum(z, 0.0) + jnp.log(1.0 + jnp.exp(-jnp.abs(z)))


def _gelu_parts(g):
    k0 = math.sqrt(2.0 / math.pi)
    t = jnp.tanh(k0 * (g + 0.044715 * g * g * g))
    gel = 0.5 * g * (1.0 + t)
    dgel = 0.5 * (1.0 + t) + 0.5 * g * (1.0 - t * t) * k0 * (1.0 + 3.0 * 0.044715 * g * g)
    return gel, dgel


def _conv_fwd(P, cw):
    T = P.shape[0]
    bt = _bt(T)
    nb = T // bt

    def body(b_ref, c_ref, v_ref, w_ref, y_ref):
        w = w_ref[...]

        def step(b, carry):
            start = pl.multiple_of(b * bt, bt)
            rows = pl.ds(start, bt)
            m = c_ref[rows, :] * v_ref[rows, :]
            ext = jnp.concatenate([_halo_prev(c_ref, start, b) * _halo_prev(v_ref, start, b), m], axis=0)
            z = w[2:3, :] * m + w[1:2, :] * _down(ext, 1) + w[0:1, :] * _down(ext, 2)
            y_ref[rows, :] = b_ref[rows, :] * z
            return carry

        lax.fori_loop(0, nb, step, 0)

    def colspec(off):
        return _bs((T, 128), lambda c: (0, off + c))

    return pl.pallas_call(
        body, grid=(DC // 128,),
        in_specs=[colspec(0), colspec(4), colspec(8), _bs((8, 128), lambda c: (0, c))],
        out_specs=_bs((T, 128), lambda c: (0, c)), out_shape=S((T, DC), F32),
        compiler_params=_cp(), name="conv_fwd")(P, P, P, cw)


def _conv_bwd(P, cw, dy):
    T = P.shape[0]
    bt = _bt(T)
    nb = T // bt

    def body(b_ref, c_ref, v_ref, w_ref, dy_ref, db_ref, dc_ref, dv_ref, dw_ref):
        w = w_ref[...]

        def step(b, carry):
            a0, a1, a2 = carry
            start = pl.multiple_of(b * bt, bt)
            rows = pl.ds(start, bt)
            cb, cc, cv, dy = b_ref[rows, :], c_ref[rows, :], v_ref[rows, :], dy_ref[rows, :]
            m = cc * cv
            ext = jnp.concatenate([_halo_prev(c_ref, start, b) * _halo_prev(v_ref, start, b), m], axis=0)
            m1, m2 = _down(ext, 1), _down(ext, 2)
            z = w[2:3, :] * m + w[1:2, :] * m1 + w[0:1, :] * m2
            db_ref[rows, :] = dy * z
            dz = dy * cb
            extn = jnp.concatenate([dz, _halo_next(dy_ref, start, bt, b, nb) * _halo_next(b_ref, start, bt, b, nb)], axis=0)
            dm = w[2:3, :] * dz + w[1:2, :] * _up(extn, 1) + w[0:1, :] * _up(extn, 2)
            dc_ref[rows, :] = dm * cv
            dv_ref[rows, :] = dm * cc
            return (a0 + jnp.sum(dz * m2, axis=0, keepdims=True),
                    a1 + jnp.sum(dz * m1, axis=0, keepdims=True),
                    a2 + jnp.sum(dz * m, axis=0, keepdims=True))

        zero = jnp.zeros((1, 128), F32)
        a0, a1, a2 = lax.fori_loop(0, nb, step, (zero, zero, zero))
        dw_ref[...] = jnp.zeros((8, 128), F32)
        dw_ref[0:1, :] = a0
        dw_ref[1:2, :] = a1
        dw_ref[2:3, :] = a2

    def colspec(off):
        return _bs((T, 128), lambda c: (0, off + c))

    own = _bs((T, 128), lambda c: (0, c))
    return pl.pallas_call(
        body, grid=(DC // 128,),
        in_specs=[colspec(0), colspec(4), colspec(8), _bs((8, 128), lambda c: (0, c)), own],
        out_specs=[own, own, own, _bs((8, 128), lambda c: (0, c))],
        out_shape=[S((T, DC), F32)] * 3 + [S((8, DC), F32)], compiler_params=_cp(), name="conv_bwd")(P, P, P, cw, dy)


def _fgate_fwd(P, fb):
    T = P.shape[0]
    bt = _bt(T)
    nb = T // bt

    def body(f_ref, b_ref, c_ref):
        bias = b_ref[...]

        def step(b, carry):
            rows = pl.ds(pl.multiple_of(b * bt, bt), bt)
            logf = -_softplus(-(f_ref[rows, :] + bias))
            _, cs = _scan_fwd(None, logf)
            cs = cs + carry
            c_ref[rows, :] = cs
            return cs[bt - 1:bt, :]

        lax.fori_loop(0, nb, step, jnp.zeros((1, 128), F32))

    return pl.pallas_call(
        body, grid=(1,),
        in_specs=[_bs((T, 128), lambda i: (0, C_F)), _bs((1, 128), lambda i: (0, 0))],
        out_specs=_bs((T, 128), lambda i: (0, 0)), out_shape=S((T, 128), F32),
        compiler_params=_cp(), name="fgate_fwd")(P, fb)


def _fgate_bwd(P, fb, dcum):
    T = P.shape[0]
    bt = _bt(T)
    nb = T // bt

    def body(f_ref, b_ref, dc_ref, df_ref, db_ref):
        bias = b_ref[...]

        def step(i, carry):
            run, acc = carry
            b = nb - 1 - i
            rows = pl.ds(pl.multiple_of(b * bt, bt), bt)
            _, rs = _scan_bwd(None, dc_ref[rows, :])
            rs = rs + run
            df = rs * jax.nn.sigmoid(-(f_ref[rows, :] + bias))
            df_ref[rows, :] = df
            return rs[0:1, :], acc + jnp.sum(df, axis=0, keepdims=True)

        zero = jnp.zeros((1, 128), F32)
        _, acc = lax.fori_loop(0, nb, step, (zero, zero))
        db_ref[...] = acc

    return pl.pallas_call(
        body, grid=(1,),
        in_specs=[_bs((T, 128), lambda i: (0, C_F)), _bs((1, 128), lambda i: (0, 0)), _bs((T, 128), lambda i: (0, 0))],
        out_specs=[_bs((T, 128), lambda i: (0, 0)), _bs((1, 128), lambda i: (0, 0))],
        out_shape=[S((T, 128), F32), S((1, 128), F32)], compiler_params=_cp(), name="fgate_bwd")(P, fb, dcum)


def _att_tile(T):
    return _tile(T, (512,))


def _attn_fwd(P, cumq, cumk):
    T = P.shape[0]
    tq = _att_tile(T)
    nq = T // tq

    def body(q_ref, k_ref, v_ref, cq_ref, ck_ref, o_ref, lse_ref, m_s, l_s, acc_s):
        i, j = pl.program_id(1), pl.program_id(2)

        @pl.when(j == 0)
        def _():
            m_s[...] = jnp.full((tq, 1), -jnp.inf, F32)
            l_s[...] = jnp.zeros((tq, 1), F32)
            acc_s[...] = jnp.zeros((tq, HD), F32)

        @pl.when(j <= i)
        def _():
            s = lax.dot_general(q_ref[...].astype(BF16), k_ref[...].astype(BF16), NT, preferred_element_type=F32)
            s = s * ATT_SCALE + cq_ref[...] - ck_ref[...]
            rowp = i * tq + lax.broadcasted_iota(jnp.int32, (tq, tq), 0)
            colp = j * tq + lax.broadcasted_iota(jnp.int32, (tq, tq), 1)
            s = jnp.where(colp <= rowp, s, -jnp.inf)
            m_old = m_s[...]
            m_new = jnp.maximum(m_old, jnp.max(s, axis=1, keepdims=True))
            p = jnp.exp(s - m_new)
            alpha = jnp.exp(m_old - m_new)
            l_s[...] = alpha * l_s[...] + jnp.sum(p, axis=1, keepdims=True)
            acc_s[...] = alpha * acc_s[...] + jnp.dot(p.astype(BF16), v_ref[...].astype(BF16), preferred_element_type=F32)
            m_s[...] = m_new

        @pl.when(j == nq - 1)
        def _():
            o_ref[...] = acc_s[...] / l_s[...]
            lse_ref[...] = m_s[...] + jnp.log(l_s[...])

    return pl.pallas_call(
        body, grid=(NH, nq, nq),
        in_specs=[_bs((tq, HD), lambda h, i, j: (i, C_Q + h)),
                  _bs((tq, HD), lambda h, i, j: (jnp.minimum(j, i), C_K + h)),
                  _bs((tq, HD), lambda h, i, j: (jnp.minimum(j, i), C_V + h)),
                  _bs((None, tq, 1), lambda h, i, j: (h, i, 0)),
                  _bs((None, 1, tq), lambda h, i, j: (h, 0, jnp.minimum(j, i)))],
        out_specs=[_bs((tq, HD), lambda h, i, j: (i, h)), _bs((None, tq, 1), lambda h, i, j: (h, i, 0))],
        out_shape=[S((T, DA), F32), S((NH, T, 1), F32)],
        scratch_shapes=[pltpu.VMEM((tq, 1), F32), pltpu.VMEM((tq, 1), F32), pltpu.VMEM((tq, HD), F32)],
        compiler_params=_cp(), name="attn_fwd")(P, P, P, cumq, cumk)


def _attn_bwd(P, cumq, cumk, lse, o, do):
    T = P.shape[0]
    tq = _att_tile(T)
    nq = T // tq

    def body(q_ref, k_ref, v_ref, cq_ref, ck_ref, lse_ref, o_ref, do_ref,
             dq_ref, dk_ref, dv_ref, dc_ref, dr_ref, dk_s, dv_s, dc_s):
        j, i = pl.program_id(1), pl.program_id(2)

        @pl.when((j == 0) & (i == 0))
        def _():
            dq_ref[...] = jnp.zeros((T, HD), F32)
            dr_ref[...] = jnp.zeros((T, 1), F32)

        @pl.when(i == 0)
        def _():
            dk_s[...] = jnp.zeros((tq, HD), F32)
            dv_s[...] = jnp.zeros((tq, HD), F32)
            dc_s[...] = jnp.zeros((1, tq), F32)

        @pl.when(i >= j)
        def _():
            q = q_ref[...].astype(BF16)
            k = k_ref[...].astype(BF16)
            v = v_ref[...].astype(BF16)
            do_f = do_ref[...]
            dob = do_f.astype(BF16)
            s = lax.dot_general(q, k, NT, preferred_element_type=F32)
            s = s * ATT_SCALE + cq_ref[...] - ck_ref[...]
            rowp = i * tq + lax.broadcasted_iota(jnp.int32, (tq, tq), 0)
            colp = j * tq + lax.broadcasted_iota(jnp.int32, (tq, tq), 1)
            p = jnp.where(colp <= rowp, jnp.exp(s - lse_ref[...]), 0.0)
            delta = jnp.sum(do_f * o_ref[...], axis=1, keepdims=True)
            dp = lax.dot_general(dob, v, NT, preferred_element_type=F32)
            ds = p * (dp - delta)
            dsb = (ds * ATT_SCALE).astype(BF16)
            dv_s[...] += lax.dot_general(p.astype(BF16), dob, TN, preferred_element_type=F32)
            dk_s[...] += lax.dot_general(dsb, q, TN, preferred_element_type=F32)
            rows = pl.ds(pl.multiple_of(i * tq, tq), tq)
            dq_ref[rows, :] += jnp.dot(dsb, k, preferred_element_type=F32)
            dc_s[...] -= jnp.sum(ds, axis=0, keepdims=True)
            dr_ref[rows, :] += jnp.sum(ds, axis=1, keepdims=True)

        @pl.when(i == nq - 1)
        def _():
            dk_ref[...] = dk_s[...]
            dv_ref[...] = dv_s[...]
            dc_ref[...] = dc_s[...]

    def qside(col):
        return _bs((tq, HD), lambda h, j, i: (jnp.maximum(i, j), col + h))

    qvec = _bs((None, tq, 1), lambda h, j, i: (h, jnp.maximum(i, j), 0))
    kv_out = _bs((tq, HD), lambda h, j, i: (j, h))
    return pl.pallas_call(
        body, grid=(NH, nq, nq),
        in_specs=[qside(C_Q), _bs((tq, HD), lambda h, j, i: (j, C_K + h)), _bs((tq, HD), lambda h, j, i: (j, C_V + h)),
                  qvec, _bs((None, 1, tq), lambda h, j, i: (h, 0, j)), qvec, qside(0), qside(0)],
        out_specs=[_bs((T, HD), lambda h, j, i: (0, h)), kv_out, kv_out, _bs((None, 1, tq), lambda h, j, i: (h, 0, j)),
                   _bs((None, T, 1), lambda h, j, i: (h, 0, 0))],
        out_shape=[S((T, DA), F32)] * 3 + [S((NH, 1, T), F32), S((NH, T, 1), F32)],
        scratch_shapes=[pltpu.VMEM((tq, HD), F32), pltpu.VMEM((tq, HD), F32), pltpu.VMEM((1, tq), F32)],
        compiler_params=_cp(), name="attn_bwd")(P, P, P, cumq, cumk, lse, o, do)


def _lru_gates(xr, wa, wx, ba, bx, sp):
    xb = xr.astype(BF16)
    r = jax.nn.sigmoid(jnp.dot(xb, wa, preferred_element_type=F32) + ba)
    ig = jax.nn.sigmoid(jnp.dot(xb, wx, preferred_element_type=F32) + bx)
    log_a = -LRU_C * r * sp
    a = jnp.exp(log_a)
    th = jnp.tanh(log_a)
    om = -2.0 * th / (1.0 - th)
    mult = jnp.sqrt(om)
    return xb, r, ig, a, om, mult


def _lru_xr(lx_ref, cw, cb, start, b, rows):
    lx = lx_ref[rows, :]
    ext = jnp.concatenate([_halo_prev(lx_ref, start, b), lx], axis=0)
    return cw[3:4, :] * lx + cw[2:3, :] * _down(ext, 1) + cw[1:2, :] * _down(ext, 2) + cw[0:1, :] * _down(ext, 3) + cb


def _lru_fwd(P, lcw, vec, wa, wx):
    T = P.shape[0]
    bt = _bt(T)
    nb = T // bt

    def body(g_ref, lx_ref, cw_ref, vec_ref, wa_ref, wx_ref, y_ref, h_ref):
        cw = cw_ref[...]
        vec = vec_ref[...]
        wa = wa_ref[...].astype(BF16)
        wx = wx_ref[...].astype(BF16)
        sp = _softplus(-vec[3:4, :])

        def step(b, carry):
            start = pl.multiple_of(b * bt, bt)
            rows = pl.ds(start, bt)
            xr = _lru_xr(lx_ref, cw, vec[0:1, :], start, b, rows)
            _, _, ig, a, _, mult = _lru_gates(xr, wa, wx, vec[1:2, :], vec[2:3, :], sp)
            u = mult * (ig * xr)
            ac, hc = _scan_fwd(a, u)
            hb = hc + ac * carry
            h_ref[rows, :] = hb
            gel, _ = _gelu_parts(g_ref[rows, :])
            y_ref[rows, :] = gel * hb
            return hb[bt - 1:bt, :]

        lax.fori_loop(0, nb, step, jnp.zeros((1, 128), F32))

    own = _bs((T, 128), lambda c: (0, c))
    return pl.pallas_call(
        body, grid=(DL // 128,),
        in_specs=[_bs((T, 128), lambda c: (0, C_GATE + c)), _bs((T, 128), lambda c: (0, C_LX + c)),
                  _bs((8, 128), lambda c: (0, c)), _bs((8, 128), lambda c: (0, c)),
                  _bs((None, LB, LB), lambda c: (c, 0, 0)), _bs((None, LB, LB), lambda c: (c, 0, 0))],
        out_specs=[own, own], out_shape=[S((T, DL), F32)] * 2, compiler_params=_cp(), name="lru_fwd")(P, P, lcw, vec, wa, wx)


def _lru_bwd(P, lcw, vec, wa, wx, hst, dy):
    T = P.shape[0]
    bt = _bt(T)
    nb = T // bt

    def body(g_ref, lx_ref, cw_ref, vec_ref, wa_ref, wx_ref, h_ref, dy_ref,
             dg_ref, dlx_ref, sm_ref, dwa_ref, dwx_ref, dxr_s):
        cw = cw_ref[...]
        vec = vec_ref[...]
        wa = wa_ref[...].astype(BF16)
        wx = wx_ref[...].astype(BF16)
        lam = vec[3:4, :]
        sp = _softplus(-lam)
        dwa_ref[...] = jnp.zeros((LB, LB), F32)
        dwx_ref[...] = jnp.zeros((LB, LB), F32)
        zero = jnp.zeros((1, 128), F32)

        def step1(i, carry):
            wc, s_cb, s_ba, s_bx, s_sp = carry
            b = nb - 1 - i
            start = pl.multiple_of(b * bt, bt)
            rows = pl.ds(start, bt)
            xr = _lru_xr(lx_ref, cw, vec[0:1, :], start, b, rows)
            xb, r, ig, a, om, mult = _lru_gates(xr, wa, wx, vec[1:2, :], vec[2:3, :], sp)
            hb = h_ref[rows, :]
            dy = dy_ref[rows, :]
            gel, dgel = _gelu_parts(g_ref[rows, :])
            dg_ref[rows, :] = dy * hb * dgel
            dh = dy * gel
            ac, wcum = _scan_bwd(a, a * dh)
            w = wcum + ac * wc
            g = dh + _up(jnp.concatenate([w, jnp.broadcast_to(wc, (8, 128))], axis=0), 1)
            hprev = _down(jnp.concatenate([_halo_prev(h_ref, start, b), hb], axis=0), 1)
            da = g * hprev
            dmult = g * (ig * xr)
            dix = g * mult
            di = dix * xr
            dlog_a = da * a - dmult * ((1.0 - om) / mult)
            dr = dlog_a * (-LRU_C * sp)
            dpr = dr * r * (1.0 - r)
            dpi = di * ig * (1.0 - ig)
            dprb, dpib = dpr.astype(BF16), dpi.astype(BF16)
            dwa_ref[...] += lax.dot_general(xb, dprb, TN, preferred_element_type=F32)
            dwx_ref[...] += lax.dot_general(xb, dpib, TN, preferred_element_type=F32)
            dxr = (dix * ig + lax.dot_general(dprb, wa, NT, preferred_element_type=F32)
                   + lax.dot_general(dpib, wx, NT, preferred_element_type=F32))
            dxr_s[rows, :] = dxr
            return (w[0:1, :], s_cb + jnp.sum(dxr, axis=0, keepdims=True), s_ba + jnp.sum(dpr, axis=0, keepdims=True),
                    s_bx + jnp.sum(dpi, axis=0, keepdims=True), s_sp + jnp.sum(dlog_a * (-LRU_C * r), axis=0, keepdims=True))

        _, s_cb, s_ba, s_bx, s_sp = lax.fori_loop(0, nb, step1, (zero, zero, zero, zero, zero))

        def step2(b, carry):
            t0, t1, t2, t3 = carry
            start = pl.multiple_of(b * bt, bt)
            rows = pl.ds(start, bt)
            dxr = dxr_s[rows, :]
            extn = jnp.concatenate([dxr, _halo_next(dxr_s, start, bt, b, nb)], axis=0)
            dlx_ref[rows, :] = (cw[3:4, :] * dxr + cw[2:3, :] * _up(extn, 1) + cw[1:2, :] * _up(extn, 2)
                                + cw[0:1, :] * _up(extn, 3))
            lx = lx_ref[rows, :]
            ext = jnp.concatenate([_halo_prev(lx_ref, start, b), lx], axis=0)
            return (t0 + jnp.sum(dxr * _down(ext, 3), axis=0, keepdims=True),
                    t1 + jnp.sum(dxr * _down(ext, 2), axis=0, keepdims=True),
                    t2 + jnp.sum(dxr * _down(ext, 1), axis=0, keepdims=True),
                    t3 + jnp.sum(dxr * lx, axis=0, keepdims=True))

        t0, t1, t2, t3 = lax.fori_loop(0, nb, step2, (zero, zero, zero, zero))
        sm_ref[...] = jnp.zeros((16, 128), F32)
        for k, val in enumerate((t0, t1, t2, t3, s_cb, s_ba, s_bx, -s_sp * jax.nn.sigmoid(-lam))):
            sm_ref[k:k + 1, :] = val

    own = _bs((T, 128), lambda c: (0, c))
    wspec = _bs((None, LB, LB), lambda c: (c, 0, 0))
    return pl.pallas_call(
        body, grid=(DL // 128,),
        in_specs=[_bs((T, 128), lambda c: (0, C_GATE + c)), _bs((T, 128), lambda c: (0, C_LX + c)),
                  _bs((8, 128), lambda c: (0, c)), _bs((8, 128), lambda c: (0, c)), wspec, wspec, own, own],
        out_specs=[own, own, _bs((16, 128), lambda c: (0, c)), wspec, wspec],
        out_shape=[S((T, DL), F32)] * 2 + [S((16, DL), F32), S((4, LB, LB), F32), S((4, LB, LB), F32)],
        scratch_shapes=[pltpu.VMEM((T, 128), F32)], compiler_params=_cp(), name="lru_bwd")(P, P, lcw, vec, wa, wx, hst, dy)


_GROUPS = ((0, DC), (DC, DC + DA), (DC + DA, D))


def _gnorm_fwd(yc, ya, yl, gain):
    T = yc.shape[0]
    tb = _tile(T, (512,))

    def body(c_ref, a_ref, l_ref, g_ref, yn_ref, r0_ref, r1_ref, r2_ref):
        for (lo, hi), src, r_ref in zip(_GROUPS, (c_ref, a_ref, l_ref), (r0_ref, r1_ref, r2_ref)):
            yv = src[...]
            r = lax.rsqrt(jnp.mean(yv * yv, axis=1, keepdims=True) + EPS)
            yn_ref[:, lo:hi] = (yv * r * g_ref[:, lo:hi]).astype(BF16)
            r_ref[...] = r

    rs = _bs((tb, 1), lambda i: (i, 0))
    return pl.pallas_call(
        body, grid=(T // tb,),
        in_specs=[_bs((tb, DC), lambda i: (i, 0)), _bs((tb, DA), lambda i: (i, 0)), _bs((tb, DL), lambda i: (i, 0)),
                  _bs((1, D), lambda i: (0, 0))],
        out_specs=[_bs((tb, D), lambda i: (i, 0)), rs, rs, rs],
        out_shape=[S((T, D), BF16)] + [S((T, 1), F32)] * 3, compiler_params=_cp(), name="gnorm_fwd")(yc, ya, yl, gain)


def _gnorm_bwd(dyn, yc, ya, yl, r0, r1, r2, gain):
    T = yc.shape[0]
    tb = _tile(T, (512,))

    def body(d_ref, c_ref, a_ref, l_ref, r0_ref, r1_ref, r2_ref, g_ref, dc_ref, da_ref, dl_ref, dg_ref):
        i = pl.program_id(0)
        for (lo, hi), src, r_ref, dst in zip(_GROUPS, (c_ref, a_ref, l_ref), (r0_ref, r1_ref, r2_ref), (dc_ref, da_ref, dl_ref)):
            r = r_ref[...]
            yhat = src[...] * r
            dy = d_ref[:, lo:hi]
            dyh = dy * g_ref[:, lo:hi]
            m = jnp.mean(dyh * yhat, axis=1, keepdims=True)
            dst[...] = r * (dyh - yhat * m)
            part = jnp.sum(dy * yhat, axis=0, keepdims=True)

            @pl.when(i == 0)
            def _():
                dg_ref[:, lo:hi] = part

            @pl.when(i > 0)
            def _():
                dg_ref[:, lo:hi] += part

    rs = _bs((tb, 1), lambda i: (i, 0))
    specs = [_bs((tb, DC), lambda i: (i, 0)), _bs((tb, DA), lambda i: (i, 0)), _bs((tb, DL), lambda i: (i, 0))]
    return pl.pallas_call(
        body, grid=(T // tb,),
        in_specs=[_bs((tb, D), lambda i: (i, 0))] + specs + [rs, rs, rs, _bs((1, D), lambda i: (0, 0))],
        out_specs=specs + [_bs((1, D), lambda i: (0, 0))],
        out_shape=[S((T, DC), F32), S((T, DA), F32), S((T, DL), F32), S((1, D), F32)],
        compiler_params=_cp(), name="gnorm_bwd")(dyn, yc, ya, yl, r0, r1, r2, gain)


HBM = pl.BlockSpec(memory_space=pltpu.HBM)
N_BIG = 6


def _place():
    x, y, c = lax.axis_index("x"), lax.axis_index("y"), lax.axis_index("c")
    return x, y, c, 2 * x + y


def _peer(x, y, j):
    return x ^ ((j + 1) >> 1), y ^ ((j + 1) & 1)


SEM = pl.BlockSpec(memory_space=pltpu.SEMAPHORE)
ANY = pl.BlockSpec(memory_space=pl.ANY)
VM = pl.BlockSpec(memory_space=pltpu.VMEM)
EFFECT = pltpu.SideEffectType.DATAFLOW_SIDE_EFFECTING
N_AG = N_BIG + 1


def _hbm(a):
    return pltpu.with_memory_space_constraint(a, pltpu.HBM)


def _ag_copy(src, land, ssem, rsem, t, j, chip):
    x, y, c, _ = _place()
    px, py = _peer(x, y, j)
    if t == N_BIG:
        s_ref, d_ref = src[t], land[t].at[chip]
    else:
        rh = src[t].shape[0] // 2
        half = pl.ds(c * rh, rh)
        s_ref, d_ref = src[t].at[half], land[t].at[chip, half]
    return pltpu.make_async_remote_copy(src_ref=s_ref, dst_ref=d_ref, send_sem=ssem.at[3 * t + j], recv_sem=rsem.at[3 * t + j],
                                        device_id=(px, py, c), device_id_type=MESH)


def _ag_start(l, srcs, dep):
    n = N_AG

    def body(*refs):
        src = refs[:n]
        ssem, rsem = refs[2 * n + 1], refs[2 * n + 2]
        land = refs[3 * n + 3:4 * n + 3]
        token = refs[4 * n + 3]
        _, _, _, me = _place()
        for t in (N_BIG,) + tuple(range(N_BIG)):
            for j in range(3):
                _ag_copy(src, land, ssem, rsem, t, j, me).start()
        token[...] = jnp.zeros_like(token)

    lands = [lax.empty((4,) + a.shape, a.dtype) for a in srcs]
    dma = pltpu.SemaphoreType.DMA
    outs = pl.pallas_call(
        body, name=f"ag_start_{l}",
        out_shape=(dma((3 * n,)), dma((3 * n,))) + tuple(pltpu.HBM(a.shape, a.dtype) for a in list(srcs) + lands) + (S((8, 128), F32),),
        in_specs=[HBM] * (2 * n) + [ANY], out_specs=(SEM, SEM) + (HBM,) * (2 * n) + (VM,),
        input_output_aliases={i: 2 + i for i in range(2 * n)},
        compiler_params=pltpu.CompilerParams(has_side_effects=EFFECT),
    )(*[_hbm(a) for a in srcs], *[_hbm(a) for a in lands], dep)
    return outs[0], outs[1], outs[2:2 + n], outs[2 + n:2 + 2 * n], outs[-1]


def _ag_wait(l, ssem, rsem, srcs, lands, after):
    n = N_AG

    def body(*refs):
        src, land = refs[:n], refs[n:2 * n]
        ssem, rsem = refs[2 * n], refs[2 * n + 1]
        x, y, _, _ = _place()
        for t in range(n):
            for j in range(3):
                px, py = _peer(x, y, j)
                cp = _ag_copy(src, land, ssem, rsem, t, j, 2 * px + py)
                cp.wait_send()
                cp.wait_recv()

    outs = pl.pallas_call(
        body, name=f"ag_wait_{l}",
        out_shape=tuple(pltpu.HBM(a.shape, a.dtype) for a in list(srcs) + list(lands)),
        in_specs=[HBM] * (2 * n) + [SEM, SEM, ANY], out_specs=(HBM,) * (2 * n),
        input_output_aliases={i: i for i in range(2 * n)},
        compiler_params=pltpu.CompilerParams(has_side_effects=EFFECT),
    )(*srcs, *lands, ssem, rsem, after)
    return outs[:n], outs[n:]


def _ag_finish(srcs, lands, dep):
    n = N_AG

    def body(*refs):
        src = refs[:n]
        out = refs[2 * n + 1:3 * n + 1]
        dsend, drecv, lsem = refs[-3:]
        x, y, c, me = _place()

        def d2d(t, j, cc):
            px, py = _peer(x, y, j)
            rh = src[t].shape[0] // 2
            part = out[t].at[2 * px + py, pl.ds(cc * rh, rh)]
            return pltpu.make_async_remote_copy(src_ref=part, dst_ref=part, send_sem=dsend.at[t, j], recv_sem=drecv.at[t, j],
                                                device_id=(x, y, 1 - c), device_id_type=MESH)

        local = [pltpu.make_async_copy(src[t], out[t].at[me], lsem.at[t]) for t in range(n)]
        for cp in local:
            cp.start()
        for t in range(N_BIG):
            for j in range(3):
                d2d(t, j, c).start()
        for t in range(N_BIG):
            for j in range(3):
                d2d(t, j, 1 - c).wait_recv()
                d2d(t, j, c).wait_send()
        for cp in local:
            cp.wait()

    dma = pltpu.SemaphoreType.DMA
    outs = pl.pallas_call(
        body, in_specs=[HBM] * (2 * n) + [ANY], out_specs=[HBM] * n,
        out_shape=[S(a.shape, a.dtype) for a in lands],
        input_output_aliases={n + i: i for i in range(n)},
        scratch_shapes=[dma((N_BIG, 3)), dma((N_BIG, 3)), dma((n,))],
        name="ag_finish")(*srcs, *lands, dep)
    return outs[:N_BIG], outs[N_BIG]


def _rs_pair(grads):
    def body(*refs):
        g = refs[:N_BIG]
        out = refs[N_BIG:2 * N_BIG]
        ssem, rsem = refs[2 * N_BIG:]
        x, y, c, _ = _place()

        def cp(t):
            rh = g[t].shape[1] // 2
            return pltpu.make_async_remote_copy(
                src_ref=g[t].at[:, pl.ds((1 - c) * rh, rh), :], dst_ref=out[t],
                send_sem=ssem.at[t], recv_sem=rsem.at[t], device_id=(x, y, 1 - c), device_id_type=MESH)

        for t in range(N_BIG):
            cp(t).start()
        for t in range(N_BIG):
            cp(t).wait()

    dma = pltpu.SemaphoreType.DMA
    return pl.pallas_call(
        body, in_specs=[HBM] * N_BIG, out_specs=[HBM] * N_BIG,
        out_shape=[S((4, g.shape[1] // 2, g.shape[2]), g.dtype) for g in grads],
        scratch_shapes=[dma((N_BIG,)), dma((N_BIG,))], name="rs_pair")(*grads)


def _rs_copy(s, land, ssem, rsem, t, j):
    x, y, c, _ = _place()
    px, py = _peer(x, y, j)
    return pltpu.make_async_remote_copy(src_ref=s[t].at[2 * px + py], dst_ref=land[t].at[j],
                                        send_sem=ssem.at[3 * t + j], recv_sem=rsem.at[3 * t + j], device_id=(px, py, c), device_id_type=MESH)


def _rs_start(l, sums, dep):
    n = N_BIG

    def body(*refs):
        s = refs[:n]
        ssem, rsem = refs[2 * n + 1], refs[2 * n + 2]
        land = refs[3 * n + 3:4 * n + 3]
        token = refs[4 * n + 3]
        for t in range(n):
            for j in range(3):
                _rs_copy(s, land, ssem, rsem, t, j).start()
        token[...] = jnp.zeros_like(token)

    lands = [lax.empty((3,) + a.shape[1:], a.dtype) for a in sums]
    dma = pltpu.SemaphoreType.DMA
    outs = pl.pallas_call(
        body, name=f"rs_start_{l}",
        out_shape=(dma((3 * n,)), dma((3 * n,))) + tuple(pltpu.HBM(a.shape, a.dtype) for a in list(sums) + lands) + (S((8, 128), F32),),
        in_specs=[HBM] * (2 * n) + [ANY], out_specs=(SEM, SEM) + (HBM,) * (2 * n) + (VM,),
        input_output_aliases={i: 2 + i for i in range(2 * n)},
        compiler_params=pltpu.CompilerParams(has_side_effects=EFFECT),
    )(*[_hbm(a) for a in sums], *[_hbm(a) for a in lands], dep)
    return outs[0], outs[1], outs[2:2 + n], outs[2 + n:2 + 2 * n], outs[-1]


def _rs_wait(l, ssem, rsem, sums, lands, after):
    n = N_BIG

    def body(*refs):
        s, land = refs[:n], refs[n:2 * n]
        ssem, rsem = refs[2 * n], refs[2 * n + 1]
        for t in range(n):
            for j in range(3):
                cp = _rs_copy(s, land, ssem, rsem, t, j)
                cp.wait_send()
                cp.wait_recv()

    outs = pl.pallas_call(
        body, name=f"rs_wait_{l}",
        out_shape=tuple(pltpu.HBM(a.shape, a.dtype) for a in list(sums) + list(lands)),
        in_specs=[HBM] * (2 * n) + [SEM, SEM, ANY], out_specs=(HBM,) * (2 * n),
        input_output_aliases={i: i for i in range(2 * n)},
        compiler_params=pltpu.CompilerParams(has_side_effects=EFFECT),
    )(*sums, *lands, ssem, rsem, after)
    return outs[n:]


def _rs_join(halves):
    def body(*refs):
        h = refs[:N_BIG]
        out = refs[N_BIG:2 * N_BIG]
        ssem, rsem = refs[2 * N_BIG:]
        x, y, c, _ = _place()

        def cp(t):
            return pltpu.make_async_remote_copy(
                src_ref=h[t], dst_ref=out[t], send_sem=ssem.at[t], recv_sem=rsem.at[t],
                device_id=(x, y, 1 - c), device_id_type=MESH)

        for t in range(N_BIG):
            cp(t).start()
        for t in range(N_BIG):
            cp(t).wait()

    dma = pltpu.SemaphoreType.DMA
    return pl.pallas_call(
        body, in_specs=[HBM] * N_BIG, out_specs=[HBM] * N_BIG,
        out_shape=[S(h.shape, h.dtype) for h in halves],
        scratch_shapes=[dma((N_BIG,)), dma((N_BIG,))], name="rs_join")(*halves)


def _all_reduce_small(pack):
    R = pack.shape[0]
    rb = _tile(R, (512, 256, 128, 8))

    def body(x_ref, all_ref, sum_ref, send_sems, recv_sems, local_sem):
        x, y, c = lax.axis_index("x"), lax.axis_index("y"), lax.axis_index("c")
        me, sibling = (x, y, c), (x, y, 1 - c)
        chips = [(1 - x, y), (x, 1 - y), (1 - x, 1 - y)]

        def rows(px, py, pc):
            return all_ref.at[pl.ds((4 * px + 2 * py + pc) * R, R), :]

        def copy(k, block, to, src=None):
            return pltpu.make_async_remote_copy(
                src_ref=rows(*block) if src is None else src, dst_ref=rows(*block),
                send_sem=send_sems.at[k], recv_sem=recv_sems.at[k], device_id=to, device_id_type=MESH)

        mine = pltpu.make_async_copy(x_ref, rows(*me), local_sem)
        mine.start()
        first = [copy(0, me, sibling, src=x_ref)]
        first += [copy(1 + j, me, (*chip, c), src=x_ref) for j, chip in enumerate(chips)]
        for cp in first:
            cp.start()
        passed = [copy(4 + j, (*chip, c), sibling) for j, chip in enumerate(chips)]
        for j, chip in enumerate(chips):
            copy(1 + j, (*chip, c), me).wait_recv()
            passed[j].start()
        copy(0, sibling, me).wait_recv()
        for j, chip in enumerate(chips):
            copy(4 + j, (*chip, 1 - c), me).wait_recv()
        for cp in first + passed:
            cp.wait_send()
        mine.wait()

        def step(b, carry):
            off = pl.multiple_of(b * rb, rb)
            acc = all_ref[pl.ds(off, rb), :]
            for k in range(1, 8):
                acc = acc + all_ref[pl.ds(pl.multiple_of(k * R + off, 8), rb), :]
            sum_ref[pl.ds(off, rb), :] = acc
            return carry

        lax.fori_loop(0, R // rb, step, 0)

    vm = pl.BlockSpec(memory_space=pltpu.VMEM)
    dma = pltpu.SemaphoreType.DMA
    _, total = pl.pallas_call(
        body, in_specs=[vm], out_specs=[vm, vm],
        out_shape=[S((8 * R, 128), F32), S((R, 128), F32)],
        scratch_shapes=[dma((7,)), dma((7,)), dma],
        compiler_params=_cp(), name="allreduce_small")(pack)
    return total


def _row_tile(rh, cc, tile_bytes=3 * 1024 * 1024 // 2):
    for t in (512, 256, 128, 64, 32, 16):
        if rh % t == 0 and t * cc * 4 <= tile_bytes:
            return t
    return 16


def _my_chip():
    return 2 * lax.axis_index("x") + lax.axis_index("y")


def _pair_sum(g, recv):
    _, r, cc = g.shape
    rh = r // 2
    tb = _row_tile(rh, cc)
    nbh = rh // tb

    def body(g_ref, r_ref, o_ref):
        o_ref[...] = (g_ref[...].astype(F32) + r_ref[...].astype(F32)).astype(BF16)

    mine = _bs((None, tb, cc), lambda k, i: (k, lax.axis_index("c") * nbh + i, 0))
    plain = _bs((None, tb, cc), lambda k, i: (k, i, 0))
    return pl.pallas_call(body, grid=(4, nbh), in_specs=[mine, plain], out_specs=plain,
                          out_shape=S((4, rh, cc), BF16), compiler_params=_cp(), name="rs_pair_sum")(g, recv)


def _owner_sum(g, recv, ici, acc, l):
    _, r, cc = g.shape
    rh = r // 2
    tb = _row_tile(rh, cc)
    nbh = rh // tb

    def body(g_ref, r_ref, i0_ref, i1_ref, i2_ref, acc_ref, o_ref):
        s = g_ref[...].astype(F32) + r_ref[...].astype(F32)
        o_ref[...] = s + i0_ref[...].astype(F32) + i1_ref[...].astype(F32) + i2_ref[...].astype(F32)

    def slot(j):
        return _bs((None, tb, cc), lambda i: (j, i, 0))

    return pl.pallas_call(
        body, grid=(nbh,),
        in_specs=[_bs((None, tb, cc), lambda i: (_my_chip(), lax.axis_index("c") * nbh + i, 0)),
                  _bs((None, tb, cc), lambda i: (_my_chip(), i, 0)),
                  slot(0), slot(1), slot(2), pl.BlockSpec(memory_space=pl.ANY)],
        out_specs=_bs((None, tb, cc), lambda i: (l, i, 0)),
        out_shape=S(acc.shape, F32), input_output_aliases={5: 0},
        compiler_params=_cp(), name="rs_owner_sum")(g, recv, ici, ici, ici, acc)


def _adam_math(w, g, m, v):
    m = B1 * m + (1.0 - B1) * g
    v = B2 * v + (1.0 - B2) * (g * g)
    m_hat = m / (1.0 - B1 ** STEP)
    v_hat = v / (1.0 - B2 ** STEP)
    delta = -LR * (m_hat / (jnp.sqrt(v_hat) + AEPS) + WD * w)
    return delta, m, v


def _adamw_big(w, g_mine, g_sib, m, v):
    L, r, cc = w.shape
    rh = r // 2
    tb = _row_tile(rh, cc)
    nbh = rh // tb

    def body(w_ref, gm_ref, gs_ref, m_ref, v_ref, go_ref, d_ref, mo_ref, vo_ref):
        mine = pl.program_id(1) == lax.axis_index("c")
        g = jnp.where(mine, gm_ref[...], gs_ref[...])
        d, m, v = _adam_math(w_ref[...], g, m_ref[...], v_ref[...])
        go_ref[...] = g
        d_ref[...] = d
        mo_ref[...] = m
        vo_ref[...] = v

    def mine_map(l, hf, i):
        c = lax.axis_index("c")
        return (l, jnp.where(hf == c, i, jnp.where(c == 0, nbh - 1, 0)), 0)

    def sib_map(l, hf, i):
        c = lax.axis_index("c")
        return (l, jnp.where(hf != c, i, jnp.where(c == 0, 0, nbh - 1)), 0)

    full = _bs((None, tb, cc), lambda l, hf, i: (l, hf * nbh + i, 0))
    return pl.pallas_call(
        body, grid=(L, 2, nbh),
        in_specs=[full, _bs((None, tb, cc), mine_map), _bs((None, tb, cc), sib_map), full, full],
        out_specs=[full] * 4, out_shape=[S(w.shape, F32)] * 4, compiler_params=_cp(), name="adamw_big")(w, g_mine, g_sib, m, v)


def _adamw_small(w, g, m, v):
    R = w.shape[0]
    tb = _tile(R, (512, 256, 128, 8))

    def body(w_ref, g_ref, m_ref, v_ref, d_ref, mo_ref, vo_ref):
        d, m, v = _adam_math(w_ref[...], g_ref[...], m_ref[...], v_ref[...])
        d_ref[...] = d
        mo_ref[...] = m
        vo_ref[...] = v

    spec = _bs((tb, 128), lambda i: (i, 0))
    return pl.pallas_call(body, grid=(R // tb,), in_specs=[spec] * 4, out_specs=[spec] * 3,
                          out_shape=[S((R, 128), F32)] * 3, compiler_params=_cp(), name="adamw_small")(w, g, m, v)


def _mix_pad(w):
    return jnp.concatenate([w[:, :4608], w[:, 4616:DIN], w[:, 4608:4616], jnp.zeros((D, PW - DIN), w.dtype)], axis=1)


def _mix_unpad(g):
    return jnp.concatenate([g[:, :4608], g[:, 5632:5640], g[:, 4608:5632]], axis=1)


def _pack(parts):
    flat = jnp.concatenate([p.reshape(-1).astype(F32) for p in parts])
    n = flat.shape[0]
    total = -(-n // (512 * 128)) * (512 * 128)
    return jnp.pad(flat, (0, total - n)).reshape(total // 128, 128)


def _unpack(pack, shapes):
    flat = pack.reshape(-1)
    out, off = [], 0
    for s in shapes:
        n = math.prod(s)
        out.append(flat[off:off + n].reshape(s))
        off += n
    return out


def _ffn_forward(x, gain, win, wout):
    h, rstd = _rms_fwd(x, gain)
    zg, zu, act = _ffn_in(h, win)
    y = _ffn_out(act, wout, x)
    return y, (x, h, rstd, zg, zu, act)


def _ffn_backward(dy, saved, gain, win, wout, dep):
    x, h, rstd, zg, zu, act = saved
    dyb = dy.astype(BF16)
    dzg, dzu = _ffn_bwd_dz(dyb, wout, zg, zu, dep)
    dwout = _mm_tn("ffn_bwd_dwout", act, dyb, scale=0.5, tm=512, tn=1024)
    dwin = _ffn_bwd_dwin(h, dzg, dzu)
    dh = _ffn_bwd_dh(dzg, dzu, win)
    dx, dgain = _rms_bwd(dh, x, rstd, gain, dy)
    return dx, dgain, dwin, dwout


def _mixer_forward(x, p):
    T = x.shape[0]
    h, rstd = _rms_fwd(x, p["norm_mix"])
    tm = _tile(T, (1024, 512))
    tn = 1152
    P = _mm("mix_in", h, p["wmix"],
            _bs((tm, D), lambda i, j, k: (i, 0)), _bs((D, tn), lambda i, j, k: (0, j)),
            _bs((tm, tn), lambda i, j, k: (i, j)), S((T, PW), F32), (T // tm, PW // tn, 1), NN, 1, (tm, tn))
    yc = _conv_fwd(P, p["cw"])
    cum = _fgate_fwd(P, p["fb"])
    cumt = cum[:, :NH].T
    cumq, cumk = cumt.reshape(NH, T, 1), cumt.reshape(NH, 1, T)
    ya, lse = _attn_fwd(P, cumq, cumk)
    yl, hst = _lru_fwd(P, p["lcw"], p["lvec"], p["lru_w_a"], p["lru_w_x"])
    yn, r0, r1, r2 = _gnorm_fwd(yc, ya, yl, p["mix_out_norm"])
    tk = 512
    y = _mm("mix_out", yn, p["wo"],
            _bs((tm, tk), lambda i, j, k: (i, k)), _bs((tk, 1024), lambda i, j, k: (k, j)),
            _bs((tm, 1024), lambda i, j, k: (i, j)), S((T, D), F32), (T // tm, D // 1024, D // tk), NN, D // tk, (tm, 1024),
            res=x, r_spec=_bs((tm, 1024), lambda i, j, k: (i, j)))
    return y, (x, h, rstd, P, cumq, cumk, lse, yc, ya, yl, hst, yn, r0, r1, r2)


def _mixer_backward(dy, saved, p):
    x, h, rstd, P, cumq, cumk, lse, yc, ya, yl, hst, yn, r0, r1, r2 = saved
    T = x.shape[0]
    dyb = dy.astype(BF16)
    dyn = _mm_nt_full("mix_bwd_dyn", dyb, p["wo"], 512)
    dwo = _mm_tn("mix_bwd_dwo", yn, dyb, tm=512, tn=1024)
    dyc, dya, dyl, dgn = _gnorm_bwd(dyn, yc, ya, yl, r0, r1, r2, p["mix_out_norm"])
    dcb, dcc, dcv, dcw = _conv_bwd(P, p["cw"], dyc)
    dq, dk, dv, dck, dcq = _attn_bwd(P, cumq, cumk, lse, ya, dya)
    dcum = jnp.pad((dck.reshape(NH, T) + dcq.reshape(NH, T)).T, ((0, 0), (0, 128 - NH)))
    df, dfb = _fgate_bwd(P, p["fb"], dcum)
    dgate, dlx, lsm, dwa, dwx = _lru_bwd(P, p["lcw"], p["lvec"], p["lru_w_a"], p["lru_w_x"], hst, dyl)
    dP = jnp.concatenate([dcb, dcc, dcv, dq, dk, dv, dgate, dlx, df], axis=1).astype(BF16)
    tm = _tile(T, (1024, 512))
    tk = 1152
    nk = PW // tk
    dh = _mm("mix_bwd_dh", dP, p["wmix"],
             _bs((tm, tk), lambda i, j, k: (i, k)), _bs((1024, tk), lambda i, j, k: (j, k)),
             _bs((tm, 1024), lambda i, j, k: (i, j)), S((T, D), F32), (T // tm, D // 1024, nk), NT, nk, (tm, 1024))
    dwmix = _mm_tn("mix_bwd_dwmix", h, dP, tm=1024, tn=1152)
    dx, dgm = _rms_bwd(dh, x, rstd, p["norm_mix"], dy)
    small = dict(norm_mix=dgm[0], mix_out_norm=dgn[0], conv_w=dcw[:3], fgate_b=dfb[0, :NH], lru_conv_w=lsm[:4],
                 lru_conv_b=lsm[4], lru_b_a=lsm[5], lru_b_x=lsm[6], lru_lambda=lsm[7], lru_w_a=dwa, lru_w_x=dwx)
    return dx, small, dwmix, dwo


BIG = ("ffn1_w_in", "ffn1_w_out", "mix_w_in", "mix_w_out", "ffn2_w_in", "ffn2_w_out")
SMALL = ("norm_ffn1", "norm_mix", "conv_w", "fgate_b", "lru_conv_w", "lru_conv_b", "lru_w_a", "lru_b_a", "lru_w_x",
         "lru_b_x", "lru_lambda", "mix_out_norm", "norm_ffn2", "final_norm")
WEIGHTS = ("norm_ffn1", "ffn1_w_in", "ffn1_w_out", "norm_mix", "mix_w_in", "conv_w", "fgate_b", "lru_conv_w", "lru_conv_b",
           "lru_w_a", "lru_b_a", "lru_w_x", "lru_b_x", "lru_lambda", "mix_out_norm", "mix_w_out", "norm_ffn2", "ffn2_w_in",
           "ffn2_w_out", "final_norm")


def _step(args):
    xx, yy, cc_ = lax.axis_index("x"), lax.axis_index("y"), lax.axis_index("c")
    me = 2 * xx + yy
    x0 = args["x"][0]
    tgt = args["loss_target"][0]
    T = x0.shape[0]
    L = args["norm_ffn1"].shape[0]

    def ag_sources(l):
        small = jnp.concatenate([args["conv_w"][l], args["lru_conv_w"][l], jnp.zeros((1, 128), F32)], axis=0)
        return [args[n][l].astype(BF16) for n in BIG] + [small]

    def layer_params(l, gat, gsm):
        w1i, w1o, wmx, wo, w2i, w2o = gat
        cwl = gsm.transpose(1, 0, 2).reshape(8, 4 * 128)
        return dict(
            w1i=w1i, w1o=w1o.reshape(-1, D), w2i=w2i, w2o=w2o.reshape(-1, D), wo=wo.reshape(D, D),
            wmix=_mix_pad(wmx.transpose(1, 0, 2).reshape(D, DIN)),
            cw=jnp.concatenate([cwl[:3], jnp.zeros((5, DC), F32)], axis=0),
            lcw=jnp.concatenate([cwl[3:7], jnp.zeros((4, DL), F32)], axis=0),
            fb=jnp.pad(args["fgate_b"][l], (0, 128 - NH)).reshape(1, 128),
            lvec=jnp.concatenate([args["lru_conv_b"][l][None], args["lru_b_a"][l][None], args["lru_b_x"][l][None],
                                  args["lru_lambda"][l][None], jnp.zeros((4, DL), F32)], axis=0),
            lru_w_a=args["lru_w_a"][l], lru_w_x=args["lru_w_x"][l],
            norm_ffn1=args["norm_ffn1"][l][None], norm_mix=args["norm_mix"][l][None],
            mix_out_norm=args["mix_out_norm"][l][None], norm_ffn2=args["norm_ffn2"][l][None])

    xs = x0
    saved, layers = [], []
    flight = _ag_start(0, ag_sources(0), x0)
    for l in range(L):
        ssem, rsem, srcs, lands, token = flight
        srcs, lands = _ag_wait(l, ssem, rsem, srcs, lands, xs)
        if l + 1 < L:
            flight = _ag_start(l + 1, ag_sources(l + 1), lands[0])
            token = flight[4]
        gat, gsm = _ag_finish(srcs, lands, token)
        p = layer_params(l, gat, gsm)
        x1, s1 = _ffn_forward(xs, p["norm_ffn1"], p["w1i"], p["w1o"])
        x2, s2 = _mixer_forward(x1, p)
        x3, s3 = _ffn_forward(x2, p["norm_ffn2"], p["w2i"], p["w2o"])
        saved.append((s1, s2, s3))
        layers.append(p)
        xs = x3
    lpart, dx, dfinal = _loss_head(xs, args["final_norm"][None], tgt)
    loss = lax.psum(lpart[0, 0], ("x", "y", "c"))

    acc = None
    small_grads = [None] * L
    pending = None
    dep = lpart

    def owner(pending, acc, after):
        lp, grads, recv, (ssem, rsem, sums, lands, _) = pending
        ici = _rs_wait(lp, ssem, rsem, sums, lands, after)
        return [_owner_sum(g, r, i3, a, lp) for g, r, i3, a in zip(grads, recv, ici, acc)]

    for l in reversed(range(L)):
        p = layers[l]
        s1, s2, s3 = saved[l]
        dx, dg2, dw2i, dw2o = _ffn_backward(dx, s3, p["norm_ffn2"], p["w2i"], p["w2o"], dep)
        dx, sm, dwmix, dwo = _mixer_backward(dx, s2, p)
        dx, dg1, dw1i, dw1o = _ffn_backward(dx, s1, p["norm_ffn1"], p["w1i"], p["w1o"], dep)
        sm["norm_ffn1"] = dg1[0]
        sm["norm_ffn2"] = dg2[0]
        small_grads[l] = sm
        F4 = dw1o.shape[0] // 4
        grads = [dw1i, dw1o.reshape(4, F4, D), _mix_unpad(dwmix).reshape(D, 4, DIN // 4).transpose(1, 0, 2),
                 dwo.reshape(4, D // 4, D), dw2i, dw2o.reshape(4, F4, D)]
        if acc is None:
            acc = [jnp.zeros((L, g.shape[1] // 2, g.shape[2]), F32) for g in grads]
        if pending is not None:
            acc = owner(pending, acc, dx)
        recv = _rs_pair(grads)
        sums = [_pair_sum(g, r) for g, r in zip(grads, recv)]
        started = _rs_start(l, sums, dx)
        pending = (l, grads, recv, started)
        dep = started[4]
    acc = owner(pending, acc, dx)
    sib = _rs_join(acc)

    out = {"loss": loss, "grad_x": dx[None]}
    for n, g_mine, g_sib in zip(BIG, acc, sib):
        g, d, m, v = _adamw_big(args[n], g_mine, g_sib, args["m_" + n], args["v_" + n])
        out["grad_" + n], out["delta_" + n], out["new_m_" + n], out["new_v_" + n] = g, d, m, v

    full = {n: (dfinal[0] if n == "final_norm" else jnp.stack([small_grads[l][n] for l in range(L)])) for n in SMALL}
    shapes = [full[n].shape for n in SMALL]
    red = dict(zip(SMALL, _unpack(_all_reduce_small(_pack([full[n] for n in SMALL])), shapes)))
    for n in ("conv_w", "lru_conv_w"):
        red[n] = lax.dynamic_slice_in_dim(red[n], me * 128, 128, axis=2)
    oshapes = [args[n].shape for n in SMALL]
    d, m, v = _adamw_small(_pack([args[n] for n in SMALL]), _pack([red[n] for n in SMALL]),
                           _pack([args["m_" + n] for n in SMALL]), _pack([args["v_" + n] for n in SMALL]))
    for n, gg, dd, mm, vv in zip(SMALL, [red[n] for n in SMALL], _unpack(d, oshapes), _unpack(m, oshapes), _unpack(v, oshapes)):
        out["grad_" + n], out["delta_" + n], out["new_m_" + n], out["new_v_" + n] = gg, dd, mm, vv
    return out


def kernel(x, norm_ffn1, ffn1_w_in, ffn1_w_out, norm_mix, mix_w_in, conv_w, fgate_b, lru_conv_w, lru_conv_b, lru_w_a, lru_b_a, lru_w_x, lru_b_x, lru_lambda, mix_out_norm, mix_w_out, norm_ffn2, ffn2_w_in, ffn2_w_out, final_norm, loss_target, m_norm_ffn1, m_ffn1_w_in, m_ffn1_w_out, m_norm_mix, m_mix_w_in, m_conv_w, m_fgate_b, m_lru_conv_w, m_lru_conv_b, m_lru_w_a, m_lru_b_a, m_lru_w_x, m_lru_b_x, m_lru_lambda, m_mix_out_norm, m_mix_w_out, m_norm_ffn2, m_ffn2_w_in, m_ffn2_w_out, m_final_norm, v_norm_ffn1, v_ffn1_w_in, v_ffn1_w_out, v_norm_mix, v_mix_w_in, v_conv_w, v_fgate_b, v_lru_conv_w, v_lru_conv_b, v_lru_w_a, v_lru_b_a, v_lru_w_x, v_lru_b_x, v_lru_lambda, v_mix_out_norm, v_mix_w_out, v_norm_ffn2, v_ffn2_w_in, v_ffn2_w_out, v_final_norm):
    args = dict(locals())
    out = _step(args)
    res = [out["loss"], out["grad_x"]]
    for prefix in ("grad_", "delta_", "new_m_", "new_v_"):
        res += [out[prefix + n] for n in WEIGHTS]
    return tuple(res)
```

```python
import functools
import math

import jax
import jax.numpy as jnp
from jax import lax
from jax.experimental import pallas as pl
from jax.experimental.pallas import tpu as pltpu

F32 = jnp.float32
BF16 = jnp.bfloat16
S = jax.ShapeDtypeStruct
MESH = pl.DeviceIdType.MESH

D = 2048
DC = 512
DA = 1024
NH = 8
HD = 128
DL = 512
LB = 128
DIN = 5640
PW = 5760
C_Q, C_K, C_V = 12, 20, 28
C_GATE, C_LX, C_F = 36, 40, 44
EPS = 1e-6
LRU_C = 8.0
ATT_SCALE = HD ** -0.5
LR, B1, B2, AEPS, WD, STEP = 0.001, 0.9, 0.999, 1e-08, 0.01, 10
VMEM_LIMIT = 56 * 1024 * 1024

NT = (((1,), (1,)), ((), ()))
TN = (((0,), (0,)), ((), ()))
NN = (((1,), (0,)), ((), ()))


def _cp():
    return pltpu.CompilerParams(vmem_limit_bytes=VMEM_LIMIT)


def _bs(shape, fn):
    return pl.BlockSpec(shape, fn)


def _mm(name, a, b, a_spec, b_spec, o_spec, o_shape, grid, dims, nk, acc_tile, scale=1.0, res=None, r_spec=None):
    has_res = res is not None

    def body(*refs):
        if has_res:
            a_ref, b_ref, r_ref, o_ref = refs[:4]
            rest = refs[4:]
        else:
            a_ref, b_ref, o_ref = refs[:3]
            rest = refs[3:]
        prod = lax.dot_general(a_ref[...].astype(BF16), b_ref[...].astype(BF16), dims, preferred_element_type=F32)

        def finish(acc):
            if scale != 1.0:
                acc = acc * scale
            if has_res:
                acc = r_ref[...] + acc
            o_ref[...] = acc.astype(o_ref.dtype)

        if nk == 1:
            finish(prod)
        else:
            acc_ref = rest[0]
            k = pl.program_id(2)

            @pl.when(k == 0)
            def _():
                acc_ref[...] = prod

            @pl.when(k > 0)
            def _():
                acc_ref[...] += prod

            @pl.when(k == nk - 1)
            def _():
                finish(acc_ref[...])

    in_specs = [a_spec, b_spec] + ([r_spec] if has_res else [])
    args = (a, b) + ((res,) if has_res else ())
    scratch = [pltpu.VMEM(acc_tile, F32)] if nk > 1 else []
    return pl.pallas_call(body, grid=grid, in_specs=in_specs, out_specs=o_spec, out_shape=o_shape,
                          scratch_shapes=scratch, compiler_params=_cp(), name=name)(*args)


def _tile(n, pref):
    for t in pref:
        if n % t == 0:
            return t
    return n


def _rms_fwd(x, gain):
    T = x.shape[0]
    tb = _tile(T, (512,))

    def body(x_ref, g_ref, h_ref, r_ref):
        xv = x_ref[...]
        r = lax.rsqrt(jnp.mean(xv * xv, axis=1, keepdims=True) + EPS)
        h_ref[...] = (xv * r * g_ref[...]).astype(BF16)
        r_ref[...] = r

    return pl.pallas_call(
        body, grid=(T // tb,),
        in_specs=[_bs((tb, D), lambda i: (i, 0)), _bs((1, D), lambda i: (0, 0))],
        out_specs=[_bs((tb, D), lambda i: (i, 0)), _bs((tb, 1), lambda i: (i, 0))],
        out_shape=[S((T, D), BF16), S((T, 1), F32)], compiler_params=_cp(), name="rms_fwd")(x, gain)


def _rms_bwd(dh, x, rstd, gain, dres):
    T = x.shape[0]
    tb = _tile(T, (512,))

    def body(dh_ref, x_ref, r_ref, g_ref, dres_ref, dx_ref, dxb_ref, dg_ref):
        i = pl.program_id(0)
        r = r_ref[...]
        xhat = x_ref[...] * r
        dh = dh_ref[...]
        dxh = dh * g_ref[...]
        m = jnp.mean(dxh * xhat, axis=1, keepdims=True)
        dx = dres_ref[...] + r * (dxh - xhat * m)
        dx_ref[...] = dx
        dxb_ref[...] = dx.astype(BF16)
        part = jnp.sum(dh * xhat, axis=0, keepdims=True)

        @pl.when(i == 0)
        def _():
            dg_ref[...] = part

        @pl.when(i > 0)
        def _():
            dg_ref[...] += part

    row = _bs((tb, D), lambda i: (i, 0))
    return pl.pallas_call(
        body, grid=(T // tb,),
        in_specs=[row, row, _bs((tb, 1), lambda i: (i, 0)), _bs((1, D), lambda i: (0, 0)), row],
        out_specs=[row, row, _bs((1, D), lambda i: (0, 0))],
        out_shape=[S((T, D), F32), S((T, D), BF16), S((1, D), F32)], compiler_params=_cp(), name="rms_bwd")(dh, x, rstd, gain, dres)


def _loss_head(x, gain, tgt):
    T = x.shape[0]
    tb = _tile(T, (512,))

    def body(x_ref, g_ref, t_ref, l_ref, dx_ref, dxb_ref, dg_ref):
        i = pl.program_id(0)
        xv = x_ref[...]
        g = g_ref[...]
        r = lax.rsqrt(jnp.mean(xv * xv, axis=1, keepdims=True) + EPS)
        xhat = xv * r
        e = xhat * g - t_ref[...]
        lpart = 0.5 * jnp.sum(jnp.sum(e * e, axis=1, keepdims=True), axis=0, keepdims=True) * (1.0 / D)
        dy = e * (1.0 / D)
        dxh = dy * g
        m = jnp.mean(dxh * xhat, axis=1, keepdims=True)
        dx = r * (dxh - xhat * m)
        dx_ref[...] = dx
        dxb_ref[...] = dx.astype(BF16)
        gpart = jnp.sum(dy * xhat, axis=0, keepdims=True)
        lrow = jnp.broadcast_to(lpart, (1, 128))

        @pl.when(i == 0)
        def _():
            dg_ref[...] = gpart
            l_ref[...] = lrow

        @pl.when(i > 0)
        def _():
            dg_ref[...] += gpart
            l_ref[...] += lrow

    row = _bs((tb, D), lambda i: (i, 0))
    return pl.pallas_call(
        body, grid=(T // tb,),
        in_specs=[row, _bs((1, D), lambda i: (0, 0)), row],
        out_specs=[_bs((1, 128), lambda i: (0, 0)), row, row, _bs((1, D), lambda i: (0, 0))],
        out_shape=[S((1, 128), F32), S((T, D), F32), S((T, D), BF16), S((1, D), F32)],
        compiler_params=_cp(), name="loss_head")(x, gain, tgt)


def _ffn_in(h, win):
    T = h.shape[0]
    Fs = win.shape[2]
    F = 2 * Fs
    tn = _tile(Fs, (256, 128))
    nb = Fs // tn
    tm = _tile(T, (1024, 512))

    def body(h_ref, wg_ref, wu_ref, zg_ref, zu_ref, a_ref):
        hv = h_ref[...]
        zg = jnp.dot(hv, wg_ref[...], preferred_element_type=F32)
        zu = jnp.dot(hv, wu_ref[...], preferred_element_type=F32)
        zg_ref[...] = zg.astype(BF16)
        zu_ref[...] = zu.astype(BF16)
        a_ref[...] = (zg * jax.nn.sigmoid(zg) * zu).astype(BF16)

    col = _bs((tm, tn), lambda i, j: (i, j))
    return pl.pallas_call(
        body, grid=(T // tm, F // tn),
        in_specs=[_bs((tm, D), lambda i, j: (i, 0)),
                  _bs((None, D, tn), lambda i, j: (j // nb, 0, j % nb)),
                  _bs((None, D, tn), lambda i, j: (2 + j // nb, 0, j % nb))],
        out_specs=[col, col, col],
        out_shape=[S((T, F), BF16)] * 3, compiler_params=_cp(), name="ffn_in")(h, win, win)


def _ffn_out(act, wout, x):
    T, F = act.shape
    tm = _tile(T, (1024, 512))
    tn = 1024
    tk = _tile(F, (1408, 512, 256))
    nk = F // tk
    return _mm("ffn_out", act, wout,
               _bs((tm, tk), lambda i, j, k: (i, k)), _bs((tk, tn), lambda i, j, k: (k, j)),
               _bs((tm, tn), lambda i, j, k: (i, j)), S((T, D), F32), (T // tm, D // tn, nk), NN, nk, (tm, tn),
               scale=0.5, res=x, r_spec=_bs((tm, tn), lambda i, j, k: (i, j)))


def _ffn_bwd_dz(dyb, wout, zg, zu, dep):
    T, F = zg.shape
    tm = _tile(T, (1024, 512))
    tn = _tile(F, (512, 256))

    def body(dy_ref, w_ref, zg_ref, zu_ref, dep_ref, dzg_ref, dzu_ref):
        da = 0.5 * lax.dot_general(dy_ref[...], w_ref[...], NT, preferred_element_type=F32)
        zg = zg_ref[...].astype(F32)
        zu = zu_ref[...].astype(F32)
        s = jax.nn.sigmoid(zg)
        dzg_ref[...] = (da * zu * (s * (1.0 + zg * (1.0 - s)))).astype(BF16)
        dzu_ref[...] = (da * (zg * s)).astype(BF16)

    col = _bs((tm, tn), lambda i, j: (i, j))
    nj = F // tn
    dzg, dzu = pl.pallas_call(
        body, grid=(T // tm, nj),
        in_specs=[_bs((tm, D), lambda i, j: (i, 0)), _bs((tn, D), lambda i, j: (j, 0)), col, col,
                  pl.BlockSpec(memory_space=pl.ANY)],
        out_specs=[col, col], out_shape=[S((T, F), BF16)] * 2, compiler_params=_cp(), name="ffn_bwd_dz")(dyb, wout, zg, zu, dep)
    return dzg, dzu


def _ffn_bwd_dh(dzg, dzu, win):
    T, F = dzg.shape
    Fs = win.shape[2]
    tk = _tile(Fs, (1408, 256, 128))
    nkb = Fs // tk
    tm = _tile(T, (1024, 512))
    tn = 1024
    nk = 2 * nkb

    def body(dzg_ref, dzu_ref, wg_ref, wu_ref, o_ref, acc_ref):
        k = pl.program_id(2)
        prod = (lax.dot_general(dzg_ref[...], wg_ref[...], NT, preferred_element_type=F32)
                + lax.dot_general(dzu_ref[...], wu_ref[...], NT, preferred_element_type=F32))

        @pl.when(k == 0)
        def _():
            acc_ref[...] = prod

        @pl.when(k > 0)
        def _():
            acc_ref[...] += prod

        @pl.when(k == nk - 1)
        def _():
            o_ref[...] = acc_ref[...]

    a_spec = _bs((tm, tk), lambda i, j, k: (i, k))
    return pl.pallas_call(
        body, grid=(T // tm, D // tn, nk),
        in_specs=[a_spec, a_spec,
                  _bs((None, tn, tk), lambda i, j, k: (k // nkb, j, k % nkb)),
                  _bs((None, tn, tk), lambda i, j, k: (2 + k // nkb, j, k % nkb))],
        out_specs=_bs((tm, tn), lambda i, j, k: (i, j)), out_shape=S((T, D), F32),
        scratch_shapes=[pltpu.VMEM((tm, tn), F32)], compiler_params=_cp(), name="ffn_bwd_dh")(dzg, dzu, win, win)


def _ffn_bwd_dwin(h, dzg, dzu):
    T, F = dzg.shape
    Fs = F // 2
    tn = _tile(Fs, (1408, 256, 128))
    nb = Fs // tn
    tm = 1024
    tk = _tile(T, (1024, 512))
    nk = T // tk

    ng = 2 * nb

    def body(h_ref, zg_ref, zu_ref, o_ref, acc_ref):
        j, k = pl.program_id(1), pl.program_id(2)

        def accumulate(dz_ref):
            prod = lax.dot_general(h_ref[...], dz_ref[...], TN, preferred_element_type=F32)

            @pl.when(k == 0)
            def _():
                acc_ref[...] = prod

            @pl.when(k > 0)
            def _():
                acc_ref[...] += prod

        @pl.when(j < ng)
        def _():
            accumulate(zg_ref)

        @pl.when(j >= ng)
        def _():
            accumulate(zu_ref)

        @pl.when(k == nk - 1)
        def _():
            o_ref[...] = acc_ref[...].astype(BF16)

    return pl.pallas_call(
        body, grid=(D // tm, 2 * ng, nk),
        in_specs=[_bs((tk, tm), lambda i, j, k: (k, i)),
                  _bs((tk, tn), lambda i, j, k: (jnp.where(j < ng, k, nk - 1), jnp.minimum(j, ng - 1))),
                  _bs((tk, tn), lambda i, j, k: (jnp.where(j >= ng, k, 0), jnp.maximum(j - ng, 0)))],
        out_specs=_bs((None, tm, tn), lambda i, j, k: (j // nb, i, j % nb)), out_shape=S((4, D, Fs), BF16),
        scratch_shapes=[pltpu.VMEM((tm, tn), F32)], compiler_params=_cp(), name="ffn_bwd_dwin")(h, dzg, dzu)


def _mm_tn(name, a, b, scale=1.0, tm=512, tn=1024):
    T, M = a.shape
    N = b.shape[1]
    tm = _tile(M, (tm, 512, 256, 128))
    tn = _tile(N, (tn, 1152, 1024, 512, 128))
    tk = _tile(T, (1024, 512))
    nk = T // tk
    return _mm(name, a, b,
               _bs((tk, tm), lambda i, j, k: (k, i)), _bs((tk, tn), lambda i, j, k: (k, j)),
               _bs((tm, tn), lambda i, j, k: (i, j)), S((M, N), BF16), (M // tm, N // tn, nk), TN, nk, (tm, tn), scale=scale)


def _mm_nt_full(name, a, b, tn):
    T, K = a.shape
    N = b.shape[0]
    tm = _tile(T, (1024, 512))
    return _mm(name, a, b,
               _bs((tm, K), lambda i, j, k: (i, 0)), _bs((tn, K), lambda i, j, k: (j, 0)),
               _bs((tm, tn), lambda i, j, k: (i, j)), S((T, N), F32), (T // tm, N // tn, 1), NT, 1, (tm, tn))


def _bt(T):
    return _tile(T, (512,))


def _down(ext, s):
    return pltpu.roll(ext, s, 0)[8:, :]


def _up(ext, s):
    n = ext.shape[0]
    return pltpu.roll(ext, n - s, 0)[: n - 8, :]


def _halo_prev(ref, start, b):
    lo = pl.multiple_of(jnp.maximum(start - 8, 0), 8)
    return ref[pl.ds(lo, 8), :] * (b > 0).astype(F32)


def _halo_next(ref, start, bt, b, nb):
    lo = pl.multiple_of(jnp.minimum(start + bt, (nb - 1) * bt), 8)
    return ref[pl.ds(lo, 8), :] * (b < nb - 1).astype(F32)


def _scan_fwd(A, U):
    n = U.shape[0]
    row = lax.broadcasted_iota(jnp.int32, U.shape, 0)
    d = 1
    while d < n:
        keep = row >= d
        Us = jnp.where(keep, pltpu.roll(U, d, 0), 0.0)
        if A is None:
            U = U + Us
        else:
            As = jnp.where(keep, pltpu.roll(A, d, 0), 1.0)
            U = A * Us + U
            A = A * As
        d *= 2
    return A, U


def _scan_bwd(A, U):
    n = U.shape[0]
    row = lax.broadcasted_iota(jnp.int32, U.shape, 0)
    d = 1
    while d < n:
        keep = row < n - d
        Us = jnp.where(keep, pltpu.roll(U, n - d, 0), 0.0)
        if A is None:
            U = U + Us
        else:
            As = jnp.where(keep, pltpu.roll(A, n - d, 0), 1.0)
            U = A * Us + U
            A = A * As
        d *= 2
    return A, U


def _softplus(z):
    return jnp.maximum(z, 0.0) + jnp.log(1.0 + jnp.exp(-jnp.abs(z)))


def _gelu_parts(g):
    k0 = math.sqrt(2.0 / math.pi)
    t = jnp.tanh(k0 * (g + 0.044715 * g * g * g))
    gel = 0.5 * g * (1.0 + t)
    dgel = 0.5 * (1.0 + t) + 0.5 * g * (1.0 - t * t) * k0 * (1.0 + 3.0 * 0.044715 * g * g)
    return gel, dgel


def _conv_fwd(P, cw):
    T = P.shape[0]
    bt = _bt(T)
    nb = T // bt

    def body(b_ref, c_ref, v_ref, w_ref, y_ref):
        w = w_ref[...]

        def step(b, carry):
            start = pl.multiple_of(b * bt, bt)
            rows = pl.ds(start, bt)
            m = c_ref[rows, :] * v_ref[rows, :]
            ext = jnp.concatenate([_halo_prev(c_ref, start, b) * _halo_prev(v_ref, start, b), m], axis=0)
            z = w[2:3, :] * m + w[1:2, :] * _down(ext, 1) + w[0:1, :] * _down(ext, 2)
            y_ref[rows, :] = b_ref[rows, :] * z
            return carry

        lax.fori_loop(0, nb, step, 0)

    def colspec(off):
        return _bs((T, 128), lambda c: (0, off + c))

    return pl.pallas_call(
        body, grid=(DC // 128,),
        in_specs=[colspec(0), colspec(4), colspec(8), _bs((8, 128), lambda c: (0, c))],
        out_specs=_bs((T, 128), lambda c: (0, c)), out_shape=S((T, DC), F32),
        compiler_params=_cp(), name="conv_fwd")(P, P, P, cw)


def _conv_bwd(P, cw, dy):
    T = P.shape[0]
    bt = _bt(T)
    nb = T // bt

    def body(b_ref, c_ref, v_ref, w_ref, dy_ref, db_ref, dc_ref, dv_ref, dw_ref):
        w = w_ref[...]

        def step(b, carry):
            a0, a1, a2 = carry
            start = pl.multiple_of(b * bt, bt)
            rows = pl.ds(start, bt)
            cb, cc, cv, dy = b_ref[rows, :], c_ref[rows, :], v_ref[rows, :], dy_ref[rows, :]
            m = cc * cv
            ext = jnp.concatenate([_halo_prev(c_ref, start, b) * _halo_prev(v_ref, start, b), m], axis=0)
            m1, m2 = _down(ext, 1), _down(ext, 2)
            z = w[2:3, :] * m + w[1:2, :] * m1 + w[0:1, :] * m2
            db_ref[rows, :] = dy * z
            dz = dy * cb
            extn = jnp.concatenate([dz, _halo_next(dy_ref, start, bt, b, nb) * _halo_next(b_ref, start, bt, b, nb)], axis=0)
            dm = w[2:3, :] * dz + w[1:2, :] * _up(extn, 1) + w[0:1, :] * _up(extn, 2)
            dc_ref[rows, :] = dm * cv
            dv_ref[rows, :] = dm * cc
            return (a0 + jnp.sum(dz * m2, axis=0, keepdims=True),
                    a1 + jnp.sum(dz * m1, axis=0, keepdims=True),
                    a2 + jnp.sum(dz * m, axis=0, keepdims=True))

        zero = jnp.zeros((1, 128), F32)
        a0, a1, a2 = lax.fori_loop(0, nb, step, (zero, zero, zero))
        dw_ref[...] = jnp.zeros((8, 128), F32)
        dw_ref[0:1, :] = a0
        dw_ref[1:2, :] = a1
        dw_ref[2:3, :] = a2

    def colspec(off):
        return _bs((T, 128), lambda c: (0, off + c))

    own = _bs((T, 128), lambda c: (0, c))
    return pl.pallas_call(
        body, grid=(DC // 128,),
        in_specs=[colspec(0), colspec(4), colspec(8), _bs((8, 128), lambda c: (0, c)), own],
        out_specs=[own, own, own, _bs((8, 128), lambda c: (0, c))],
        out_shape=[S((T, DC), F32)] * 3 + [S((8, DC), F32)], compiler_params=_cp(), name="conv_bwd")(P, P, P, cw, dy)


def _fgate_fwd(P, fb):
    T = P.shape[0]
    bt = _bt(T)
    nb = T // bt

    def body(f_ref, b_ref, c_ref):
        bias = b_ref[...]

        def step(b, carry):
            rows = pl.ds(pl.multiple_of(b * bt, bt), bt)
            logf = -_softplus(-(f_ref[rows, :] + bias))
            _, cs = _scan_fwd(None, logf)
            cs = cs + carry
            c_ref[rows, :] = cs
            return cs[bt - 1:bt, :]

        lax.fori_loop(0, nb, step, jnp.zeros((1, 128), F32))

    return pl.pallas_call(
        body, grid=(1,),
        in_specs=[_bs((T, 128), lambda i: (0, C_F)), _bs((1, 128), lambda i: (0, 0))],
        out_specs=_bs((T, 128), lambda i: (0, 0)), out_shape=S((T, 128), F32),
        compiler_params=_cp(), name="fgate_fwd")(P, fb)


def _fgate_bwd(P, fb, dcum):
    T = P.shape[0]
    bt = _bt(T)
    nb = T // bt

    def body(f_ref, b_ref, dc_ref, df_ref, db_ref):
        bias = b_ref[...]

        def step(i, carry):
            run, acc = carry
            b = nb - 1 - i
            rows = pl.ds(pl.multiple_of(b * bt, bt), bt)
            _, rs = _scan_bwd(None, dc_ref[rows, :])
            rs = rs + run
            df = rs * jax.nn.sigmoid(-(f_ref[rows, :] + bias))
            df_ref[rows, :] = df
            return rs[0:1, :], acc + jnp.sum(df, axis=0, keepdims=True)

        zero = jnp.zeros((1, 128), F32)
        _, acc = lax.fori_loop(0, nb, step, (zero, zero))
        db_ref[...] = acc

    return pl.pallas_call(
        body, grid=(1,),
        in_specs=[_bs((T, 128), lambda i: (0, C_F)), _bs((1, 128), lambda i: (0, 0)), _bs((T, 128), lambda i: (0, 0))],
        out_specs=[_bs((T, 128), lambda i: (0, 0)), _bs((1, 128), lambda i: (0, 0))],
        out_shape=[S((T, 128), F32), S((1, 128), F32)], compiler_params=_cp(), name="fgate_bwd")(P, fb, dcum)


def _att_tile(T):
    return _tile(T, (512,))


def _causal_mask(tq):
    return lax.broadcasted_iota(jnp.int32, (tq, tq), 1) <= lax.broadcasted_iota(jnp.int32, (tq, tq), 0)


def _attn_fwd(qkv, cumq, cumk):
    T = qkv.shape[0]
    tq = _att_tile(T)
    nq = T // tq

    def body(q_ref, k_ref, v_ref, cq_ref, ck_ref, o_ref, lse_ref):
        i = pl.program_id(1)
        q = q_ref[...]
        cq = cq_ref[...]

        def block(j, carry, diagonal):
            m_old, l_old, acc = carry
            rows = pl.ds(pl.multiple_of(j * tq, tq), tq)
            s = lax.dot_general(q, k_ref[rows, :], NT, preferred_element_type=F32)
            s = s * ATT_SCALE + cq - ck_ref[j]
            if diagonal:
                s = jnp.where(_causal_mask(tq), s, -jnp.inf)
            m_new = jnp.maximum(m_old, jnp.max(s, axis=1, keepdims=True))
            p = jnp.exp(s - m_new)
            alpha = jnp.exp(m_old - m_new)
            l_new = alpha * l_old + jnp.sum(p, axis=1, keepdims=True)
            acc = alpha * acc + jnp.dot(p.astype(BF16), v_ref[rows, :], preferred_element_type=F32)
            return m_new, l_new, acc

        init = (jnp.full((tq, 1), -jnp.inf, F32), jnp.zeros((tq, 1), F32), jnp.zeros((tq, HD), F32))
        carry = lax.fori_loop(0, i, lambda j, c: block(j, c, False), init)
        m, l, acc = block(i, carry, True)
        o_ref[...] = acc / l
        lse_ref[...] = m + jnp.log(l)

    return pl.pallas_call(
        body, grid=(NH, nq),
        in_specs=[_bs((tq, HD), lambda h, i: (i, h)),
                  _bs((T, HD), lambda h, i: (0, NH + h)),
                  _bs((T, HD), lambda h, i: (0, 2 * NH + h)),
                  _bs((None, tq, 1), lambda h, i: (h, i, 0)),
                  _bs((None, nq, 1, tq), lambda h, i: (h, 0, 0, 0))],
        out_specs=[_bs((tq, HD), lambda h, i: (i, h)), _bs((None, tq, 1), lambda h, i: (h, i, 0))],
        out_shape=[S((T, DA), F32), S((NH, T, 1), F32)],
        compiler_params=_cp(), name="attn_fwd")(qkv, qkv, qkv, cumq, cumk)


def _attn_bwd(qkv, cumq, cumk, lse, o, do):
    T = qkv.shape[0]
    tq = _att_tile(T)
    nq = T // tq

    def body(q_ref, k_ref, v_ref, cq_ref, ck_ref, lse_ref, o_ref, do_ref, dq_ref, dk_ref, dv_ref, dc_ref, dr_ref):
        j = pl.program_id(1)

        @pl.when(j == 0)
        def _():
            dq_ref[...] = jnp.zeros((T, HD), F32)
            dr_ref[...] = jnp.zeros((T, 1), F32)

        k = k_ref[...]
        v = v_ref[...]
        ck = ck_ref[...]

        def block(i, carry, diagonal):
            dk_acc, dv_acc, dc_acc = carry
            rows = pl.ds(pl.multiple_of(i * tq, tq), tq)
            q = q_ref[rows, :]
            do_f = do_ref[rows, :]
            dob = do_f.astype(BF16)
            s = lax.dot_general(q, k, NT, preferred_element_type=F32)
            p = jnp.exp(s * ATT_SCALE + cq_ref[rows, :] - ck - lse_ref[rows, :])
            if diagonal:
                p = jnp.where(_causal_mask(tq), p, 0.0)
            delta = jnp.sum(do_f * o_ref[rows, :], axis=1, keepdims=True)
            dp = lax.dot_general(dob, v, NT, preferred_element_type=F32)
            ds = p * (dp - delta)
            dsb = (ds * ATT_SCALE).astype(BF16)
            dq_ref[rows, :] += jnp.dot(dsb, k, preferred_element_type=F32)
            dr_ref[rows, :] += jnp.sum(ds, axis=1, keepdims=True)
            return (dk_acc + lax.dot_general(dsb, q, TN, preferred_element_type=F32),
                    dv_acc + lax.dot_general(p.astype(BF16), dob, TN, preferred_element_type=F32),
                    dc_acc - jnp.sum(ds, axis=0, keepdims=True))

        init = (jnp.zeros((tq, HD), F32), jnp.zeros((tq, HD), F32), jnp.zeros((1, tq), F32))
        carry = block(j, init, True)
        dk_acc, dv_acc, dc_acc = lax.fori_loop(j + 1, nq, lambda i, c: block(i, c, False), carry)
        dk_ref[...] = dk_acc
        dv_ref[...] = dv_acc
        dc_ref[...] = dc_acc

    def whole(col):
        return _bs((T, HD), lambda h, j: (0, col + h))

    qvec = _bs((None, T, 1), lambda h, j: (h, 0, 0))
    kv_out = _bs((tq, HD), lambda h, j: (j, h))
    return pl.pallas_call(
        body, grid=(NH, nq),
        in_specs=[whole(0), _bs((tq, HD), lambda h, j: (j, NH + h)), _bs((tq, HD), lambda h, j: (j, 2 * NH + h)),
                  qvec, _bs((None, None, 1, tq), lambda h, j: (h, j, 0, 0)), qvec, whole(0), whole(0)],
        out_specs=[whole(0), kv_out, kv_out, _bs((None, 1, tq), lambda h, j: (h, 0, j)), qvec],
        out_shape=[S((T, DA), F32)] * 3 + [S((NH, 1, T), F32), S((NH, T, 1), F32)],
        compiler_params=_cp(), name="attn_bwd")(qkv, qkv, qkv, cumq, cumk, lse, o, do)


def _lru_gates(xr, wa, wx, ba, bx, sp):
    xb = xr.astype(BF16)
    r = jax.nn.sigmoid(jnp.dot(xb, wa, preferred_element_type=F32) + ba)
    ig = jax.nn.sigmoid(jnp.dot(xb, wx, preferred_element_type=F32) + bx)
    log_a = -LRU_C * r * sp
    a = jnp.exp(log_a)
    th = jnp.tanh(log_a)
    om = -2.0 * th / (1.0 - th)
    mult = jnp.sqrt(om)
    return xb, r, ig, a, om, mult


def _lru_xr(lx_ref, cw, cb, start, b, rows):
    lx = lx_ref[rows, :]
    ext = jnp.concatenate([_halo_prev(lx_ref, start, b), lx], axis=0)
    return cw[3:4, :] * lx + cw[2:3, :] * _down(ext, 1) + cw[1:2, :] * _down(ext, 2) + cw[0:1, :] * _down(ext, 3) + cb


def _lru_fwd(P, lcw, vec, wa, wx):
    T = P.shape[0]
    bt = _bt(T)
    nb = T // bt

    def body(g_ref, lx_ref, cw_ref, vec_ref, wa_ref, wx_ref, y_ref, h_ref):
        cw = cw_ref[...]
        vec = vec_ref[...]
        wa = wa_ref[...].astype(BF16)
        wx = wx_ref[...].astype(BF16)
        sp = _softplus(-vec[3:4, :])

        def step(b, carry):
            start = pl.multiple_of(b * bt, bt)
            rows = pl.ds(start, bt)
            xr = _lru_xr(lx_ref, cw, vec[0:1, :], start, b, rows)
            _, _, ig, a, _, mult = _lru_gates(xr, wa, wx, vec[1:2, :], vec[2:3, :], sp)
            u = mult * (ig * xr)
            ac, hc = _scan_fwd(a, u)
            hb = hc + ac * carry
            h_ref[rows, :] = hb
            gel, _ = _gelu_parts(g_ref[rows, :])
            y_ref[rows, :] = gel * hb
            return hb[bt - 1:bt, :]

        lax.fori_loop(0, nb, step, jnp.zeros((1, 128), F32))

    own = _bs((T, 128), lambda c: (0, c))
    return pl.pallas_call(
        body, grid=(DL // 128,),
        in_specs=[_bs((T, 128), lambda c: (0, C_GATE + c)), _bs((T, 128), lambda c: (0, C_LX + c)),
                  _bs((8, 128), lambda c: (0, c)), _bs((8, 128), lambda c: (0, c)),
                  _bs((None, LB, LB), lambda c: (c, 0, 0)), _bs((None, LB, LB), lambda c: (c, 0, 0))],
        out_specs=[own, own], out_shape=[S((T, DL), F32)] * 2, compiler_params=_cp(), name="lru_fwd")(P, P, lcw, vec, wa, wx)


def _lru_bwd(P, lcw, vec, wa, wx, hst, dy):
    T = P.shape[0]
    bt = _bt(T)
    nb = T // bt

    def body(g_ref, lx_ref, cw_ref, vec_ref, wa_ref, wx_ref, h_ref, dy_ref,
             dg_ref, dlx_ref, sm_ref, dwa_ref, dwx_ref, dxr_s):
        cw = cw_ref[...]
        vec = vec_ref[...]
        wa = wa_ref[...].astype(BF16)
        wx = wx_ref[...].astype(BF16)
        lam = vec[3:4, :]
        sp = _softplus(-lam)
        dwa_ref[...] = jnp.zeros((LB, LB), F32)
        dwx_ref[...] = jnp.zeros((LB, LB), F32)
        zero = jnp.zeros((1, 128), F32)

        def step1(i, carry):
            wc, s_cb, s_ba, s_bx, s_sp = carry
            b = nb - 1 - i
            start = pl.multiple_of(b * bt, bt)
            rows = pl.ds(start, bt)
            xr = _lru_xr(lx_ref, cw, vec[0:1, :], start, b, rows)
            xb, r, ig, a, om, mult = _lru_gates(xr, wa, wx, vec[1:2, :], vec[2:3, :], sp)
            hb = h_ref[rows, :]
            dy = dy_ref[rows, :]
            gel, dgel = _gelu_parts(g_ref[rows, :])
            dg_ref[rows, :] = dy * hb * dgel
            dh = dy * gel
            ac, wcum = _scan_bwd(a, a * dh)
            w = wcum + ac * wc
            g = dh + _up(jnp.concatenate([w, jnp.broadcast_to(wc, (8, 128))], axis=0), 1)
            hprev = _down(jnp.concatenate([_halo_prev(h_ref, start, b), hb], axis=0), 1)
            da = g * hprev
            dmult = g * (ig * xr)
            dix = g * mult
            di = dix * xr
            dlog_a = da * a - dmult * ((1.0 - om) / mult)
            dr = dlog_a * (-LRU_C * sp)
            dpr = dr * r * (1.0 - r)
            dpi = di * ig * (1.0 - ig)
            dprb, dpib = dpr.astype(BF16), dpi.astype(BF16)
            dwa_ref[...] += lax.dot_general(xb, dprb, TN, preferred_element_type=F32)
            dwx_ref[...] += lax.dot_general(xb, dpib, TN, preferred_element_type=F32)
            dxr = (dix * ig + lax.dot_general(dprb, wa, NT, preferred_element_type=F32)
                   + lax.dot_general(dpib, wx, NT, preferred_element_type=F32))
            dxr_s[rows, :] = dxr
            return (w[0:1, :], s_cb + jnp.sum(dxr, axis=0, keepdims=True), s_ba + jnp.sum(dpr, axis=0, keepdims=True),
                    s_bx + jnp.sum(dpi, axis=0, keepdims=True), s_sp + jnp.sum(dlog_a * (-LRU_C * r), axis=0, keepdims=True))

        _, s_cb, s_ba, s_bx, s_sp = lax.fori_loop(0, nb, step1, (zero, zero, zero, zero, zero))

        def step2(b, carry):
            t0, t1, t2, t3 = carry
            start = pl.multiple_of(b * bt, bt)
            rows = pl.ds(start, bt)
            dxr = dxr_s[rows, :]
            extn = jnp.concatenate([dxr, _halo_next(dxr_s, start, bt, b, nb)], axis=0)
            dlx_ref[rows, :] = (cw[3:4, :] * dxr + cw[2:3, :] * _up(extn, 1) + cw[1:2, :] * _up(extn, 2)
                                + cw[0:1, :] * _up(extn, 3))
            lx = lx_ref[rows, :]
            ext = jnp.concatenate([_halo_prev(lx_ref, start, b), lx], axis=0)
            return (t0 + jnp.sum(dxr * _down(ext, 3), axis=0, keepdims=True),
                    t1 + jnp.sum(dxr * _down(ext, 2), axis=0, keepdims=True),
                    t2 + jnp.sum(dxr * _down(ext, 1), axis=0, keepdims=True),
                    t3 + jnp.sum(dxr * lx, axis=0, keepdims=True))

        t0, t1, t2, t3 = lax.fori_loop(0, nb, step2, (zero, zero, zero, zero))
        sm_ref[...] = jnp.zeros((16, 128), F32)
        for k, val in enumerate((t0, t1, t2, t3, s_cb, s_ba, s_bx, -s_sp * jax.nn.sigmoid(-lam))):
            sm_ref[k:k + 1, :] = val

    own = _bs((T, 128), lambda c: (0, c))
    wspec = _bs((None, LB, LB), lambda c: (c, 0, 0))
    return pl.pallas_call(
        body, grid=(DL // 128,),
        in_specs=[_bs((T, 128), lambda c: (0, C_GATE + c)), _bs((T, 128), lambda c: (0, C_LX + c)),
                  _bs((8, 128), lambda c: (0, c)), _bs((8, 128), lambda c: (0, c)), wspec, wspec, own, own],
        out_specs=[own, own, _bs((16, 128), lambda c: (0, c)), wspec, wspec],
        out_shape=[S((T, DL), F32)] * 2 + [S((16, DL), F32), S((4, LB, LB), F32), S((4, LB, LB), F32)],
        scratch_shapes=[pltpu.VMEM((T, 128), F32)], compiler_params=_cp(), name="lru_bwd")(P, P, lcw, vec, wa, wx, hst, dy)


_GROUPS = ((0, DC), (DC, DC + DA), (DC + DA, D))


def _gnorm_fwd(yc, ya, yl, gain):
    T = yc.shape[0]
    tb = _tile(T, (512,))

    def body(c_ref, a_ref, l_ref, g_ref, yn_ref, r0_ref, r1_ref, r2_ref):
        for (lo, hi), src, r_ref in zip(_GROUPS, (c_ref, a_ref, l_ref), (r0_ref, r1_ref, r2_ref)):
            yv = src[...]
            r = lax.rsqrt(jnp.mean(yv * yv, axis=1, keepdims=True) + EPS)
            yn_ref[:, lo:hi] = (yv * r * g_ref[:, lo:hi]).astype(BF16)
            r_ref[...] = r

    rs = _bs((tb, 1), lambda i: (i, 0))
    return pl.pallas_call(
        body, grid=(T // tb,),
        in_specs=[_bs((tb, DC), lambda i: (i, 0)), _bs((tb, DA), lambda i: (i, 0)), _bs((tb, DL), lambda i: (i, 0)),
                  _bs((1, D), lambda i: (0, 0))],
        out_specs=[_bs((tb, D), lambda i: (i, 0)), rs, rs, rs],
        out_shape=[S((T, D), BF16)] + [S((T, 1), F32)] * 3, compiler_params=_cp(), name="gnorm_fwd")(yc, ya, yl, gain)


def _gnorm_bwd(dyn, yc, ya, yl, r0, r1, r2, gain):
    T = yc.shape[0]
    tb = _tile(T, (512,))

    def body(d_ref, c_ref, a_ref, l_ref, r0_ref, r1_ref, r2_ref, g_ref, dc_ref, da_ref, dl_ref, dg_ref):
        i = pl.program_id(0)
        for (lo, hi), src, r_ref, dst in zip(_GROUPS, (c_ref, a_ref, l_ref), (r0_ref, r1_ref, r2_ref), (dc_ref, da_ref, dl_ref)):
            r = r_ref[...]
            yhat = src[...] * r
            dy = d_ref[:, lo:hi]
            dyh = dy * g_ref[:, lo:hi]
            m = jnp.mean(dyh * yhat, axis=1, keepdims=True)
            dst[...] = r * (dyh - yhat * m)
            part = jnp.sum(dy * yhat, axis=0, keepdims=True)

            @pl.when(i == 0)
            def _():
                dg_ref[:, lo:hi] = part

            @pl.when(i > 0)
            def _():
                dg_ref[:, lo:hi] += part

    rs = _bs((tb, 1), lambda i: (i, 0))
    specs = [_bs((tb, DC), lambda i: (i, 0)), _bs((tb, DA), lambda i: (i, 0)), _bs((tb, DL), lambda i: (i, 0))]
    return pl.pallas_call(
        body, grid=(T // tb,),
        in_specs=[_bs((tb, D), lambda i: (i, 0))] + specs + [rs, rs, rs, _bs((1, D), lambda i: (0, 0))],
        out_specs=specs + [_bs((1, D), lambda i: (0, 0))],
        out_shape=[S((T, DC), F32), S((T, DA), F32), S((T, DL), F32), S((1, D), F32)],
        compiler_params=_cp(), name="gnorm_bwd")(dyn, yc, ya, yl, r0, r1, r2, gain)


HBM = pl.BlockSpec(memory_space=pltpu.HBM)
N_BIG = 6


def _place():
    x, y, c = lax.axis_index("x"), lax.axis_index("y"), lax.axis_index("c")
    return x, y, c, 2 * x + y


def _peer(x, y, j):
    return x ^ ((j + 1) >> 1), y ^ ((j + 1) & 1)


SEM = pl.BlockSpec(memory_space=pltpu.SEMAPHORE)
ANY = pl.BlockSpec(memory_space=pl.ANY)
VM = pl.BlockSpec(memory_space=pltpu.VMEM)
EFFECT = pltpu.SideEffectType.DATAFLOW_SIDE_EFFECTING
N_AG = N_BIG + 1


def _hbm(a):
    return pltpu.with_memory_space_constraint(a, pltpu.HBM)


def _ag_copy(src, land, ssem, rsem, t, j, chip):
    x, y, c, _ = _place()
    px, py = _peer(x, y, j)
    if t == N_BIG:
        s_ref, d_ref = src[t], land[t].at[chip]
    else:
        rh = src[t].shape[0] // 2
        half = pl.ds(c * rh, rh)
        s_ref, d_ref = src[t].at[half], land[t].at[chip, half]
    return pltpu.make_async_remote_copy(src_ref=s_ref, dst_ref=d_ref, send_sem=ssem.at[3 * t + j], recv_sem=rsem.at[3 * t + j],
                                        device_id=(px, py, c), device_id_type=MESH)


def _ag_start(l, srcs, dep):
    n = N_AG

    def body(*refs):
        src = refs[:n]
        ssem, rsem = refs[2 * n + 1], refs[2 * n + 2]
        land = refs[3 * n + 3:4 * n + 3]
        token = refs[4 * n + 3]
        _, _, _, me = _place()
        for t in (N_BIG,) + tuple(range(N_BIG)):
            for j in range(3):
                _ag_copy(src, land, ssem, rsem, t, j, me).start()
        token[...] = jnp.zeros_like(token)

    lands = [lax.empty((4,) + a.shape, a.dtype) for a in srcs]
    dma = pltpu.SemaphoreType.DMA
    outs = pl.pallas_call(
        body, name=f"ag_start_{l}",
        out_shape=(dma((3 * n,)), dma((3 * n,))) + tuple(pltpu.HBM(a.shape, a.dtype) for a in list(srcs) + lands) + (S((8, 128), F32),),
        in_specs=[HBM] * (2 * n) + [ANY], out_specs=(SEM, SEM) + (HBM,) * (2 * n) + (VM,),
        input_output_aliases={i: 2 + i for i in range(2 * n)},
        compiler_params=pltpu.CompilerParams(has_side_effects=EFFECT),
    )(*[_hbm(a) for a in srcs], *[_hbm(a) for a in lands], dep)
    return outs[0], outs[1], outs[2:2 + n], outs[2 + n:2 + 2 * n], outs[-1]


def _ag_wait(l, ssem, rsem, srcs, lands, after):
    n = N_AG

    def body(*refs):
        src, land = refs[:n], refs[n:2 * n]
        ssem, rsem = refs[2 * n], refs[2 * n + 1]
        x, y, _, _ = _place()
        for t in range(n):
            for j in range(3):
                px, py = _peer(x, y, j)
                cp = _ag_copy(src, land, ssem, rsem, t, j, 2 * px + py)
                cp.wait_send()
                cp.wait_recv()

    outs = pl.pallas_call(
        body, name=f"ag_wait_{l}",
        out_shape=tuple(pltpu.HBM(a.shape, a.dtype) for a in list(srcs) + list(lands)),
        in_specs=[HBM] * (2 * n) + [SEM, SEM, ANY], out_specs=(HBM,) * (2 * n),
        input_output_aliases={i: i for i in range(2 * n)},
        compiler_params=pltpu.CompilerParams(has_side_effects=EFFECT),
    )(*srcs, *lands, ssem, rsem, after)
    return outs[:n], outs[n:]


def _ag_finish(srcs, lands, dep):
    n = N_AG

    def body(*refs):
        src = refs[:n]
        out = refs[2 * n + 1:3 * n + 1]
        dsend, drecv, osend, orecv = refs[-4:]
        x, y, c, me = _place()

        def d2d(t, j, cc):
            px, py = _peer(x, y, j)
            rh = src[t].shape[0] // 2
            part = out[t].at[2 * px + py, pl.ds(cc * rh, rh)]
            return pltpu.make_async_remote_copy(src_ref=part, dst_ref=part, send_sem=dsend.at[t, j], recv_sem=drecv.at[t, j],
                                                device_id=(x, y, 1 - c), device_id_type=MESH)

        own = [pltpu.make_async_remote_copy(src_ref=src[t], dst_ref=out[t].at[me], send_sem=osend.at[t], recv_sem=orecv.at[t],
                                            device_id=(x, y, 1 - c), device_id_type=MESH) for t in range(n)]
        for cp in own:
            cp.start()
        for t in range(N_BIG):
            for j in range(3):
                d2d(t, j, c).start()
        for t in range(N_BIG):
            for j in range(3):
                d2d(t, j, 1 - c).wait_recv()
                d2d(t, j, c).wait_send()
        for cp in own:
            cp.wait()

    dma = pltpu.SemaphoreType.DMA
    outs = pl.pallas_call(
        body, in_specs=[HBM] * (2 * n) + [ANY], out_specs=[HBM] * n,
        out_shape=[S(a.shape, a.dtype) for a in lands],
        input_output_aliases={n + i: i for i in range(n)},
        scratch_shapes=[dma((N_BIG, 3)), dma((N_BIG, 3)), dma((n,)), dma((n,))],
        name="ag_finish")(*srcs, *lands, dep)
    return outs[:N_BIG], outs[N_BIG]


def _rs_pair(grads):
    def body(*refs):
        g = refs[:N_BIG]
        out = refs[N_BIG:2 * N_BIG]
        ssem, rsem = refs[2 * N_BIG:]
        x, y, c, _ = _place()

        def cp(t):
            rh = g[t].shape[1] // 2
            return pltpu.make_async_remote_copy(
                src_ref=g[t].at[:, pl.ds((1 - c) * rh, rh), :], dst_ref=out[t],
                send_sem=ssem.at[t], recv_sem=rsem.at[t], device_id=(x, y, 1 - c), device_id_type=MESH)

        for t in range(N_BIG):
            cp(t).start()
        for t in range(N_BIG):
            cp(t).wait()

    dma = pltpu.SemaphoreType.DMA
    return pl.pallas_call(
        body, in_specs=[HBM] * N_BIG, out_specs=[HBM] * N_BIG,
        out_shape=[S((4, g.shape[1] // 2, g.shape[2]), g.dtype) for g in grads],
        scratch_shapes=[dma((N_BIG,)), dma((N_BIG,))], name="rs_pair")(*grads)


def _rs_copy(s, land, ssem, rsem, t, j):
    x, y, c, _ = _place()
    px, py = _peer(x, y, j)
    return pltpu.make_async_remote_copy(src_ref=s[t].at[2 * px + py], dst_ref=land[t].at[j],
                                        send_sem=ssem.at[3 * t + j], recv_sem=rsem.at[3 * t + j], device_id=(px, py, c), device_id_type=MESH)


def _rs_start(l, sums, dep):
    n = N_BIG

    def body(*refs):
        s = refs[:n]
        ssem, rsem = refs[2 * n + 1], refs[2 * n + 2]
        land = refs[3 * n + 3:4 * n + 3]
        token = refs[4 * n + 3]
        for t in range(n):
            for j in range(3):
                _rs_copy(s, land, ssem, rsem, t, j).start()
        token[...] = jnp.zeros_like(token)

    lands = [lax.empty((3,) + a.shape[1:], a.dtype) for a in sums]
    dma = pltpu.SemaphoreType.DMA
    outs = pl.pallas_call(
        body, name=f"rs_start_{l}",
        out_shape=(dma((3 * n,)), dma((3 * n,))) + tuple(pltpu.HBM(a.shape, a.dtype) for a in list(sums) + lands) + (S((8, 128), F32),),
        in_specs=[HBM] * (2 * n) + [ANY], out_specs=(SEM, SEM) + (HBM,) * (2 * n) + (VM,),
        input_output_aliases={i: 2 + i for i in range(2 * n)},
        compiler_params=pltpu.CompilerParams(has_side_effects=EFFECT),
    )(*[_hbm(a) for a in sums], *[_hbm(a) for a in lands], dep)
    return outs[0], outs[1], outs[2:2 + n], outs[2 + n:2 + 2 * n], outs[-1]


def _rs_wait(l, ssem, rsem, sums, lands, after):
    n = N_BIG

    def body(*refs):
        s, land = refs[:n], refs[n:2 * n]
        ssem, rsem = refs[2 * n], refs[2 * n + 1]
        for t in range(n):
            for j in range(3):
                cp = _rs_copy(s, land, ssem, rsem, t, j)
                cp.wait_send()
                cp.wait_recv()

    outs = pl.pallas_call(
        body, name=f"rs_wait_{l}",
        out_shape=tuple(pltpu.HBM(a.shape, a.dtype) for a in list(sums) + list(lands)),
        in_specs=[HBM] * (2 * n) + [SEM, SEM, ANY], out_specs=(HBM,) * (2 * n),
        input_output_aliases={i: i for i in range(2 * n)},
        compiler_params=pltpu.CompilerParams(has_side_effects=EFFECT),
    )(*sums, *lands, ssem, rsem, after)
    return outs[n:]


def _rs_join(halves, lo, hi, dep):
    def body(*refs):
        h = refs[:N_BIG]
        out = refs[N_BIG + 1:2 * N_BIG + 1]
        ssem, rsem = refs[2 * N_BIG + 1:]
        x, y, c, _ = _place()

        def cp(t):
            return pltpu.make_async_remote_copy(
                src_ref=h[t].at[pl.ds(lo, hi - lo)], dst_ref=out[t], send_sem=ssem.at[t], recv_sem=rsem.at[t],
                device_id=(x, y, 1 - c), device_id_type=MESH)

        for t in range(N_BIG):
            cp(t).start()
        for t in range(N_BIG):
            cp(t).wait()

    dma = pltpu.SemaphoreType.DMA
    return pl.pallas_call(
        body, in_specs=[HBM] * N_BIG + [ANY], out_specs=[HBM] * N_BIG,
        out_shape=[S((hi - lo,) + h.shape[1:], h.dtype) for h in halves],
        scratch_shapes=[dma((N_BIG,)), dma((N_BIG,))], name="rs_join")(*halves, dep)


def _all_reduce_small(pack):
    R = pack.shape[0]
    rb = _tile(R, (512, 256, 128, 8))

    def body(x_ref, all_ref, sum_ref, send_sems, recv_sems, local_sem):
        x, y, c = lax.axis_index("x"), lax.axis_index("y"), lax.axis_index("c")
        me, sibling = (x, y, c), (x, y, 1 - c)
        chips = [(1 - x, y), (x, 1 - y), (1 - x, 1 - y)]

        def rows(px, py, pc):
            return all_ref.at[pl.ds((4 * px + 2 * py + pc) * R, R), :]

        def copy(k, block, to, src=None):
            return pltpu.make_async_remote_copy(
                src_ref=rows(*block) if src is None else src, dst_ref=rows(*block),
                send_sem=send_sems.at[k], recv_sem=recv_sems.at[k], device_id=to, device_id_type=MESH)

        mine = pltpu.make_async_copy(x_ref, rows(*me), local_sem)
        mine.start()
        first = [copy(0, me, sibling, src=x_ref)]
        first += [copy(1 + j, me, (*chip, c), src=x_ref) for j, chip in enumerate(chips)]
        for cp in first:
            cp.start()
        passed = [copy(4 + j, (*chip, c), sibling) for j, chip in enumerate(chips)]
        for j, chip in enumerate(chips):
            copy(1 + j, (*chip, c), me).wait_recv()
            passed[j].start()
        copy(0, sibling, me).wait_recv()
        for j, chip in enumerate(chips):
            copy(4 + j, (*chip, 1 - c), me).wait_recv()
        for cp in first + passed:
            cp.wait_send()
        mine.wait()

        def step(b, carry):
            off = pl.multiple_of(b * rb, rb)
            acc = all_ref[pl.ds(off, rb), :]
            for k in range(1, 8):
                acc = acc + all_ref[pl.ds(pl.multiple_of(k * R + off, 8), rb), :]
            sum_ref[pl.ds(off, rb), :] = acc
            return carry

        lax.fori_loop(0, R // rb, step, 0)

    vm = pl.BlockSpec(memory_space=pltpu.VMEM)
    dma = pltpu.SemaphoreType.DMA
    _, total = pl.pallas_call(
        body, in_specs=[vm], out_specs=[vm, vm],
        out_shape=[S((8 * R, 128), F32), S((R, 128), F32)],
        scratch_shapes=[dma((7,)), dma((7,)), dma],
        compiler_params=_cp(), name="allreduce_small")(pack)
    return total


def _row_tile(rh, cc, tile_bytes=3 * 1024 * 1024 // 2):
    for t in (512, 256, 128, 64, 32, 16):
        if rh % t == 0 and t * cc * 4 <= tile_bytes:
            return t
    return 16


def _my_chip():
    return 2 * lax.axis_index("x") + lax.axis_index("y")


def _pair_sum(g, recv):
    _, r, cc = g.shape
    rh = r // 2
    tb = _row_tile(rh, cc)
    nbh = rh // tb

    def body(g_ref, r_ref, o_ref):
        o_ref[...] = (g_ref[...].astype(F32) + r_ref[...].astype(F32)).astype(BF16)

    mine = _bs((None, tb, cc), lambda k, i: (k, lax.axis_index("c") * nbh + i, 0))
    plain = _bs((None, tb, cc), lambda k, i: (k, i, 0))
    return pl.pallas_call(body, grid=(4, nbh), in_specs=[mine, plain], out_specs=plain,
                          out_shape=S((4, rh, cc), BF16), compiler_params=_cp(), name="rs_pair_sum")(g, recv)


def _owner_sum(g, recv, ici, acc, l):
    _, r, cc = g.shape
    rh = r // 2
    tb = _row_tile(rh, cc)
    nbh = rh // tb

    def body(g_ref, r_ref, i0_ref, i1_ref, i2_ref, acc_ref, o_ref):
        s = g_ref[...].astype(F32) + r_ref[...].astype(F32)
        o_ref[...] = s + i0_ref[...].astype(F32) + i1_ref[...].astype(F32) + i2_ref[...].astype(F32)

    def slot(j):
        return _bs((None, tb, cc), lambda i: (j, i, 0))

    return pl.pallas_call(
        body, grid=(nbh,),
        in_specs=[_bs((None, tb, cc), lambda i: (_my_chip(), lax.axis_index("c") * nbh + i, 0)),
                  _bs((None, tb, cc), lambda i: (_my_chip(), i, 0)),
                  slot(0), slot(1), slot(2), pl.BlockSpec(memory_space=pl.ANY)],
        out_specs=_bs((None, tb, cc), lambda i: (l, i, 0)),
        out_shape=S(acc.shape, F32), input_output_aliases={5: 0},
        compiler_params=_cp(), name="rs_owner_sum")(g, recv, ici, ici, ici, acc)


def _adam_math(w, g, m, v):
    m = B1 * m + (1.0 - B1) * g
    v = B2 * v + (1.0 - B2) * (g * g)
    m_hat = m / (1.0 - B1 ** STEP)
    v_hat = v / (1.0 - B2 ** STEP)
    delta = -LR * (m_hat / (jnp.sqrt(v_hat) + AEPS) + WD * w)
    return delta, m, v


def _adamw_big(w, g_mine, g_sib, m, v, lo, hi, prev):
    L, r, cc = w.shape
    rh = r // 2
    tb = _row_tile(rh, cc)
    nbh = rh // tb

    def body(w_ref, gm_ref, gs_ref, m_ref, v_ref, *rest):
        go_ref, d_ref, mo_ref, vo_ref = rest[-4:]
        mine = pl.program_id(1) == lax.axis_index("c")
        g = jnp.where(mine, gm_ref[...], gs_ref[...])
        d, m, v = _adam_math(w_ref[...], g, m_ref[...], v_ref[...])
        go_ref[...] = g
        d_ref[...] = d
        mo_ref[...] = m
        vo_ref[...] = v

    def mine_map(l, hf, i):
        c = lax.axis_index("c")
        return (l + lo, jnp.where(hf == c, i, jnp.where(c == 0, nbh - 1, 0)), 0)

    def sib_map(l, hf, i):
        c = lax.axis_index("c")
        return (l, jnp.where(hf != c, i, jnp.where(c == 0, 0, nbh - 1)), 0)

    full = _bs((None, tb, cc), lambda l, hf, i: (l + lo, hf * nbh + i, 0))
    extra = [] if prev is None else list(prev)
    return pl.pallas_call(
        body, grid=(hi - lo, 2, nbh),
        in_specs=[full, _bs((None, tb, cc), mine_map), _bs((None, tb, cc), sib_map), full, full]
        + [pl.BlockSpec(memory_space=pl.ANY)] * len(extra),
        out_specs=[full] * 4, out_shape=[S(w.shape, F32)] * 4,
        input_output_aliases={5 + k: k for k in range(len(extra))},
        compiler_params=_cp(), name="adamw_big")(w, g_mine, g_sib, m, v, *extra)


def _adamw_small(w, g, m, v):
    R = w.shape[0]
    tb = _tile(R, (512, 256, 128, 8))

    def body(w_ref, g_ref, m_ref, v_ref, d_ref, mo_ref, vo_ref):
        d, m, v = _adam_math(w_ref[...], g_ref[...], m_ref[...], v_ref[...])
        d_ref[...] = d
        mo_ref[...] = m
        vo_ref[...] = v

    spec = _bs((tb, 128), lambda i: (i, 0))
    return pl.pallas_call(body, grid=(R // tb,), in_specs=[spec] * 4, out_specs=[spec] * 3,
                          out_shape=[S((R, 128), F32)] * 3, compiler_params=_cp(), name="adamw_small")(w, g, m, v)


def _mix_pad(w):
    return jnp.concatenate([w[:, :4608], w[:, 4616:DIN], w[:, 4608:4616], jnp.zeros((D, PW - DIN), w.dtype)], axis=1)


def _mix_unpad(g):
    return jnp.concatenate([g[:, :4608], g[:, 5632:5640], g[:, 4608:5632]], axis=1)


def _pack(parts):
    flat = jnp.concatenate([p.reshape(-1).astype(F32) for p in parts])
    n = flat.shape[0]
    total = -(-n // (512 * 128)) * (512 * 128)
    return jnp.pad(flat, (0, total - n)).reshape(total // 128, 128)


def _unpack(pack, shapes):
    flat = pack.reshape(-1)
    out, off = [], 0
    for s in shapes:
        n = math.prod(s)
        out.append(flat[off:off + n].reshape(s))
        off += n
    return out


def _ffn_forward(x, gain, win, wout):
    h, rstd = _rms_fwd(x, gain)
    zg, zu, act = _ffn_in(h, win)
    y = _ffn_out(act, wout, x)
    return y, (x, h, rstd, zg, zu, act)


def _ffn_backward(dy, dyb, saved, gain, win, wout, dep):
    x, h, rstd, zg, zu, act = saved
    dzg, dzu = _ffn_bwd_dz(dyb, wout, zg, zu, dep)
    dwout = _mm_tn("ffn_bwd_dwout", act, dyb, scale=0.5, tm=512, tn=1024)
    dwin = _ffn_bwd_dwin(h, dzg, dzu)
    dh = _ffn_bwd_dh(dzg, dzu, win)
    dx, dxb, dgain = _rms_bwd(dh, x, rstd, gain, dy)
    return dx, dxb, dgain, dwin, dwout


def _mixer_forward(x, p):
    T = x.shape[0]
    h, rstd = _rms_fwd(x, p["norm_mix"])
    tm = _tile(T, (1024, 512))
    tn = 1152
    P = _mm("mix_in", h, p["wmix"],
            _bs((tm, D), lambda i, j, k: (i, 0)), _bs((D, tn), lambda i, j, k: (0, j)),
            _bs((tm, tn), lambda i, j, k: (i, j)), S((T, PW), F32), (T // tm, PW // tn, 1), NN, 1, (tm, tn))
    yc = _conv_fwd(P, p["cw"])
    cum = _fgate_fwd(P, p["fb"])
    cumt = cum[:, :NH].T
    tq = _att_tile(T)
    cumq, cumk = cumt.reshape(NH, T, 1), cumt.reshape(NH, T // tq, 1, tq)
    qkv = P[:, C_Q * 128:C_GATE * 128].astype(BF16)
    ya, lse = _attn_fwd(qkv, cumq, cumk)
    yl, hst = _lru_fwd(P, p["lcw"], p["lvec"], p["lru_w_a"], p["lru_w_x"])
    yn, r0, r1, r2 = _gnorm_fwd(yc, ya, yl, p["mix_out_norm"])
    tk = 512
    y = _mm("mix_out", yn, p["wo"],
            _bs((tm, tk), lambda i, j, k: (i, k)), _bs((tk, 1024), lambda i, j, k: (k, j)),
            _bs((tm, 1024), lambda i, j, k: (i, j)), S((T, D), F32), (T // tm, D // 1024, D // tk), NN, D // tk, (tm, 1024),
            res=x, r_spec=_bs((tm, 1024), lambda i, j, k: (i, j)))
    return y, (x, h, rstd, P, qkv, cumq, cumk, lse, yc, ya, yl, hst, yn, r0, r1, r2)


def _mixer_backward(dy, dyb, saved, p):
    x, h, rstd, P, qkv, cumq, cumk, lse, yc, ya, yl, hst, yn, r0, r1, r2 = saved
    T = x.shape[0]
    dyn = _mm_nt_full("mix_bwd_dyn", dyb, p["wo"], 512)
    dwo = _mm_tn("mix_bwd_dwo", yn, dyb, tm=512, tn=1024)
    dyc, dya, dyl, dgn = _gnorm_bwd(dyn, yc, ya, yl, r0, r1, r2, p["mix_out_norm"])
    dcb, dcc, dcv, dcw = _conv_bwd(P, p["cw"], dyc)
    dq, dk, dv, dck, dcq = _attn_bwd(qkv, cumq, cumk, lse, ya, dya)
    dcum = jnp.pad((dck.reshape(NH, T) + dcq.reshape(NH, T)).T, ((0, 0), (0, 128 - NH)))
    df, dfb = _fgate_bwd(P, p["fb"], dcum)
    dgate, dlx, lsm, dwa, dwx = _lru_bwd(P, p["lcw"], p["lvec"], p["lru_w_a"], p["lru_w_x"], hst, dyl)
    dP = jnp.concatenate([dcb, dcc, dcv, dq, dk, dv, dgate, dlx, df], axis=1).astype(BF16)
    tm = _tile(T, (1024, 512))
    tk = 1152
    nk = PW // tk
    dh = _mm("mix_bwd_dh", dP, p["wmix"],
             _bs((tm, tk), lambda i, j, k: (i, k)), _bs((1024, tk), lambda i, j, k: (j, k)),
             _bs((tm, 1024), lambda i, j, k: (i, j)), S((T, D), F32), (T // tm, D // 1024, nk), NT, nk, (tm, 1024))
    dwmix = _mm_tn("mix_bwd_dwmix", h, dP, tm=1024, tn=1152)
    dx, dxb, dgm = _rms_bwd(dh, x, rstd, p["norm_mix"], dy)
    small = dict(norm_mix=dgm[0], mix_out_norm=dgn[0], conv_w=dcw[:3], fgate_b=dfb[0, :NH], lru_conv_w=lsm[:4],
                 lru_conv_b=lsm[4], lru_b_a=lsm[5], lru_b_x=lsm[6], lru_lambda=lsm[7], lru_w_a=dwa, lru_w_x=dwx)
    return dx, dxb, small, dwmix, dwo


BIG =("ffn1_w_in", "ffn1_w_out", "mix_w_in", "mix_w_out", "ffn2_w_in", "ffn2_w_out")
SMALL = ("norm_ffn1", "norm_mix", "conv_w", "fgate_b", "lru_conv_w", "lru_conv_b", "lru_w_a", "lru_b_a", "lru_w_x",
         "lru_b_x", "lru_lambda", "mix_out_norm", "norm_ffn2", "final_norm")
WEIGHTS = ("norm_ffn1", "ffn1_w_in", "ffn1_w_out", "norm_mix", "mix_w_in", "conv_w", "fgate_b", "lru_conv_w", "lru_conv_b",
           "lru_w_a", "lru_b_a", "lru_w_x", "lru_b_x", "lru_lambda", "mix_out_norm", "mix_w_out", "norm_ffn2", "ffn2_w_in",
           "ffn2_w_out", "final_norm")


def _step(args):
    xx, yy, cc_ = lax.axis_index("x"), lax.axis_index("y"), lax.axis_index("c")
    me = 2 * xx + yy
    x0 = args["x"][0]
    tgt = args["loss_target"][0]
    T = x0.shape[0]
    L = args["norm_ffn1"].shape[0]

    def ag_sources(l):
        small = jnp.concatenate([args["conv_w"][l], args["lru_conv_w"][l], jnp.zeros((1, 128), F32)], axis=0)
        return [args[n][l].astype(BF16) for n in BIG] + [small]

    def layer_params(l, gat, gsm):
        w1i, w1o, wmx, wo, w2i, w2o = gat
        cwl = gsm.transpose(1, 0, 2).reshape(8, 4 * 128)
        return dict(
            w1i=w1i, w1o=w1o.reshape(-1, D), w2i=w2i, w2o=w2o.reshape(-1, D), wo=wo.reshape(D, D),
            wmix=_mix_pad(wmx.transpose(1, 0, 2).reshape(D, DIN)),
            cw=jnp.concatenate([cwl[:3], jnp.zeros((5, DC), F32)], axis=0),
            lcw=jnp.concatenate([cwl[3:7], jnp.zeros((4, DL), F32)], axis=0),
            fb=jnp.pad(args["fgate_b"][l], (0, 128 - NH)).reshape(1, 128),
            lvec=jnp.concatenate([args["lru_conv_b"][l][None], args["lru_b_a"][l][None], args["lru_b_x"][l][None],
                                  args["lru_lambda"][l][None], jnp.zeros((4, DL), F32)], axis=0),
            lru_w_a=args["lru_w_a"][l], lru_w_x=args["lru_w_x"][l],
            norm_ffn1=args["norm_ffn1"][l][None], norm_mix=args["norm_mix"][l][None],
            mix_out_norm=args["mix_out_norm"][l][None], norm_ffn2=args["norm_ffn2"][l][None])

    xs = x0
    saved, layers = [], []
    flight = _ag_start(0, ag_sources(0), x0)
    for l in range(L):
        ssem, rsem, srcs, lands, token = flight
        srcs, lands = _ag_wait(l, ssem, rsem, srcs, lands, xs)
        if l + 1 < L:
            flight = _ag_start(l + 1, ag_sources(l + 1), lands[0])
            token = flight[4]
        gat, gsm = _ag_finish(srcs, lands, token)
        p = layer_params(l, gat, gsm)
        x1, s1 = _ffn_forward(xs, p["norm_ffn1"], p["w1i"], p["w1o"])
        x2, s2 = _mixer_forward(x1, p)
        x3, s3 = _ffn_forward(x2, p["norm_ffn2"], p["w2i"], p["w2o"])
        saved.append((s1, s2, s3))
        layers.append(p)
        xs = x3
    lpart, dx, dxb, dfinal = _loss_head(xs, args["final_norm"][None], tgt)
    loss = lax.psum(lpart[0, 0], ("x", "y", "c"))

    acc = None
    small_grads = [None] * L
    pending = None
    dep = lpart

    def owner(pending, acc, after):
        lp, grads, recv, (ssem, rsem, sums, lands, _) = pending
        ici = _rs_wait(lp, ssem, rsem, sums, lands, after)
        return [_owner_sum(g, r, i3, a, lp) for g, r, i3, a in zip(grads, recv, ici, acc)]

    for l in reversed(range(L)):
        p = layers[l]
        s1, s2, s3 = saved[l]
        dx, dxb, dg2, dw2i, dw2o = _ffn_backward(dx, dxb, s3, p["norm_ffn2"], p["w2i"], p["w2o"], dep)
        dx, dxb, sm, dwmix, dwo = _mixer_backward(dx, dxb, s2, p)
        dx, dxb, dg1, dw1i, dw1o = _ffn_backward(dx, dxb, s1, p["norm_ffn1"], p["w1i"], p["w1o"], dep)
        sm["norm_ffn1"] = dg1[0]
        sm["norm_ffn2"] = dg2[0]
        small_grads[l] = sm
        F4 = dw1o.shape[0] // 4
        grads = [dw1i, dw1o.reshape(4, F4, D), _mix_unpad(dwmix).reshape(D, 4, DIN // 4).transpose(1, 0, 2),
                 dwo.reshape(4, D // 4, D), dw2i, dw2o.reshape(4, F4, D)]
        if acc is None:
            acc = [jnp.zeros((L, g.shape[1] // 2, g.shape[2]), F32) for g in grads]
        if pending is not None:
            acc = owner(pending, acc, dx)
        recv = _rs_pair(grads)
        sums = [_pair_sum(g, r) for g, r in zip(grads, recv)]
        started = _rs_start(l, sums, dx)
        pending = (l, grads, recv, started)
        dep = started[4]
    res = [None] * N_BIG
    after = dx
    if L > 1:
        sib = _rs_join(acc, 1, L, pending[3][4])
        for k, n in enumerate(BIG):
            res[k] = _adamw_big(args[n], acc[k], sib[k], args["m_" + n], args["v_" + n], 1, L, None)
        after = res[N_BIG - 1][0]
    acc = owner(pending, acc, after)
    sib = _rs_join(acc, 0, 1, dx)
    out = {"loss": loss, "grad_x": dx[None]}
    for k, n in enumerate(BIG):
        g, d, m, v = _adamw_big(args[n], acc[k], sib[k], args["m_" + n], args["v_" + n], 0, 1, res[k])
        out["grad_" + n], out["delta_" + n], out["new_m_" + n], out["new_v_" + n] = g, d, m, v

    full = {n: (dfinal[0] if n == "final_norm" else jnp.stack([small_grads[l][n] for l in range(L)])) for n in SMALL}
    shapes = [full[n].shape for n in SMALL]
    red = dict(zip(SMALL, _unpack(_all_reduce_small(_pack([full[n] for n in SMALL])), shapes)))
    for n in ("conv_w", "lru_conv_w"):
        red[n] = lax.dynamic_slice_in_dim(red[n], me * 128, 128, axis=2)
    oshapes = [args[n].shape for n in SMALL]
    d, m, v = _adamw_small(_pack([args[n] for n in SMALL]), _pack([red[n] for n in SMALL]),
                           _pack([args["m_" + n] for n in SMALL]), _pack([args["v_" + n] for n in SMALL]))
    for n, gg, dd, mm, vv in zip(SMALL, [red[n] for n in SMALL], _unpack(d, oshapes), _unpack(m, oshapes), _unpack(v, oshapes)):
        out["grad_" + n], out["delta_" + n], out["new_m_" + n], out["new_v_" + n] = gg, dd, mm, vv
    return out


def kernel(x, norm_ffn1, ffn1_w_in, ffn1_w_out, norm_mix, mix_w_in, conv_w, fgate_b, lru_conv_w, lru_conv_b, lru_w_a, lru_b_a, lru_w_x, lru_b_x, lru_lambda, mix_out_norm, mix_w_out, norm_ffn2, ffn2_w_in, ffn2_w_out, final_norm, loss_target, m_norm_ffn1, m_ffn1_w_in, m_ffn1_w_out, m_norm_mix, m_mix_w_in, m_conv_w, m_fgate_b, m_lru_conv_w, m_lru_conv_b, m_lru_w_a, m_lru_b_a, m_lru_w_x, m_lru_b_x, m_lru_lambda, m_mix_out_norm, m_mix_w_out, m_norm_ffn2, m_ffn2_w_in, m_ffn2_w_out, m_final_norm, v_norm_ffn1, v_ffn1_w_in, v_ffn1_w_out, v_norm_mix, v_mix_w_in, v_conv_w, v_fgate_b, v_lru_conv_w, v_lru_conv_b, v_lru_w_a, v_lru_b_a, v_lru_w_x, v_lru_b_x, v_lru_lambda, v_mix_out_norm, v_mix_w_out, v_norm_ffn2, v_ffn2_w_in, v_ffn2_w_out, v_final_norm):
    args = dict(locals())
    out = _step(args)
    res = [out["loss"], out["grad_x"]]
    for prefix in ("grad_", "delta_", "new_m_", "new_v_"):
        res += [out[prefix + n] for n in WEIGHTS]
    return tuple(res)
```

```python
import functools
import math

import jax
import jax.numpy as jnp
from jax import lax
from jax.experimental import pallas as pl
from jax.experimental.pallas import tpu as pltpu

F32 = jnp.float32
BF16 = jnp.bfloat16
S = jax.ShapeDtypeStruct
MESH = pl.DeviceIdType.MESH

D = 2048
DC = 512
DA = 1024
NH = 8
HD = 128
DL = 512
LB = 128
DIN = 5640
PW = 5760
C_Q, C_K, C_V = 12, 20, 28
C_GATE, C_LX, C_F = 36, 40, 44
EPS = 1e-6
LRU_C = 8.0
ATT_SCALE = HD ** -0.5
LR, B1, B2, AEPS, WD, STEP = 0.001, 0.9, 0.999, 1e-08, 0.01, 10
VMEM_LIMIT = 56 * 1024 * 1024

NT = (((1,), (1,)), ((), ()))
TN = (((0,), (0,)), ((), ()))
NN = (((1,), (0,)), ((), ()))


def _cp():
    return pltpu.CompilerParams(vmem_limit_bytes=VMEM_LIMIT)


def _bs(shape, fn):
    return pl.BlockSpec(shape, fn)


def _mm(name, a, b, a_spec, b_spec, o_spec, o_shape, grid, dims, nk, acc_tile, scale=1.0, res=None, r_spec=None):
    has_res = res is not None

    def body(*refs):
        if has_res:
            a_ref, b_ref, r_ref, o_ref = refs[:4]
            rest = refs[4:]
        else:
            a_ref, b_ref, o_ref = refs[:3]
            rest = refs[3:]
        prod = lax.dot_general(a_ref[...].astype(BF16), b_ref[...].astype(BF16), dims, preferred_element_type=F32)

        def finish(acc):
            if scale != 1.0:
                acc = acc * scale
            if has_res:
                acc = r_ref[...] + acc
            o_ref[...] = acc.astype(o_ref.dtype)

        if nk == 1:
            finish(prod)
        else:
            acc_ref = rest[0]
            k = pl.program_id(2)

            @pl.when(k == 0)
            def _():
                acc_ref[...] = prod

            @pl.when(k > 0)
            def _():
                acc_ref[...] += prod

            @pl.when(k == nk - 1)
            def _():
                finish(acc_ref[...])

    in_specs = [a_spec, b_spec] + ([r_spec] if has_res else [])
    args = (a, b) + ((res,) if has_res else ())
    scratch = [pltpu.VMEM(acc_tile, F32)] if nk > 1 else []
    return pl.pallas_call(body, grid=grid, in_specs=in_specs, out_specs=o_spec, out_shape=o_shape,
                          scratch_shapes=scratch, compiler_params=_cp(), name=name)(*args)


def _tile(n, pref):
    for t in pref:
        if n % t == 0:
            return t
    return n


def _rms_fwd(x, gain):
    T = x.shape[0]
    tb = _tile(T, (512,))

    def body(x_ref, g_ref, h_ref, r_ref):
        xv = x_ref[...]
        r = lax.rsqrt(jnp.mean(xv * xv, axis=1, keepdims=True) + EPS)
        h_ref[...] = (xv * r * g_ref[...]).astype(BF16)
        r_ref[...] = r

    return pl.pallas_call(
        body, grid=(T // tb,),
        in_specs=[_bs((tb, D), lambda i: (i, 0)), _bs((1, D), lambda i: (0, 0))],
        out_specs=[_bs((tb, D), lambda i: (i, 0)), _bs((tb, 1), lambda i: (i, 0))],
        out_shape=[S((T, D), BF16), S((T, 1), F32)], compiler_params=_cp(), name="rms_fwd")(x, gain)


def _rms_bwd(dh, x, rstd, gain, dres):
    T = x.shape[0]
    tb = _tile(T, (512,))

    def body(dh_ref, x_ref, r_ref, g_ref, dres_ref, dx_ref, dxb_ref, dg_ref):
        i = pl.program_id(0)
        r = r_ref[...]
        xhat = x_ref[...] * r
        dh = dh_ref[...]
        dxh = dh * g_ref[...]
        m = jnp.mean(dxh * xhat, axis=1, keepdims=True)
        dx = dres_ref[...] + r * (dxh - xhat * m)
        dx_ref[...] = dx
        dxb_ref[...] = dx.astype(BF16)
        part = jnp.sum(dh * xhat, axis=0, keepdims=True)

        @pl.when(i == 0)
        def _():
            dg_ref[...] = part

        @pl.when(i > 0)
        def _():
            dg_ref[...] += part

    row = _bs((tb, D), lambda i: (i, 0))
    return pl.pallas_call(
        body, grid=(T // tb,),
        in_specs=[row, row, _bs((tb, 1), lambda i: (i, 0)), _bs((1, D), lambda i: (0, 0)), row],
        out_specs=[row, row, _bs((1, D), lambda i: (0, 0))],
        out_shape=[S((T, D), F32), S((T, D), BF16), S((1, D), F32)], compiler_params=_cp(), name="rms_bwd")(dh, x, rstd, gain, dres)


def _loss_head(x, gain, tgt):
    T = x.shape[0]
    tb = _tile(T, (512,))

    def body(x_ref, g_ref, t_ref, l_ref, dx_ref, dxb_ref, dg_ref):
        i = pl.program_id(0)
        xv = x_ref[...]
        g = g_ref[...]
        r = lax.rsqrt(jnp.mean(xv * xv, axis=1, keepdims=True) + EPS)
        xhat = xv * r
        e = xhat * g - t_ref[...]
        lpart = 0.5 * jnp.sum(jnp.sum(e * e, axis=1, keepdims=True), axis=0, keepdims=True) * (1.0 / D)
        dy = e * (1.0 / D)
        dxh = dy * g
        m = jnp.mean(dxh * xhat, axis=1, keepdims=True)
        dx = r * (dxh - xhat * m)
        dx_ref[...] = dx
        dxb_ref[...] = dx.astype(BF16)
        gpart = jnp.sum(dy * xhat, axis=0, keepdims=True)
        lrow = jnp.broadcast_to(lpart, (1, 128))

        @pl.when(i == 0)
        def _():
            dg_ref[...] = gpart
            l_ref[...] = lrow

        @pl.when(i > 0)
        def _():
            dg_ref[...] += gpart
            l_ref[...] += lrow

    row = _bs((tb, D), lambda i: (i, 0))
    return pl.pallas_call(
        body, grid=(T // tb,),
        in_specs=[row, _bs((1, D), lambda i: (0, 0)), row],
        out_specs=[_bs((1, 128), lambda i: (0, 0)), row, row, _bs((1, D), lambda i: (0, 0))],
        out_shape=[S((1, 128), F32), S((T, D), F32), S((T, D), BF16), S((1, D), F32)],
        compiler_params=_cp(), name="loss_head")(x, gain, tgt)


def _sigmoid(z):
    return 0.5 * jnp.tanh(0.5 * z) + 0.5


def _ffn_in(h, win):
    T = h.shape[0]
    Fs = win.shape[2]
    F = 2 * Fs
    tn = _tile(Fs, (256, 128))
    nb = Fs // tn
    tm = _tile(T, (1024, 512))

    def body(h_ref, wg_ref, wu_ref, zg_ref, zu_ref, a_ref):
        hv = h_ref[...]
        zg = jnp.dot(hv, wg_ref[...], preferred_element_type=F32)
        zu = jnp.dot(hv, wu_ref[...], preferred_element_type=F32)
        zg_ref[...] = zg.astype(BF16)
        zu_ref[...] = zu.astype(BF16)
        a_ref[...] = (zg * _sigmoid(zg) * zu).astype(BF16)

    col = _bs((tm, tn), lambda i, j: (i, j))
    return pl.pallas_call(
        body, grid=(T // tm, F // tn),
        in_specs=[_bs((tm, D), lambda i, j: (i, 0)),
                  _bs((None, D, tn), lambda i, j: (j // nb, 0, j % nb)),
                  _bs((None, D, tn), lambda i, j: (2 + j // nb, 0, j % nb))],
        out_specs=[col, col, col],
        out_shape=[S((T, F), BF16)] * 3, compiler_params=_cp(), name="ffn_in")(h, win, win)


def _ffn_out(act, wout, x):
    T, F = act.shape
    tm = _tile(T, (512,))
    tn = 1024
    return _mm("ffn_out", act, wout,
               _bs((tm, F), lambda j, i, k: (i, 0)), _bs((F, tn), lambda j, i, k: (0, j)),
               _bs((tm, tn), lambda j, i, k: (i, j)), S((T, D), F32), (D // tn, T // tm, 1), NN, 1, (tm, tn),
               scale=0.5, res=x, r_spec=_bs((tm, tn), lambda j, i, k: (i, j)))


def _ffn_bwd_dz(dyb, wout, zg, zu, dep):
    T, F = zg.shape
    tm = _tile(T, (1024, 512))
    tn = _tile(F, (512, 256))

    def body(dy_ref, w_ref, zg_ref, zu_ref, dep_ref, dz_ref):
        da = 0.5 * lax.dot_general(dy_ref[...], w_ref[...], NT, preferred_element_type=F32)
        zg = zg_ref[...].astype(F32)
        zu = zu_ref[...].astype(F32)
        s = _sigmoid(zg)
        dz_ref[0] = (da * zu * (s * (1.0 + zg * (1.0 - s)))).astype(BF16)
        dz_ref[1] = (da * (zg * s)).astype(BF16)

    col = _bs((tm, tn), lambda i, j: (i, j))
    return pl.pallas_call(
        body, grid=(T // tm, F // tn),
        in_specs=[_bs((tm, D), lambda i, j: (i, 0)), _bs((tn, D), lambda i, j: (j, 0)), col, col,
                  pl.BlockSpec(memory_space=pl.ANY)],
        out_specs=_bs((2, tm, tn), lambda i, j: (0, i, j)), out_shape=S((2, T, F), BF16),
        compiler_params=_cp(), name="ffn_bwd_dz")(dyb, wout, zg, zu, dep)


def _ffn_bwd_dh(dz, win):
    _, T, F = dz.shape
    Fs = win.shape[2]
    tk = _tile(Fs, (1408, 256, 128))
    nkb = Fs // tk
    tm = _tile(T, (1024, 512))
    tn = 1024
    nk = 2 * nkb

    def body(dzg_ref, dzu_ref, wg_ref, wu_ref, o_ref, acc_ref):
        k = pl.program_id(2)
        prod = (lax.dot_general(dzg_ref[...], wg_ref[...], NT, preferred_element_type=F32)
                + lax.dot_general(dzu_ref[...], wu_ref[...], NT, preferred_element_type=F32))

        @pl.when(k == 0)
        def _():
            acc_ref[...] = prod

        @pl.when(k > 0)
        def _():
            acc_ref[...] += prod

        @pl.when(k == nk - 1)
        def _():
            o_ref[...] = acc_ref[...]

    def a_spec(half):
        return _bs((None, tm, tk), lambda i, j, k: (half, i, k))

    return pl.pallas_call(
        body, grid=(T // tm, D // tn, nk),
        in_specs=[a_spec(0), a_spec(1),
                  _bs((None, tn, tk), lambda i, j, k: (k // nkb, j, k % nkb)),
                  _bs((None, tn, tk), lambda i, j, k: (2 + k // nkb, j, k % nkb))],
        out_specs=_bs((tm, tn), lambda i, j, k: (i, j)), out_shape=S((T, D), F32),
        scratch_shapes=[pltpu.VMEM((tm, tn), F32)], compiler_params=_cp(), name="ffn_bwd_dh")(dz, dz, win, win)


def _ffn_bwd_dwin(h, dz):
    _, T, F = dz.shape
    Fs = F // 2
    tn = _tile(Fs, (1408, 256, 128))
    nb = Fs // tn
    tm = 512
    return _mm("ffn_bwd_dwin", h, dz,
               _bs((T, tm), lambda i, j, k: (0, i)), _bs((None, T, tn), lambda i, j, k: (j // (2 * nb), 0, j % (2 * nb))),
               _bs((None, tm, tn), lambda i, j, k: (j // nb, i, j % nb)), S((4, D, Fs), BF16),
               (D // tm, 4 * nb, 1), TN, 1, (tm, tn))


def _mm_tn(name, a, b, scale=1.0, tm=512, tn=1024):
    T, M = a.shape
    N = b.shape[1]
    tm = _tile(M, (tm, 512, 256, 128))
    tn = _tile(N, (tn, 1152, 1024, 512, 128))
    return _mm(name, a, b,
               _bs((T, tm), lambda i, j, k: (0, i)), _bs((T, tn), lambda i, j, k: (0, j)),
               _bs((tm, tn), lambda i, j, k: (i, j)), S((M, N), BF16), (M // tm, N // tn, 1), TN, 1, (tm, tn), scale=scale)


def _mm_nt_full(name, a, b, tn):
    T, K = a.shape
    N = b.shape[0]
    tm = _tile(T, (1024, 512))
    return _mm(name, a, b,
               _bs((tm, K), lambda i, j, k: (i, 0)), _bs((tn, K), lambda i, j, k: (j, 0)),
               _bs((tm, tn), lambda i, j, k: (i, j)), S((T, N), F32), (T // tm, N // tn, 1), NT, 1, (tm, tn))


def _bt(T):
    return _tile(T, (512,))


def _down(ext, s):
    return pltpu.roll(ext, s, 0)[8:, :]


def _up(ext, s):
    n = ext.shape[0]
    return pltpu.roll(ext, n - s, 0)[: n - 8, :]


def _halo_prev(ref, start, b):
    lo = pl.multiple_of(jnp.maximum(start - 8, 0), 8)
    return ref[pl.ds(lo, 8), :] * (b > 0).astype(F32)


def _halo_next(ref, start, bt, b, nb):
    lo = pl.multiple_of(jnp.minimum(start + bt, (nb - 1) * bt), 8)
    return ref[pl.ds(lo, 8), :] * (b < nb - 1).astype(F32)


def _scan_fwd(A, U):
    n = U.shape[0]
    row = lax.broadcasted_iota(jnp.int32, U.shape, 0)
    d = 1
    while d < n:
        keep = row >= d
        Us = jnp.where(keep, pltpu.roll(U, d, 0), 0.0)
        if A is None:
            U = U + Us
        else:
            As = jnp.where(keep, pltpu.roll(A, d, 0), 1.0)
            U = A * Us + U
            A = A * As
        d *= 2
    return A, U


def _scan_bwd(A, U):
    n = U.shape[0]
    row = lax.broadcasted_iota(jnp.int32, U.shape, 0)
    d = 1
    while d < n:
        keep = row < n - d
        Us = jnp.where(keep, pltpu.roll(U, n - d, 0), 0.0)
        if A is None:
            U = U + Us
        else:
            As = jnp.where(keep, pltpu.roll(A, n - d, 0), 1.0)
            U = A * Us + U
            A = A * As
        d *= 2
    return A, U


def _softplus(z):
    return jnp.maximum(z, 0.0) + jnp.log(1.0 + jnp.exp(-jnp.abs(z)))


def _gelu_parts(g):
    k0 = math.sqrt(2.0 / math.pi)
    t = jnp.tanh(k0 * (g + 0.044715 * g * g * g))
    gel = 0.5 * g * (1.0 + t)
    dgel = 0.5 * (1.0 + t) + 0.5 * g * (1.0 - t * t) * k0 * (1.0 + 3.0 * 0.044715 * g * g)
    return gel, dgel


def _conv_fwd(P, cw):
    T = P.shape[0]
    bt = _bt(T)
    nb = T // bt

    def body(b_ref, c_ref, v_ref, w_ref, y_ref):
        w = w_ref[...]

        def step(b, carry):
            start = pl.multiple_of(b * bt, bt)
            rows = pl.ds(start, bt)
            m = c_ref[rows, :] * v_ref[rows, :]
            ext = jnp.concatenate([_halo_prev(c_ref, start, b) * _halo_prev(v_ref, start, b), m], axis=0)
            z = w[2:3, :] * m + w[1:2, :] * _down(ext, 1) + w[0:1, :] * _down(ext, 2)
            y_ref[rows, :] = b_ref[rows, :] * z
            return carry

        lax.fori_loop(0, nb, step, 0)

    def colspec(off):
        return _bs((T, 128), lambda c: (0, off + c))

    return pl.pallas_call(
        body, grid=(DC // 128,),
        in_specs=[colspec(0), colspec(4), colspec(8), _bs((8, 128), lambda c: (0, c))],
        out_specs=_bs((T, 128), lambda c: (0, c)), out_shape=S((T, DC), F32),
        compiler_params=_cp(), name="conv_fwd")(P, P, P, cw)


def _conv_bwd(P, cw, dy):
    T = P.shape[0]
    bt = _bt(T)
    nb = T // bt

    def body(b_ref, c_ref, v_ref, w_ref, dy_ref, db_ref, dc_ref, dv_ref, dw_ref):
        w = w_ref[...]

        def step(b, carry):
            a0, a1, a2 = carry
            start = pl.multiple_of(b * bt, bt)
            rows = pl.ds(start, bt)
            cb, cc, cv, dy = b_ref[rows, :], c_ref[rows, :], v_ref[rows, :], dy_ref[rows, :]
            m = cc * cv
            ext = jnp.concatenate([_halo_prev(c_ref, start, b) * _halo_prev(v_ref, start, b), m], axis=0)
            m1, m2 = _down(ext, 1), _down(ext, 2)
            z = w[2:3, :] * m + w[1:2, :] * m1 + w[0:1, :] * m2
            db_ref[rows, :] = dy * z
            dz = dy * cb
            extn = jnp.concatenate([dz, _halo_next(dy_ref, start, bt, b, nb) * _halo_next(b_ref, start, bt, b, nb)], axis=0)
            dm = w[2:3, :] * dz + w[1:2, :] * _up(extn, 1) + w[0:1, :] * _up(extn, 2)
            dc_ref[rows, :] = dm * cv
            dv_ref[rows, :] = dm * cc
            return (a0 + jnp.sum(dz * m2, axis=0, keepdims=True),
                    a1 + jnp.sum(dz * m1, axis=0, keepdims=True),
                    a2 + jnp.sum(dz * m, axis=0, keepdims=True))

        zero = jnp.zeros((1, 128), F32)
        a0, a1, a2 = lax.fori_loop(0, nb, step, (zero, zero, zero))
        dw_ref[...] = jnp.zeros((8, 128), F32)
        dw_ref[0:1, :] = a0
        dw_ref[1:2, :] = a1
        dw_ref[2:3, :] = a2

    def colspec(off):
        return _bs((T, 128), lambda c: (0, off + c))

    own = _bs((T, 128), lambda c: (0, c))
    return pl.pallas_call(
        body, grid=(DC // 128,),
        in_specs=[colspec(0), colspec(4), colspec(8), _bs((8, 128), lambda c: (0, c)), own],
        out_specs=[own, own, own, _bs((8, 128), lambda c: (0, c))],
        out_shape=[S((T, DC), F32)] * 3 + [S((8, DC), F32)], compiler_params=_cp(), name="conv_bwd")(P, P, P, cw, dy)


def _fgate_fwd(P, fb):
    T = P.shape[0]
    bt = _bt(T)
    nb = T // bt

    def body(f_ref, b_ref, c_ref):
        bias = b_ref[...]

        def step(b, carry):
            rows = pl.ds(pl.multiple_of(b * bt, bt), bt)
            logf = -_softplus(-(f_ref[rows, :] + bias))
            _, cs = _scan_fwd(None, logf)
            cs = cs + carry
            c_ref[rows, :] = cs
            return cs[bt - 1:bt, :]

        lax.fori_loop(0, nb, step, jnp.zeros((1, 128), F32))

    return pl.pallas_call(
        body, grid=(1,),
        in_specs=[_bs((T, 128), lambda i: (0, C_F)), _bs((1, 128), lambda i: (0, 0))],
        out_specs=_bs((T, 128), lambda i: (0, 0)), out_shape=S((T, 128), F32),
        compiler_params=_cp(), name="fgate_fwd")(P, fb)


def _fgate_bwd(P, fb, dcum):
    T = P.shape[0]
    bt = _bt(T)
    nb = T // bt

    def body(f_ref, b_ref, dc_ref, df_ref, db_ref):
        bias = b_ref[...]

        def step(i, carry):
            run, acc = carry
            b = nb - 1 - i
            rows = pl.ds(pl.multiple_of(b * bt, bt), bt)
            _, rs = _scan_bwd(None, dc_ref[rows, :])
            rs = rs + run
            df = rs * jax.nn.sigmoid(-(f_ref[rows, :] + bias))
            df_ref[rows, :] = df
            return rs[0:1, :], acc + jnp.sum(df, axis=0, keepdims=True)

        zero = jnp.zeros((1, 128), F32)
        _, acc = lax.fori_loop(0, nb, step, (zero, zero))
        db_ref[...] = acc

    return pl.pallas_call(
        body, grid=(1,),
        in_specs=[_bs((T, 128), lambda i: (0, C_F)), _bs((1, 128), lambda i: (0, 0)), _bs((T, 128), lambda i: (0, 0))],
        out_specs=[_bs((T, 128), lambda i: (0, 0)), _bs((1, 128), lambda i: (0, 0))],
        out_shape=[S((T, 128), F32), S((1, 128), F32)], compiler_params=_cp(), name="fgate_bwd")(P, fb, dcum)


def _att_tile(T):
    return _tile(T, (512,))


def _causal_mask(tq):
    return lax.broadcasted_iota(jnp.int32, (tq, tq), 1) <= lax.broadcasted_iota(jnp.int32, (tq, tq), 0)


def _attn_fwd(qkv, cumq, cumk):
    T = qkv.shape[0]
    tq = _att_tile(T)
    nq = T // tq

    def body(q_ref, k_ref, v_ref, cq_ref, ck_ref, o_ref, lse_ref):
        i = pl.program_id(1)
        q = q_ref[...]
        cq = cq_ref[...]

        def block(j, carry, diagonal):
            m_old, l_old, acc = carry
            rows = pl.ds(pl.multiple_of(j * tq, tq), tq)
            s = lax.dot_general(q, k_ref[rows, :], NT, preferred_element_type=F32)
            s = s * ATT_SCALE + cq - ck_ref[j]
            if diagonal:
                s = jnp.where(_causal_mask(tq), s, -jnp.inf)
            m_new = jnp.maximum(m_old, jnp.max(s, axis=1, keepdims=True))
            p = jnp.exp(s - m_new)
            alpha = jnp.exp(m_old - m_new)
            l_new = alpha * l_old + jnp.sum(p, axis=1, keepdims=True)
            acc = alpha * acc + jnp.dot(p.astype(BF16), v_ref[rows, :], preferred_element_type=F32)
            return m_new, l_new, acc

        init = (jnp.full((tq, 1), -jnp.inf, F32), jnp.zeros((tq, 1), F32), jnp.zeros((tq, HD), F32))
        carry = lax.fori_loop(0, i, lambda j, c: block(j, c, False), init)
        m, l, acc = block(i, carry, True)
        o_ref[...] = acc / l
        lse_ref[...] = m + jnp.log(l)

    return pl.pallas_call(
        body, grid=(NH, nq),
        in_specs=[_bs((tq, HD), lambda h, i: (i, h)),
                  _bs((T, HD), lambda h, i: (0, NH + h)),
                  _bs((T, HD), lambda h, i: (0, 2 * NH + h)),
                  _bs((None, tq, 1), lambda h, i: (h, i, 0)),
                  _bs((None, nq, 1, tq), lambda h, i: (h, 0, 0, 0))],
        out_specs=[_bs((tq, HD), lambda h, i: (i, h)), _bs((None, tq, 1), lambda h, i: (h, i, 0))],
        out_shape=[S((T, DA), F32), S((NH, T, 1), F32)],
        compiler_params=_cp(), name="attn_fwd")(qkv, qkv, qkv, cumq, cumk)


def _attn_bwd(qkv, cumq, cumk, lse, o, do):
    T = qkv.shape[0]
    tq = _att_tile(T)
    nq = T // tq

    def body(q_ref, k_ref, v_ref, cq_ref, ck_ref, lse_ref, o_ref, do_ref, dq_ref, dk_ref, dv_ref, dc_ref, dr_ref):
        j = pl.program_id(1)

        @pl.when(j == 0)
        def _():
            dq_ref[...] = jnp.zeros((T, HD), F32)
            dr_ref[...] = jnp.zeros((T, 1), F32)

        k = k_ref[...]
        v = v_ref[...]
        ck = ck_ref[...]

        def block(i, carry, diagonal):
            dk_acc, dv_acc, dc_acc = carry
            rows = pl.ds(pl.multiple_of(i * tq, tq), tq)
            q = q_ref[rows, :]
            do_f = do_ref[rows, :]
            dob = do_f.astype(BF16)
            s = lax.dot_general(q, k, NT, preferred_element_type=F32)
            p = jnp.exp(s * ATT_SCALE + cq_ref[rows, :] - ck - lse_ref[rows, :])
            if diagonal:
                p = jnp.where(_causal_mask(tq), p, 0.0)
            delta = jnp.sum(do_f * o_ref[rows, :], axis=1, keepdims=True)
            dp = lax.dot_general(dob, v, NT, preferred_element_type=F32)
            ds = p * (dp - delta)
            dsb = (ds * ATT_SCALE).astype(BF16)
            dq_ref[rows, :] += jnp.dot(dsb, k, preferred_element_type=F32)
            dr_ref[rows, :] += jnp.sum(ds, axis=1, keepdims=True)
            return (dk_acc + lax.dot_general(dsb, q, TN, preferred_element_type=F32),
                    dv_acc + lax.dot_general(p.astype(BF16), dob, TN, preferred_element_type=F32),
                    dc_acc - jnp.sum(ds, axis=0, keepdims=True))

        init = (jnp.zeros((tq, HD), F32), jnp.zeros((tq, HD), F32), jnp.zeros((1, tq), F32))
        carry = block(j, init, True)
        dk_acc, dv_acc, dc_acc = lax.fori_loop(j + 1, nq, lambda i, c: block(i, c, False), carry)
        dk_ref[...] = dk_acc
        dv_ref[...] = dv_acc
        dc_ref[...] = dc_acc

    def whole(col):
        return _bs((T, HD), lambda h, j: (0, col + h))

    qvec = _bs((None, T, 1), lambda h, j: (h, 0, 0))
    kv_out = _bs((tq, HD), lambda h, j: (j, h))
    return pl.pallas_call(
        body, grid=(NH, nq),
        in_specs=[whole(0), _bs((tq, HD), lambda h, j: (j, NH + h)), _bs((tq, HD), lambda h, j: (j, 2 * NH + h)),
                  qvec, _bs((None, None, 1, tq), lambda h, j: (h, j, 0, 0)), qvec, whole(0), whole(0)],
        out_specs=[whole(0), kv_out, kv_out, _bs((None, 1, tq), lambda h, j: (h, 0, j)), qvec],
        out_shape=[S((T, DA), F32)] * 3 + [S((NH, 1, T), F32), S((NH, T, 1), F32)],
        compiler_params=_cp(), name="attn_bwd")(qkv, qkv, qkv, cumq, cumk, lse, o, do)


def _lru_gates(xr, wa, wx, ba, bx, sp):
    xb = xr.astype(BF16)
    r = jax.nn.sigmoid(jnp.dot(xb, wa, preferred_element_type=F32) + ba)
    ig = jax.nn.sigmoid(jnp.dot(xb, wx, preferred_element_type=F32) + bx)
    log_a = -LRU_C * r * sp
    a = jnp.exp(log_a)
    th = jnp.tanh(log_a)
    om = -2.0 * th / (1.0 - th)
    mult = jnp.sqrt(om)
    return xb, r, ig, a, om, mult


def _lru_xr(lx_ref, cw, cb, start, b, rows):
    lx = lx_ref[rows, :]
    ext = jnp.concatenate([_halo_prev(lx_ref, start, b), lx], axis=0)
    return cw[3:4, :] * lx + cw[2:3, :] * _down(ext, 1) + cw[1:2, :] * _down(ext, 2) + cw[0:1, :] * _down(ext, 3) + cb


def _lru_fwd(P, lcw, vec, wa, wx):
    T = P.shape[0]
    bt = _bt(T)
    nb = T // bt

    def body(g_ref, lx_ref, cw_ref, vec_ref, wa_ref, wx_ref, y_ref, h_ref):
        cw = cw_ref[...]
        vec = vec_ref[...]
        wa = wa_ref[...].astype(BF16)
        wx = wx_ref[...].astype(BF16)
        sp = _softplus(-vec[3:4, :])

        def step(b, carry):
            start = pl.multiple_of(b * bt, bt)
            rows = pl.ds(start, bt)
            xr = _lru_xr(lx_ref, cw, vec[0:1, :], start, b, rows)
            _, _, ig, a, _, mult = _lru_gates(xr, wa, wx, vec[1:2, :], vec[2:3, :], sp)
            u = mult * (ig * xr)
            ac, hc = _scan_fwd(a, u)
            hb = hc + ac * carry
            h_ref[rows, :] = hb
            gel, _ = _gelu_parts(g_ref[rows, :])
            y_ref[rows, :] = gel * hb
            return hb[bt - 1:bt, :]

        lax.fori_loop(0, nb, step, jnp.zeros((1, 128), F32))

    own = _bs((T, 128), lambda c: (0, c))
    return pl.pallas_call(
        body, grid=(DL // 128,),
        in_specs=[_bs((T, 128), lambda c: (0, C_GATE + c)), _bs((T, 128), lambda c: (0, C_LX + c)),
                  _bs((8, 128), lambda c: (0, c)), _bs((8, 128), lambda c: (0, c)),
                  _bs((None, LB, LB), lambda c: (c, 0, 0)), _bs((None, LB, LB), lambda c: (c, 0, 0))],
        out_specs=[own, own], out_shape=[S((T, DL), F32)] * 2, compiler_params=_cp(), name="lru_fwd")(P, P, lcw, vec, wa, wx)


def _lru_bwd(P, lcw, vec, wa, wx, hst, dy):
    T = P.shape[0]
    bt = _bt(T)
    nb = T // bt

    def body(g_ref, lx_ref, cw_ref, vec_ref, wa_ref, wx_ref, h_ref, dy_ref,
             dg_ref, dlx_ref, sm_ref, dwa_ref, dwx_ref, dxr_s):
        cw = cw_ref[...]
        vec = vec_ref[...]
        wa = wa_ref[...].astype(BF16)
        wx = wx_ref[...].astype(BF16)
        lam = vec[3:4, :]
        sp = _softplus(-lam)
        dwa_ref[...] = jnp.zeros((LB, LB), F32)
        dwx_ref[...] = jnp.zeros((LB, LB), F32)
        zero = jnp.zeros((1, 128), F32)

        def step1(i, carry):
            wc, s_cb, s_ba, s_bx, s_sp = carry
            b = nb - 1 - i
            start = pl.multiple_of(b * bt, bt)
            rows = pl.ds(start, bt)
            xr = _lru_xr(lx_ref, cw, vec[0:1, :], start, b, rows)
            xb, r, ig, a, om, mult = _lru_gates(xr, wa, wx, vec[1:2, :], vec[2:3, :], sp)
            hb = h_ref[rows, :]
            dy = dy_ref[rows, :]
            gel, dgel = _gelu_parts(g_ref[rows, :])
            dg_ref[rows, :] = dy * hb * dgel
            dh = dy * gel
            ac, wcum = _scan_bwd(a, a * dh)
            w = wcum + ac * wc
            g = dh + _up(jnp.concatenate([w, jnp.broadcast_to(wc, (8, 128))], axis=0), 1)
            hprev = _down(jnp.concatenate([_halo_prev(h_ref, start, b), hb], axis=0), 1)
            da = g * hprev
            dmult = g * (ig * xr)
            dix = g * mult
            di = dix * xr
            dlog_a = da * a - dmult * ((1.0 - om) / mult)
            dr = dlog_a * (-LRU_C * sp)
            dpr = dr * r * (1.0 - r)
            dpi = di * ig * (1.0 - ig)
            dprb, dpib = dpr.astype(BF16), dpi.astype(BF16)
            dwa_ref[...] += lax.dot_general(xb, dprb, TN, preferred_element_type=F32)
            dwx_ref[...] += lax.dot_general(xb, dpib, TN, preferred_element_type=F32)
            dxr = (dix * ig + lax.dot_general(dprb, wa, NT, preferred_element_type=F32)
                   + lax.dot_general(dpib, wx, NT, preferred_element_type=F32))
            dxr_s[rows, :] = dxr
            return (w[0:1, :], s_cb + jnp.sum(dxr, axis=0, keepdims=True), s_ba + jnp.sum(dpr, axis=0, keepdims=True),
                    s_bx + jnp.sum(dpi, axis=0, keepdims=True), s_sp + jnp.sum(dlog_a * (-LRU_C * r), axis=0, keepdims=True))

        _, s_cb, s_ba, s_bx, s_sp = lax.fori_loop(0, nb, step1, (zero, zero, zero, zero, zero))

        def step2(b, carry):
            t0, t1, t2, t3 = carry
            start = pl.multiple_of(b * bt, bt)
            rows = pl.ds(start, bt)
            dxr = dxr_s[rows, :]
            extn = jnp.concatenate([dxr, _halo_next(dxr_s, start, bt, b, nb)], axis=0)
            dlx_ref[rows, :] = (cw[3:4, :] * dxr + cw[2:3, :] * _up(extn, 1) + cw[1:2, :] * _up(extn, 2)
                                + cw[0:1, :] * _up(extn, 3))
            lx = lx_ref[rows, :]
            ext = jnp.concatenate([_halo_prev(lx_ref, start, b), lx], axis=0)
            return (t0 + jnp.sum(dxr * _down(ext, 3), axis=0, keepdims=True),
                    t1 + jnp.sum(dxr * _down(ext, 2), axis=0, keepdims=True),
                    t2 + jnp.sum(dxr * _down(ext, 1), axis=0, keepdims=True),
                    t3 + jnp.sum(dxr * lx, axis=0, keepdims=True))

        t0, t1, t2, t3 = lax.fori_loop(0, nb, step2, (zero, zero, zero, zero))
        sm_ref[...] = jnp.zeros((16, 128), F32)
        for k, val in enumerate((t0, t1, t2, t3, s_cb, s_ba, s_bx, -s_sp * jax.nn.sigmoid(-lam))):
            sm_ref[k:k + 1, :] = val

    own = _bs((T, 128), lambda c: (0, c))
    wspec = _bs((None, LB, LB), lambda c: (c, 0, 0))
    return pl.pallas_call(
        body, grid=(DL // 128,),
        in_specs=[_bs((T, 128), lambda c: (0, C_GATE + c)), _bs((T, 128), lambda c: (0, C_LX + c)),
                  _bs((8, 128), lambda c: (0, c)), _bs((8, 128), lambda c: (0, c)), wspec, wspec, own, own],
        out_specs=[own, own, _bs((16, 128), lambda c: (0, c)), wspec, wspec],
        out_shape=[S((T, DL), F32)] * 2 + [S((16, DL), F32), S((4, LB, LB), F32), S((4, LB, LB), F32)],
        scratch_shapes=[pltpu.VMEM((T, 128), F32)], compiler_params=_cp(), name="lru_bwd")(P, P, lcw, vec, wa, wx, hst, dy)


_GROUPS = ((0, DC), (DC, DC + DA), (DC + DA, D))


def _gnorm_fwd(yc, ya, yl, gain):
    T = yc.shape[0]
    tb = _tile(T, (512,))

    def body(c_ref, a_ref, l_ref, g_ref, yn_ref, r0_ref, r1_ref, r2_ref):
        for (lo, hi), src, r_ref in zip(_GROUPS, (c_ref, a_ref, l_ref), (r0_ref, r1_ref, r2_ref)):
            yv = src[...]
            r = lax.rsqrt(jnp.mean(yv * yv, axis=1, keepdims=True) + EPS)
            yn_ref[:, lo:hi] = (yv * r * g_ref[:, lo:hi]).astype(BF16)
            r_ref[...] = r

    rs = _bs((tb, 1), lambda i: (i, 0))
    return pl.pallas_call(
        body, grid=(T // tb,),
        in_specs=[_bs((tb, DC), lambda i: (i, 0)), _bs((tb, DA), lambda i: (i, 0)), _bs((tb, DL), lambda i: (i, 0)),
                  _bs((1, D), lambda i: (0, 0))],
        out_specs=[_bs((tb, D), lambda i: (i, 0)), rs, rs, rs],
        out_shape=[S((T, D), BF16)] + [S((T, 1), F32)] * 3, compiler_params=_cp(), name="gnorm_fwd")(yc, ya, yl, gain)


def _gnorm_bwd(dyn, yc, ya, yl, r0, r1, r2, gain):
    T = yc.shape[0]
    tb = _tile(T, (512,))

    def body(d_ref, c_ref, a_ref, l_ref, r0_ref, r1_ref, r2_ref, g_ref, dc_ref, da_ref, dl_ref, dg_ref):
        i = pl.program_id(0)
        for (lo, hi), src, r_ref, dst in zip(_GROUPS, (c_ref, a_ref, l_ref), (r0_ref, r1_ref, r2_ref), (dc_ref, da_ref, dl_ref)):
            r = r_ref[...]
            yhat = src[...] * r
            dy = d_ref[:, lo:hi]
            dyh = dy * g_ref[:, lo:hi]
            m = jnp.mean(dyh * yhat, axis=1, keepdims=True)
            dst[...] = r * (dyh - yhat * m)
            part = jnp.sum(dy * yhat, axis=0, keepdims=True)

            @pl.when(i == 0)
            def _():
                dg_ref[:, lo:hi] = part

            @pl.when(i > 0)
            def _():
                dg_ref[:, lo:hi] += part

    rs = _bs((tb, 1), lambda i: (i, 0))
    specs = [_bs((tb, DC), lambda i: (i, 0)), _bs((tb, DA), lambda i: (i, 0)), _bs((tb, DL), lambda i: (i, 0))]
    return pl.pallas_call(
        body, grid=(T // tb,),
        in_specs=[_bs((tb, D), lambda i: (i, 0))] + specs + [rs, rs, rs, _bs((1, D), lambda i: (0, 0))],
        out_specs=specs + [_bs((1, D), lambda i: (0, 0))],
        out_shape=[S((T, DC), F32), S((T, DA), F32), S((T, DL), F32), S((1, D), F32)],
        compiler_params=_cp(), name="gnorm_bwd")(dyn, yc, ya, yl, r0, r1, r2, gain)


HBM = pl.BlockSpec(memory_space=pltpu.HBM)
N_BIG = 6


def _place():
    x, y, c = lax.axis_index("x"), lax.axis_index("y"), lax.axis_index("c")
    return x, y, c, 2 * x + y


def _peer(x, y, j):
    return x ^ ((j + 1) >> 1), y ^ ((j + 1) & 1)


SEM = pl.BlockSpec(memory_space=pltpu.SEMAPHORE)
ANY = pl.BlockSpec(memory_space=pl.ANY)
VM = pl.BlockSpec(memory_space=pltpu.VMEM)
EFFECT = pltpu.SideEffectType.DATAFLOW_SIDE_EFFECTING
N_AG = N_BIG + 1


def _hbm(a):
    return pltpu.with_memory_space_constraint(a, pltpu.HBM)


AG_ORDER = (0, 1, N_BIG, 2, 3, 4, 5)
AG_FIRST, AG_REST = (0, 1), (N_BIG, 2, 3, 4, 5)


def _ag_copy(src, land, ssem, rsem, t, j, chip):
    x, y, c, _ = _place()
    px, py = _peer(x, y, j)
    if t == N_BIG:
        s_ref, d_ref = src, land.at[chip]
    else:
        rh = src.shape[0] // 2
        half = pl.ds(c * rh, rh)
        s_ref, d_ref = src.at[half], land.at[chip, half]
    return pltpu.make_async_remote_copy(src_ref=s_ref, dst_ref=d_ref, send_sem=ssem.at[3 * t + j], recv_sem=rsem.at[3 * t + j],
                                        device_id=(px, py, c), device_id_type=MESH)


def _ag_start(l, srcs, dep):
    n = N_AG

    def body(*refs):
        src = refs[:n]
        ssem, rsem = refs[2 * n + 1], refs[2 * n + 2]
        land = refs[3 * n + 3:4 * n + 3]
        token = refs[4 * n + 3]
        _, _, _, me = _place()
        for t in AG_ORDER:
            for j in range(3):
                _ag_copy(src[t], land[t], ssem, rsem, t, j, me).start()
        token[...] = jnp.zeros_like(token)

    lands = [lax.empty((4,) + a.shape, a.dtype) for a in srcs]
    dma = pltpu.SemaphoreType.DMA
    outs = pl.pallas_call(
        body, name=f"ag_start_{l}",
        out_shape=(dma((3 * n,)), dma((3 * n,))) + tuple(pltpu.HBM(a.shape, a.dtype) for a in list(srcs) + lands) + (S((8, 128), F32),),
        in_specs=[HBM] * (2 * n) + [ANY], out_specs=(SEM, SEM) + (HBM,) * (2 * n) + (VM,),
        input_output_aliases={i: 2 + i for i in range(2 * n)},
        compiler_params=pltpu.CompilerParams(has_side_effects=EFFECT),
    )(*[_hbm(a) for a in srcs], *[_hbm(a) for a in lands], dep)
    return outs[0], outs[1], outs[2:2 + n], outs[2 + n:2 + 2 * n], outs[-1]


def _ag_wait(name, idx, ssem, rsem, srcs, lands, after):
    n = len(idx)

    def body(*refs):
        src, land = refs[:n], refs[n:2 * n]
        ssem, rsem = refs[2 * n], refs[2 * n + 1]
        x, y, _, _ = _place()
        for p, t in enumerate(idx):
            for j in range(3):
                px, py = _peer(x, y, j)
                cp = _ag_copy(src[p], land[p], ssem, rsem, t, j, 2 * px + py)
                cp.wait_send()
                cp.wait_recv()

    outs = pl.pallas_call(
        body, name=name,
        out_shape=tuple(pltpu.HBM(a.shape, a.dtype) for a in list(srcs) + list(lands)),
        in_specs=[HBM] * (2 * n) + [SEM, SEM, ANY], out_specs=(HBM,) * (2 * n),
        input_output_aliases={i: i for i in range(2 * n)},
        compiler_params=pltpu.CompilerParams(has_side_effects=EFFECT),
    )(*srcs, *lands, ssem, rsem, after)
    return outs[:n], outs[n:]


def _ag_finish(idx, srcs, lands, dep):
    n = len(idx)

    def body(*refs):
        src = refs[:n]
        out = refs[2 * n + 1:3 * n + 1]
        dsend, drecv, osend, orecv = refs[-4:]
        x, y, c, me = _place()
        big = [p for p, t in enumerate(idx) if t != N_BIG]

        def d2d(p, j, cc):
            px, py = _peer(x, y, j)
            rh = src[p].shape[0] // 2
            part = out[p].at[2 * px + py, pl.ds(cc * rh, rh)]
            return pltpu.make_async_remote_copy(src_ref=part, dst_ref=part, send_sem=dsend.at[p, j], recv_sem=drecv.at[p, j],
                                                device_id=(x, y, 1 - c), device_id_type=MESH)

        own = [pltpu.make_async_remote_copy(src_ref=src[p], dst_ref=out[p].at[me], send_sem=osend.at[p], recv_sem=orecv.at[p],
                                            device_id=(x, y, 1 - c), device_id_type=MESH) for p in range(n)]
        for cp in own:
            cp.start()
        for p in big:
            for j in range(3):
                d2d(p, j, c).start()
        for p in big:
            for j in range(3):
                d2d(p, j, 1 - c).wait_recv()
                d2d(p, j, c).wait_send()
        for cp in own:
            cp.wait()

    dma = pltpu.SemaphoreType.DMA
    return pl.pallas_call(
        body, in_specs=[HBM] * (2 * n) + [ANY], out_specs=[HBM] * n,
        out_shape=[S(a.shape, a.dtype) for a in lands],
        input_output_aliases={n + i: i for i in range(n)},
        scratch_shapes=[dma((n, 3)), dma((n, 3)), dma((n,)), dma((n,))],
        name="ag_finish")(*srcs, *lands, dep)


def _rs_pair(grads):
    def body(*refs):
        g = refs[:N_BIG]
        out = refs[N_BIG:2 * N_BIG]
        ssem, rsem = refs[2 * N_BIG:]
        x, y, c, _ = _place()

        def cp(t):
            rh = g[t].shape[1] // 2
            return pltpu.make_async_remote_copy(
                src_ref=g[t].at[:, pl.ds((1 - c) * rh, rh), :], dst_ref=out[t],
                send_sem=ssem.at[t], recv_sem=rsem.at[t], device_id=(x, y, 1 - c), device_id_type=MESH)

        for t in range(N_BIG):
            cp(t).start()
        for t in range(N_BIG):
            cp(t).wait()

    dma = pltpu.SemaphoreType.DMA
    return pl.pallas_call(
        body, in_specs=[HBM] * N_BIG, out_specs=[HBM] * N_BIG,
        out_shape=[S((4, g.shape[1] // 2, g.shape[2]), g.dtype) for g in grads],
        scratch_shapes=[dma((N_BIG,)), dma((N_BIG,))], name="rs_pair")(*grads)


def _rs_copy(s, land, ssem, rsem, t, j):
    x, y, c, _ = _place()
    px, py = _peer(x, y, j)
    return pltpu.make_async_remote_copy(src_ref=s[t].at[2 * px + py], dst_ref=land[t].at[j],
                                        send_sem=ssem.at[3 * t + j], recv_sem=rsem.at[3 * t + j], device_id=(px, py, c), device_id_type=MESH)


def _rs_start(l, sums, dep):
    n = N_BIG

    def body(*refs):
        s = refs[:n]
        ssem, rsem = refs[2 * n + 1], refs[2 * n + 2]
        land = refs[3 * n + 3:4 * n + 3]
        token = refs[4 * n + 3]
        for t in range(n):
            for j in range(3):
                _rs_copy(s, land, ssem, rsem, t, j).start()
        token[...] = jnp.zeros_like(token)

    lands = [lax.empty((3,) + a.shape[1:], a.dtype) for a in sums]
    dma = pltpu.SemaphoreType.DMA
    outs = pl.pallas_call(
        body, name=f"rs_start_{l}",
        out_shape=(dma((3 * n,)), dma((3 * n,))) + tuple(pltpu.HBM(a.shape, a.dtype) for a in list(sums) + lands) + (S((8, 128), F32),),
        in_specs=[HBM] * (2 * n) + [ANY], out_specs=(SEM, SEM) + (HBM,) * (2 * n) + (VM,),
        input_output_aliases={i: 2 + i for i in range(2 * n)},
        compiler_params=pltpu.CompilerParams(has_side_effects=EFFECT),
    )(*[_hbm(a) for a in sums], *[_hbm(a) for a in lands], dep)
    return outs[0], outs[1], outs[2:2 + n], outs[2 + n:2 + 2 * n], outs[-1]


def _rs_wait(l, ssem, rsem, sums, lands, after):
    n = N_BIG

    def body(*refs):
        s, land = refs[:n], refs[n:2 * n]
        ssem, rsem = refs[2 * n], refs[2 * n + 1]
        for t in range(n):
            for j in range(3):
                cp = _rs_copy(s, land, ssem, rsem, t, j)
                cp.wait_send()
                cp.wait_recv()

    outs = pl.pallas_call(
        body, name=f"rs_wait_{l}",
        out_shape=tuple(pltpu.HBM(a.shape, a.dtype) for a in list(sums) + list(lands)),
        in_specs=[HBM] * (2 * n) + [SEM, SEM, ANY], out_specs=(HBM,) * (2 * n),
        input_output_aliases={i: i for i in range(2 * n)},
        compiler_params=pltpu.CompilerParams(has_side_effects=EFFECT),
    )(*sums, *lands, ssem, rsem, after)
    return outs[n:]


def _rs_join(halves, lo, hi, dep):
    def body(*refs):
        h = refs[:N_BIG]
        out = refs[N_BIG + 1:2 * N_BIG + 1]
        ssem, rsem = refs[2 * N_BIG + 1:]
        x, y, c, _ = _place()

        def cp(t):
            return pltpu.make_async_remote_copy(
                src_ref=h[t].at[pl.ds(lo, hi - lo)], dst_ref=out[t], send_sem=ssem.at[t], recv_sem=rsem.at[t],
                device_id=(x, y, 1 - c), device_id_type=MESH)

        for t in range(N_BIG):
            cp(t).start()
        for t in range(N_BIG):
            cp(t).wait()

    dma = pltpu.SemaphoreType.DMA
    return pl.pallas_call(
        body, in_specs=[HBM] * N_BIG + [ANY], out_specs=[HBM] * N_BIG,
        out_shape=[S((hi - lo,) + h.shape[1:], h.dtype) for h in halves],
        scratch_shapes=[dma((N_BIG,)), dma((N_BIG,))], name="rs_join")(*halves, dep)


def _all_reduce_small(pack):
    R = pack.shape[0]
    rb = _tile(R, (512, 256, 128, 8))

    def body(x_ref, all_ref, sum_ref, send_sems, recv_sems, local_sem):
        x, y, c = lax.axis_index("x"), lax.axis_index("y"), lax.axis_index("c")
        me, sibling = (x, y, c), (x, y, 1 - c)
        chips = [(1 - x, y), (x, 1 - y), (1 - x, 1 - y)]

        def rows(px, py, pc):
            return all_ref.at[pl.ds((4 * px + 2 * py + pc) * R, R), :]

        def copy(k, block, to, src=None):
            return pltpu.make_async_remote_copy(
                src_ref=rows(*block) if src is None else src, dst_ref=rows(*block),
                send_sem=send_sems.at[k], recv_sem=recv_sems.at[k], device_id=to, device_id_type=MESH)

        mine = pltpu.make_async_copy(x_ref, rows(*me), local_sem)
        mine.start()
        first = [copy(0, me, sibling, src=x_ref)]
        first += [copy(1 + j, me, (*chip, c), src=x_ref) for j, chip in enumerate(chips)]
        for cp in first:
            cp.start()
        passed = [copy(4 + j, (*chip, c), sibling) for j, chip in enumerate(chips)]
        for j, chip in enumerate(chips):
            copy(1 + j, (*chip, c), me).wait_recv()
            passed[j].start()
        copy(0, sibling, me).wait_recv()
        for j, chip in enumerate(chips):
            copy(4 + j, (*chip, 1 - c), me).wait_recv()
        for cp in first + passed:
            cp.wait_send()
        mine.wait()

        def step(b, carry):
            off = pl.multiple_of(b * rb, rb)
            acc = all_ref[pl.ds(off, rb), :]
            for k in range(1, 8):
                acc = acc + all_ref[pl.ds(pl.multiple_of(k * R + off, 8), rb), :]
            sum_ref[pl.ds(off, rb), :] = acc
            return carry

        lax.fori_loop(0, R // rb, step, 0)

    vm = pl.BlockSpec(memory_space=pltpu.VMEM)
    dma = pltpu.SemaphoreType.DMA
    _, total = pl.pallas_call(
        body, in_specs=[vm], out_specs=[vm, vm],
        out_shape=[S((8 * R, 128), F32), S((R, 128), F32)],
        scratch_shapes=[dma((7,)), dma((7,)), dma],
        compiler_params=_cp(), name="allreduce_small")(pack)
    return total


def _row_tile(rh, cc, tile_bytes=3 * 1024 * 1024 // 2):
    for t in (512, 256, 128, 64, 32, 16):
        if rh % t == 0 and t * cc * 4 <= tile_bytes:
            return t
    return 16


def _my_chip():
    return 2 * lax.axis_index("x") + lax.axis_index("y")


def _pair_sum(g, recv):
    _, r, cc = g.shape
    rh = r // 2
    tb = _row_tile(rh, cc)
    nbh = rh // tb

    def body(g_ref, r_ref, o_ref):
        o_ref[...] = (g_ref[...].astype(F32) + r_ref[...].astype(F32)).astype(BF16)

    mine = _bs((None, tb, cc), lambda k, i: (k, lax.axis_index("c") * nbh + i, 0))
    plain = _bs((None, tb, cc), lambda k, i: (k, i, 0))
    return pl.pallas_call(body, grid=(4, nbh), in_specs=[mine, plain], out_specs=plain,
                          out_shape=S((4, rh, cc), BF16), compiler_params=_cp(), name="rs_pair_sum")(g, recv)


def _owner_sum(g, recv, ici, acc, l):
    _, r, cc = g.shape
    rh = r // 2
    tb = _row_tile(rh, cc)
    nbh = rh // tb

    def body(g_ref, r_ref, i0_ref, i1_ref, i2_ref, acc_ref, o_ref):
        s = g_ref[...].astype(F32) + r_ref[...].astype(F32)
        o_ref[...] = s + i0_ref[...].astype(F32) + i1_ref[...].astype(F32) + i2_ref[...].astype(F32)

    def slot(j):
        return _bs((None, tb, cc), lambda i: (j, i, 0))

    return pl.pallas_call(
        body, grid=(nbh,),
        in_specs=[_bs((None, tb, cc), lambda i: (_my_chip(), lax.axis_index("c") * nbh + i, 0)),
                  _bs((None, tb, cc), lambda i: (_my_chip(), i, 0)),
                  slot(0), slot(1), slot(2), pl.BlockSpec(memory_space=pl.ANY)],
        out_specs=_bs((None, tb, cc), lambda i: (l, i, 0)),
        out_shape=S(acc.shape, F32), input_output_aliases={5: 0},
        compiler_params=_cp(), name="rs_owner_sum")(g, recv, ici, ici, ici, acc)


def _adam_math(w, g, m, v):
    m = B1 * m + (1.0 - B1) * g
    v = B2 * v + (1.0 - B2) * (g * g)
    m_hat = m / (1.0 - B1 ** STEP)
    v_hat = v / (1.0 - B2 ** STEP)
    delta = -LR * (m_hat / (jnp.sqrt(v_hat) + AEPS) + WD * w)
    return delta, m, v


def _adamw_big(w, g_mine, g_sib, m, v, lo, hi, prev):
    L, r, cc = w.shape
    rh = r // 2
    tb = _row_tile(rh, cc)
    nbh = rh // tb

    def body(w_ref, gm_ref, gs_ref, m_ref, v_ref, *rest):
        go_ref, d_ref, mo_ref, vo_ref = rest[-4:]
        mine = pl.program_id(1) == lax.axis_index("c")
        g = jnp.where(mine, gm_ref[...], gs_ref[...])
        d, m, v = _adam_math(w_ref[...], g, m_ref[...], v_ref[...])
        go_ref[...] = g
        d_ref[...] = d
        mo_ref[...] = m
        vo_ref[...] = v

    def mine_map(l, hf, i):
        c = lax.axis_index("c")
        return (l + lo, jnp.where(hf == c, i, jnp.where(c == 0, nbh - 1, 0)), 0)

    def sib_map(l, hf, i):
        c = lax.axis_index("c")
        return (l, jnp.where(hf != c, i, jnp.where(c == 0, 0, nbh - 1)), 0)

    full = _bs((None, tb, cc), lambda l, hf, i: (l + lo, hf * nbh + i, 0))
    extra = [] if prev is None else list(prev)
    return pl.pallas_call(
        body, grid=(hi - lo, 2, nbh),
        in_specs=[full, _bs((None, tb, cc), mine_map), _bs((None, tb, cc), sib_map), full, full]
        + [pl.BlockSpec(memory_space=pl.ANY)] * len(extra),
        out_specs=[full] * 4, out_shape=[S(w.shape, F32)] * 4,
        input_output_aliases={5 + k: k for k in range(len(extra))},
        compiler_params=_cp(), name="adamw_big")(w, g_mine, g_sib, m, v, *extra)


def _adamw_small(w, g, m, v):
    R = w.shape[0]
    tb = _tile(R, (512, 256, 128, 8))

    def body(w_ref, g_ref, m_ref, v_ref, d_ref, mo_ref, vo_ref):
        d, m, v = _adam_math(w_ref[...], g_ref[...], m_ref[...], v_ref[...])
        d_ref[...] = d
        mo_ref[...] = m
        vo_ref[...] = v

    spec = _bs((tb, 128), lambda i: (i, 0))
    return pl.pallas_call(body, grid=(R // tb,), in_specs=[spec] * 4, out_specs=[spec] * 3,
                          out_shape=[S((R, 128), F32)] * 3, compiler_params=_cp(), name="adamw_small")(w, g, m, v)


def _mix_pad(w):
    return jnp.concatenate([w[:, :4608], w[:, 4616:DIN], w[:, 4608:4616], jnp.zeros((D, PW - DIN), w.dtype)], axis=1)


def _mix_unpad(g):
    return jnp.concatenate([g[:, :4608], g[:, 5632:5640], g[:, 4608:5632]], axis=1)


def _pack(parts):
    flat = jnp.concatenate([p.reshape(-1).astype(F32) for p in parts])
    n = flat.shape[0]
    total = -(-n // (512 * 128)) * (512 * 128)
    return jnp.pad(flat, (0, total - n)).reshape(total // 128, 128)


def _unpack(pack, shapes):
    flat = pack.reshape(-1)
    out, off = [], 0
    for s in shapes:
        n = math.prod(s)
        out.append(flat[off:off + n].reshape(s))
        off += n
    return out


def _ffn_forward(x, gain, win, wout):
    h, rstd = _rms_fwd(x, gain)
    zg, zu, act = _ffn_in(h, win)
    y = _ffn_out(act, wout, x)
    return y, (x, h, rstd, zg, zu, act)


def _ffn_backward(dy, dyb, saved, gain, win, wout, dep):
    x, h, rstd, zg, zu, act = saved
    dz = _ffn_bwd_dz(dyb, wout, zg, zu, dep)
    dwout = _mm_tn("ffn_bwd_dwout", act, dyb, scale=0.5, tm=512, tn=1024)
    dwin = _ffn_bwd_dwin(h, dz)
    dh = _ffn_bwd_dh(dz, win)
    dx, dxb, dgain = _rms_bwd(dh, x, rstd, gain, dy)
    return dx, dxb, dgain, dwin, dwout


def _mixer_forward(x, p):
    T = x.shape[0]
    h, rstd = _rms_fwd(x, p["norm_mix"])
    tm = _tile(T, (1024, 512))
    tn = 1152
    P = _mm("mix_in", h, p["wmix"],
            _bs((tm, D), lambda i, j, k: (i, 0)), _bs((D, tn), lambda i, j, k: (0, j)),
            _bs((tm, tn), lambda i, j, k: (i, j)), S((T, PW), F32), (T // tm, PW // tn, 1), NN, 1, (tm, tn))
    yc = _conv_fwd(P, p["cw"])
    cum = _fgate_fwd(P, p["fb"])
    cumt = cum[:, :NH].T
    tq = _att_tile(T)
    cumq, cumk = cumt.reshape(NH, T, 1), cumt.reshape(NH, T // tq, 1, tq)
    qkv = P[:, C_Q * 128:C_GATE * 128].astype(BF16)
    ya, lse = _attn_fwd(qkv, cumq, cumk)
    yl, hst = _lru_fwd(P, p["lcw"], p["lvec"], p["lru_w_a"], p["lru_w_x"])
    yn, r0, r1, r2 = _gnorm_fwd(yc, ya, yl, p["mix_out_norm"])
    y = _mm("mix_out", yn, p["wo"],
            _bs((tm, D), lambda i, j, k: (i, 0)), _bs((D, 1024), lambda i, j, k: (0, j)),
            _bs((tm, 1024), lambda i, j, k: (i, j)), S((T, D), F32), (T // tm, D // 1024, 1), NN, 1, (tm, 1024),
            res=x, r_spec=_bs((tm, 1024), lambda i, j, k: (i, j)))
    return y, (x, h, rstd, P, qkv, cumq, cumk, lse, yc, ya, yl, hst, yn, r0, r1, r2)


def _mixer_backward(dy, dyb, saved, p):
    x, h, rstd, P, qkv, cumq, cumk, lse, yc, ya, yl, hst, yn, r0, r1, r2 = saved
    T = x.shape[0]
    dyn = _mm_nt_full("mix_bwd_dyn", dyb, p["wo"], 512)
    dwo = _mm_tn("mix_bwd_dwo", yn, dyb, tm=512, tn=1024)
    dyc, dya, dyl, dgn = _gnorm_bwd(dyn, yc, ya, yl, r0, r1, r2, p["mix_out_norm"])
    dcb, dcc, dcv, dcw = _conv_bwd(P, p["cw"], dyc)
    dq, dk, dv, dck, dcq = _attn_bwd(qkv, cumq, cumk, lse, ya, dya)
    dcum = jnp.pad((dck.reshape(NH, T) + dcq.reshape(NH, T)).T, ((0, 0), (0, 128 - NH)))
    df, dfb = _fgate_bwd(P, p["fb"], dcum)
    dgate, dlx, lsm, dwa, dwx = _lru_bwd(P, p["lcw"], p["lvec"], p["lru_w_a"], p["lru_w_x"], hst, dyl)
    dP = jnp.concatenate([dcb, dcc, dcv, dq, dk, dv, dgate, dlx, df], axis=1).astype(BF16)
    tm = _tile(T, (512,))
    dh = _mm("mix_bwd_dh", dP, p["wmix"],
             _bs((tm, PW), lambda j, i, k: (i, 0)), _bs((1024, PW), lambda j, i, k: (j, 0)),
             _bs((tm, 1024), lambda j, i, k: (i, j)), S((T, D), F32), (D // 1024, T // tm, 1), NT, 1, (tm, 1024))
    dwmix = _mm_tn("mix_bwd_dwmix", h, dP, tm=512, tn=1152)
    dx, dxb, dgm = _rms_bwd(dh, x, rstd, p["norm_mix"], dy)
    small = dict(norm_mix=dgm[0], mix_out_norm=dgn[0], conv_w=dcw[:3], fgate_b=dfb[0, :NH], lru_conv_w=lsm[:4],
                 lru_conv_b=lsm[4], lru_b_a=lsm[5], lru_b_x=lsm[6], lru_lambda=lsm[7], lru_w_a=dwa, lru_w_x=dwx)
    return dx, dxb, small, dwmix, dwo


BIG =("ffn1_w_in", "ffn1_w_out", "mix_w_in", "mix_w_out", "ffn2_w_in", "ffn2_w_out")
SMALL = ("norm_ffn1", "norm_mix", "conv_w", "fgate_b", "lru_conv_w", "lru_conv_b", "lru_w_a", "lru_b_a", "lru_w_x",
         "lru_b_x", "lru_lambda", "mix_out_norm", "norm_ffn2", "final_norm")
WEIGHTS = ("norm_ffn1", "ffn1_w_in", "ffn1_w_out", "norm_mix", "mix_w_in", "conv_w", "fgate_b", "lru_conv_w", "lru_conv_b",
           "lru_w_a", "lru_b_a", "lru_w_x", "lru_b_x", "lru_lambda", "mix_out_norm", "mix_w_out", "norm_ffn2", "ffn2_w_in",
           "ffn2_w_out", "final_norm")


def _step(args):
    xx, yy, cc_ = lax.axis_index("x"), lax.axis_index("y"), lax.axis_index("c")
    me = 2 * xx + yy
    x0 = args["x"][0]
    tgt = args["loss_target"][0]
    T = x0.shape[0]
    L = args["norm_ffn1"].shape[0]

    def ag_sources(l):
        small = jnp.concatenate([args["conv_w"][l], args["lru_conv_w"][l], jnp.zeros((1, 128), F32)], axis=0)
        return [args[n][l].astype(BF16) for n in BIG] + [small]

    def layer_params(l, gat, gsm):
        w1i, w1o, wmx, wo, w2i, w2o = gat
        cwl = gsm.transpose(1, 0, 2).reshape(8, 4 * 128)
        return dict(
            w1i=w1i, w1o=w1o.reshape(-1, D), w2i=w2i, w2o=w2o.reshape(-1, D), wo=wo.reshape(D, D),
            wmix=_mix_pad(wmx.transpose(1, 0, 2).reshape(D, DIN)),
            cw=jnp.concatenate([cwl[:3], jnp.zeros((5, DC), F32)], axis=0),
            lcw=jnp.concatenate([cwl[3:7], jnp.zeros((4, DL), F32)], axis=0),
            fb=jnp.pad(args["fgate_b"][l], (0, 128 - NH)).reshape(1, 128),
            lvec=jnp.concatenate([args["lru_conv_b"][l][None], args["lru_b_a"][l][None], args["lru_b_x"][l][None],
                                  args["lru_lambda"][l][None], jnp.zeros((4, DL), F32)], axis=0),
            lru_w_a=args["lru_w_a"][l], lru_w_x=args["lru_w_x"][l],
            norm_ffn1=args["norm_ffn1"][l][None], norm_mix=args["norm_mix"][l][None],
            mix_out_norm=args["mix_out_norm"][l][None], norm_ffn2=args["norm_ffn2"][l][None])

    xs = x0
    saved, layers = [], []
    flight = _ag_start(0, ag_sources(0), x0)
    for l in range(L):
        ssem, rsem, srcs, lands, token = flight
        s_a, l_a = _ag_wait(f"ag_wait_{l}a", AG_FIRST, ssem, rsem, [srcs[t] for t in AG_FIRST], [lands[t] for t in AG_FIRST], xs)
        if l + 1 < L:
            flight = _ag_start(l + 1, ag_sources(l + 1), l_a[0])
            token = flight[4]
        w1i, w1o = _ag_finish(AG_FIRST, s_a, l_a, token)
        x1, s1 = _ffn_forward(xs, args["norm_ffn1"][l][None], w1i, w1o.reshape(-1, D))
        s_b, l_b = _ag_wait(f"ag_wait_{l}b", AG_REST, ssem, rsem, [srcs[t] for t in AG_REST], [lands[t] for t in AG_REST], x1)
        gsm, wmx, wo, w2i, w2o = _ag_finish(AG_REST, s_b, l_b, x1)
        p = layer_params(l, (w1i, w1o, wmx, wo, w2i, w2o), gsm)
        x2, s2 = _mixer_forward(x1, p)
        x3, s3 = _ffn_forward(x2, p["norm_ffn2"], p["w2i"], p["w2o"])
        saved.append((s1, s2, s3))
        layers.append(p)
        xs = x3
    lpart, dx, dxb, dfinal = _loss_head(xs, args["final_norm"][None], tgt)
    loss = lax.psum(lpart[0, 0], ("x", "y", "c"))

    acc = None
    small_grads = [None] * L
    pending = None
    dep = lpart

    def owner(pending, acc, after):
        lp, grads, recv, (ssem, rsem, sums, lands, _) = pending
        ici = _rs_wait(lp, ssem, rsem, sums, lands, after)
        return [_owner_sum(g, r, i3, a, lp) for g, r, i3, a in zip(grads, recv, ici, acc)]

    for l in reversed(range(L)):
        p = layers[l]
        s1, s2, s3 = saved[l]
        dx, dxb, dg2, dw2i, dw2o = _ffn_backward(dx, dxb, s3, p["norm_ffn2"], p["w2i"], p["w2o"], dep)
        dx, dxb, sm, dwmix, dwo = _mixer_backward(dx, dxb, s2, p)
        dx, dxb, dg1, dw1i, dw1o = _ffn_backward(dx, dxb, s1, p["norm_ffn1"], p["w1i"], p["w1o"], dep)
        sm["norm_ffn1"] = dg1[0]
        sm["norm_ffn2"] = dg2[0]
        small_grads[l] = sm
        F4 = dw1o.shape[0] // 4
        grads = [dw1i, dw1o.reshape(4, F4, D), _mix_unpad(dwmix).reshape(D, 4, DIN // 4).transpose(1, 0, 2),
                 dwo.reshape(4, D // 4, D), dw2i, dw2o.reshape(4, F4, D)]
        if acc is None:
            acc = [jnp.zeros((L, g.shape[1] // 2, g.shape[2]), F32) for g in grads]
        if pending is not None:
            acc = owner(pending, acc, dx)
        recv = _rs_pair(grads)
        sums = [_pair_sum(g, r) for g, r in zip(grads, recv)]
        started = _rs_start(l, sums, dx)
        pending = (l, grads, recv, started)
        dep = started[4]
    res = [None] * N_BIG
    after = dx
    if L > 1:
        sib = _rs_join(acc, 1, L, pending[3][4])
        for k, n in enumerate(BIG):
            res[k] = _adamw_big(args[n], acc[k], sib[k], args["m_" + n], args["v_" + n], 1, L, None)
        after = res[N_BIG - 1][0]
    acc = owner(pending, acc, after)
    sib = _rs_join(acc, 0, 1, dx)
    out = {"loss": loss, "grad_x": dx[None]}
    for k, n in enumerate(BIG):
        g, d, m, v = _adamw_big(args[n], acc[k], sib[k], args["m_" + n], args["v_" + n], 0, 1, res[k])
        out["grad_" + n], out["delta_" + n], out["new_m_" + n], out["new_v_" + n] = g, d, m, v

    full = {n: (dfinal[0] if n == "final_norm" else jnp.stack([small_grads[l][n] for l in range(L)])) for n in SMALL}
    shapes = [full[n].shape for n in SMALL]
    red = dict(zip(SMALL, _unpack(_all_reduce_small(_pack([full[n] for n in SMALL])), shapes)))
    for n in ("conv_w", "lru_conv_w"):
        red[n] = lax.dynamic_slice_in_dim(red[n], me * 128, 128, axis=2)
    oshapes = [args[n].shape for n in SMALL]
    d, m, v = _adamw_small(_pack([args[n] for n in SMALL]), _pack([red[n] for n in SMALL]),
                           _pack([args["m_" + n] for n in SMALL]), _pack([args["v_" + n] for n in SMALL]))
    for n, gg, dd, mm, vv in zip(SMALL, [red[n] for n in SMALL], _unpack(d, oshapes), _unpack(m, oshapes), _unpack(v, oshapes)):
        out["grad_" + n], out["delta_" + n], out["new_m_" + n], out["new_v_" + n] = gg, dd, mm, vv
    return out


def kernel(x, norm_ffn1, ffn1_w_in, ffn1_w_out, norm_mix, mix_w_in, conv_w, fgate_b, lru_conv_w, lru_conv_b, lru_w_a, lru_b_a, lru_w_x, lru_b_x, lru_lambda, mix_out_norm, mix_w_out, norm_ffn2, ffn2_w_in, ffn2_w_out, final_norm, loss_target, m_norm_ffn1, m_ffn1_w_in, m_ffn1_w_out, m_norm_mix, m_mix_w_in, m_conv_w, m_fgate_b, m_lru_conv_w, m_lru_conv_b, m_lru_w_a, m_lru_b_a, m_lru_w_x, m_lru_b_x, m_lru_lambda, m_mix_out_norm, m_mix_w_out, m_norm_ffn2, m_ffn2_w_in, m_ffn2_w_out, m_final_norm, v_norm_ffn1, v_ffn1_w_in, v_ffn1_w_out, v_norm_mix, v_mix_w_in, v_conv_w, v_fgate_b, v_lru_conv_w, v_lru_conv_b, v_lru_w_a, v_lru_b_a, v_lru_w_x, v_lru_b_x, v_lru_lambda, v_mix_out_norm, v_mix_w_out, v_norm_ffn2, v_ffn2_w_in, v_ffn2_w_out, v_final_norm):
    args = dict(locals())
    out = _step(args)
    res = [out["loss"], out["grad_x"]]
    for prefix in ("grad_", "delta_", "new_m_", "new_v_"):
        res += [out[prefix + n] for n in WEIGHTS]
    return tuple(res)
```

```python
import functools
import math

import jax
import jax.numpy as jnp
from jax import lax
from jax.experimental import pallas as pl
from jax.experimental.pallas import tpu as pltpu

F32 = jnp.float32
BF16 = jnp.bfloat16
S = jax.ShapeDtypeStruct
MESH = pl.DeviceIdType.MESH

D = 2048
DC = 512
DA = 1024
NH = 8
HD = 128
DL = 512
LB = 128
DIN = 5640
PW = 5760
C_Q, C_K, C_V = 12, 20, 28
C_GATE, C_LX, C_F = 36, 40, 44
EPS = 1e-6
LRU_C = 8.0
ATT_SCALE = HD ** -0.5
LR, B1, B2, AEPS, WD, STEP = 0.001, 0.9, 0.999, 1e-08, 0.01, 10
VMEM_LIMIT = 56 * 1024 * 1024

NT = (((1,), (1,)), ((), ()))
TN = (((0,), (0,)), ((), ()))
NN = (((1,), (0,)), ((), ()))


def _cp():
    return pltpu.CompilerParams(vmem_limit_bytes=VMEM_LIMIT)


def _bs(shape, fn):
    return pl.BlockSpec(shape, fn)


def _mm(name, a, b, a_spec, b_spec, o_spec, o_shape, grid, dims, nk, acc_tile, scale=1.0, res=None, r_spec=None, dep=None):
    has_res = res is not None
    has_dep = dep is not None

    def body(*refs):
        if has_dep:
            refs = refs[:2 + has_res] + refs[3 + has_res:]
        if has_res:
            a_ref, b_ref, r_ref, o_ref = refs[:4]
            rest = refs[4:]
        else:
            a_ref, b_ref, o_ref = refs[:3]
            rest = refs[3:]
        prod = lax.dot_general(a_ref[...].astype(BF16), b_ref[...].astype(BF16), dims, preferred_element_type=F32)

        def finish(acc):
            if scale != 1.0:
                acc = acc * scale
            if has_res:
                acc = r_ref[...] + acc
            o_ref[...] = acc.astype(o_ref.dtype)

        if nk == 1:
            finish(prod)
        else:
            acc_ref = rest[0]
            k = pl.program_id(2)

            @pl.when(k == 0)
            def _():
                acc_ref[...] = prod

            @pl.when(k > 0)
            def _():
                acc_ref[...] += prod

            @pl.when(k == nk - 1)
            def _():
                finish(acc_ref[...])

    in_specs = [a_spec, b_spec] + ([r_spec] if has_res else []) + ([pl.BlockSpec(memory_space=pl.ANY)] if has_dep else [])
    args = (a, b) + ((res,) if has_res else ()) + ((dep,) if has_dep else ())
    scratch = [pltpu.VMEM(acc_tile, F32)] if nk > 1 else []
    return pl.pallas_call(body, grid=grid, in_specs=in_specs, out_specs=o_spec, out_shape=o_shape,
                          scratch_shapes=scratch, compiler_params=_cp(), name=name)(*args)


def _tile(n, pref):
    for t in pref:
        if n % t == 0:
            return t
    return n


def _rms_fwd(x, gain):
    T = x.shape[0]
    tb = _tile(T, (512,))

    def body(x_ref, g_ref, h_ref, r_ref):
        xv = x_ref[...]
        r = lax.rsqrt(jnp.mean(xv * xv, axis=1, keepdims=True) + EPS)
        h_ref[...] = (xv * r * g_ref[...]).astype(BF16)
        r_ref[...] = r

    return pl.pallas_call(
        body, grid=(T // tb,),
        in_specs=[_bs((tb, D), lambda i: (i, 0)), _bs((1, D), lambda i: (0, 0))],
        out_specs=[_bs((tb, D), lambda i: (i, 0)), _bs((tb, 1), lambda i: (i, 0))],
        out_shape=[S((T, D), BF16), S((T, 1), F32)], compiler_params=_cp(), name="rms_fwd")(x, gain)


def _rms_bwd(dh, x, rstd, gain, dres):
    T = x.shape[0]
    tb = _tile(T, (512,))

    def body(dh_ref, x_ref, r_ref, g_ref, dres_ref, dx_ref, dxb_ref, dg_ref):
        i = pl.program_id(0)
        r = r_ref[...]
        xhat = x_ref[...] * r
        dh = dh_ref[...]
        dxh = dh * g_ref[...]
        m = jnp.mean(dxh * xhat, axis=1, keepdims=True)
        dx = dres_ref[...] + r * (dxh - xhat * m)
        dx_ref[...] = dx
        dxb_ref[...] = dx.astype(BF16)
        part = jnp.sum(dh * xhat, axis=0, keepdims=True)

        @pl.when(i == 0)
        def _():
            dg_ref[...] = part

        @pl.when(i > 0)
        def _():
            dg_ref[...] += part

    row = _bs((tb, D), lambda i: (i, 0))
    return pl.pallas_call(
        body, grid=(T // tb,),
        in_specs=[row, row, _bs((tb, 1), lambda i: (i, 0)), _bs((1, D), lambda i: (0, 0)), row],
        out_specs=[row, row, _bs((1, D), lambda i: (0, 0))],
        out_shape=[S((T, D), F32), S((T, D), BF16), S((1, D), F32)], compiler_params=_cp(), name="rms_bwd")(dh, x, rstd, gain, dres)


def _loss_head(x, gain, tgt):
    T = x.shape[0]
    tb = _tile(T, (512,))

    def body(x_ref, g_ref, t_ref, l_ref, dx_ref, dxb_ref, dg_ref):
        i = pl.program_id(0)
        xv = x_ref[...]
        g = g_ref[...]
        r = lax.rsqrt(jnp.mean(xv * xv, axis=1, keepdims=True) + EPS)
        xhat = xv * r
        e = xhat * g - t_ref[...]
        lpart = 0.5 * jnp.sum(jnp.sum(e * e, axis=1, keepdims=True), axis=0, keepdims=True) * (1.0 / D)
        dy = e * (1.0 / D)
        dxh = dy * g
        m = jnp.mean(dxh * xhat, axis=1, keepdims=True)
        dx = r * (dxh - xhat * m)
        dx_ref[...] = dx
        dxb_ref[...] = dx.astype(BF16)
        gpart = jnp.sum(dy * xhat, axis=0, keepdims=True)
        lrow = jnp.broadcast_to(lpart, (1, 128))

        @pl.when(i == 0)
        def _():
            dg_ref[...] = gpart
            l_ref[...] = lrow

        @pl.when(i > 0)
        def _():
            dg_ref[...] += gpart
            l_ref[...] += lrow

    row = _bs((tb, D), lambda i: (i, 0))
    return pl.pallas_call(
        body, grid=(T // tb,),
        in_specs=[row, _bs((1, D), lambda i: (0, 0)), row],
        out_specs=[_bs((1, 128), lambda i: (0, 0)), row, row, _bs((1, D), lambda i: (0, 0))],
        out_shape=[S((1, 128), F32), S((T, D), F32), S((T, D), BF16), S((1, D), F32)],
        compiler_params=_cp(), name="loss_head")(x, gain, tgt)


def _sigmoid(z):
    return 0.5 * jnp.tanh(0.5 * z) + 0.5


def _ffn_in(h, win):
    T = h.shape[0]
    Fs = win.shape[2]
    F = 2 * Fs
    tn = _tile(Fs, (256, 128))
    nb = Fs // tn
    tm = _tile(T, (1024, 512))

    def body(h_ref, wg_ref, wu_ref, zg_ref, zu_ref, a_ref):
        hv = h_ref[...]
        zg = jnp.dot(hv, wg_ref[...], preferred_element_type=F32)
        zu = jnp.dot(hv, wu_ref[...], preferred_element_type=F32)
        zg_ref[...] = zg.astype(BF16)
        zu_ref[...] = zu.astype(BF16)
        a_ref[...] = (zg * _sigmoid(zg) * zu).astype(BF16)

    col = _bs((tm, tn), lambda i, j: (i, j))
    return pl.pallas_call(
        body, grid=(T // tm, F // tn),
        in_specs=[_bs((tm, D), lambda i, j: (i, 0)),
                  _bs((None, D, tn), lambda i, j: (j // nb, 0, j % nb)),
                  _bs((None, D, tn), lambda i, j: (2 + j // nb, 0, j % nb))],
        out_specs=[col, col, col],
        out_shape=[S((T, F), BF16)] * 3, compiler_params=_cp(), name="ffn_in")(h, win, win)


def _ffn_out(act, wout, x):
    T, F = act.shape
    tm = _tile(T, (512,))
    tn = 1024
    return _mm("ffn_out", act, wout,
               _bs((tm, F), lambda j, i, k: (i, 0)), _bs((F, tn), lambda j, i, k: (0, j)),
               _bs((tm, tn), lambda j, i, k: (i, j)), S((T, D), F32), (D // tn, T // tm, 1), NN, 1, (tm, tn),
               scale=0.5, res=x, r_spec=_bs((tm, tn), lambda j, i, k: (i, j)))


def _ffn_bwd_dz(dyb, wout, zg, zu, dep):
    T, F = zg.shape
    tm = _tile(T, (1024, 512))
    tn = _tile(F, (512, 256))

    ch = _tile(tm, (256,))

    def body(dy_ref, w_ref, zg_ref, zu_ref, dep_ref, dz_ref):
        for r in range(tm // ch):
            rows = slice(r * ch, (r + 1) * ch)
            da = 0.5 * lax.dot_general(dy_ref[rows, :], w_ref[...], NT, preferred_element_type=F32)
            zg = zg_ref[rows, :].astype(F32)
            zu = zu_ref[rows, :].astype(F32)
            s = _sigmoid(zg)
            dz_ref[0, rows, :] = (da * zu * (s * (1.0 + zg * (1.0 - s)))).astype(BF16)
            dz_ref[1, rows, :] = (da * (zg * s)).astype(BF16)

    col = _bs((tm, tn), lambda i, j: (i, j))
    return pl.pallas_call(
        body, grid=(T // tm, F // tn),
        in_specs=[_bs((tm, D), lambda i, j: (i, 0)), _bs((tn, D), lambda i, j: (j, 0)), col, col,
                  pl.BlockSpec(memory_space=pl.ANY)],
        out_specs=_bs((2, tm, tn), lambda i, j: (0, i, j)), out_shape=S((2, T, F), BF16),
        compiler_params=_cp(), name="ffn_bwd_dz")(dyb, wout, zg, zu, dep)


def _ffn_bwd_dh(dz, win):
    _, T, F = dz.shape
    Fs = win.shape[2]
    tk = _tile(Fs, (1408, 256, 128))
    nkb = Fs // tk
    tm = _tile(T, (1024, 512))
    tn = 1024
    nk = 2 * nkb

    def body(dzg_ref, dzu_ref, wg_ref, wu_ref, o_ref, acc_ref):
        k = pl.program_id(2)
        prod = (lax.dot_general(dzg_ref[...], wg_ref[...], NT, preferred_element_type=F32)
                + lax.dot_general(dzu_ref[...], wu_ref[...], NT, preferred_element_type=F32))

        @pl.when(k == 0)
        def _():
            acc_ref[...] = prod

        @pl.when(k > 0)
        def _():
            acc_ref[...] += prod

        @pl.when(k == nk - 1)
        def _():
            o_ref[...] = acc_ref[...]

    def a_spec(half):
        return _bs((None, tm, tk), lambda i, j, k: (half, i, k))

    return pl.pallas_call(
        body, grid=(T // tm, D // tn, nk),
        in_specs=[a_spec(0), a_spec(1),
                  _bs((None, tn, tk), lambda i, j, k: (k // nkb, j, k % nkb)),
                  _bs((None, tn, tk), lambda i, j, k: (2 + k // nkb, j, k % nkb))],
        out_specs=_bs((tm, tn), lambda i, j, k: (i, j)), out_shape=S((T, D), F32),
        scratch_shapes=[pltpu.VMEM((tm, tn), F32)], compiler_params=_cp(), name="ffn_bwd_dh")(dz, dz, win, win)


def _ffn_bwd_dwin(h, dz):
    _, T, F = dz.shape
    Fs = F // 2
    tn = _tile(Fs, (1408, 256, 128))
    nb = Fs // tn
    tm = 512
    return _mm("ffn_bwd_dwin", h, dz,
               _bs((T, tm), lambda i, j, k: (0, i)), _bs((None, T, tn), lambda i, j, k: (j // (2 * nb), 0, j % (2 * nb))),
               _bs((None, tm, tn), lambda i, j, k: (j // nb, i, j % nb)), S((4, D, Fs), BF16),
               (D // tm, 4 * nb, 1), TN, 1, (tm, tn))


def _mm_tn(name, a, b, scale=1.0, tm=512, tn=1024):
    T, M = a.shape
    N = b.shape[1]
    tm = _tile(M, (tm, 512, 256, 128))
    tn = _tile(N, (tn, 1152, 1024, 512, 128))
    return _mm(name, a, b,
               _bs((T, tm), lambda i, j, k: (0, i)), _bs((T, tn), lambda i, j, k: (0, j)),
               _bs((tm, tn), lambda i, j, k: (i, j)), S((M, N), BF16), (M // tm, N // tn, 1), TN, 1, (tm, tn), scale=scale)


def _mm_nt_full(name, a, b, tn, dep=None):
    T, K = a.shape
    N = b.shape[0]
    tm = _tile(T, (1024, 512))
    return _mm(name, a, b,
               _bs((tm, K), lambda i, j, k: (i, 0)), _bs((tn, K), lambda i, j, k: (j, 0)),
               _bs((tm, tn), lambda i, j, k: (i, j)), S((T, N), F32), (T // tm, N // tn, 1), NT, 1, (tm, tn), dep=dep)


def _bt(T):
    return _tile(T, (512,))


def _down(ext, s):
    return pltpu.roll(ext, s, 0)[8:, :]


def _up(ext, s):
    n = ext.shape[0]
    return pltpu.roll(ext, n - s, 0)[: n - 8, :]


def _halo_prev(ref, start, b):
    lo = pl.multiple_of(jnp.maximum(start - 8, 0), 8)
    return ref[pl.ds(lo, 8), :] * (b > 0).astype(F32)


def _halo_next(ref, start, bt, b, nb):
    lo = pl.multiple_of(jnp.minimum(start + bt, (nb - 1) * bt), 8)
    return ref[pl.ds(lo, 8), :] * (b < nb - 1).astype(F32)


def _scan_fwd(A, U):
    n = U.shape[0]
    row = lax.broadcasted_iota(jnp.int32, U.shape, 0)
    d = 1
    while d < n:
        keep = row >= d
        Us = jnp.where(keep, pltpu.roll(U, d, 0), 0.0)
        if A is None:
            U = U + Us
        else:
            As = jnp.where(keep, pltpu.roll(A, d, 0), 1.0)
            U = A * Us + U
            A = A * As
        d *= 2
    return A, U


def _scan_bwd(A, U):
    n = U.shape[0]
    row = lax.broadcasted_iota(jnp.int32, U.shape, 0)
    d = 1
    while d < n:
        keep = row < n - d
        Us = jnp.where(keep, pltpu.roll(U, n - d, 0), 0.0)
        if A is None:
            U = U + Us
        else:
            As = jnp.where(keep, pltpu.roll(A, n - d, 0), 1.0)
            U = A * Us + U
            A = A * As
        d *= 2
    return A, U


def _softplus(z):
    return jnp.maximum(z, 0.0) + jnp.log(1.0 + jnp.exp(-jnp.abs(z)))


def _gelu_parts(g):
    k0 = math.sqrt(2.0 / math.pi)
    t = jnp.tanh(k0 * (g + 0.044715 * g * g * g))
    gel = 0.5 * g * (1.0 + t)
    dgel = 0.5 * (1.0 + t) + 0.5 * g * (1.0 - t * t) * k0 * (1.0 + 3.0 * 0.044715 * g * g)
    return gel, dgel


def _conv_fwd(P, cw):
    T = P.shape[0]
    bt = _bt(T)
    nb = T // bt

    def body(b_ref, c_ref, v_ref, w_ref, y_ref):
        w = w_ref[...]

        def step(b, carry):
            start = pl.multiple_of(b * bt, bt)
            rows = pl.ds(start, bt)
            m = c_ref[rows, :] * v_ref[rows, :]
            ext = jnp.concatenate([_halo_prev(c_ref, start, b) * _halo_prev(v_ref, start, b), m], axis=0)
            z = w[2:3, :] * m + w[1:2, :] * _down(ext, 1) + w[0:1, :] * _down(ext, 2)
            y_ref[rows, :] = b_ref[rows, :] * z
            return carry

        lax.fori_loop(0, nb, step, 0)

    def colspec(off):
        return _bs((T, 128), lambda c: (0, off + c))

    return pl.pallas_call(
        body, grid=(DC // 128,),
        in_specs=[colspec(0), colspec(4), colspec(8), _bs((8, 128), lambda c: (0, c))],
        out_specs=_bs((T, 128), lambda c: (0, c)), out_shape=S((T, DC), F32),
        compiler_params=_cp(), name="conv_fwd")(P, P, P, cw)


def _conv_bwd(P, cw, dy):
    T = P.shape[0]
    bt = _bt(T)
    nb = T // bt

    def body(b_ref, c_ref, v_ref, w_ref, dy_ref, db_ref, dc_ref, dv_ref, dw_ref):
        w = w_ref[...]

        def step(b, carry):
            a0, a1, a2 = carry
            start = pl.multiple_of(b * bt, bt)
            rows = pl.ds(start, bt)
            cb, cc, cv, dy = b_ref[rows, :], c_ref[rows, :], v_ref[rows, :], dy_ref[rows, :]
            m = cc * cv
            ext = jnp.concatenate([_halo_prev(c_ref, start, b) * _halo_prev(v_ref, start, b), m], axis=0)
            m1, m2 = _down(ext, 1), _down(ext, 2)
            z = w[2:3, :] * m + w[1:2, :] * m1 + w[0:1, :] * m2
            db_ref[rows, :] = dy * z
            dz = dy * cb
            extn = jnp.concatenate([dz, _halo_next(dy_ref, start, bt, b, nb) * _halo_next(b_ref, start, bt, b, nb)], axis=0)
            dm = w[2:3, :] * dz + w[1:2, :] * _up(extn, 1) + w[0:1, :] * _up(extn, 2)
            dc_ref[rows, :] = dm * cv
            dv_ref[rows, :] = dm * cc
            return (a0 + jnp.sum(dz * m2, axis=0, keepdims=True),
                    a1 + jnp.sum(dz * m1, axis=0, keepdims=True),
                    a2 + jnp.sum(dz * m, axis=0, keepdims=True))

        zero = jnp.zeros((1, 128), F32)
        a0, a1, a2 = lax.fori_loop(0, nb, step, (zero, zero, zero))
        dw_ref[...] = jnp.zeros((8, 128), F32)
        dw_ref[0:1, :] = a0
        dw_ref[1:2, :] = a1
        dw_ref[2:3, :] = a2

    def colspec(off):
        return _bs((T, 128), lambda c: (0, off + c))

    own = _bs((T, 128), lambda c: (0, c))
    return pl.pallas_call(
        body, grid=(DC // 128,),
        in_specs=[colspec(0), colspec(4), colspec(8), _bs((8, 128), lambda c: (0, c)), own],
        out_specs=[own, own, own, _bs((8, 128), lambda c: (0, c))],
        out_shape=[S((T, DC), F32)] * 3 + [S((8, DC), F32)], compiler_params=_cp(), name="conv_bwd")(P, P, P, cw, dy)


def _fgate_fwd(P, fb):
    T = P.shape[0]
    bt = _bt(T)
    nb = T // bt

    def body(f_ref, b_ref, c_ref):
        bias = b_ref[...]

        def step(b, carry):
            rows = pl.ds(pl.multiple_of(b * bt, bt), bt)
            logf = -_softplus(-(f_ref[rows, :] + bias))
            _, cs = _scan_fwd(None, logf)
            cs = cs + carry
            c_ref[rows, :] = cs
            return cs[bt - 1:bt, :]

        lax.fori_loop(0, nb, step, jnp.zeros((1, 128), F32))

    return pl.pallas_call(
        body, grid=(1,),
        in_specs=[_bs((T, 128), lambda i: (0, C_F)), _bs((1, 128), lambda i: (0, 0))],
        out_specs=_bs((T, 128), lambda i: (0, 0)), out_shape=S((T, 128), F32),
        compiler_params=_cp(), name="fgate_fwd")(P, fb)


def _fgate_bwd(P, fb, dcum):
    T = P.shape[0]
    bt = _bt(T)
    nb = T // bt

    def body(f_ref, b_ref, dc_ref, df_ref, db_ref):
        bias = b_ref[...]

        def step(i, carry):
            run, acc = carry
            b = nb - 1 - i
            rows = pl.ds(pl.multiple_of(b * bt, bt), bt)
            _, rs = _scan_bwd(None, dc_ref[rows, :])
            rs = rs + run
            df = rs * jax.nn.sigmoid(-(f_ref[rows, :] + bias))
            df_ref[rows, :] = df
            return rs[0:1, :], acc + jnp.sum(df, axis=0, keepdims=True)

        zero = jnp.zeros((1, 128), F32)
        _, acc = lax.fori_loop(0, nb, step, (zero, zero))
        db_ref[...] = acc

    return pl.pallas_call(
        body, grid=(1,),
        in_specs=[_bs((T, 128), lambda i: (0, C_F)), _bs((1, 128), lambda i: (0, 0)), _bs((T, 128), lambda i: (0, 0))],
        out_specs=[_bs((T, 128), lambda i: (0, 0)), _bs((1, 128), lambda i: (0, 0))],
        out_shape=[S((T, 128), F32), S((1, 128), F32)], compiler_params=_cp(), name="fgate_bwd")(P, fb, dcum)


def _att_tile(T):
    return _tile(T, (512,))


def _causal_mask(tq):
    return lax.broadcasted_iota(jnp.int32, (tq, tq), 1) <= lax.broadcasted_iota(jnp.int32, (tq, tq), 0)


def _attn_fwd(qkv, cumq, cumk):
    T = qkv.shape[0]
    tq = _att_tile(T)
    nq = T // tq

    def body(q_ref, k_ref, v_ref, cq_ref, ck_ref, o_ref, lse_ref):
        i = pl.program_id(1)
        q = q_ref[...]
        cq = cq_ref[...]

        def block(j, carry, diagonal):
            m_old, l_old, acc = carry
            rows = pl.ds(pl.multiple_of(j * tq, tq), tq)
            s = lax.dot_general(q, k_ref[rows, :], NT, preferred_element_type=F32)
            s = s * ATT_SCALE + cq - ck_ref[j]
            if diagonal:
                s = jnp.where(_causal_mask(tq), s, -jnp.inf)
            m_new = jnp.maximum(m_old, jnp.max(s, axis=1, keepdims=True))
            p = jnp.exp(s - m_new)
            alpha = jnp.exp(m_old - m_new)
            l_new = alpha * l_old + jnp.sum(p, axis=1, keepdims=True)
            acc = alpha * acc + jnp.dot(p.astype(BF16), v_ref[rows, :], preferred_element_type=F32)
            return m_new, l_new, acc

        init = (jnp.full((tq, 1), -jnp.inf, F32), jnp.zeros((tq, 1), F32), jnp.zeros((tq, HD), F32))
        carry = lax.fori_loop(0, i, lambda j, c: block(j, c, False), init)
        m, l, acc = block(i, carry, True)
        o_ref[...] = acc / l
        lse_ref[...] = m + jnp.log(l)

    return pl.pallas_call(
        body, grid=(NH, nq),
        in_specs=[_bs((tq, HD), lambda h, i: (i, h)),
                  _bs((T, HD), lambda h, i: (0, NH + h)),
                  _bs((T, HD), lambda h, i: (0, 2 * NH + h)),
                  _bs((None, tq, 1), lambda h, i: (h, i, 0)),
                  _bs((None, nq, 1, tq), lambda h, i: (h, 0, 0, 0))],
        out_specs=[_bs((tq, HD), lambda h, i: (i, h)), _bs((None, tq, 1), lambda h, i: (h, i, 0))],
        out_shape=[S((T, DA), F32), S((NH, T, 1), F32)],
        compiler_params=_cp(), name="attn_fwd")(qkv, qkv, qkv, cumq, cumk)


def _attn_bwd(qkv, cumq, cumk, lse, o, do):
    T = qkv.shape[0]
    tq = _att_tile(T)
    nq = T // tq

    def body(q_ref, k_ref, v_ref, cq_ref, ck_ref, lse_ref, o_ref, do_ref, dq_ref, dk_ref, dv_ref, dc_ref, dr_ref):
        j = pl.program_id(1)

        @pl.when(j == 0)
        def _():
            dq_ref[...] = jnp.zeros((T, HD), F32)
            dr_ref[...] = jnp.zeros((T, 1), F32)

        k = k_ref[...]
        v = v_ref[...]
        ck = ck_ref[...]

        def block(i, carry, diagonal):
            dk_acc, dv_acc, dc_acc = carry
            rows = pl.ds(pl.multiple_of(i * tq, tq), tq)
            q = q_ref[rows, :]
            do_f = do_ref[rows, :]
            dob = do_f.astype(BF16)
            s = lax.dot_general(q, k, NT, preferred_element_type=F32)
            p = jnp.exp(s * ATT_SCALE + cq_ref[rows, :] - ck - lse_ref[rows, :])
            if diagonal:
                p = jnp.where(_causal_mask(tq), p, 0.0)
            delta = jnp.sum(do_f * o_ref[rows, :], axis=1, keepdims=True)
            dp = lax.dot_general(dob, v, NT, preferred_element_type=F32)
            ds = p * (dp - delta)
            dsb = (ds * ATT_SCALE).astype(BF16)
            dq_ref[rows, :] += jnp.dot(dsb, k, preferred_element_type=F32)
            dr_ref[rows, :] += jnp.sum(ds, axis=1, keepdims=True)
            return (dk_acc + lax.dot_general(dsb, q, TN, preferred_element_type=F32),
                    dv_acc + lax.dot_general(p.astype(BF16), dob, TN, preferred_element_type=F32),
                    dc_acc - jnp.sum(ds, axis=0, keepdims=True))

        init = (jnp.zeros((tq, HD), F32), jnp.zeros((tq, HD), F32), jnp.zeros((1, tq), F32))
        carry = block(j, init, True)
        dk_acc, dv_acc, dc_acc = lax.fori_loop(j + 1, nq, lambda i, c: block(i, c, False), carry)
        dk_ref[...] = dk_acc
        dv_ref[...] = dv_acc
        dc_ref[...] = dc_acc

    def whole(col):
        return _bs((T, HD), lambda h, j: (0, col + h))

    qvec = _bs((None, T, 1), lambda h, j: (h, 0, 0))
    kv_out = _bs((tq, HD), lambda h, j: (j, h))
    return pl.pallas_call(
        body, grid=(NH, nq),
        in_specs=[whole(0), _bs((tq, HD), lambda h, j: (j, NH + h)), _bs((tq, HD), lambda h, j: (j, 2 * NH + h)),
                  qvec, _bs((None, None, 1, tq), lambda h, j: (h, j, 0, 0)), qvec, whole(0), whole(0)],
        out_specs=[whole(0), kv_out, kv_out, _bs((None, 1, tq), lambda h, j: (h, 0, j)), qvec],
        out_shape=[S((T, DA), F32)] * 3 + [S((NH, 1, T), F32), S((NH, T, 1), F32)],
        compiler_params=_cp(), name="attn_bwd")(qkv, qkv, qkv, cumq, cumk, lse, o, do)


def _lru_gates(xr, wa, wx, ba, bx, sp):
    xb = xr.astype(BF16)
    r = jax.nn.sigmoid(jnp.dot(xb, wa, preferred_element_type=F32) + ba)
    ig = jax.nn.sigmoid(jnp.dot(xb, wx, preferred_element_type=F32) + bx)
    log_a = -LRU_C * r * sp
    a = jnp.exp(log_a)
    th = jnp.tanh(log_a)
    om = -2.0 * th / (1.0 - th)
    mult = jnp.sqrt(om)
    return xb, r, ig, a, om, mult


def _lru_xr(lx_ref, cw, cb, start, b, rows):
    lx = lx_ref[rows, :]
    ext = jnp.concatenate([_halo_prev(lx_ref, start, b), lx], axis=0)
    return cw[3:4, :] * lx + cw[2:3, :] * _down(ext, 1) + cw[1:2, :] * _down(ext, 2) + cw[0:1, :] * _down(ext, 3) + cb


def _lru_fwd(P, lcw, vec, wa, wx):
    T = P.shape[0]
    bt = _bt(T)
    nb = T // bt

    def body(g_ref, lx_ref, cw_ref, vec_ref, wa_ref, wx_ref, y_ref, h_ref):
        cw = cw_ref[...]
        vec = vec_ref[...]
        wa = wa_ref[...].astype(BF16)
        wx = wx_ref[...].astype(BF16)
        sp = _softplus(-vec[3:4, :])

        def step(b, carry):
            start = pl.multiple_of(b * bt, bt)
            rows = pl.ds(start, bt)
            xr = _lru_xr(lx_ref, cw, vec[0:1, :], start, b, rows)
            _, _, ig, a, _, mult = _lru_gates(xr, wa, wx, vec[1:2, :], vec[2:3, :], sp)
            u = mult * (ig * xr)
            ac, hc = _scan_fwd(a, u)
            hb = hc + ac * carry
            h_ref[rows, :] = hb
            gel, _ = _gelu_parts(g_ref[rows, :])
            y_ref[rows, :] = gel * hb
            return hb[bt - 1:bt, :]

        lax.fori_loop(0, nb, step, jnp.zeros((1, 128), F32))

    own = _bs((T, 128), lambda c: (0, c))
    return pl.pallas_call(
        body, grid=(DL // 128,),
        in_specs=[_bs((T, 128), lambda c: (0, C_GATE + c)), _bs((T, 128), lambda c: (0, C_LX + c)),
                  _bs((8, 128), lambda c: (0, c)), _bs((8, 128), lambda c: (0, c)),
                  _bs((None, LB, LB), lambda c: (c, 0, 0)), _bs((None, LB, LB), lambda c: (c, 0, 0))],
        out_specs=[own, own], out_shape=[S((T, DL), F32)] * 2, compiler_params=_cp(), name="lru_fwd")(P, P, lcw, vec, wa, wx)


def _lru_bwd(P, lcw, vec, wa, wx, hst, dy):
    T = P.shape[0]
    bt = _bt(T)
    nb = T // bt

    def body(g_ref, lx_ref, cw_ref, vec_ref, wa_ref, wx_ref, h_ref, dy_ref,
             dg_ref, dlx_ref, sm_ref, dwa_ref, dwx_ref, dxr_s):
        cw = cw_ref[...]
        vec = vec_ref[...]
        wa = wa_ref[...].astype(BF16)
        wx = wx_ref[...].astype(BF16)
        lam = vec[3:4, :]
        sp = _softplus(-lam)
        dwa_ref[...] = jnp.zeros((LB, LB), F32)
        dwx_ref[...] = jnp.zeros((LB, LB), F32)
        zero = jnp.zeros((1, 128), F32)

        def step1(i, carry):
            wc, s_cb, s_ba, s_bx, s_sp = carry
            b = nb - 1 - i
            start = pl.multiple_of(b * bt, bt)
            rows = pl.ds(start, bt)
            xr = _lru_xr(lx_ref, cw, vec[0:1, :], start, b, rows)
            xb, r, ig, a, om, mult = _lru_gates(xr, wa, wx, vec[1:2, :], vec[2:3, :], sp)
            hb = h_ref[rows, :]
            dy = dy_ref[rows, :]
            gel, dgel = _gelu_parts(g_ref[rows, :])
            dg_ref[rows, :] = dy * hb * dgel
            dh = dy * gel
            ac, wcum = _scan_bwd(a, a * dh)
            w = wcum + ac * wc
            g = dh + _up(jnp.concatenate([w, jnp.broadcast_to(wc, (8, 128))], axis=0), 1)
            hprev = _down(jnp.concatenate([_halo_prev(h_ref, start, b), hb], axis=0), 1)
            da = g * hprev
            dmult = g * (ig * xr)
            dix = g * mult
            di = dix * xr
            dlog_a = da * a - dmult * ((1.0 - om) / mult)
            dr = dlog_a * (-LRU_C * sp)
            dpr = dr * r * (1.0 - r)
            dpi = di * ig * (1.0 - ig)
            dprb, dpib = dpr.astype(BF16), dpi.astype(BF16)
            dwa_ref[...] += lax.dot_general(xb, dprb, TN, preferred_element_type=F32)
            dwx_ref[...] += lax.dot_general(xb, dpib, TN, preferred_element_type=F32)
            dxr = (dix * ig + lax.dot_general(dprb, wa, NT, preferred_element_type=F32)
                   + lax.dot_general(dpib, wx, NT, preferred_element_type=F32))
            dxr_s[rows, :] = dxr
            return (w[0:1, :], s_cb + jnp.sum(dxr, axis=0, keepdims=True), s_ba + jnp.sum(dpr, axis=0, keepdims=True),
                    s_bx + jnp.sum(dpi, axis=0, keepdims=True), s_sp + jnp.sum(dlog_a * (-LRU_C * r), axis=0, keepdims=True))

        _, s_cb, s_ba, s_bx, s_sp = lax.fori_loop(0, nb, step1, (zero, zero, zero, zero, zero))

        def step2(b, carry):
            t0, t1, t2, t3 = carry
            start = pl.multiple_of(b * bt, bt)
            rows = pl.ds(start, bt)
            dxr = dxr_s[rows, :]
            extn = jnp.concatenate([dxr, _halo_next(dxr_s, start, bt, b, nb)], axis=0)
            dlx_ref[rows, :] = (cw[3:4, :] * dxr + cw[2:3, :] * _up(extn, 1) + cw[1:2, :] * _up(extn, 2)
                                + cw[0:1, :] * _up(extn, 3))
            lx = lx_ref[rows, :]
            ext = jnp.concatenate([_halo_prev(lx_ref, start, b), lx], axis=0)
            return (t0 + jnp.sum(dxr * _down(ext, 3), axis=0, keepdims=True),
                    t1 + jnp.sum(dxr * _down(ext, 2), axis=0, keepdims=True),
                    t2 + jnp.sum(dxr * _down(ext, 1), axis=0, keepdims=True),
                    t3 + jnp.sum(dxr * lx, axis=0, keepdims=True))

        t0, t1, t2, t3 = lax.fori_loop(0, nb, step2, (zero, zero, zero, zero))
        sm_ref[...] = jnp.zeros((16, 128), F32)
        for k, val in enumerate((t0, t1, t2, t3, s_cb, s_ba, s_bx, -s_sp * jax.nn.sigmoid(-lam))):
            sm_ref[k:k + 1, :] = val

    own = _bs((T, 128), lambda c: (0, c))
    wspec = _bs((None, LB, LB), lambda c: (c, 0, 0))
    return pl.pallas_call(
        body, grid=(DL // 128,),
        in_specs=[_bs((T, 128), lambda c: (0, C_GATE + c)), _bs((T, 128), lambda c: (0, C_LX + c)),
                  _bs((8, 128), lambda c: (0, c)), _bs((8, 128), lambda c: (0, c)), wspec, wspec, own, own],
        out_specs=[own, own, _bs((16, 128), lambda c: (0, c)), wspec, wspec],
        out_shape=[S((T, DL), F32)] * 2 + [S((16, DL), F32), S((4, LB, LB), F32), S((4, LB, LB), F32)],
        scratch_shapes=[pltpu.VMEM((T, 128), F32)], compiler_params=_cp(), name="lru_bwd")(P, P, lcw, vec, wa, wx, hst, dy)


_GROUPS = ((0, DC), (DC, DC + DA), (DC + DA, D))


def _gnorm_fwd(yc, ya, yl, gain):
    T = yc.shape[0]
    tb = _tile(T, (512,))

    def body(c_ref, a_ref, l_ref, g_ref, yn_ref, r0_ref, r1_ref, r2_ref):
        for (lo, hi), src, r_ref in zip(_GROUPS, (c_ref, a_ref, l_ref), (r0_ref, r1_ref, r2_ref)):
            yv = src[...]
            r = lax.rsqrt(jnp.mean(yv * yv, axis=1, keepdims=True) + EPS)
            yn_ref[:, lo:hi] = (yv * r * g_ref[:, lo:hi]).astype(BF16)
            r_ref[...] = r

    rs = _bs((tb, 1), lambda i: (i, 0))
    return pl.pallas_call(
        body, grid=(T // tb,),
        in_specs=[_bs((tb, DC), lambda i: (i, 0)), _bs((tb, DA), lambda i: (i, 0)), _bs((tb, DL), lambda i: (i, 0)),
                  _bs((1, D), lambda i: (0, 0))],
        out_specs=[_bs((tb, D), lambda i: (i, 0)), rs, rs, rs],
        out_shape=[S((T, D), BF16)] + [S((T, 1), F32)] * 3, compiler_params=_cp(), name="gnorm_fwd")(yc, ya, yl, gain)


def _gnorm_bwd(dyn, yc, ya, yl, r0, r1, r2, gain):
    T = yc.shape[0]
    tb = _tile(T, (512,))

    def body(d_ref, c_ref, a_ref, l_ref, r0_ref, r1_ref, r2_ref, g_ref, dc_ref, da_ref, dl_ref, dg_ref):
        i = pl.program_id(0)
        for (lo, hi), src, r_ref, dst in zip(_GROUPS, (c_ref, a_ref, l_ref), (r0_ref, r1_ref, r2_ref), (dc_ref, da_ref, dl_ref)):
            r = r_ref[...]
            yhat = src[...] * r
            dy = d_ref[:, lo:hi]
            dyh = dy * g_ref[:, lo:hi]
            m = jnp.mean(dyh * yhat, axis=1, keepdims=True)
            dst[...] = r * (dyh - yhat * m)
            part = jnp.sum(dy * yhat, axis=0, keepdims=True)

            @pl.when(i == 0)
            def _():
                dg_ref[:, lo:hi] = part

            @pl.when(i > 0)
            def _():
                dg_ref[:, lo:hi] += part

    rs = _bs((tb, 1), lambda i: (i, 0))
    specs = [_bs((tb, DC), lambda i: (i, 0)), _bs((tb, DA), lambda i: (i, 0)), _bs((tb, DL), lambda i: (i, 0))]
    return pl.pallas_call(
        body, grid=(T // tb,),
        in_specs=[_bs((tb, D), lambda i: (i, 0))] + specs + [rs, rs, rs, _bs((1, D), lambda i: (0, 0))],
        out_specs=specs + [_bs((1, D), lambda i: (0, 0))],
        out_shape=[S((T, DC), F32), S((T, DA), F32), S((T, DL), F32), S((1, D), F32)],
        compiler_params=_cp(), name="gnorm_bwd")(dyn, yc, ya, yl, r0, r1, r2, gain)


HBM = pl.BlockSpec(memory_space=pltpu.HBM)
N_BIG = 6


def _place():
    x, y, c = lax.axis_index("x"), lax.axis_index("y"), lax.axis_index("c")
    return x, y, c, 2 * x + y


def _peer(x, y, j):
    return x ^ ((j + 1) >> 1), y ^ ((j + 1) & 1)


SEM = pl.BlockSpec(memory_space=pltpu.SEMAPHORE)
ANY = pl.BlockSpec(memory_space=pl.ANY)
VM = pl.BlockSpec(memory_space=pltpu.VMEM)
EFFECT = pltpu.SideEffectType.DATAFLOW_SIDE_EFFECTING
N_AG = N_BIG + 1


def _hbm(a):
    return pltpu.with_memory_space_constraint(a, pltpu.HBM)


AG_ORDER = (0, 1, N_BIG, 2, 3, 4, 5)
AG_FIRST, AG_REST = (0, 1), (N_BIG, 2, 3, 4, 5)


def _ag_copy(src, land, ssem, rsem, t, j, chip):
    x, y, c, _ = _place()
    px, py = _peer(x, y, j)
    if t == N_BIG:
        s_ref, d_ref = src, land.at[chip]
    else:
        rh = src.shape[0] // 2
        half = pl.ds(c * rh, rh)
        s_ref, d_ref = src.at[half], land.at[chip, half]
    return pltpu.make_async_remote_copy(src_ref=s_ref, dst_ref=d_ref, send_sem=ssem.at[3 * t + j], recv_sem=rsem.at[3 * t + j],
                                        device_id=(px, py, c), device_id_type=MESH)


def _ag_start(l, srcs, dep):
    n = N_AG

    def body(*refs):
        src = refs[:n]
        ssem, rsem = refs[2 * n + 1], refs[2 * n + 2]
        land = refs[3 * n + 3:4 * n + 3]
        token = refs[4 * n + 3]
        _, _, _, me = _place()
        for t in AG_ORDER:
            for j in range(3):
                _ag_copy(src[t], land[t], ssem, rsem, t, j, me).start()
        token[...] = jnp.zeros_like(token)

    lands = [lax.empty((4,) + a.shape, a.dtype) for a in srcs]
    dma = pltpu.SemaphoreType.DMA
    outs = pl.pallas_call(
        body, name=f"ag_start_{l}",
        out_shape=(dma((3 * n,)), dma((3 * n,))) + tuple(pltpu.HBM(a.shape, a.dtype) for a in list(srcs) + lands) + (S((8, 128), F32),),
        in_specs=[HBM] * (2 * n) + [ANY], out_specs=(SEM, SEM) + (HBM,) * (2 * n) + (VM,),
        input_output_aliases={i: 2 + i for i in range(2 * n)},
        compiler_params=pltpu.CompilerParams(has_side_effects=EFFECT),
    )(*[_hbm(a) for a in srcs], *[_hbm(a) for a in lands], dep)
    return outs[0], outs[1], outs[2:2 + n], outs[2 + n:2 + 2 * n], outs[-1]


def _ag_wait(name, idx, ssem, rsem, srcs, lands, after):
    n = len(idx)

    def body(*refs):
        src, land = refs[:n], refs[n:2 * n]
        ssem, rsem = refs[2 * n], refs[2 * n + 1]
        x, y, _, _ = _place()
        for p, t in enumerate(idx):
            for j in range(3):
                px, py = _peer(x, y, j)
                cp = _ag_copy(src[p], land[p], ssem, rsem, t, j, 2 * px + py)
                cp.wait_send()
                cp.wait_recv()

    outs = pl.pallas_call(
        body, name=name,
        out_shape=tuple(pltpu.HBM(a.shape, a.dtype) for a in list(srcs) + list(lands)),
        in_specs=[HBM] * (2 * n) + [SEM, SEM, ANY], out_specs=(HBM,) * (2 * n),
        input_output_aliases={i: i for i in range(2 * n)},
        compiler_params=pltpu.CompilerParams(has_side_effects=EFFECT),
    )(*srcs, *lands, ssem, rsem, after)
    return outs[:n], outs[n:]


def _ag_finish(idx, srcs, lands, dep):
    n = len(idx)

    def body(*refs):
        src = refs[:n]
        out = refs[2 * n + 1:3 * n + 1]
        dsend, drecv, osend, orecv = refs[-4:]
        x, y, c, me = _place()
        big = [p for p, t in enumerate(idx) if t != N_BIG]

        def d2d(p, j, cc):
            px, py = _peer(x, y, j)
            rh = src[p].shape[0] // 2
            part = out[p].at[2 * px + py, pl.ds(cc * rh, rh)]
            return pltpu.make_async_remote_copy(src_ref=part, dst_ref=part, send_sem=dsend.at[p, j], recv_sem=drecv.at[p, j],
                                                device_id=(x, y, 1 - c), device_id_type=MESH)

        own = [pltpu.make_async_remote_copy(src_ref=src[p], dst_ref=out[p].at[me], send_sem=osend.at[p], recv_sem=orecv.at[p],
                                            device_id=(x, y, 1 - c), device_id_type=MESH) for p in range(n)]
        for cp in own:
            cp.start()
        for p in big:
            for j in range(3):
                d2d(p, j, c).start()
        for p in big:
            for j in range(3):
                d2d(p, j, 1 - c).wait_recv()
                d2d(p, j, c).wait_send()
        for cp in own:
            cp.wait()

    dma = pltpu.SemaphoreType.DMA
    return pl.pallas_call(
        body, in_specs=[HBM] * (2 * n) + [ANY], out_specs=[HBM] * n,
        out_shape=[S(a.shape, a.dtype) for a in lands],
        input_output_aliases={n + i: i for i in range(n)},
        scratch_shapes=[dma((n, 3)), dma((n, 3)), dma((n,)), dma((n,))],
        name="ag_finish")(*srcs, *lands, dep)


def _pair_copy(g, land, ssem, rsem, t):
    x, y, c, _ = _place()
    rh = g.shape[1] // 2
    return pltpu.make_async_remote_copy(src_ref=g.at[:, pl.ds((1 - c) * rh, rh), :], dst_ref=land,
                                        send_sem=ssem.at[t], recv_sem=rsem.at[t], device_id=(x, y, 1 - c), device_id_type=MESH)


def _rs_pair(grads):
    n = len(grads)

    def body(*refs):
        g, out = refs[:n], refs[n:2 * n]
        ssem, rsem = refs[2 * n:]
        for t in range(n):
            _pair_copy(g[t], out[t], ssem, rsem, t).start()
        for t in range(n):
            _pair_copy(g[t], out[t], ssem, rsem, t).wait()

    dma = pltpu.SemaphoreType.DMA
    return pl.pallas_call(
        body, in_specs=[HBM] * n, out_specs=[HBM] * n,
        out_shape=[S((4, g.shape[1] // 2, g.shape[2]), g.dtype) for g in grads],
        scratch_shapes=[dma((n,)), dma((n,))], name="rs_pair")(*grads)


def _rs_pair_start(name, grads, dep):
    n = len(grads)

    def body(*refs):
        g = refs[:n]
        ssem, rsem = refs[2 * n + 1], refs[2 * n + 2]
        land = refs[3 * n + 3:4 * n + 3]
        token = refs[4 * n + 3]
        for t in range(n):
            _pair_copy(g[t], land[t], ssem, rsem, t).start()
        token[...] = jnp.zeros_like(token)

    lands = [lax.empty((4, a.shape[1] // 2, a.shape[2]), a.dtype) for a in grads]
    dma = pltpu.SemaphoreType.DMA
    outs = pl.pallas_call(
        body, name=name,
        out_shape=(dma((n,)), dma((n,))) + tuple(pltpu.HBM(a.shape, a.dtype) for a in list(grads) + lands) + (S((8, 128), F32),),
        in_specs=[HBM] * (2 * n) + [ANY], out_specs=(SEM, SEM) + (HBM,) * (2 * n) + (VM,),
        input_output_aliases={i: 2 + i for i in range(2 * n)},
        compiler_params=pltpu.CompilerParams(has_side_effects=EFFECT),
    )(*[_hbm(a) for a in grads], *[_hbm(a) for a in lands], dep)
    return outs[0], outs[1], outs[2:2 + n], outs[2 + n:2 + 2 * n], outs[-1]


def _rs_pair_wait(name, ssem, rsem, grads, lands, after):
    n = len(grads)

    def body(*refs):
        g, land = refs[:n], refs[n:2 * n]
        ssem, rsem = refs[2 * n], refs[2 * n + 1]
        for t in range(n):
            cp = _pair_copy(g[t], land[t], ssem, rsem, t)
            cp.wait_send()
            cp.wait_recv()

    outs = pl.pallas_call(
        body, name=name,
        out_shape=tuple(pltpu.HBM(a.shape, a.dtype) for a in list(grads) + list(lands)),
        in_specs=[HBM] * (2 * n) + [SEM, SEM, ANY], out_specs=(HBM,) * (2 * n),
        input_output_aliases={i: i for i in range(2 * n)},
        compiler_params=pltpu.CompilerParams(has_side_effects=EFFECT),
    )(*grads, *lands, ssem, rsem, after)
    return outs[:n], outs[n:]


def _rs_copy(s, land, ssem, rsem, t, j):
    x, y, c, _ = _place()
    px, py = _peer(x, y, j)
    return pltpu.make_async_remote_copy(src_ref=s[t].at[2 * px + py], dst_ref=land[t].at[j],
                                        send_sem=ssem.at[3 * t + j], recv_sem=rsem.at[3 * t + j], device_id=(px, py, c), device_id_type=MESH)


def _rs_start(name, sums, dep):
    n = len(sums)

    def body(*refs):
        s = refs[:n]
        ssem, rsem = refs[2 * n + 1], refs[2 * n + 2]
        land = refs[3 * n + 3:4 * n + 3]
        token = refs[4 * n + 3]
        for t in range(n):
            for j in range(3):
                _rs_copy(s, land, ssem, rsem, t, j).start()
        token[...] = jnp.zeros_like(token)

    lands = [lax.empty((3,) + a.shape[1:], a.dtype) for a in sums]
    dma = pltpu.SemaphoreType.DMA
    outs = pl.pallas_call(
        body, name=name,
        out_shape=(dma((3 * n,)), dma((3 * n,))) + tuple(pltpu.HBM(a.shape, a.dtype) for a in list(sums) + lands) + (S((8, 128), F32),),
        in_specs=[HBM] * (2 * n) + [ANY], out_specs=(SEM, SEM) + (HBM,) * (2 * n) + (VM,),
        input_output_aliases={i: 2 + i for i in range(2 * n)},
        compiler_params=pltpu.CompilerParams(has_side_effects=EFFECT),
    )(*[_hbm(a) for a in sums], *[_hbm(a) for a in lands], dep)
    return outs[0], outs[1], outs[2:2 + n], outs[2 + n:2 + 2 * n], outs[-1]


def _rs_wait(name, ssem, rsem, sums, lands, after):
    n = len(sums)

    def body(*refs):
        s, land = refs[:n], refs[n:2 * n]
        ssem, rsem = refs[2 * n], refs[2 * n + 1]
        for t in range(n):
            for j in range(3):
                cp = _rs_copy(s, land, ssem, rsem, t, j)
                cp.wait_send()
                cp.wait_recv()

    outs = pl.pallas_call(
        body, name=name,
        out_shape=tuple(pltpu.HBM(a.shape, a.dtype) for a in list(sums) + list(lands)),
        in_specs=[HBM] * (2 * n) + [SEM, SEM, ANY], out_specs=(HBM,) * (2 * n),
        input_output_aliases={i: i for i in range(2 * n)},
        compiler_params=pltpu.CompilerParams(has_side_effects=EFFECT),
    )(*sums, *lands, ssem, rsem, after)
    return outs[n:]


def _rs_join(halves, lo, hi, dep):
    def body(*refs):
        h = refs[:N_BIG]
        out = refs[N_BIG + 1:2 * N_BIG + 1]
        ssem, rsem = refs[2 * N_BIG + 1:]
        x, y, c, _ = _place()

        def cp(t):
            return pltpu.make_async_remote_copy(
                src_ref=h[t].at[pl.ds(lo, hi - lo)], dst_ref=out[t], send_sem=ssem.at[t], recv_sem=rsem.at[t],
                device_id=(x, y, 1 - c), device_id_type=MESH)

        for t in range(N_BIG):
            cp(t).start()
        for t in range(N_BIG):
            cp(t).wait()

    dma = pltpu.SemaphoreType.DMA
    return pl.pallas_call(
        body, in_specs=[HBM] * N_BIG + [ANY], out_specs=[HBM] * N_BIG,
        out_shape=[S((hi - lo,) + h.shape[1:], h.dtype) for h in halves],
        scratch_shapes=[dma((N_BIG,)), dma((N_BIG,))], name="rs_join")(*halves, dep)


def _all_reduce_small(pack):
    R = pack.shape[0]
    rb = _tile(R, (512, 256, 128, 8))

    def body(x_ref, all_ref, sum_ref, send_sems, recv_sems, local_sem):
        x, y, c = lax.axis_index("x"), lax.axis_index("y"), lax.axis_index("c")
        me, sibling = (x, y, c), (x, y, 1 - c)
        chips = [(1 - x, y), (x, 1 - y), (1 - x, 1 - y)]

        def rows(px, py, pc):
            return all_ref.at[pl.ds((4 * px + 2 * py + pc) * R, R), :]

        def copy(k, block, to, src=None):
            return pltpu.make_async_remote_copy(
                src_ref=rows(*block) if src is None else src, dst_ref=rows(*block),
                send_sem=send_sems.at[k], recv_sem=recv_sems.at[k], device_id=to, device_id_type=MESH)

        mine = pltpu.make_async_copy(x_ref, rows(*me), local_sem)
        mine.start()
        first = [copy(0, me, sibling, src=x_ref)]
        first += [copy(1 + j, me, (*chip, c), src=x_ref) for j, chip in enumerate(chips)]
        for cp in first:
            cp.start()
        passed = [copy(4 + j, (*chip, c), sibling) for j, chip in enumerate(chips)]
        for j, chip in enumerate(chips):
            copy(1 + j, (*chip, c), me).wait_recv()
            passed[j].start()
        copy(0, sibling, me).wait_recv()
        for j, chip in enumerate(chips):
            copy(4 + j, (*chip, 1 - c), me).wait_recv()
        for cp in first + passed:
            cp.wait_send()
        mine.wait()

        def step(b, carry):
            off = pl.multiple_of(b * rb, rb)
            acc = all_ref[pl.ds(off, rb), :]
            for k in range(1, 8):
                acc = acc + all_ref[pl.ds(pl.multiple_of(k * R + off, 8), rb), :]
            sum_ref[pl.ds(off, rb), :] = acc
            return carry

        lax.fori_loop(0, R // rb, step, 0)

    vm = pl.BlockSpec(memory_space=pltpu.VMEM)
    dma = pltpu.SemaphoreType.DMA
    _, total = pl.pallas_call(
        body, in_specs=[vm], out_specs=[vm, vm],
        out_shape=[S((8 * R, 128), F32), S((R, 128), F32)],
        scratch_shapes=[dma((7,)), dma((7,)), dma],
        compiler_params=_cp(), name="allreduce_small")(pack)
    return total


def _row_tile(rh, cc, tile_bytes=3 * 1024 * 1024 // 2):
    for t in (512, 256, 128, 64, 32, 16):
        if rh % t == 0 and t * cc * 4 <= tile_bytes:
            return t
    return 16


def _my_chip():
    return 2 * lax.axis_index("x") + lax.axis_index("y")


def _pair_sum(g, recv):
    _, r, cc = g.shape
    rh = r // 2
    tb = _row_tile(rh, cc)
    nbh = rh // tb

    def body(g_ref, r_ref, o_ref):
        o_ref[...] = (g_ref[...].astype(F32) + r_ref[...].astype(F32)).astype(BF16)

    def chip(k):
        return (_my_chip() + 1 + k) % 4

    mine = _bs((None, tb, cc), lambda k, i: (chip(k), lax.axis_index("c") * nbh + i, 0))
    plain = _bs((None, tb, cc), lambda k, i: (chip(k), i, 0))
    return pl.pallas_call(body, grid=(3, nbh), in_specs=[mine, plain], out_specs=plain,
                          out_shape=S((4, rh, cc), BF16), compiler_params=_cp(), name="rs_pair_sum")(g, recv)


def _owner_sum(g, recv, ici, acc, l):
    _, r, cc = g.shape
    rh = r // 2
    tb = _row_tile(rh, cc)
    nbh = rh // tb

    def body(g_ref, r_ref, i0_ref, i1_ref, i2_ref, acc_ref, o_ref):
        s = g_ref[...].astype(F32) + r_ref[...].astype(F32)
        o_ref[...] = s + i0_ref[...].astype(F32) + i1_ref[...].astype(F32) + i2_ref[...].astype(F32)

    def slot(j):
        return _bs((None, tb, cc), lambda i: (j, i, 0))

    return pl.pallas_call(
        body, grid=(nbh,),
        in_specs=[_bs((None, tb, cc), lambda i: (_my_chip(), lax.axis_index("c") * nbh + i, 0)),
                  _bs((None, tb, cc), lambda i: (_my_chip(), i, 0)),
                  slot(0), slot(1), slot(2), pl.BlockSpec(memory_space=pl.ANY)],
        out_specs=_bs((None, tb, cc), lambda i: (l, i, 0)),
        out_shape=S(acc.shape, F32), input_output_aliases={5: 0},
        compiler_params=_cp(), name="rs_owner_sum")(g, recv, ici, ici, ici, acc)


def _adam_math(w, g, m, v):
    m = B1 * m + (1.0 - B1) * g
    v = B2 * v + (1.0 - B2) * (g * g)
    m_hat = m / (1.0 - B1 ** STEP)
    v_hat = v / (1.0 - B2 ** STEP)
    delta = -LR * (m_hat / (jnp.sqrt(v_hat) + AEPS) + WD * w)
    return delta, m, v


def _adamw_big(w, g_mine, g_sib, m, v, lo, hi, prev):
    L, r, cc = w.shape
    rh = r // 2
    tb = _row_tile(rh, cc)
    nbh = rh // tb

    def body(w_ref, gm_ref, gs_ref, m_ref, v_ref, *rest):
        go_ref, d_ref, mo_ref, vo_ref = rest[-4:]
        mine = pl.program_id(1) == lax.axis_index("c")
        g = jnp.where(mine, gm_ref[...], gs_ref[...])
        d, m, v = _adam_math(w_ref[...], g, m_ref[...], v_ref[...])
        go_ref[...] = g
        d_ref[...] = d
        mo_ref[...] = m
        vo_ref[...] = v

    def mine_map(l, hf, i):
        c = lax.axis_index("c")
        return (l + lo, jnp.where(hf == c, i, jnp.where(c == 0, nbh - 1, 0)), 0)

    def sib_map(l, hf, i):
        c = lax.axis_index("c")
        return (l, jnp.where(hf != c, i, jnp.where(c == 0, 0, nbh - 1)), 0)

    full = _bs((None, tb, cc), lambda l, hf, i: (l + lo, hf * nbh + i, 0))
    extra = [] if prev is None else list(prev)
    return pl.pallas_call(
        body, grid=(hi - lo, 2, nbh),
        in_specs=[full, _bs((None, tb, cc), mine_map), _bs((None, tb, cc), sib_map), full, full]
        + [pl.BlockSpec(memory_space=pl.ANY)] * len(extra),
        out_specs=[full] * 4, out_shape=[S(w.shape, F32)] * 4,
        input_output_aliases={5 + k: k for k in range(len(extra))},
        compiler_params=_cp(), name="adamw_big")(w, g_mine, g_sib, m, v, *extra)


def _adamw_small(w, g, m, v):
    R = w.shape[0]
    tb = _tile(R, (512, 256, 128, 8))

    def body(w_ref, g_ref, m_ref, v_ref, d_ref, mo_ref, vo_ref):
        d, m, v = _adam_math(w_ref[...], g_ref[...], m_ref[...], v_ref[...])
        d_ref[...] = d
        mo_ref[...] = m
        vo_ref[...] = v

    spec = _bs((tb, 128), lambda i: (i, 0))
    return pl.pallas_call(body, grid=(R // tb,), in_specs=[spec] * 4, out_specs=[spec] * 3,
                          out_shape=[S((R, 128), F32)] * 3, compiler_params=_cp(), name="adamw_small")(w, g, m, v)


def _mix_pad(w):
    return jnp.concatenate([w[:, :4608], w[:, 4616:DIN], w[:, 4608:4616], jnp.zeros((D, PW - DIN), w.dtype)], axis=1)


def _mix_unpad(g):
    return jnp.concatenate([g[:, :4608], g[:, 5632:5640], g[:, 4608:5632]], axis=1)


def _pack(parts):
    flat = jnp.concatenate([p.reshape(-1).astype(F32) for p in parts])
    n = flat.shape[0]
    total = -(-n // (512 * 128)) * (512 * 128)
    return jnp.pad(flat, (0, total - n)).reshape(total // 128, 128)


def _unpack(pack, shapes):
    flat = pack.reshape(-1)
    out, off = [], 0
    for s in shapes:
        n = math.prod(s)
        out.append(flat[off:off + n].reshape(s))
        off += n
    return out


def _ffn_forward(x, gain, win, wout):
    h, rstd = _rms_fwd(x, gain)
    zg, zu, act = _ffn_in(h, win)
    y = _ffn_out(act, wout, x)
    return y, (x, h, rstd, zg, zu, act)


def _ffn_backward(dy, dyb, saved, gain, win, wout, dep):
    x, h, rstd, zg, zu, act = saved
    dz = _ffn_bwd_dz(dyb, wout, zg, zu, dep)
    dwout = _mm_tn("ffn_bwd_dwout", act, dyb, scale=0.5, tm=512, tn=1024)
    dwin = _ffn_bwd_dwin(h, dz)
    dh = _ffn_bwd_dh(dz, win)
    dx, dxb, dgain = _rms_bwd(dh, x, rstd, gain, dy)
    return dx, dxb, dgain, dwin, dwout


def _mixer_forward(x, p):
    T = x.shape[0]
    h, rstd = _rms_fwd(x, p["norm_mix"])
    tm = _tile(T, (1024, 512))
    tn = 1152
    P = _mm("mix_in", h, p["wmix"],
            _bs((tm, D), lambda i, j, k: (i, 0)), _bs((D, tn), lambda i, j, k: (0, j)),
            _bs((tm, tn), lambda i, j, k: (i, j)), S((T, PW), F32), (T // tm, PW // tn, 1), NN, 1, (tm, tn))
    yc = _conv_fwd(P, p["cw"])
    cum = _fgate_fwd(P, p["fb"])
    cumt = cum[:, :NH].T
    tq = _att_tile(T)
    cumq, cumk = cumt.reshape(NH, T, 1), cumt.reshape(NH, T // tq, 1, tq)
    qkv = P[:, C_Q * 128:C_GATE * 128].astype(BF16)
    ya, lse = _attn_fwd(qkv, cumq, cumk)
    yl, hst = _lru_fwd(P, p["lcw"], p["lvec"], p["lru_w_a"], p["lru_w_x"])
    yn, r0, r1, r2 = _gnorm_fwd(yc, ya, yl, p["mix_out_norm"])
    y = _mm("mix_out", yn, p["wo"],
            _bs((tm, D), lambda i, j, k: (i, 0)), _bs((D, 1024), lambda i, j, k: (0, j)),
            _bs((tm, 1024), lambda i, j, k: (i, j)), S((T, D), F32), (T // tm, D // 1024, 1), NN, 1, (tm, 1024),
            res=x, r_spec=_bs((tm, 1024), lambda i, j, k: (i, j)))
    return y, (x, h, rstd, P, qkv, cumq, cumk, lse, yc, ya, yl, hst, yn, r0, r1, r2)


def _mixer_backward(dy, dyb, saved, p, dep):
    x, h, rstd, P, qkv, cumq, cumk, lse, yc, ya, yl, hst, yn, r0, r1, r2 = saved
    T = x.shape[0]
    dyn = _mm_nt_full("mix_bwd_dyn", dyb, p["wo"], 512, dep=dep)
    dwo = _mm_tn("mix_bwd_dwo", yn, dyb, tm=512, tn=1024)
    dyc, dya, dyl, dgn = _gnorm_bwd(dyn, yc, ya, yl, r0, r1, r2, p["mix_out_norm"])
    dcb, dcc, dcv, dcw = _conv_bwd(P, p["cw"], dyc)
    dq, dk, dv, dck, dcq = _attn_bwd(qkv, cumq, cumk, lse, ya, dya)
    dcum = jnp.pad((dck.reshape(NH, T) + dcq.reshape(NH, T)).T, ((0, 0), (0, 128 - NH)))
    df, dfb = _fgate_bwd(P, p["fb"], dcum)
    dgate, dlx, lsm, dwa, dwx = _lru_bwd(P, p["lcw"], p["lvec"], p["lru_w_a"], p["lru_w_x"], hst, dyl)
    dP = jnp.concatenate([dcb, dcc, dcv, dq, dk, dv, dgate, dlx, df], axis=1).astype(BF16)
    tm = _tile(T, (512,))
    dh = _mm("mix_bwd_dh", dP, p["wmix"],
             _bs((tm, PW), lambda j, i, k: (i, 0)), _bs((1024, PW), lambda j, i, k: (j, 0)),
             _bs((tm, 1024), lambda j, i, k: (i, j)), S((T, D), F32), (D // 1024, T // tm, 1), NT, 1, (tm, 1024))
    dwmix = _mm_tn("mix_bwd_dwmix", h, dP, tm=512, tn=1152)
    dx, dxb, dgm = _rms_bwd(dh, x, rstd, p["norm_mix"], dy)
    small = dict(norm_mix=dgm[0], mix_out_norm=dgn[0], conv_w=dcw[:3], fgate_b=dfb[0, :NH], lru_conv_w=lsm[:4],
                 lru_conv_b=lsm[4], lru_b_a=lsm[5], lru_b_x=lsm[6], lru_lambda=lsm[7], lru_w_a=dwa, lru_w_x=dwx)
    return dx, dxb, small, dwmix, dwo


BIG =("ffn1_w_in", "ffn1_w_out", "mix_w_in", "mix_w_out", "ffn2_w_in", "ffn2_w_out")
SMALL = ("norm_ffn1", "norm_mix", "conv_w", "fgate_b", "lru_conv_w", "lru_conv_b", "lru_w_a", "lru_b_a", "lru_w_x",
         "lru_b_x", "lru_lambda", "mix_out_norm", "norm_ffn2", "final_norm")
WEIGHTS = ("norm_ffn1", "ffn1_w_in", "ffn1_w_out", "norm_mix", "mix_w_in", "conv_w", "fgate_b", "lru_conv_w", "lru_conv_b",
           "lru_w_a", "lru_b_a", "lru_w_x", "lru_b_x", "lru_lambda", "mix_out_norm", "mix_w_out", "norm_ffn2", "ffn2_w_in",
           "ffn2_w_out", "final_norm")


def _step(args):
    xx, yy, cc_ = lax.axis_index("x"), lax.axis_index("y"), lax.axis_index("c")
    me = 2 * xx + yy
    x0 = args["x"][0]
    tgt = args["loss_target"][0]
    T = x0.shape[0]
    L = args["norm_ffn1"].shape[0]

    def ag_sources(l):
        small = jnp.concatenate([args["conv_w"][l], args["lru_conv_w"][l], jnp.zeros((1, 128), F32)], axis=0)
        return [args[n][l].astype(BF16) for n in BIG] + [small]

    def layer_params(l, gat, gsm):
        w1i, w1o, wmx, wo, w2i, w2o = gat
        cwl = gsm.transpose(1, 0, 2).reshape(8, 4 * 128)
        return dict(
            w1i=w1i, w1o=w1o.reshape(-1, D), w2i=w2i, w2o=w2o.reshape(-1, D), wo=wo.reshape(D, D),
            wmix=_mix_pad(wmx.transpose(1, 0, 2).reshape(D, DIN)),
            cw=jnp.concatenate([cwl[:3], jnp.zeros((5, DC), F32)], axis=0),
            lcw=jnp.concatenate([cwl[3:7], jnp.zeros((4, DL), F32)], axis=0),
            fb=jnp.pad(args["fgate_b"][l], (0, 128 - NH)).reshape(1, 128),
            lvec=jnp.concatenate([args["lru_conv_b"][l][None], args["lru_b_a"][l][None], args["lru_b_x"][l][None],
                                  args["lru_lambda"][l][None], jnp.zeros((4, DL), F32)], axis=0),
            lru_w_a=args["lru_w_a"][l], lru_w_x=args["lru_w_x"][l],
            norm_ffn1=args["norm_ffn1"][l][None], norm_mix=args["norm_mix"][l][None],
            mix_out_norm=args["mix_out_norm"][l][None], norm_ffn2=args["norm_ffn2"][l][None])

    xs = x0
    saved, layers = [], []
    flight = _ag_start(0, ag_sources(0), x0)
    for l in range(L):
        ssem, rsem, srcs, lands, token = flight
        s_a, l_a = _ag_wait(f"ag_wait_{l}a", AG_FIRST, ssem, rsem, [srcs[t] for t in AG_FIRST], [lands[t] for t in AG_FIRST], xs)
        if l + 1 < L:
            flight = _ag_start(l + 1, ag_sources(l + 1), l_a[0])
            token = flight[4]
        w1i, w1o = _ag_finish(AG_FIRST, s_a, l_a, token)
        x1, s1 = _ffn_forward(xs, args["norm_ffn1"][l][None], w1i, w1o.reshape(-1, D))
        s_b, l_b = _ag_wait(f"ag_wait_{l}b", AG_REST, ssem, rsem, [srcs[t] for t in AG_REST], [lands[t] for t in AG_REST], x1)
        gsm, wmx, wo, w2i, w2o = _ag_finish(AG_REST, s_b, l_b, x1)
        p = layer_params(l, (w1i, w1o, wmx, wo, w2i, w2o), gsm)
        x2, s2 = _mixer_forward(x1, p)
        x3, s3 = _ffn_forward(x2, p["norm_ffn2"], p["w2i"], p["w2o"])
        saved.append((s1, s2, s3))
        layers.append(p)
        xs = x3
    lpart, dx, dxb, dfinal = _loss_head(xs, args["final_norm"][None], tgt)
    loss = lax.psum(lpart[0, 0], ("x", "y", "c"))

    acc = [None] * N_BIG
    small_grads = [None] * L
    every = tuple(range(N_BIG))
    in_air = []
    pair_flight = None
    dep = lpart

    def to_blocks(t, g):
        if t in (1, 5):
            return g.reshape(4, g.shape[0] // 4, D)
        if t == 2:
            return _mix_unpad(g).reshape(D, 4, DIN // 4).transpose(1, 0, 2)
        if t == 3:
            return g.reshape(4, D // 4, D)
        return g

    def launch(name, l, idx, grads, recv, after):
        sums = [_pair_sum(g, r) for g, r in zip(grads, recv)]
        started = _rs_start("rs_start_" + name, sums, after)
        in_air.append(("rs_wait_" + name, l, idx, grads, recv, started))
        return started[4]

    def land(entry, after):
        name, l, idx, grads, recv, (ssem, rsem, sums, lands, _) = entry
        ici = _rs_wait(name, ssem, rsem, sums, lands, after)
        for t, g, r, i3 in zip(idx, grads, recv, ici):
            if acc[t] is None:
                acc[t] = jnp.zeros((L, g.shape[1] // 2, g.shape[2]), F32)
            acc[t] = _owner_sum(g, r, i3, acc[t], l)

    for l in reversed(range(L)):
        p = layers[l]
        s1, s2, s3 = saved[l]
        dx, dxb, dg2, dw2i, dw2o = _ffn_backward(dx, dxb, s3, p["norm_ffn2"], p["w2i"], p["w2o"], dep)
        if pair_flight is not None:
            lp, (ssem, rsem, g_thru, lands, _) = pair_flight
            g_thru, recv = _rs_pair_wait(f"rs_pair_wait_{lp}", ssem, rsem, g_thru, lands, dx)
            dep = launch(str(lp), lp, every, g_thru, recv, dx)
            pair_flight = None
        if l == 0:
            part = [to_blocks(4, dw2i), to_blocks(5, dw2o)]
            dep = launch("0c", 0, (4, 5), part, _rs_pair(part), dx)
        dx, dxb, sm, dwmix, dwo = _mixer_backward(dx, dxb, s2, p, dep)
        if l == 0:
            part = [to_blocks(2, dwmix), to_blocks(3, dwo)]
            dep = launch("0b", 0, (2, 3), part, _rs_pair(part), dx)
        dx, dxb, dg1, dw1i, dw1o = _ffn_backward(dx, dxb, s1, p["norm_ffn1"], p["w1i"], p["w1o"], dep)
        sm["norm_ffn1"] = dg1[0]
        sm["norm_ffn2"] = dg2[0]
        small_grads[l] = sm
        for entry in [e for e in in_air if e[1] > l]:
            land(entry, dx)
            in_air.remove(entry)
        if l > 0:
            grads = [to_blocks(t, g) for t, g in enumerate((dw1i, dw1o, dwmix, dwo, dw2i, dw2o))]
            pair_flight = (l, _rs_pair_start(f"rs_pair_start_{l}", grads, dx))
            dep = pair_flight[1][4]
        else:
            part = [to_blocks(0, dw1i), to_blocks(1, dw1o)]
            dep = launch("0a", 0, (0, 1), part, _rs_pair(part), dx)

    def adamw(k, lo, hi, sib, prev):
        n = BIG[k]
        return _adamw_big(args[n], acc[k], sib[k], args["m_" + n], args["v_" + n], lo, hi, prev)

    res = [None] * N_BIG
    after = dx
    if L > 1:
        sib = _rs_join(acc, 1, L, dep)
        for k in range(N_BIG):
            res[k] = adamw(k, 1, L, sib, None)
        after = res[N_BIG - 1][0]
    for entry in list(in_air):
        land(entry, after)
    sib = _rs_join(acc, 0, 1, dx)
    out = {"loss": loss, "grad_x": dx[None]}
    for k, n in enumerate(BIG):
        outs = adamw(k, 0, 1, sib, res[k])
        out["grad_" + n], out["delta_" + n], out["new_m_" + n], out["new_v_" + n] = outs

    full = {n: (dfinal[0] if n == "final_norm" else jnp.stack([small_grads[l][n] for l in range(L)])) for n in SMALL}
    shapes = [full[n].shape for n in SMALL]
    red = dict(zip(SMALL, _unpack(_all_reduce_small(_pack([full[n] for n in SMALL])), shapes)))
    for n in ("conv_w", "lru_conv_w"):
        red[n] = lax.dynamic_slice_in_dim(red[n], me * 128, 128, axis=2)
    oshapes = [args[n].shape for n in SMALL]
    d, m, v = _adamw_small(_pack([args[n] for n in SMALL]), _pack([red[n] for n in SMALL]),
                           _pack([args["m_" + n] for n in SMALL]), _pack([args["v_" + n] for n in SMALL]))
    for n, gg, dd, mm, vv in zip(SMALL, [red[n] for n in SMALL], _unpack(d, oshapes), _unpack(m, oshapes), _unpack(v, oshapes)):
        out["grad_" + n], out["delta_" + n], out["new_m_" + n], out["new_v_" + n] = gg, dd, mm, vv
    return out


def kernel(x, norm_ffn1, ffn1_w_in, ffn1_w_out, norm_mix, mix_w_in, conv_w, fgate_b, lru_conv_w, lru_conv_b, lru_w_a, lru_b_a, lru_w_x, lru_b_x, lru_lambda, mix_out_norm, mix_w_out, norm_ffn2, ffn2_w_in, ffn2_w_out, final_norm, loss_target, m_norm_ffn1, m_ffn1_w_in, m_ffn1_w_out, m_norm_mix, m_mix_w_in, m_conv_w, m_fgate_b, m_lru_conv_w, m_lru_conv_b, m_lru_w_a, m_lru_b_a, m_lru_w_x, m_lru_b_x, m_lru_lambda, m_mix_out_norm, m_mix_w_out, m_norm_ffn2, m_ffn2_w_in, m_ffn2_w_out, m_final_norm, v_norm_ffn1, v_ffn1_w_in, v_ffn1_w_out, v_norm_mix, v_mix_w_in, v_conv_w, v_fgate_b, v_lru_conv_w, v_lru_conv_b, v_lru_w_a, v_lru_b_a, v_lru_w_x, v_lru_b_x, v_lru_lambda, v_mix_out_norm, v_mix_w_out, v_norm_ffn2, v_ffn2_w_in, v_ffn2_w_out, v_final_norm):
    args = dict(locals())
    out = _step(args)
    res = [out["loss"], out["grad_x"]]
    for prefix in ("grad_", "delta_", "new_m_", "new_v_"):
        res += [out[prefix + n] for n in WEIGHTS]
    return tuple(res)
```

```python
import functools
import math

import jax
import jax.numpy as jnp
from jax import lax
from jax.experimental import pallas as pl
from jax.experimental.pallas import tpu as pltpu

F32 = jnp.float32
BF16 = jnp.bfloat16
S = jax.ShapeDtypeStruct
MESH = pl.DeviceIdType.MESH

D = 2048
DC = 512
DA = 1024
NH = 8
HD = 128
DL = 512
LB = 128
DIN = 5640
PW = 5760
C_Q, C_K, C_V = 12, 20, 28
C_GATE, C_LX, C_F = 36, 40, 44
EPS = 1e-6
LRU_C = 8.0
ATT_SCALE = HD ** -0.5
LR, B1, B2, AEPS, WD, STEP = 0.001, 0.9, 0.999, 1e-08, 0.01, 10
VMEM_LIMIT = 56 * 1024 * 1024

NT = (((1,), (1,)), ((), ()))
TN = (((0,), (0,)), ((), ()))
NN = (((1,), (0,)), ((), ()))


def _cp():
    return pltpu.CompilerParams(vmem_limit_bytes=VMEM_LIMIT)


def _bs(shape, fn):
    return pl.BlockSpec(shape, fn)


def _mm(name, a, b, a_spec, b_spec, o_spec, o_shape, grid, dims, nk, acc_tile, scale=1.0, res=None, r_spec=None, dep=None):
    has_res = res is not None
    has_dep = dep is not None

    def body(*refs):
        if has_dep:
            refs = refs[:2 + has_res] + refs[3 + has_res:]
        if has_res:
            a_ref, b_ref, r_ref, o_ref = refs[:4]
            rest = refs[4:]
        else:
            a_ref, b_ref, o_ref = refs[:3]
            rest = refs[3:]
        prod = lax.dot_general(a_ref[...].astype(BF16), b_ref[...].astype(BF16), dims, preferred_element_type=F32)

        def finish(acc):
            if scale != 1.0:
                acc = acc * scale
            if has_res:
                acc = r_ref[...] + acc
            o_ref[...] = acc.astype(o_ref.dtype)

        if nk == 1:
            finish(prod)
        else:
            acc_ref = rest[0]
            k = pl.program_id(2)

            @pl.when(k == 0)
            def _():
                acc_ref[...] = prod

            @pl.when(k > 0)
            def _():
                acc_ref[...] += prod

            @pl.when(k == nk - 1)
            def _():
                finish(acc_ref[...])

    in_specs = [a_spec, b_spec] + ([r_spec] if has_res else []) + ([pl.BlockSpec(memory_space=pl.ANY)] if has_dep else [])
    args = (a, b) + ((res,) if has_res else ()) + ((dep,) if has_dep else ())
    scratch = [pltpu.VMEM(acc_tile, F32)] if nk > 1 else []
    return pl.pallas_call(body, grid=grid, in_specs=in_specs, out_specs=o_spec, out_shape=o_shape,
                          scratch_shapes=scratch, compiler_params=_cp(), name=name)(*args)


def _tile(n, pref):
    for t in pref:
        if n % t == 0:
            return t
    return n


def _rms_fwd(x, gain):
    T = x.shape[0]
    tb = _tile(T, (512,))

    def body(x_ref, g_ref, h_ref, r_ref):
        xv = x_ref[...]
        r = lax.rsqrt(jnp.mean(xv * xv, axis=1, keepdims=True) + EPS)
        h_ref[...] = (xv * r * g_ref[...]).astype(BF16)
        r_ref[...] = r

    return pl.pallas_call(
        body, grid=(T // tb,),
        in_specs=[_bs((tb, D), lambda i: (i, 0)), _bs((1, D), lambda i: (0, 0))],
        out_specs=[_bs((tb, D), lambda i: (i, 0)), _bs((tb, 1), lambda i: (i, 0))],
        out_shape=[S((T, D), BF16), S((T, 1), F32)], compiler_params=_cp(), name="rms_fwd")(x, gain)


def _rms_bwd(dh, x, rstd, gain, dres):
    T = x.shape[0]
    tb = _tile(T, (512,))

    def body(dh_ref, x_ref, r_ref, g_ref, dres_ref, dx_ref, dxb_ref, dg_ref):
        i = pl.program_id(0)
        r = r_ref[...]
        xhat = x_ref[...] * r
        dh = dh_ref[...]
        dxh = dh * g_ref[...]
        m = jnp.mean(dxh * xhat, axis=1, keepdims=True)
        dx = dres_ref[...] + r * (dxh - xhat * m)
        dx_ref[...] = dx
        dxb_ref[...] = dx.astype(BF16)
        part = jnp.sum(dh * xhat, axis=0, keepdims=True)

        @pl.when(i == 0)
        def _():
            dg_ref[...] = part

        @pl.when(i > 0)
        def _():
            dg_ref[...] += part

    row = _bs((tb, D), lambda i: (i, 0))
    return pl.pallas_call(
        body, grid=(T // tb,),
        in_specs=[row, row, _bs((tb, 1), lambda i: (i, 0)), _bs((1, D), lambda i: (0, 0)), row],
        out_specs=[row, row, _bs((1, D), lambda i: (0, 0))],
        out_shape=[S((T, D), F32), S((T, D), BF16), S((1, D), F32)], compiler_params=_cp(), name="rms_bwd")(dh, x, rstd, gain, dres)


def _loss_head(x, gain, tgt):
    T = x.shape[0]
    tb = _tile(T, (512,))

    def body(x_ref, g_ref, t_ref, l_ref, dx_ref, dxb_ref, dg_ref):
        i = pl.program_id(0)
        xv = x_ref[...]
        g = g_ref[...]
        r = lax.rsqrt(jnp.mean(xv * xv, axis=1, keepdims=True) + EPS)
        xhat = xv * r
        e = xhat * g - t_ref[...]
        lpart = 0.5 * jnp.sum(jnp.sum(e * e, axis=1, keepdims=True), axis=0, keepdims=True) * (1.0 / D)
        dy = e * (1.0 / D)
        dxh = dy * g
        m = jnp.mean(dxh * xhat, axis=1, keepdims=True)
        dx = r * (dxh - xhat * m)
        dx_ref[...] = dx
        dxb_ref[...] = dx.astype(BF16)
        gpart = jnp.sum(dy * xhat, axis=0, keepdims=True)
        lrow = jnp.broadcast_to(lpart, (1, 128))

        @pl.when(i == 0)
        def _():
            dg_ref[...] = gpart
            l_ref[...] = lrow

        @pl.when(i > 0)
        def _():
            dg_ref[...] += gpart
            l_ref[...] += lrow

    row = _bs((tb, D), lambda i: (i, 0))
    return pl.pallas_call(
        body, grid=(T // tb,),
        in_specs=[row, _bs((1, D), lambda i: (0, 0)), row],
        out_specs=[_bs((1, 128), lambda i: (0, 0)), row, row, _bs((1, D), lambda i: (0, 0))],
        out_shape=[S((1, 128), F32), S((T, D), F32), S((T, D), BF16), S((1, D), F32)],
        compiler_params=_cp(), name="loss_head")(x, gain, tgt)


def _sigmoid(z):
    return 0.5 * jnp.tanh(0.5 * z) + 0.5


def _ffn_in(h, win):
    T = h.shape[0]
    Fs = win.shape[2]
    F = 2 * Fs
    tn = _tile(Fs, (256, 128))
    nb = Fs // tn
    tm = _tile(T, (1024, 512))

    def body(h_ref, wg_ref, wu_ref, zg_ref, zu_ref, a_ref):
        hv = h_ref[...]
        zg = jnp.dot(hv, wg_ref[...], preferred_element_type=F32)
        zu = jnp.dot(hv, wu_ref[...], preferred_element_type=F32)
        zg_ref[...] = zg.astype(BF16)
        zu_ref[...] = zu.astype(BF16)
        a_ref[...] = (zg * _sigmoid(zg) * zu).astype(BF16)

    col = _bs((tm, tn), lambda i, j: (i, j))
    return pl.pallas_call(
        body, grid=(T // tm, F // tn),
        in_specs=[_bs((tm, D), lambda i, j: (i, 0)),
                  _bs((None, D, tn), lambda i, j: (j // nb, 0, j % nb)),
                  _bs((None, D, tn), lambda i, j: (2 + j // nb, 0, j % nb))],
        out_specs=[col, col, col],
        out_shape=[S((T, F), BF16)] * 3, compiler_params=_cp(), name="ffn_in")(h, win, win)


def _ffn_out(act, wout, x, dep=None):
    T, F = act.shape
    tm = _tile(T, (512,))
    tn = 1024
    return _mm("ffn_out", act, wout,
               _bs((tm, F), lambda j, i, k: (i, 0)), _bs((F, tn), lambda j, i, k: (0, j)),
               _bs((tm, tn), lambda j, i, k: (i, j)), S((T, D), F32), (D // tn, T // tm, 1), NN, 1, (tm, tn),
               scale=0.5, res=x, r_spec=_bs((tm, tn), lambda j, i, k: (i, j)), dep=dep)


def _ffn_bwd_dz(dyb, wout, zg, zu, dep):
    T, F = zg.shape
    tm = _tile(T, (1024, 512))
    tn = _tile(F, (512, 256))

    def body(dy_ref, w_ref, zg_ref, zu_ref, dep_ref, dz_ref):
        da = 0.5 * lax.dot_general(dy_ref[...], w_ref[...], NT, preferred_element_type=F32)
        zg = zg_ref[...].astype(F32)
        zu = zu_ref[...].astype(F32)
        s = _sigmoid(zg)
        dz_ref[0] = (da * zu * (s * (1.0 + zg * (1.0 - s)))).astype(BF16)
        dz_ref[1] = (da * (zg * s)).astype(BF16)

    col = _bs((tm, tn), lambda i, j: (i, j))
    return pl.pallas_call(
        body, grid=(T // tm, F // tn),
        in_specs=[_bs((tm, D), lambda i, j: (i, 0)), _bs((tn, D), lambda i, j: (j, 0)), col, col,
                  pl.BlockSpec(memory_space=pl.ANY)],
        out_specs=_bs((2, tm, tn), lambda i, j: (0, i, j)), out_shape=S((2, T, F), BF16),
        compiler_params=_cp(), name="ffn_bwd_dz")(dyb, wout, zg, zu, dep)


def _ffn_bwd_dh(dz, win):
    _, T, F = dz.shape
    Fs = win.shape[2]
    tk = _tile(Fs, (1408, 256, 128))
    nkb = Fs // tk
    tm = _tile(T, (1024, 512))
    tn = 1024
    nk = 2 * nkb

    def body(dzg_ref, dzu_ref, wg_ref, wu_ref, o_ref, acc_ref):
        k = pl.program_id(2)
        prod = (lax.dot_general(dzg_ref[...], wg_ref[...], NT, preferred_element_type=F32)
                + lax.dot_general(dzu_ref[...], wu_ref[...], NT, preferred_element_type=F32))

        @pl.when(k == 0)
        def _():
            acc_ref[...] = prod

        @pl.when(k > 0)
        def _():
            acc_ref[...] += prod

        @pl.when(k == nk - 1)
        def _():
            o_ref[...] = acc_ref[...]

    def a_spec(half):
        return _bs((None, tm, tk), lambda i, j, k: (half, i, k))

    return pl.pallas_call(
        body, grid=(T // tm, D // tn, nk),
        in_specs=[a_spec(0), a_spec(1),
                  _bs((None, tn, tk), lambda i, j, k: (k // nkb, j, k % nkb)),
                  _bs((None, tn, tk), lambda i, j, k: (2 + k // nkb, j, k % nkb))],
        out_specs=_bs((tm, tn), lambda i, j, k: (i, j)), out_shape=S((T, D), F32),
        scratch_shapes=[pltpu.VMEM((tm, tn), F32)], compiler_params=_cp(), name="ffn_bwd_dh")(dz, dz, win, win)


def _ffn_bwd_dwin(h, dz):
    _, T, F = dz.shape
    Fs = F // 2
    tn = _tile(Fs, (1408, 256, 128))
    nb = Fs // tn
    tm = 512
    return _mm("ffn_bwd_dwin", h, dz,
               _bs((T, tm), lambda i, j, k: (0, i)), _bs((None, T, tn), lambda i, j, k: (j // (2 * nb), 0, j % (2 * nb))),
               _bs((None, tm, tn), lambda i, j, k: (j // nb, i, j % nb)), S((4, D, Fs), BF16),
               (D // tm, 4 * nb, 1), TN, 1, (tm, tn))


def _mm_tn(name, a, b, scale=1.0, tm=512, tn=1024):
    T, M = a.shape
    N = b.shape[1]
    tm = _tile(M, (tm, 512, 256, 128))
    tn = _tile(N, (tn, 1152, 1024, 512, 128))
    return _mm(name, a, b,
               _bs((T, tm), lambda i, j, k: (0, i)), _bs((T, tn), lambda i, j, k: (0, j)),
               _bs((tm, tn), lambda i, j, k: (i, j)), S((M, N), BF16), (M // tm, N // tn, 1), TN, 1, (tm, tn), scale=scale)


def _mm_nt_full(name, a, b, tn, dep=None):
    T, K = a.shape
    N = b.shape[0]
    tm = _tile(T, (1024, 512))
    return _mm(name, a, b,
               _bs((tm, K), lambda i, j, k: (i, 0)), _bs((tn, K), lambda i, j, k: (j, 0)),
               _bs((tm, tn), lambda i, j, k: (i, j)), S((T, N), F32), (T // tm, N // tn, 1), NT, 1, (tm, tn), dep=dep)


def _bt(T):
    return _tile(T, (512,))


def _down(ext, s):
    return pltpu.roll(ext, s, 0)[8:, :]


def _up(ext, s):
    n = ext.shape[0]
    return pltpu.roll(ext, n - s, 0)[: n - 8, :]


def _halo_prev(ref, start, b):
    lo = pl.multiple_of(jnp.maximum(start - 8, 0), 8)
    return ref[pl.ds(lo, 8), :] * (b > 0).astype(F32)


def _halo_next(ref, start, bt, b, nb):
    lo = pl.multiple_of(jnp.minimum(start + bt, (nb - 1) * bt), 8)
    return ref[pl.ds(lo, 8), :] * (b < nb - 1).astype(F32)


def _scan_fwd(A, U):
    n = U.shape[0]
    row = lax.broadcasted_iota(jnp.int32, U.shape, 0)
    d = 1
    while d < n:
        keep = row >= d
        Us = jnp.where(keep, pltpu.roll(U, d, 0), 0.0)
        if A is None:
            U = U + Us
        else:
            As = jnp.where(keep, pltpu.roll(A, d, 0), 1.0)
            U = A * Us + U
            A = A * As
        d *= 2
    return A, U


def _scan_bwd(A, U):
    n = U.shape[0]
    row = lax.broadcasted_iota(jnp.int32, U.shape, 0)
    d = 1
    while d < n:
        keep = row < n - d
        Us = jnp.where(keep, pltpu.roll(U, n - d, 0), 0.0)
        if A is None:
            U = U + Us
        else:
            As = jnp.where(keep, pltpu.roll(A, n - d, 0), 1.0)
            U = A * Us + U
            A = A * As
        d *= 2
    return A, U


def _softplus(z):
    return jnp.maximum(z, 0.0) + jnp.log(1.0 + jnp.exp(-jnp.abs(z)))


def _gelu_parts(g):
    k0 = math.sqrt(2.0 / math.pi)
    t = jnp.tanh(k0 * (g + 0.044715 * g * g * g))
    gel = 0.5 * g * (1.0 + t)
    dgel = 0.5 * (1.0 + t) + 0.5 * g * (1.0 - t * t) * k0 * (1.0 + 3.0 * 0.044715 * g * g)
    return gel, dgel


def _conv_fwd(P, cw):
    T = P.shape[0]
    bt = _bt(T)
    nb = T // bt

    def body(b_ref, c_ref, v_ref, w_ref, y_ref):
        w = w_ref[...]

        def step(b, carry):
            start = pl.multiple_of(b * bt, bt)
            rows = pl.ds(start, bt)
            m = c_ref[rows, :] * v_ref[rows, :]
            ext = jnp.concatenate([_halo_prev(c_ref, start, b) * _halo_prev(v_ref, start, b), m], axis=0)
            z = w[2:3, :] * m + w[1:2, :] * _down(ext, 1) + w[0:1, :] * _down(ext, 2)
            y_ref[rows, :] = b_ref[rows, :] * z
            return carry

        lax.fori_loop(0, nb, step, 0)

    def colspec(off):
        return _bs((T, 128), lambda c: (0, off + c))

    return pl.pallas_call(
        body, grid=(DC // 128,),
        in_specs=[colspec(0), colspec(4), colspec(8), _bs((8, 128), lambda c: (0, c))],
        out_specs=_bs((T, 128), lambda c: (0, c)), out_shape=S((T, DC), F32),
        compiler_params=_cp(), name="conv_fwd")(P, P, P, cw)


def _conv_bwd(P, cw, dy):
    T = P.shape[0]
    bt = _bt(T)
    nb = T // bt

    def body(b_ref, c_ref, v_ref, w_ref, dy_ref, db_ref, dc_ref, dv_ref, dw_ref):
        w = w_ref[...]

        def step(b, carry):
            a0, a1, a2 = carry
            start = pl.multiple_of(b * bt, bt)
            rows = pl.ds(start, bt)
            cb, cc, cv, dy = b_ref[rows, :], c_ref[rows, :], v_ref[rows, :], dy_ref[rows, :]
            m = cc * cv
            ext = jnp.concatenate([_halo_prev(c_ref, start, b) * _halo_prev(v_ref, start, b), m], axis=0)
            m1, m2 = _down(ext, 1), _down(ext, 2)
            z = w[2:3, :] * m + w[1:2, :] * m1 + w[0:1, :] * m2
            db_ref[rows, :] = dy * z
            dz = dy * cb
            extn = jnp.concatenate([dz, _halo_next(dy_ref, start, bt, b, nb) * _halo_next(b_ref, start, bt, b, nb)], axis=0)
            dm = w[2:3, :] * dz + w[1:2, :] * _up(extn, 1) + w[0:1, :] * _up(extn, 2)
            dc_ref[rows, :] = dm * cv
            dv_ref[rows, :] = dm * cc
            return (a0 + jnp.sum(dz * m2, axis=0, keepdims=True),
                    a1 + jnp.sum(dz * m1, axis=0, keepdims=True),
                    a2 + jnp.sum(dz * m, axis=0, keepdims=True))

        zero = jnp.zeros((1, 128), F32)
        a0, a1, a2 = lax.fori_loop(0, nb, step, (zero, zero, zero))
        dw_ref[...] = jnp.zeros((8, 128), F32)
        dw_ref[0:1, :] = a0
        dw_ref[1:2, :] = a1
        dw_ref[2:3, :] = a2

    def colspec(off):
        return _bs((T, 128), lambda c: (0, off + c))

    own = _bs((T, 128), lambda c: (0, c))
    return pl.pallas_call(
        body, grid=(DC // 128,),
        in_specs=[colspec(0), colspec(4), colspec(8), _bs((8, 128), lambda c: (0, c)), own],
        out_specs=[own, own, own, _bs((8, 128), lambda c: (0, c))],
        out_shape=[S((T, DC), F32)] * 3 + [S((8, DC), F32)], compiler_params=_cp(), name="conv_bwd")(P, P, P, cw, dy)


def _fgate_fwd(P, fb):
    T = P.shape[0]
    bt = _bt(T)
    nb = T // bt

    def body(f_ref, b_ref, c_ref):
        bias = b_ref[...]

        def step(b, carry):
            rows = pl.ds(pl.multiple_of(b * bt, bt), bt)
            logf = -_softplus(-(f_ref[rows, :] + bias))
            _, cs = _scan_fwd(None, logf)
            cs = cs + carry
            c_ref[rows, :] = cs
            return cs[bt - 1:bt, :]

        lax.fori_loop(0, nb, step, jnp.zeros((1, 128), F32))

    return pl.pallas_call(
        body, grid=(1,),
        in_specs=[_bs((T, 128), lambda i: (0, C_F)), _bs((1, 128), lambda i: (0, 0))],
        out_specs=_bs((T, 128), lambda i: (0, 0)), out_shape=S((T, 128), F32),
        compiler_params=_cp(), name="fgate_fwd")(P, fb)


def _fgate_bwd(P, fb, dcum):
    T = P.shape[0]
    bt = _bt(T)
    nb = T // bt

    def body(f_ref, b_ref, dc_ref, df_ref, db_ref):
        bias = b_ref[...]

        def step(i, carry):
            run, acc = carry
            b = nb - 1 - i
            rows = pl.ds(pl.multiple_of(b * bt, bt), bt)
            _, rs = _scan_bwd(None, dc_ref[rows, :])
            rs = rs + run
            df = rs * jax.nn.sigmoid(-(f_ref[rows, :] + bias))
            df_ref[rows, :] = df
            return rs[0:1, :], acc + jnp.sum(df, axis=0, keepdims=True)

        zero = jnp.zeros((1, 128), F32)
        _, acc = lax.fori_loop(0, nb, step, (zero, zero))
        db_ref[...] = acc

    return pl.pallas_call(
        body, grid=(1,),
        in_specs=[_bs((T, 128), lambda i: (0, C_F)), _bs((1, 128), lambda i: (0, 0)), _bs((T, 128), lambda i: (0, 0))],
        out_specs=[_bs((T, 128), lambda i: (0, 0)), _bs((1, 128), lambda i: (0, 0))],
        out_shape=[S((T, 128), F32), S((1, 128), F32)], compiler_params=_cp(), name="fgate_bwd")(P, fb, dcum)


def _att_tile(T):
    return _tile(T, (512,))


def _causal_mask(tq):
    return lax.broadcasted_iota(jnp.int32, (tq, tq), 1) <= lax.broadcasted_iota(jnp.int32, (tq, tq), 0)


def _attn_fwd(qkv, cumq, cumk):
    T = qkv.shape[0]
    tq = _att_tile(T)
    nq = T // tq

    def body(q_ref, k_ref, v_ref, cq_ref, ck_ref, o_ref, lse_ref):
        i = pl.program_id(1)
        q = q_ref[...]
        cq = cq_ref[...]

        def block(j, carry, diagonal):
            m_old, l_old, acc = carry
            rows = pl.ds(pl.multiple_of(j * tq, tq), tq)
            s = lax.dot_general(q, k_ref[rows, :], NT, preferred_element_type=F32)
            s = s * ATT_SCALE + cq - ck_ref[j]
            if diagonal:
                s = jnp.where(_causal_mask(tq), s, -jnp.inf)
            m_new = jnp.maximum(m_old, jnp.max(s, axis=1, keepdims=True))
            p = jnp.exp(s - m_new)
            alpha = jnp.exp(m_old - m_new)
            l_new = alpha * l_old + jnp.sum(p, axis=1, keepdims=True)
            acc = alpha * acc + jnp.dot(p.astype(BF16), v_ref[rows, :], preferred_element_type=F32)
            return m_new, l_new, acc

        init = (jnp.full((tq, 1), -jnp.inf, F32), jnp.zeros((tq, 1), F32), jnp.zeros((tq, HD), F32))
        carry = lax.fori_loop(0, i, lambda j, c: block(j, c, False), init)
        m, l, acc = block(i, carry, True)
        o_ref[...] = acc / l
        lse_ref[...] = m + jnp.log(l)

    return pl.pallas_call(
        body, grid=(NH, nq),
        in_specs=[_bs((tq, HD), lambda h, i: (i, h)),
                  _bs((T, HD), lambda h, i: (0, NH + h)),
                  _bs((T, HD), lambda h, i: (0, 2 * NH + h)),
                  _bs((None, tq, 1), lambda h, i: (h, i, 0)),
                  _bs((None, nq, 1, tq), lambda h, i: (h, 0, 0, 0))],
        out_specs=[_bs((tq, HD), lambda h, i: (i, h)), _bs((None, tq, 1), lambda h, i: (h, i, 0))],
        out_shape=[S((T, DA), F32), S((NH, T, 1), F32)],
        compiler_params=_cp(), name="attn_fwd")(qkv, qkv, qkv, cumq, cumk)


def _attn_bwd(qkv, cumq, cumk, lse, o, do):
    T = qkv.shape[0]
    tq = _att_tile(T)
    nq = T // tq

    def body(q_ref, k_ref, v_ref, cq_ref, ck_ref, lse_ref, o_ref, do_ref, dq_ref, dk_ref, dv_ref, dc_ref, dr_ref):
        j = pl.program_id(1)

        @pl.when(j == 0)
        def _():
            dq_ref[...] = jnp.zeros((T, HD), F32)
            dr_ref[...] = jnp.zeros((T, 1), F32)

        k = k_ref[...]
        v = v_ref[...]
        ck = ck_ref[...]

        def block(i, carry, diagonal):
            dk_acc, dv_acc, dc_acc = carry
            rows = pl.ds(pl.multiple_of(i * tq, tq), tq)
            q = q_ref[rows, :]
            do_f = do_ref[rows, :]
            dob = do_f.astype(BF16)
            s = lax.dot_general(q, k, NT, preferred_element_type=F32)
            p = jnp.exp(s * ATT_SCALE + cq_ref[rows, :] - ck - lse_ref[rows, :])
            if diagonal:
                p = jnp.where(_causal_mask(tq), p, 0.0)
            delta = jnp.sum(do_f * o_ref[rows, :], axis=1, keepdims=True)
            dp = lax.dot_general(dob, v, NT, preferred_element_type=F32)
            ds = p * (dp - delta)
            dsb = (ds * ATT_SCALE).astype(BF16)
            dq_ref[rows, :] += jnp.dot(dsb, k, preferred_element_type=F32)
            dr_ref[rows, :] += jnp.sum(ds, axis=1, keepdims=True)
            return (dk_acc + lax.dot_general(dsb, q, TN, preferred_element_type=F32),
                    dv_acc + lax.dot_general(p.astype(BF16), dob, TN, preferred_element_type=F32),
                    dc_acc - jnp.sum(ds, axis=0, keepdims=True))

        init = (jnp.zeros((tq, HD), F32), jnp.zeros((tq, HD), F32), jnp.zeros((1, tq), F32))
        carry = block(j, init, True)
        dk_acc, dv_acc, dc_acc = lax.fori_loop(j + 1, nq, lambda i, c: block(i, c, False), carry)
        dk_ref[...] = dk_acc
        dv_ref[...] = dv_acc
        dc_ref[...] = dc_acc

    def whole(col):
        return _bs((T, HD), lambda h, j: (0, col + h))

    qvec = _bs((None, T, 1), lambda h, j: (h, 0, 0))
    kv_out = _bs((tq, HD), lambda h, j: (j, h))
    return pl.pallas_call(
        body, grid=(NH, nq),
        in_specs=[whole(0), _bs((tq, HD), lambda h, j: (j, NH + h)), _bs((tq, HD), lambda h, j: (j, 2 * NH + h)),
                  qvec, _bs((None, None, 1, tq), lambda h, j: (h, j, 0, 0)), qvec, whole(0), whole(0)],
        out_specs=[whole(0), kv_out, kv_out, _bs((None, 1, tq), lambda h, j: (h, 0, j)), qvec],
        out_shape=[S((T, DA), F32)] * 3 + [S((NH, 1, T), F32), S((NH, T, 1), F32)],
        compiler_params=_cp(), name="attn_bwd")(qkv, qkv, qkv, cumq, cumk, lse, o, do)


def _lru_gates(xr, wa, wx, ba, bx, sp):
    xb = xr.astype(BF16)
    r = jax.nn.sigmoid(jnp.dot(xb, wa, preferred_element_type=F32) + ba)
    ig = jax.nn.sigmoid(jnp.dot(xb, wx, preferred_element_type=F32) + bx)
    log_a = -LRU_C * r * sp
    a = jnp.exp(log_a)
    th = jnp.tanh(log_a)
    om = -2.0 * th / (1.0 - th)
    mult = jnp.sqrt(om)
    return xb, r, ig, a, om, mult


def _lru_xr(lx_ref, cw, cb, start, b, rows):
    lx = lx_ref[rows, :]
    ext = jnp.concatenate([_halo_prev(lx_ref, start, b), lx], axis=0)
    return cw[3:4, :] * lx + cw[2:3, :] * _down(ext, 1) + cw[1:2, :] * _down(ext, 2) + cw[0:1, :] * _down(ext, 3) + cb


def _lru_fwd(P, lcw, vec, wa, wx):
    T = P.shape[0]
    bt = _bt(T)
    nb = T // bt

    def body(g_ref, lx_ref, cw_ref, vec_ref, wa_ref, wx_ref, y_ref, h_ref):
        cw = cw_ref[...]
        vec = vec_ref[...]
        wa = wa_ref[...].astype(BF16)
        wx = wx_ref[...].astype(BF16)
        sp = _softplus(-vec[3:4, :])

        def step(b, carry):
            start = pl.multiple_of(b * bt, bt)
            rows = pl.ds(start, bt)
            xr = _lru_xr(lx_ref, cw, vec[0:1, :], start, b, rows)
            _, _, ig, a, _, mult = _lru_gates(xr, wa, wx, vec[1:2, :], vec[2:3, :], sp)
            u = mult * (ig * xr)
            ac, hc = _scan_fwd(a, u)
            hb = hc + ac * carry
            h_ref[rows, :] = hb
            gel, _ = _gelu_parts(g_ref[rows, :])
            y_ref[rows, :] = gel * hb
            return hb[bt - 1:bt, :]

        lax.fori_loop(0, nb, step, jnp.zeros((1, 128), F32))

    own = _bs((T, 128), lambda c: (0, c))
    return pl.pallas_call(
        body, grid=(DL // 128,),
        in_specs=[_bs((T, 128), lambda c: (0, C_GATE + c)), _bs((T, 128), lambda c: (0, C_LX + c)),
                  _bs((8, 128), lambda c: (0, c)), _bs((8, 128), lambda c: (0, c)),
                  _bs((None, LB, LB), lambda c: (c, 0, 0)), _bs((None, LB, LB), lambda c: (c, 0, 0))],
        out_specs=[own, own], out_shape=[S((T, DL), F32)] * 2, compiler_params=_cp(), name="lru_fwd")(P, P, lcw, vec, wa, wx)


def _lru_bwd(P, lcw, vec, wa, wx, hst, dy):
    T = P.shape[0]
    bt = _bt(T)
    nb = T // bt

    def body(g_ref, lx_ref, cw_ref, vec_ref, wa_ref, wx_ref, h_ref, dy_ref,
             dg_ref, dlx_ref, sm_ref, dwa_ref, dwx_ref, dxr_s):
        cw = cw_ref[...]
        vec = vec_ref[...]
        wa = wa_ref[...].astype(BF16)
        wx = wx_ref[...].astype(BF16)
        lam = vec[3:4, :]
        sp = _softplus(-lam)
        dwa_ref[...] = jnp.zeros((LB, LB), F32)
        dwx_ref[...] = jnp.zeros((LB, LB), F32)
        zero = jnp.zeros((1, 128), F32)

        def step1(i, carry):
            wc, s_cb, s_ba, s_bx, s_sp = carry
            b = nb - 1 - i
            start = pl.multiple_of(b * bt, bt)
            rows = pl.ds(start, bt)
            xr = _lru_xr(lx_ref, cw, vec[0:1, :], start, b, rows)
            xb, r, ig, a, om, mult = _lru_gates(xr, wa, wx, vec[1:2, :], vec[2:3, :], sp)
            hb = h_ref[rows, :]
            dy = dy_ref[rows, :]
            gel, dgel = _gelu_parts(g_ref[rows, :])
            dg_ref[rows, :] = dy * hb * dgel
            dh = dy * gel
            ac, wcum = _scan_bwd(a, a * dh)
            w = wcum + ac * wc
            g = dh + _up(jnp.concatenate([w, jnp.broadcast_to(wc, (8, 128))], axis=0), 1)
            hprev = _down(jnp.concatenate([_halo_prev(h_ref, start, b), hb], axis=0), 1)
            da = g * hprev
            dmult = g * (ig * xr)
            dix = g * mult
            di = dix * xr
            dlog_a = da * a - dmult * ((1.0 - om) / mult)
            dr = dlog_a * (-LRU_C * sp)
            dpr = dr * r * (1.0 - r)
            dpi = di * ig * (1.0 - ig)
            dprb, dpib = dpr.astype(BF16), dpi.astype(BF16)
            dwa_ref[...] += lax.dot_general(xb, dprb, TN, preferred_element_type=F32)
            dwx_ref[...] += lax.dot_general(xb, dpib, TN, preferred_element_type=F32)
            dxr = (dix * ig + lax.dot_general(dprb, wa, NT, preferred_element_type=F32)
                   + lax.dot_general(dpib, wx, NT, preferred_element_type=F32))
            dxr_s[rows, :] = dxr
            return (w[0:1, :], s_cb + jnp.sum(dxr, axis=0, keepdims=True), s_ba + jnp.sum(dpr, axis=0, keepdims=True),
                    s_bx + jnp.sum(dpi, axis=0, keepdims=True), s_sp + jnp.sum(dlog_a * (-LRU_C * r), axis=0, keepdims=True))

        _, s_cb, s_ba, s_bx, s_sp = lax.fori_loop(0, nb, step1, (zero, zero, zero, zero, zero))

        def step2(b, carry):
            t0, t1, t2, t3 = carry
            start = pl.multiple_of(b * bt, bt)
            rows = pl.ds(start, bt)
            dxr = dxr_s[rows, :]
            extn = jnp.concatenate([dxr, _halo_next(dxr_s, start, bt, b, nb)], axis=0)
            dlx_ref[rows, :] = (cw[3:4, :] * dxr + cw[2:3, :] * _up(extn, 1) + cw[1:2, :] * _up(extn, 2)
                                + cw[0:1, :] * _up(extn, 3))
            lx = lx_ref[rows, :]
            ext = jnp.concatenate([_halo_prev(lx_ref, start, b), lx], axis=0)
            return (t0 + jnp.sum(dxr * _down(ext, 3), axis=0, keepdims=True),
                    t1 + jnp.sum(dxr * _down(ext, 2), axis=0, keepdims=True),
                    t2 + jnp.sum(dxr * _down(ext, 1), axis=0, keepdims=True),
                    t3 + jnp.sum(dxr * lx, axis=0, keepdims=True))

        t0, t1, t2, t3 = lax.fori_loop(0, nb, step2, (zero, zero, zero, zero))
        sm_ref[...] = jnp.zeros((16, 128), F32)
        for k, val in enumerate((t0, t1, t2, t3, s_cb, s_ba, s_bx, -s_sp * jax.nn.sigmoid(-lam))):
            sm_ref[k:k + 1, :] = val

    own = _bs((T, 128), lambda c: (0, c))
    wspec = _bs((None, LB, LB), lambda c: (c, 0, 0))
    return pl.pallas_call(
        body, grid=(DL // 128,),
        in_specs=[_bs((T, 128), lambda c: (0, C_GATE + c)), _bs((T, 128), lambda c: (0, C_LX + c)),
                  _bs((8, 128), lambda c: (0, c)), _bs((8, 128), lambda c: (0, c)), wspec, wspec, own, own],
        out_specs=[own, own, _bs((16, 128), lambda c: (0, c)), wspec, wspec],
        out_shape=[S((T, DL), F32)] * 2 + [S((16, DL), F32), S((4, LB, LB), F32), S((4, LB, LB), F32)],
        scratch_shapes=[pltpu.VMEM((T, 128), F32)], compiler_params=_cp(), name="lru_bwd")(P, P, lcw, vec, wa, wx, hst, dy)


_GROUPS = ((0, DC), (DC, DC + DA), (DC + DA, D))


def _gnorm_fwd(yc, ya, yl, gain):
    T = yc.shape[0]
    tb = _tile(T, (512,))

    def body(c_ref, a_ref, l_ref, g_ref, yn_ref, r0_ref, r1_ref, r2_ref):
        for (lo, hi), src, r_ref in zip(_GROUPS, (c_ref, a_ref, l_ref), (r0_ref, r1_ref, r2_ref)):
            yv = src[...]
            r = lax.rsqrt(jnp.mean(yv * yv, axis=1, keepdims=True) + EPS)
            yn_ref[:, lo:hi] = (yv * r * g_ref[:, lo:hi]).astype(BF16)
            r_ref[...] = r

    rs = _bs((tb, 1), lambda i: (i, 0))
    return pl.pallas_call(
        body, grid=(T // tb,),
        in_specs=[_bs((tb, DC), lambda i: (i, 0)), _bs((tb, DA), lambda i: (i, 0)), _bs((tb, DL), lambda i: (i, 0)),
                  _bs((1, D), lambda i: (0, 0))],
        out_specs=[_bs((tb, D), lambda i: (i, 0)), rs, rs, rs],
        out_shape=[S((T, D), BF16)] + [S((T, 1), F32)] * 3, compiler_params=_cp(), name="gnorm_fwd")(yc, ya, yl, gain)


def _gnorm_bwd(dyn, yc, ya, yl, r0, r1, r2, gain):
    T = yc.shape[0]
    tb = _tile(T, (512,))

    def body(d_ref, c_ref, a_ref, l_ref, r0_ref, r1_ref, r2_ref, g_ref, dc_ref, da_ref, dl_ref, dg_ref):
        i = pl.program_id(0)
        for (lo, hi), src, r_ref, dst in zip(_GROUPS, (c_ref, a_ref, l_ref), (r0_ref, r1_ref, r2_ref), (dc_ref, da_ref, dl_ref)):
            r = r_ref[...]
            yhat = src[...] * r
            dy = d_ref[:, lo:hi]
            dyh = dy * g_ref[:, lo:hi]
            m = jnp.mean(dyh * yhat, axis=1, keepdims=True)
            dst[...] = r * (dyh - yhat * m)
            part = jnp.sum(dy * yhat, axis=0, keepdims=True)

            @pl.when(i == 0)
            def _():
                dg_ref[:, lo:hi] = part

            @pl.when(i > 0)
            def _():
                dg_ref[:, lo:hi] += part

    rs = _bs((tb, 1), lambda i: (i, 0))
    specs = [_bs((tb, DC), lambda i: (i, 0)), _bs((tb, DA), lambda i: (i, 0)), _bs((tb, DL), lambda i: (i, 0))]
    return pl.pallas_call(
        body, grid=(T // tb,),
        in_specs=[_bs((tb, D), lambda i: (i, 0))] + specs + [rs, rs, rs, _bs((1, D), lambda i: (0, 0))],
        out_specs=specs + [_bs((1, D), lambda i: (0, 0))],
        out_shape=[S((T, DC), F32), S((T, DA), F32), S((T, DL), F32), S((1, D), F32)],
        compiler_params=_cp(), name="gnorm_bwd")(dyn, yc, ya, yl, r0, r1, r2, gain)


HBM = pl.BlockSpec(memory_space=pltpu.HBM)
N_BIG = 6


def _place():
    x, y, c = lax.axis_index("x"), lax.axis_index("y"), lax.axis_index("c")
    return x, y, c, 2 * x + y


def _peer(x, y, j):
    return x ^ ((j + 1) >> 1), y ^ ((j + 1) & 1)


SEM = pl.BlockSpec(memory_space=pltpu.SEMAPHORE)
ANY = pl.BlockSpec(memory_space=pl.ANY)
VM = pl.BlockSpec(memory_space=pltpu.VMEM)
EFFECT = pltpu.SideEffectType.DATAFLOW_SIDE_EFFECTING
N_AG = N_BIG + 1


def _hbm(a):
    return pltpu.with_memory_space_constraint(a, pltpu.HBM)


AG_ORDER = (0, 1, N_BIG, 2, 3, 4, 5)
AG_FIRST, AG_REST = (0, 1), (N_BIG, 2, 3, 4, 5)


def _ag_copy(src, land, ssem, rsem, t, j, chip):
    x, y, c, _ = _place()
    px, py = _peer(x, y, j)
    if t == N_BIG:
        s_ref, d_ref = src, land.at[chip]
    else:
        rh = src.shape[0] // 2
        half = pl.ds(c * rh, rh)
        s_ref, d_ref = src.at[half], land.at[chip, half]
    return pltpu.make_async_remote_copy(src_ref=s_ref, dst_ref=d_ref, send_sem=ssem.at[3 * t + j], recv_sem=rsem.at[3 * t + j],
                                        device_id=(px, py, c), device_id_type=MESH)


def _ag_start(l, srcs, dep):
    n = N_AG

    def body(*refs):
        src = refs[:n]
        ssem, rsem = refs[2 * n + 1], refs[2 * n + 2]
        land = refs[3 * n + 3:4 * n + 3]
        token = refs[4 * n + 3]
        _, _, _, me = _place()
        for t in AG_ORDER:
            for j in range(3):
                _ag_copy(src[t], land[t], ssem, rsem, t, j, me).start()
        token[...] = jnp.zeros_like(token)

    lands = [lax.empty((4,) + a.shape, a.dtype) for a in srcs]
    dma = pltpu.SemaphoreType.DMA
    outs = pl.pallas_call(
        body, name=f"ag_start_{l}",
        out_shape=(dma((3 * n,)), dma((3 * n,))) + tuple(pltpu.HBM(a.shape, a.dtype) for a in list(srcs) + lands) + (S((8, 128), F32),),
        in_specs=[HBM] * (2 * n) + [ANY], out_specs=(SEM, SEM) + (HBM,) * (2 * n) + (VM,),
        input_output_aliases={i: 2 + i for i in range(2 * n)},
        compiler_params=pltpu.CompilerParams(has_side_effects=EFFECT),
    )(*[_hbm(a) for a in srcs], *[_hbm(a) for a in lands], dep)
    return outs[0], outs[1], outs[2:2 + n], outs[2 + n:2 + 2 * n], outs[-1]


def _ag_wait(name, idx, ssem, rsem, srcs, lands, after):
    n = len(idx)

    def body(*refs):
        src, land = refs[:n], refs[n:2 * n]
        ssem, rsem = refs[2 * n], refs[2 * n + 1]
        x, y, _, _ = _place()
        for p, t in enumerate(idx):
            for j in range(3):
                px, py = _peer(x, y, j)
                cp = _ag_copy(src[p], land[p], ssem, rsem, t, j, 2 * px + py)
                cp.wait_send()
                cp.wait_recv()

    outs = pl.pallas_call(
        body, name=name,
        out_shape=tuple(pltpu.HBM(a.shape, a.dtype) for a in list(srcs) + list(lands)),
        in_specs=[HBM] * (2 * n) + [SEM, SEM, ANY], out_specs=(HBM,) * (2 * n),
        input_output_aliases={i: i for i in range(2 * n)},
        compiler_params=pltpu.CompilerParams(has_side_effects=EFFECT),
    )(*srcs, *lands, ssem, rsem, after)
    return outs[:n], outs[n:]


def _fwd_copies(idx, src, land, ssem, rsem, recv_side):
    x, y, c, me = _place()
    cps = []
    for p, t in enumerate(idx):
        cps.append(pltpu.make_async_remote_copy(src_ref=src[p], dst_ref=land[p].at[me], send_sem=ssem.at[4 * p], recv_sem=rsem.at[4 * p],
                                                device_id=(x, y, 1 - c), device_id_type=MESH))
        if t == N_BIG:
            continue
        rh = src[p].shape[0] // 2
        for j in range(3):
            px, py = _peer(x, y, j)
            part = land[p].at[2 * px + py, pl.ds(((1 - c) if recv_side else c) * rh, rh)]
            cps.append(pltpu.make_async_remote_copy(src_ref=part, dst_ref=part, send_sem=ssem.at[4 * p + 1 + j],
                                                    recv_sem=rsem.at[4 * p + 1 + j], device_id=(x, y, 1 - c), device_id_type=MESH))
    return cps


def _ag_fwd_start(name, idx, srcs, lands, dep):
    n = len(idx)

    def body(*refs):
        src = refs[:n]
        ssem, rsem = refs[2 * n + 1], refs[2 * n + 2]
        land = refs[3 * n + 3:4 * n + 3]
        token = refs[4 * n + 3]
        for cp in _fwd_copies(idx, src, land, ssem, rsem, False):
            cp.start()
        token[...] = jnp.zeros_like(token)

    dma = pltpu.SemaphoreType.DMA
    outs = pl.pallas_call(
        body, name=name,
        out_shape=(dma((4 * n,)), dma((4 * n,))) + tuple(pltpu.HBM(a.shape, a.dtype) for a in list(srcs) + list(lands)) + (S((8, 128), F32),),
        in_specs=[HBM] * (2 * n) + [ANY], out_specs=(SEM, SEM) + (HBM,) * (2 * n) + (VM,),
        input_output_aliases={i: 2 + i for i in range(2 * n)},
        compiler_params=pltpu.CompilerParams(has_side_effects=EFFECT),
    )(*srcs, *lands, dep)
    return outs[0], outs[1], outs[2:2 + n], outs[2 + n:2 + 2 * n], outs[-1]


def _ag_fwd_wait(name, idx, ssem, rsem, srcs, lands, after):
    n = len(idx)

    def body(*refs):
        src, land = refs[:n], refs[n:2 * n]
        ssem, rsem = refs[2 * n], refs[2 * n + 1]
        for mine, theirs in zip(_fwd_copies(idx, src, land, ssem, rsem, False), _fwd_copies(idx, src, land, ssem, rsem, True)):
            mine.wait_send()
            theirs.wait_recv()

    outs = pl.pallas_call(
        body, name=name,
        out_shape=tuple(pltpu.HBM(a.shape, a.dtype) for a in list(srcs) + list(lands)),
        in_specs=[HBM] * (2 * n) + [SEM, SEM, ANY], out_specs=(HBM,) * (2 * n),
        input_output_aliases={i: i for i in range(2 * n)},
        compiler_params=pltpu.CompilerParams(has_side_effects=EFFECT),
    )(*srcs, *lands, ssem, rsem, after)
    return outs[n:]


def _pair_copy(g, land, ssem, rsem, t):
    x, y, c, _ = _place()
    rh = g.shape[1] // 2
    return pltpu.make_async_remote_copy(src_ref=g.at[:, pl.ds((1 - c) * rh, rh), :], dst_ref=land,
                                        send_sem=ssem.at[t], recv_sem=rsem.at[t], device_id=(x, y, 1 - c), device_id_type=MESH)


def _rs_pair(grads):
    n = len(grads)

    def body(*refs):
        g, out = refs[:n], refs[n:2 * n]
        ssem, rsem = refs[2 * n:]
        for t in range(n):
            _pair_copy(g[t], out[t], ssem, rsem, t).start()
        for t in range(n):
            _pair_copy(g[t], out[t], ssem, rsem, t).wait()

    dma = pltpu.SemaphoreType.DMA
    return pl.pallas_call(
        body, in_specs=[HBM] * n, out_specs=[HBM] * n,
        out_shape=[S((4, g.shape[1] // 2, g.shape[2]), g.dtype) for g in grads],
        scratch_shapes=[dma((n,)), dma((n,))], name="rs_pair")(*grads)


def _rs_pair_start(name, grads, dep):
    n = len(grads)

    def body(*refs):
        g = refs[:n]
        ssem, rsem = refs[2 * n + 1], refs[2 * n + 2]
        land = refs[3 * n + 3:4 * n + 3]
        token = refs[4 * n + 3]
        for t in range(n):
            _pair_copy(g[t], land[t], ssem, rsem, t).start()
        token[...] = jnp.zeros_like(token)

    lands = [lax.empty((4, a.shape[1] // 2, a.shape[2]), a.dtype) for a in grads]
    dma = pltpu.SemaphoreType.DMA
    outs = pl.pallas_call(
        body, name=name,
        out_shape=(dma((n,)), dma((n,))) + tuple(pltpu.HBM(a.shape, a.dtype) for a in list(grads) + lands) + (S((8, 128), F32),),
        in_specs=[HBM] * (2 * n) + [ANY], out_specs=(SEM, SEM) + (HBM,) * (2 * n) + (VM,),
        input_output_aliases={i: 2 + i for i in range(2 * n)},
        compiler_params=pltpu.CompilerParams(has_side_effects=EFFECT),
    )(*[_hbm(a) for a in grads], *[_hbm(a) for a in lands], dep)
    return outs[0], outs[1], outs[2:2 + n], outs[2 + n:2 + 2 * n], outs[-1]


def _rs_pair_wait(name, ssem, rsem, grads, lands, after):
    n = len(grads)

    def body(*refs):
        g, land = refs[:n], refs[n:2 * n]
        ssem, rsem = refs[2 * n], refs[2 * n + 1]
        for t in range(n):
            cp = _pair_copy(g[t], land[t], ssem, rsem, t)
            cp.wait_send()
            cp.wait_recv()

    outs = pl.pallas_call(
        body, name=name,
        out_shape=tuple(pltpu.HBM(a.shape, a.dtype) for a in list(grads) + list(lands)),
        in_specs=[HBM] * (2 * n) + [SEM, SEM, ANY], out_specs=(HBM,) * (2 * n),
        input_output_aliases={i: i for i in range(2 * n)},
        compiler_params=pltpu.CompilerParams(has_side_effects=EFFECT),
    )(*grads, *lands, ssem, rsem, after)
    return outs[:n], outs[n:]


def _rs_copy(s, land, ssem, rsem, t, j):
    x, y, c, _ = _place()
    px, py = _peer(x, y, j)
    return pltpu.make_async_remote_copy(src_ref=s[t].at[2 * px + py], dst_ref=land[t].at[j],
                                        send_sem=ssem.at[3 * t + j], recv_sem=rsem.at[3 * t + j], device_id=(px, py, c), device_id_type=MESH)


def _rs_start(name, sums, dep):
    n = len(sums)

    def body(*refs):
        s = refs[:n]
        ssem, rsem = refs[2 * n + 1], refs[2 * n + 2]
        land = refs[3 * n + 3:4 * n + 3]
        token = refs[4 * n + 3]
        for t in range(n):
            for j in range(3):
                _rs_copy(s, land, ssem, rsem, t, j).start()
        token[...] = jnp.zeros_like(token)

    lands = [lax.empty((3,) + a.shape[1:], a.dtype) for a in sums]
    dma = pltpu.SemaphoreType.DMA
    outs = pl.pallas_call(
        body, name=name,
        out_shape=(dma((3 * n,)), dma((3 * n,))) + tuple(pltpu.HBM(a.shape, a.dtype) for a in list(sums) + lands) + (S((8, 128), F32),),
        in_specs=[HBM] * (2 * n) + [ANY], out_specs=(SEM, SEM) + (HBM,) * (2 * n) + (VM,),
        input_output_aliases={i: 2 + i for i in range(2 * n)},
        compiler_params=pltpu.CompilerParams(has_side_effects=EFFECT),
    )(*[_hbm(a) for a in sums], *[_hbm(a) for a in lands], dep)
    return outs[0], outs[1], outs[2:2 + n], outs[2 + n:2 + 2 * n], outs[-1]


def _rs_wait(name, ssem, rsem, sums, lands, after):
    n = len(sums)

    def body(*refs):
        s, land = refs[:n], refs[n:2 * n]
        ssem, rsem = refs[2 * n], refs[2 * n + 1]
        for t in range(n):
            for j in range(3):
                cp = _rs_copy(s, land, ssem, rsem, t, j)
                cp.wait_send()
                cp.wait_recv()

    outs = pl.pallas_call(
        body, name=name,
        out_shape=tuple(pltpu.HBM(a.shape, a.dtype) for a in list(sums) + list(lands)),
        in_specs=[HBM] * (2 * n) + [SEM, SEM, ANY], out_specs=(HBM,) * (2 * n),
        input_output_aliases={i: i for i in range(2 * n)},
        compiler_params=pltpu.CompilerParams(has_side_effects=EFFECT),
    )(*sums, *lands, ssem, rsem, after)
    return outs[n:]


def _rs_join(halves, lo, hi, dep):
    def body(*refs):
        h = refs[:N_BIG]
        out = refs[N_BIG + 1:2 * N_BIG + 1]
        ssem, rsem = refs[2 * N_BIG + 1:]
        x, y, c, _ = _place()

        def cp(t):
            return pltpu.make_async_remote_copy(
                src_ref=h[t].at[pl.ds(lo, hi - lo)], dst_ref=out[t], send_sem=ssem.at[t], recv_sem=rsem.at[t],
                device_id=(x, y, 1 - c), device_id_type=MESH)

        for t in range(N_BIG):
            cp(t).start()
        for t in range(N_BIG):
            cp(t).wait()

    dma = pltpu.SemaphoreType.DMA
    return pl.pallas_call(
        body, in_specs=[HBM] * N_BIG + [ANY], out_specs=[HBM] * N_BIG,
        out_shape=[S((hi - lo,) + h.shape[1:], h.dtype) for h in halves],
        scratch_shapes=[dma((N_BIG,)), dma((N_BIG,))], name="rs_join")(*halves, dep)


def _all_reduce_small(pack):
    R = pack.shape[0]
    rb = _tile(R, (512, 256, 128, 8))

    def body(x_ref, all_ref, sum_ref, send_sems, recv_sems, local_sem):
        x, y, c = lax.axis_index("x"), lax.axis_index("y"), lax.axis_index("c")
        me, sibling = (x, y, c), (x, y, 1 - c)
        chips = [(1 - x, y), (x, 1 - y), (1 - x, 1 - y)]

        def rows(px, py, pc):
            return all_ref.at[pl.ds((4 * px + 2 * py + pc) * R, R), :]

        def copy(k, block, to, src=None):
            return pltpu.make_async_remote_copy(
                src_ref=rows(*block) if src is None else src, dst_ref=rows(*block),
                send_sem=send_sems.at[k], recv_sem=recv_sems.at[k], device_id=to, device_id_type=MESH)

        mine = pltpu.make_async_copy(x_ref, rows(*me), local_sem)
        mine.start()
        first = [copy(0, me, sibling, src=x_ref)]
        first += [copy(1 + j, me, (*chip, c), src=x_ref) for j, chip in enumerate(chips)]
        for cp in first:
            cp.start()
        passed = [copy(4 + j, (*chip, c), sibling) for j, chip in enumerate(chips)]
        for j, chip in enumerate(chips):
            copy(1 + j, (*chip, c), me).wait_recv()
            passed[j].start()
        copy(0, sibling, me).wait_recv()
        for j, chip in enumerate(chips):
            copy(4 + j, (*chip, 1 - c), me).wait_recv()
        for cp in first + passed:
            cp.wait_send()
        mine.wait()

        def step(b, carry):
            off = pl.multiple_of(b * rb, rb)
            acc = all_ref[pl.ds(off, rb), :]
            for k in range(1, 8):
                acc = acc + all_ref[pl.ds(pl.multiple_of(k * R + off, 8), rb), :]
            sum_ref[pl.ds(off, rb), :] = acc
            return carry

        lax.fori_loop(0, R // rb, step, 0)

    vm = pl.BlockSpec(memory_space=pltpu.VMEM)
    dma = pltpu.SemaphoreType.DMA
    _, total = pl.pallas_call(
        body, in_specs=[vm], out_specs=[vm, vm],
        out_shape=[S((8 * R, 128), F32), S((R, 128), F32)],
        scratch_shapes=[dma((7,)), dma((7,)), dma],
        compiler_params=_cp(), name="allreduce_small")(pack)
    return total


def _row_tile(rh, cc, tile_bytes=3 * 1024 * 1024 // 2):
    for t in (512, 256, 128, 64, 32, 16):
        if rh % t == 0 and t * cc * 4 <= tile_bytes:
            return t
    return 16


def _my_chip():
    return 2 * lax.axis_index("x") + lax.axis_index("y")


def _pair_sum(g, recv):
    _, r, cc = g.shape
    rh = r // 2
    tb = _row_tile(rh, cc)
    nbh = rh // tb

    def body(g_ref, r_ref, o_ref):
        o_ref[...] = (g_ref[...].astype(F32) + r_ref[...].astype(F32)).astype(BF16)

    def chip(k):
        return (_my_chip() + 1 + k) % 4

    mine = _bs((None, tb, cc), lambda k, i: (chip(k), lax.axis_index("c") * nbh + i, 0))
    plain = _bs((None, tb, cc), lambda k, i: (chip(k), i, 0))
    return pl.pallas_call(body, grid=(3, nbh), in_specs=[mine, plain], out_specs=plain,
                          out_shape=S((4, rh, cc), BF16), compiler_params=_cp(), name="rs_pair_sum")(g, recv)


def _owner_sum(g, recv, ici, acc, l):
    _, r, cc = g.shape
    rh = r // 2
    tb = _row_tile(rh, cc)
    nbh = rh // tb

    def body(g_ref, r_ref, i0_ref, i1_ref, i2_ref, acc_ref, o_ref):
        s = g_ref[...].astype(F32) + r_ref[...].astype(F32)
        o_ref[...] = s + i0_ref[...].astype(F32) + i1_ref[...].astype(F32) + i2_ref[...].astype(F32)

    def slot(j):
        return _bs((None, tb, cc), lambda i: (j, i, 0))

    return pl.pallas_call(
        body, grid=(nbh,),
        in_specs=[_bs((None, tb, cc), lambda i: (_my_chip(), lax.axis_index("c") * nbh + i, 0)),
                  _bs((None, tb, cc), lambda i: (_my_chip(), i, 0)),
                  slot(0), slot(1), slot(2), pl.BlockSpec(memory_space=pl.ANY)],
        out_specs=_bs((None, tb, cc), lambda i: (l, i, 0)),
        out_shape=S(acc.shape, F32), input_output_aliases={5: 0},
        compiler_params=_cp(), name="rs_owner_sum")(g, recv, ici, ici, ici, acc)


def _adam_math(w, g, m, v):
    m = B1 * m + (1.0 - B1) * g
    v = B2 * v + (1.0 - B2) * (g * g)
    m_hat = m / (1.0 - B1 ** STEP)
    v_hat = v / (1.0 - B2 ** STEP)
    delta = -LR * (m_hat / (jnp.sqrt(v_hat) + AEPS) + WD * w)
    return delta, m, v


def _adamw_big(w, g_mine, g_sib, m, v, lo, hi, prev):
    L, r, cc = w.shape
    rh = r // 2
    tb = _row_tile(rh, cc)
    nbh = rh // tb

    def body(w_ref, gm_ref, gs_ref, m_ref, v_ref, *rest):
        go_ref, d_ref, mo_ref, vo_ref = rest[-4:]
        mine = pl.program_id(1) == lax.axis_index("c")
        g = jnp.where(mine, gm_ref[...], gs_ref[...])
        d, m, v = _adam_math(w_ref[...], g, m_ref[...], v_ref[...])
        go_ref[...] = g
        d_ref[...] = d
        mo_ref[...] = m
        vo_ref[...] = v

    def mine_map(l, hf, i):
        c = lax.axis_index("c")
        return (l + lo, jnp.where(hf == c, i, jnp.where(c == 0, nbh - 1, 0)), 0)

    def sib_map(l, hf, i):
        c = lax.axis_index("c")
        return (l, jnp.where(hf != c, i, jnp.where(c == 0, 0, nbh - 1)), 0)

    full = _bs((None, tb, cc), lambda l, hf, i: (l + lo, hf * nbh + i, 0))
    extra = [] if prev is None else list(prev)
    return pl.pallas_call(
        body, grid=(hi - lo, 2, nbh),
        in_specs=[full, _bs((None, tb, cc), mine_map), _bs((None, tb, cc), sib_map), full, full]
        + [pl.BlockSpec(memory_space=pl.ANY)] * len(extra),
        out_specs=[full] * 4, out_shape=[S(w.shape, F32)] * 4,
        input_output_aliases={5 + k: k for k in range(len(extra))},
        compiler_params=_cp(), name="adamw_big")(w, g_mine, g_sib, m, v, *extra)


def _adamw_small(w, g, m, v):
    R = w.shape[0]
    tb = _tile(R, (512, 256, 128, 8))

    def body(w_ref, g_ref, m_ref, v_ref, d_ref, mo_ref, vo_ref):
        d, m, v = _adam_math(w_ref[...], g_ref[...], m_ref[...], v_ref[...])
        d_ref[...] = d
        mo_ref[...] = m
        vo_ref[...] = v

    spec = _bs((tb, 128), lambda i: (i, 0))
    return pl.pallas_call(body, grid=(R // tb,), in_specs=[spec] * 4, out_specs=[spec] * 3,
                          out_shape=[S((R, 128), F32)] * 3, compiler_params=_cp(), name="adamw_small")(w, g, m, v)


def _mix_pad(w):
    return jnp.concatenate([w[:, :4608], w[:, 4616:DIN], w[:, 4608:4616], jnp.zeros((D, PW - DIN), w.dtype)], axis=1)


def _mix_unpad(g):
    return jnp.concatenate([g[:, :4608], g[:, 5632:5640], g[:, 4608:5632]], axis=1)


def _pack(parts):
    flat = jnp.concatenate([p.reshape(-1).astype(F32) for p in parts])
    n = flat.shape[0]
    total = -(-n // (512 * 128)) * (512 * 128)
    return jnp.pad(flat, (0, total - n)).reshape(total // 128, 128)


def _unpack(pack, shapes):
    flat = pack.reshape(-1)
    out, off = [], 0
    for s in shapes:
        n = math.prod(s)
        out.append(flat[off:off + n].reshape(s))
        off += n
    return out


def _ffn_forward(x, gain, win, wout):
    h, rstd = _rms_fwd(x, gain)
    zg, zu, act = _ffn_in(h, win)
    y = _ffn_out(act, wout, x)
    return y, (x, h, rstd, zg, zu, act)


def _ffn_backward(dy, dyb, saved, gain, win, wout, dep):
    x, h, rstd, zg, zu, act = saved
    dz = _ffn_bwd_dz(dyb, wout, zg, zu, dep)
    dwout = _mm_tn("ffn_bwd_dwout", act, dyb, scale=0.5, tm=512, tn=1024)
    dwin = _ffn_bwd_dwin(h, dz)
    dh = _ffn_bwd_dh(dz, win)
    dx, dxb, dgain = _rms_bwd(dh, x, rstd, gain, dy)
    return dx, dxb, dgain, dwin, dwout


def _mixer_forward(x, p):
    T = x.shape[0]
    h, rstd = _rms_fwd(x, p["norm_mix"])
    tm = _tile(T, (1024, 512))
    tn = 1152
    P = _mm("mix_in", h, p["wmix"],
            _bs((tm, D), lambda i, j, k: (i, 0)), _bs((D, tn), lambda i, j, k: (0, j)),
            _bs((tm, tn), lambda i, j, k: (i, j)), S((T, PW), F32), (T // tm, PW // tn, 1), NN, 1, (tm, tn))
    yc = _conv_fwd(P, p["cw"])
    cum = _fgate_fwd(P, p["fb"])
    cumt = cum[:, :NH].T
    tq = _att_tile(T)
    cumq, cumk = cumt.reshape(NH, T, 1), cumt.reshape(NH, T // tq, 1, tq)
    qkv = P[:, C_Q * 128:C_GATE * 128].astype(BF16)
    ya, lse = _attn_fwd(qkv, cumq, cumk)
    yl, hst = _lru_fwd(P, p["lcw"], p["lvec"], p["lru_w_a"], p["lru_w_x"])
    yn, r0, r1, r2 = _gnorm_fwd(yc, ya, yl, p["mix_out_norm"])
    y = _mm("mix_out", yn, p["wo"],
            _bs((tm, D), lambda i, j, k: (i, 0)), _bs((D, 1024), lambda i, j, k: (0, j)),
            _bs((tm, 1024), lambda i, j, k: (i, j)), S((T, D), F32), (T // tm, D // 1024, 1), NN, 1, (tm, 1024),
            res=x, r_spec=_bs((tm, 1024), lambda i, j, k: (i, j)))
    return y, (x, h, rstd, P, qkv, cumq, cumk, lse, yc, ya, yl, hst, yn, r0, r1, r2)


def _mixer_backward(dy, dyb, saved, p, dep):
    x, h, rstd, P, qkv, cumq, cumk, lse, yc, ya, yl, hst, yn, r0, r1, r2 = saved
    T = x.shape[0]
    dyn = _mm_nt_full("mix_bwd_dyn", dyb, p["wo"], 512, dep=dep)
    dwo = _mm_tn("mix_bwd_dwo", yn, dyb, tm=512, tn=1024)
    dyc, dya, dyl, dgn = _gnorm_bwd(dyn, yc, ya, yl, r0, r1, r2, p["mix_out_norm"])
    dcb, dcc, dcv, dcw = _conv_bwd(P, p["cw"], dyc)
    dq, dk, dv, dck, dcq = _attn_bwd(qkv, cumq, cumk, lse, ya, dya)
    dcum = jnp.pad((dck.reshape(NH, T) + dcq.reshape(NH, T)).T, ((0, 0), (0, 128 - NH)))
    df, dfb = _fgate_bwd(P, p["fb"], dcum)
    dgate, dlx, lsm, dwa, dwx = _lru_bwd(P, p["lcw"], p["lvec"], p["lru_w_a"], p["lru_w_x"], hst, dyl)
    dP = jnp.concatenate([dcb, dcc, dcv, dq, dk, dv, dgate, dlx, df], axis=1).astype(BF16)
    tm = _tile(T, (512,))
    dh = _mm("mix_bwd_dh", dP, p["wmix"],
             _bs((tm, PW), lambda j, i, k: (i, 0)), _bs((1024, PW), lambda j, i, k: (j, 0)),
             _bs((tm, 1024), lambda j, i, k: (i, j)), S((T, D), F32), (D // 1024, T // tm, 1), NT, 1, (tm, 1024))
    dwmix = _mm_tn("mix_bwd_dwmix", h, dP, tm=512, tn=1152)
    dx, dxb, dgm = _rms_bwd(dh, x, rstd, p["norm_mix"], dy)
    small = dict(norm_mix=dgm[0], mix_out_norm=dgn[0], conv_w=dcw[:3], fgate_b=dfb[0, :NH], lru_conv_w=lsm[:4],
                 lru_conv_b=lsm[4], lru_b_a=lsm[5], lru_b_x=lsm[6], lru_lambda=lsm[7], lru_w_a=dwa, lru_w_x=dwx)
    return dx, dxb, small, dwmix, dwo


BIG =("ffn1_w_in", "ffn1_w_out", "mix_w_in", "mix_w_out", "ffn2_w_in", "ffn2_w_out")
SMALL = ("norm_ffn1", "norm_mix", "conv_w", "fgate_b", "lru_conv_w", "lru_conv_b", "lru_w_a", "lru_b_a", "lru_w_x",
         "lru_b_x", "lru_lambda", "mix_out_norm", "norm_ffn2", "final_norm")
WEIGHTS = ("norm_ffn1", "ffn1_w_in", "ffn1_w_out", "norm_mix", "mix_w_in", "conv_w", "fgate_b", "lru_conv_w", "lru_conv_b",
           "lru_w_a", "lru_b_a", "lru_w_x", "lru_b_x", "lru_lambda", "mix_out_norm", "mix_w_out", "norm_ffn2", "ffn2_w_in",
           "ffn2_w_out", "final_norm")


def _step(args):
    xx, yy, cc_ = lax.axis_index("x"), lax.axis_index("y"), lax.axis_index("c")
    me = 2 * xx + yy
    x0 = args["x"][0]
    tgt = args["loss_target"][0]
    T = x0.shape[0]
    L = args["norm_ffn1"].shape[0]

    def ag_sources(l):
        small = jnp.concatenate([args["conv_w"][l], args["lru_conv_w"][l], jnp.zeros((1, 128), F32)], axis=0)
        return [args[n][l].astype(BF16) for n in BIG] + [small]

    def layer_params(l, gat, gsm):
        w1i, w1o, wmx, wo, w2i, w2o = gat
        cwl = gsm.transpose(1, 0, 2).reshape(8, 4 * 128)
        return dict(
            w1i=w1i, w1o=w1o.reshape(-1, D), w2i=w2i, w2o=w2o.reshape(-1, D), wo=wo.reshape(D, D),
            wmix=_mix_pad(wmx.transpose(1, 0, 2).reshape(D, DIN)),
            cw=jnp.concatenate([cwl[:3], jnp.zeros((5, DC), F32)], axis=0),
            lcw=jnp.concatenate([cwl[3:7], jnp.zeros((4, DL), F32)], axis=0),
            fb=jnp.pad(args["fgate_b"][l], (0, 128 - NH)).reshape(1, 128),
            lvec=jnp.concatenate([args["lru_conv_b"][l][None], args["lru_b_a"][l][None], args["lru_b_x"][l][None],
                                  args["lru_lambda"][l][None], jnp.zeros((4, DL), F32)], axis=0),
            lru_w_a=args["lru_w_a"][l], lru_w_x=args["lru_w_x"][l],
            norm_ffn1=args["norm_ffn1"][l][None], norm_mix=args["norm_mix"][l][None],
            mix_out_norm=args["mix_out_norm"][l][None], norm_ffn2=args["norm_ffn2"][l][None])

    xs = x0
    saved, layers = [], []
    def ici_done(name, idx, flight, after):
        ssem, rsem, srcs, lands, _ = flight
        s, ld = _ag_wait("ag_wait_" + name, idx, ssem, rsem, [srcs[t] for t in idx], [lands[t] for t in idx], after)
        return _ag_fwd_start("ag_fwd_start_" + name, idx, s, ld, after)

    def gathered(name, idx, fwd, after):
        ssem, rsem, s, ld, _ = fwd
        return _ag_fwd_wait("ag_fwd_wait_" + name, idx, ssem, rsem, s, ld, after)

    every_w = tuple(range(N_AG))
    flight = _ag_start(0, ag_sources(0), x0)
    fwd = None
    for l in range(L):
        if l == 0:
            w1i, w1o = gathered("0a", AG_FIRST, ici_done("0a", AG_FIRST, flight, xs), xs)
        else:
            w1i, w1o, wmx, wo, w2i, w2o, gsm = gathered(str(l), every_w, fwd, xs)
        nxt = _ag_start(l + 1, ag_sources(l + 1), w1i) if l + 1 < L else None
        x1, s1 = _ffn_forward(xs, args["norm_ffn1"][l][None], w1i, w1o.reshape(-1, D))
        if l == 0:
            gsm, wmx, wo, w2i, w2o = gathered("0b", AG_REST, ici_done("0b", AG_REST, flight, x1), x1)
        p = layer_params(l, (w1i, w1o, wmx, wo, w2i, w2o), gsm)
        x2, s2 = _mixer_forward(x1, p)
        h2, rstd2 = _rms_fwd(x2, p["norm_ffn2"])
        zg2, zu2, act2 = _ffn_in(h2, p["w2i"])
        fwd = ici_done(str(l + 1), every_w, nxt, act2) if nxt is not None else None
        x3 = _ffn_out(act2, p["w2o"], x2, dep=None if fwd is None else fwd[4])
        s3 = (x2, h2, rstd2, zg2, zu2, act2)
        saved.append((s1, s2, s3))
        layers.append(p)
        xs = x3
    lpart, dx, dxb, dfinal = _loss_head(xs, args["final_norm"][None], tgt)
    loss = lax.psum(lpart[0, 0], ("x", "y", "c"))

    acc = [None] * N_BIG
    small_grads = [None] * L
    every = tuple(range(N_BIG))
    in_air = []
    pair_flight = None
    dep = lpart

    def to_blocks(t, g):
        if t in (1, 5):
            return g.reshape(4, g.shape[0] // 4, D)
        if t == 2:
            return _mix_unpad(g).reshape(D, 4, DIN // 4).transpose(1, 0, 2)
        if t == 3:
            return g.reshape(4, D // 4, D)
        return g

    def launch(name, l, idx, grads, recv, after):
        sums = [_pair_sum(g, r) for g, r in zip(grads, recv)]
        started = _rs_start("rs_start_" + name, sums, after)
        in_air.append(("rs_wait_" + name, l, idx, grads, recv, started))
        return started[4]

    def land(entry, after):
        name, l, idx, grads, recv, (ssem, rsem, sums, lands, _) = entry
        ici = _rs_wait(name, ssem, rsem, sums, lands, after)
        for t, g, r, i3 in zip(idx, grads, recv, ici):
            if acc[t] is None:
                acc[t] = jnp.zeros((L, g.shape[1] // 2, g.shape[2]), F32)
            acc[t] = _owner_sum(g, r, i3, acc[t], l)

    for l in reversed(range(L)):
        p = layers[l]
        s1, s2, s3 = saved[l]
        dx, dxb, dg2, dw2i, dw2o = _ffn_backward(dx, dxb, s3, p["norm_ffn2"], p["w2i"], p["w2o"], dep)
        if pair_flight is not None:
            lp, (ssem, rsem, g_thru, lands, _) = pair_flight
            g_thru, recv = _rs_pair_wait(f"rs_pair_wait_{lp}", ssem, rsem, g_thru, lands, dx)
            dep = launch(str(lp), lp, every, g_thru, recv, dx)
            pair_flight = None
        if l == 0:
            part = [to_blocks(4, dw2i), to_blocks(5, dw2o)]
            dep = launch("0c", 0, (4, 5), part, _rs_pair(part), dx)
        dx, dxb, sm, dwmix, dwo = _mixer_backward(dx, dxb, s2, p, dep)
        if l == 0:
            part = [to_blocks(2, dwmix), to_blocks(3, dwo)]
            dep = launch("0b", 0, (2, 3), part, _rs_pair(part), dx)
        dx, dxb, dg1, dw1i, dw1o = _ffn_backward(dx, dxb, s1, p["norm_ffn1"], p["w1i"], p["w1o"], dep)
        sm["norm_ffn1"] = dg1[0]
        sm["norm_ffn2"] = dg2[0]
        small_grads[l] = sm
        for entry in [e for e in in_air if e[1] > l]:
            land(entry, dx)
            in_air.remove(entry)
        if l > 0:
            grads = [to_blocks(t, g) for t, g in enumerate((dw1i, dw1o, dwmix, dwo, dw2i, dw2o))]
            pair_flight = (l, _rs_pair_start(f"rs_pair_start_{l}", grads, dx))
            dep = pair_flight[1][4]

    full = {n: (dfinal[0] if n == "final_norm" else jnp.stack([small_grads[l][n] for l in range(L)])) for n in SMALL}
    red_pack = _all_reduce_small(_pack([full[n] for n in SMALL]))
    part = [to_blocks(0, dw1i), to_blocks(1, dw1o)]
    dep = launch("0a", 0, (0, 1), part, _rs_pair(part), red_pack)

    def adamw(k, lo, hi, sib, prev):
        n = BIG[k]
        return _adamw_big(args[n], acc[k], sib[k], args["m_" + n], args["v_" + n], lo, hi, prev)

    res = [None] * N_BIG
    after = dx
    if L > 1:
        sib = _rs_join(acc, 1, L, dep)
        for k in range(N_BIG):
            res[k] = adamw(k, 1, L, sib, None)
        after = res[N_BIG - 1][0]
    for entry in list(in_air):
        land(entry, after)
    sib = _rs_join(acc, 0, 1, dx)
    out = {"loss": loss, "grad_x": dx[None]}
    for k, n in enumerate(BIG):
        outs = adamw(k, 0, 1, sib, res[k])
        out["grad_" + n], out["delta_" + n], out["new_m_" + n], out["new_v_" + n] = outs

    shapes = [full[n].shape for n in SMALL]
    red = dict(zip(SMALL, _unpack(red_pack, shapes)))
    for n in ("conv_w", "lru_conv_w"):
        red[n] = lax.dynamic_slice_in_dim(red[n], me * 128, 128, axis=2)
    oshapes = [args[n].shape for n in SMALL]
    d, m, v = _adamw_small(_pack([args[n] for n in SMALL]), _pack([red[n] for n in SMALL]),
                           _pack([args["m_" + n] for n in SMALL]), _pack([args["v_" + n] for n in SMALL]))
    for n, gg, dd, mm, vv in zip(SMALL, [red[n] for n in SMALL], _unpack(d, oshapes), _unpack(m, oshapes), _unpack(v, oshapes)):
        out["grad_" + n], out["delta_" + n], out["new_m_" + n], out["new_v_" + n] = gg, dd, mm, vv
    return out


def kernel(x, norm_ffn1, ffn1_w_in, ffn1_w_out, norm_mix, mix_w_in, conv_w, fgate_b, lru_conv_w, lru_conv_b, lru_w_a, lru_b_a, lru_w_x, lru_b_x, lru_lambda, mix_out_norm, mix_w_out, norm_ffn2, ffn2_w_in, ffn2_w_out, final_norm, loss_target, m_norm_ffn1, m_ffn1_w_in, m_ffn1_w_out, m_norm_mix, m_mix_w_in, m_conv_w, m_fgate_b, m_lru_conv_w, m_lru_conv_b, m_lru_w_a, m_lru_b_a, m_lru_w_x, m_lru_b_x, m_lru_lambda, m_mix_out_norm, m_mix_w_out, m_norm_ffn2, m_ffn2_w_in, m_ffn2_w_out, m_final_norm, v_norm_ffn1, v_ffn1_w_in, v_ffn1_w_out, v_norm_mix, v_mix_w_in, v_conv_w, v_fgate_b, v_lru_conv_w, v_lru_conv_b, v_lru_w_a, v_lru_b_a, v_lru_w_x, v_lru_b_x, v_lru_lambda, v_mix_out_norm, v_mix_w_out, v_norm_ffn2, v_ffn2_w_in, v_ffn2_w_out, v_final_norm):
    args = dict(locals())
    out = _step(args)
    res = [out["loss"], out["grad_x"]]
    for prefix in ("grad_", "delta_", "new_m_", "new_v_"):
        res += [out[prefix + n] for n in WEIGHTS]
    return tuple(res)
```

```python
import functools
import math

import jax
import jax.numpy as jnp
from jax import lax
from jax.experimental import pallas as pl
from jax.experimental.pallas import tpu as pltpu

F32 = jnp.float32
BF16 = jnp.bfloat16
S = jax.ShapeDtypeStruct
MESH = pl.DeviceIdType.MESH

D = 2048
DC = 512
DA = 1024
NH = 8
HD = 128
DL = 512
LB = 128
DIN = 5640
PW = 5760
C_Q, C_K, C_V = 12, 20, 28
C_GATE, C_LX, C_F = 36, 40, 44
EPS = 1e-6
LRU_C = 8.0
ATT_SCALE = HD ** -0.5
LR, B1, B2, AEPS, WD, STEP = 0.001, 0.9, 0.999, 1e-08, 0.01, 10
VMEM_LIMIT = 56 * 1024 * 1024

NT = (((1,), (1,)), ((), ()))
TN = (((0,), (0,)), ((), ()))
NN = (((1,), (0,)), ((), ()))


def _cp():
    return pltpu.CompilerParams(vmem_limit_bytes=VMEM_LIMIT)


def _bs(shape, fn):
    return pl.BlockSpec(shape, fn)


def _mm(name, a, b, a_spec, b_spec, o_spec, o_shape, grid, dims, nk, acc_tile, scale=1.0, res=None, r_spec=None, dep=None):
    has_res = res is not None
    has_dep = dep is not None

    def body(*refs):
        if has_dep:
            refs = refs[:2 + has_res] + refs[3 + has_res:]
        if has_res:
            a_ref, b_ref, r_ref, o_ref = refs[:4]
            rest = refs[4:]
        else:
            a_ref, b_ref, o_ref = refs[:3]
            rest = refs[3:]
        prod = lax.dot_general(a_ref[...].astype(BF16), b_ref[...].astype(BF16), dims, preferred_element_type=F32)

        def finish(acc):
            if scale != 1.0:
                acc = acc * scale
            if has_res:
                acc = r_ref[...] + acc
            o_ref[...] = acc.astype(o_ref.dtype)

        if nk == 1:
            finish(prod)
        else:
            acc_ref = rest[0]
            k = pl.program_id(2)

            @pl.when(k == 0)
            def _():
                acc_ref[...] = prod

            @pl.when(k > 0)
            def _():
                acc_ref[...] += prod

            @pl.when(k == nk - 1)
            def _():
                finish(acc_ref[...])

    in_specs = [a_spec, b_spec] + ([r_spec] if has_res else []) + ([pl.BlockSpec(memory_space=pl.ANY)] if has_dep else [])
    args = (a, b) + ((res,) if has_res else ()) + ((dep,) if has_dep else ())
    scratch = [pltpu.VMEM(acc_tile, F32)] if nk > 1 else []
    return pl.pallas_call(body, grid=grid, in_specs=in_specs, out_specs=o_spec, out_shape=o_shape,
                          scratch_shapes=scratch, compiler_params=_cp(), name=name)(*args)


def _tile(n, pref):
    for t in pref:
        if n % t == 0:
            return t
    return n


def _rms_fwd(x, gain):
    T = x.shape[0]
    tb = _tile(T, (512,))

    def body(x_ref, g_ref, h_ref, r_ref):
        xv = x_ref[...]
        r = lax.rsqrt(jnp.mean(xv * xv, axis=1, keepdims=True) + EPS)
        h_ref[...] = (xv * r * g_ref[...]).astype(BF16)
        r_ref[...] = r

    return pl.pallas_call(
        body, grid=(T // tb,),
        in_specs=[_bs((tb, D), lambda i: (i, 0)), _bs((1, D), lambda i: (0, 0))],
        out_specs=[_bs((tb, D), lambda i: (i, 0)), _bs((tb, 1), lambda i: (i, 0))],
        out_shape=[S((T, D), BF16), S((T, 1), F32)], compiler_params=_cp(), name="rms_fwd")(x, gain)


def _rms_bwd(dh, x, rstd, gain, dres):
    T = x.shape[0]
    tb = _tile(T, (512,))

    def body(dh_ref, x_ref, r_ref, g_ref, dres_ref, dx_ref, dxb_ref, dg_ref):
        i = pl.program_id(0)
        r = r_ref[...]
        xhat = x_ref[...] * r
        dh = dh_ref[...]
        dxh = dh * g_ref[...]
        m = jnp.mean(dxh * xhat, axis=1, keepdims=True)
        dx = dres_ref[...] + r * (dxh - xhat * m)
        dx_ref[...] = dx
        dxb_ref[...] = dx.astype(BF16)
        part = jnp.sum(dh * xhat, axis=0, keepdims=True)

        @pl.when(i == 0)
        def _():
            dg_ref[...] = part

        @pl.when(i > 0)
        def _():
            dg_ref[...] += part

    row = _bs((tb, D), lambda i: (i, 0))
    return pl.pallas_call(
        body, grid=(T // tb,),
        in_specs=[row, row, _bs((tb, 1), lambda i: (i, 0)), _bs((1, D), lambda i: (0, 0)), row],
        out_specs=[row, row, _bs((1, D), lambda i: (0, 0))],
        out_shape=[S((T, D), F32), S((T, D), BF16), S((1, D), F32)], compiler_params=_cp(), name="rms_bwd")(dh, x, rstd, gain, dres)


def _loss_head(x, gain, tgt):
    T = x.shape[0]
    tb = _tile(T, (512,))

    def body(x_ref, g_ref, t_ref, l_ref, dx_ref, dxb_ref, dg_ref):
        i = pl.program_id(0)
        xv = x_ref[...]
        g = g_ref[...]
        r = lax.rsqrt(jnp.mean(xv * xv, axis=1, keepdims=True) + EPS)
        xhat = xv * r
        e = xhat * g - t_ref[...]
        lpart = 0.5 * jnp.sum(jnp.sum(e * e, axis=1, keepdims=True), axis=0, keepdims=True) * (1.0 / D)
        dy = e * (1.0 / D)
        dxh = dy * g
        m = jnp.mean(dxh * xhat, axis=1, keepdims=True)
        dx = r * (dxh - xhat * m)
        dx_ref[...] = dx
        dxb_ref[...] = dx.astype(BF16)
        gpart = jnp.sum(dy * xhat, axis=0, keepdims=True)
        lrow = jnp.broadcast_to(lpart, (1, 128))

        @pl.when(i == 0)
        def _():
            dg_ref[...] = gpart
            l_ref[...] = lrow

        @pl.when(i > 0)
        def _():
            dg_ref[...] += gpart
            l_ref[...] += lrow

    row = _bs((tb, D), lambda i: (i, 0))
    return pl.pallas_call(
        body, grid=(T // tb,),
        in_specs=[row, _bs((1, D), lambda i: (0, 0)), row],
        out_specs=[_bs((1, 128), lambda i: (0, 0)), row, row, _bs((1, D), lambda i: (0, 0))],
        out_shape=[S((1, 128), F32), S((T, D), F32), S((T, D), BF16), S((1, D), F32)],
        compiler_params=_cp(), name="loss_head")(x, gain, tgt)


def _sigmoid(z):
    return 0.5 * jnp.tanh(0.5 * z) + 0.5


def _ffn_in(h, win):
    T = h.shape[0]
    Fs = win.shape[2]
    F = 2 * Fs
    tn = _tile(Fs, (256, 128))
    nb = Fs // tn
    tm = _tile(T, (1024, 512))

    def body(h_ref, wg_ref, wu_ref, zg_ref, zu_ref, a_ref):
        hv = h_ref[...]
        zg = jnp.dot(hv, wg_ref[...], preferred_element_type=F32)
        zu = jnp.dot(hv, wu_ref[...], preferred_element_type=F32)
        zg_ref[...] = zg.astype(BF16)
        zu_ref[...] = zu.astype(BF16)
        a_ref[...] = (zg * _sigmoid(zg) * zu).astype(BF16)

    col = _bs((tm, tn), lambda i, j: (i, j))
    return pl.pallas_call(
        body, grid=(T // tm, F // tn),
        in_specs=[_bs((tm, D), lambda i, j: (i, 0)),
                  _bs((None, D, tn), lambda i, j: (j // nb, 0, j % nb)),
                  _bs((None, D, tn), lambda i, j: (2 + j // nb, 0, j % nb))],
        out_specs=[col, col, col],
        out_shape=[S((T, F), BF16)] * 3, compiler_params=_cp(), name="ffn_in")(h, win, win)


def _ffn_out(act, wout, x, dep=None):
    T, F = act.shape
    tm = _tile(T, (512,))
    tn = 1024
    return _mm("ffn_out", act, wout,
               _bs((tm, F), lambda j, i, k: (i, 0)), _bs((F, tn), lambda j, i, k: (0, j)),
               _bs((tm, tn), lambda j, i, k: (i, j)), S((T, D), F32), (D // tn, T // tm, 1), NN, 1, (tm, tn),
               scale=0.5, res=x, r_spec=_bs((tm, tn), lambda j, i, k: (i, j)), dep=dep)


def _ffn_bwd_dz(dyb, wout, zg, zu, dep):
    T, F = zg.shape
    tm = _tile(T, (1024, 512))
    tn = _tile(F, (512, 256))

    def body(dy_ref, w_ref, zg_ref, zu_ref, dep_ref, dz_ref):
        da = 0.5 * lax.dot_general(dy_ref[...], w_ref[...], NT, preferred_element_type=F32)
        zg = zg_ref[...].astype(F32)
        zu = zu_ref[...].astype(F32)
        s = _sigmoid(zg)
        dz_ref[0] = (da * zu * (s * (1.0 + zg * (1.0 - s)))).astype(BF16)
        dz_ref[1] = (da * (zg * s)).astype(BF16)

    col = _bs((tm, tn), lambda i, j: (i, j))
    return pl.pallas_call(
        body, grid=(T // tm, F // tn),
        in_specs=[_bs((tm, D), lambda i, j: (i, 0)), _bs((tn, D), lambda i, j: (j, 0)), col, col,
                  pl.BlockSpec(memory_space=pl.ANY)],
        out_specs=_bs((2, tm, tn), lambda i, j: (0, i, j)), out_shape=S((2, T, F), BF16),
        compiler_params=_cp(), name="ffn_bwd_dz")(dyb, wout, zg, zu, dep)


def _ffn_bwd_dh(dz, win):
    _, T, F = dz.shape
    Fs = win.shape[2]
    tk = _tile(Fs, (1408, 256, 128))
    nkb = Fs // tk
    tm = _tile(T, (1024, 512))
    tn = 1024
    nk = 2 * nkb

    def body(dzg_ref, dzu_ref, wg_ref, wu_ref, o_ref, acc_ref):
        k = pl.program_id(2)
        prod = (lax.dot_general(dzg_ref[...], wg_ref[...], NT, preferred_element_type=F32)
                + lax.dot_general(dzu_ref[...], wu_ref[...], NT, preferred_element_type=F32))

        @pl.when(k == 0)
        def _():
            acc_ref[...] = prod

        @pl.when(k > 0)
        def _():
            acc_ref[...] += prod

        @pl.when(k == nk - 1)
        def _():
            o_ref[...] = acc_ref[...]

    def a_spec(half):
        return _bs((None, tm, tk), lambda i, j, k: (half, i, k))

    return pl.pallas_call(
        body, grid=(T // tm, D // tn, nk),
        in_specs=[a_spec(0), a_spec(1),
                  _bs((None, tn, tk), lambda i, j, k: (k // nkb, j, k % nkb)),
                  _bs((None, tn, tk), lambda i, j, k: (2 + k // nkb, j, k % nkb))],
        out_specs=_bs((tm, tn), lambda i, j, k: (i, j)), out_shape=S((T, D), F32),
        scratch_shapes=[pltpu.VMEM((tm, tn), F32)], compiler_params=_cp(), name="ffn_bwd_dh")(dz, dz, win, win)


def _ffn_bwd_dwin(h, dz):
    _, T, F = dz.shape
    Fs = F // 2
    tn = _tile(Fs, (1408, 256, 128))
    nb = Fs // tn
    tm = 512
    return _mm("ffn_bwd_dwin", h, dz,
               _bs((T, tm), lambda i, j, k: (0, i)), _bs((None, T, tn), lambda i, j, k: (j // (2 * nb), 0, j % (2 * nb))),
               _bs((None, tm, tn), lambda i, j, k: (j // nb, i, j % nb)), S((4, D, Fs), BF16),
               (D // tm, 4 * nb, 1), TN, 1, (tm, tn))


def _mm_tn(name, a, b, scale=1.0, tm=512, tn=1024):
    T, M = a.shape
    N = b.shape[1]
    tm = _tile(M, (tm, 512, 256, 128))
    tn = _tile(N, (tn, 1152, 1024, 512, 128))
    return _mm(name, a, b,
               _bs((T, tm), lambda i, j, k: (0, i)), _bs((T, tn), lambda i, j, k: (0, j)),
               _bs((tm, tn), lambda i, j, k: (i, j)), S((M, N), BF16), (M // tm, N // tn, 1), TN, 1, (tm, tn), scale=scale)


def _mm_nt_full(name, a, b, tn, dep=None):
    T, K = a.shape
    N = b.shape[0]
    tm = _tile(T, (1024, 512))
    return _mm(name, a, b,
               _bs((tm, K), lambda i, j, k: (i, 0)), _bs((tn, K), lambda i, j, k: (j, 0)),
               _bs((tm, tn), lambda i, j, k: (i, j)), S((T, N), F32), (T // tm, N // tn, 1), NT, 1, (tm, tn), dep=dep)


def _bt(T):
    return _tile(T, (512,))


def _down(ext, s):
    return pltpu.roll(ext, s, 0)[8:, :]


def _up(ext, s):
    n = ext.shape[0]
    return pltpu.roll(ext, n - s, 0)[: n - 8, :]


def _halo_prev(ref, start, b):
    lo = pl.multiple_of(jnp.maximum(start - 8, 0), 8)
    return ref[pl.ds(lo, 8), :] * (b > 0).astype(F32)


def _halo_next(ref, start, bt, b, nb):
    lo = pl.multiple_of(jnp.minimum(start + bt, (nb - 1) * bt), 8)
    return ref[pl.ds(lo, 8), :] * (b < nb - 1).astype(F32)


def _scan_fwd(A, U):
    n = U.shape[0]
    row = lax.broadcasted_iota(jnp.int32, U.shape, 0)
    d = 1
    while d < n:
        keep = row >= d
        Us = jnp.where(keep, pltpu.roll(U, d, 0), 0.0)
        if A is None:
            U = U + Us
        else:
            As = jnp.where(keep, pltpu.roll(A, d, 0), 1.0)
            U = A * Us + U
            A = A * As
        d *= 2
    return A, U


def _scan_bwd(A, U):
    n = U.shape[0]
    row = lax.broadcasted_iota(jnp.int32, U.shape, 0)
    d = 1
    while d < n:
        keep = row < n - d
        Us = jnp.where(keep, pltpu.roll(U, n - d, 0), 0.0)
        if A is None:
            U = U + Us
        else:
            As = jnp.where(keep, pltpu.roll(A, n - d, 0), 1.0)
            U = A * Us + U
            A = A * As
        d *= 2
    return A, U


def _softplus(z):
    return jnp.maximum(z, 0.0) + jnp.log(1.0 + jnp.exp(-jnp.abs(z)))


def _gelu_parts(g):
    k0 = math.sqrt(2.0 / math.pi)
    t = jnp.tanh(k0 * (g + 0.044715 * g * g * g))
    gel = 0.5 * g * (1.0 + t)
    dgel = 0.5 * (1.0 + t) + 0.5 * g * (1.0 - t * t) * k0 * (1.0 + 3.0 * 0.044715 * g * g)
    return gel, dgel


def _conv_fwd(P, cw):
    T = P.shape[0]
    bt = _bt(T)
    nb = T // bt

    def body(b_ref, c_ref, v_ref, w_ref, y_ref):
        w = w_ref[...]

        def step(b, carry):
            start = pl.multiple_of(b * bt, bt)
            rows = pl.ds(start, bt)
            m = c_ref[rows, :] * v_ref[rows, :]
            ext = jnp.concatenate([_halo_prev(c_ref, start, b) * _halo_prev(v_ref, start, b), m], axis=0)
            z = w[2:3, :] * m + w[1:2, :] * _down(ext, 1) + w[0:1, :] * _down(ext, 2)
            y_ref[rows, :] = b_ref[rows, :] * z
            return carry

        lax.fori_loop(0, nb, step, 0)

    def colspec(off):
        return _bs((T, 128), lambda c: (0, off + c))

    return pl.pallas_call(
        body, grid=(DC // 128,),
        in_specs=[colspec(0), colspec(4), colspec(8), _bs((8, 128), lambda c: (0, c))],
        out_specs=_bs((T, 128), lambda c: (0, c)), out_shape=S((T, DC), F32),
        compiler_params=_cp(), name="conv_fwd")(P, P, P, cw)


def _conv_bwd(P, cw, dy):
    T = P.shape[0]
    bt = _bt(T)
    nb = T // bt

    def body(b_ref, c_ref, v_ref, w_ref, dy_ref, db_ref, dc_ref, dv_ref, dw_ref):
        w = w_ref[...]

        def step(b, carry):
            a0, a1, a2 = carry
            start = pl.multiple_of(b * bt, bt)
            rows = pl.ds(start, bt)
            cb, cc, cv, dy = b_ref[rows, :], c_ref[rows, :], v_ref[rows, :], dy_ref[rows, :]
            m = cc * cv
            ext = jnp.concatenate([_halo_prev(c_ref, start, b) * _halo_prev(v_ref, start, b), m], axis=0)
            m1, m2 = _down(ext, 1), _down(ext, 2)
            z = w[2:3, :] * m + w[1:2, :] * m1 + w[0:1, :] * m2
            db_ref[rows, :] = dy * z
            dz = dy * cb
            extn = jnp.concatenate([dz, _halo_next(dy_ref, start, bt, b, nb) * _halo_next(b_ref, start, bt, b, nb)], axis=0)
            dm = w[2:3, :] * dz + w[1:2, :] * _up(extn, 1) + w[0:1, :] * _up(extn, 2)
            dc_ref[rows, :] = dm * cv
            dv_ref[rows, :] = dm * cc
            return (a0 + jnp.sum(dz * m2, axis=0, keepdims=True),
                    a1 + jnp.sum(dz * m1, axis=0, keepdims=True),
                    a2 + jnp.sum(dz * m, axis=0, keepdims=True))

        zero = jnp.zeros((1, 128), F32)
        a0, a1, a2 = lax.fori_loop(0, nb, step, (zero, zero, zero))
        dw_ref[...] = jnp.zeros((8, 128), F32)
        dw_ref[0:1, :] = a0
        dw_ref[1:2, :] = a1
        dw_ref[2:3, :] = a2

    def colspec(off):
        return _bs((T, 128), lambda c: (0, off + c))

    own = _bs((T, 128), lambda c: (0, c))
    return pl.pallas_call(
        body, grid=(DC // 128,),
        in_specs=[colspec(0), colspec(4), colspec(8), _bs((8, 128), lambda c: (0, c)), own],
        out_specs=[own, own, own, _bs((8, 128), lambda c: (0, c))],
        out_shape=[S((T, DC), F32)] * 3 + [S((8, DC), F32)], compiler_params=_cp(), name="conv_bwd")(P, P, P, cw, dy)


def _fgate_fwd(P, fb):
    T = P.shape[0]
    bt = _bt(T)
    nb = T // bt

    def body(f_ref, b_ref, c_ref):
        bias = b_ref[...]

        def step(b, carry):
            rows = pl.ds(pl.multiple_of(b * bt, bt), bt)
            logf = -_softplus(-(f_ref[rows, :] + bias))
            _, cs = _scan_fwd(None, logf)
            cs = cs + carry
            c_ref[rows, :] = cs
            return cs[bt - 1:bt, :]

        lax.fori_loop(0, nb, step, jnp.zeros((1, 128), F32))

    return pl.pallas_call(
        body, grid=(1,),
        in_specs=[_bs((T, 128), lambda i: (0, C_F)), _bs((1, 128), lambda i: (0, 0))],
        out_specs=_bs((T, 128), lambda i: (0, 0)), out_shape=S((T, 128), F32),
        compiler_params=_cp(), name="fgate_fwd")(P, fb)


def _fgate_bwd(P, fb, dcum):
    T = P.shape[0]
    bt = _bt(T)
    nb = T // bt

    def body(f_ref, b_ref, dc_ref, df_ref, db_ref):
        bias = b_ref[...]

        def step(i, carry):
            run, acc = carry
            b = nb - 1 - i
            rows = pl.ds(pl.multiple_of(b * bt, bt), bt)
            _, rs = _scan_bwd(None, dc_ref[rows, :])
            rs = rs + run
            df = rs * jax.nn.sigmoid(-(f_ref[rows, :] + bias))
            df_ref[rows, :] = df
            return rs[0:1, :], acc + jnp.sum(df, axis=0, keepdims=True)

        zero = jnp.zeros((1, 128), F32)
        _, acc = lax.fori_loop(0, nb, step, (zero, zero))
        db_ref[...] = acc

    return pl.pallas_call(
        body, grid=(1,),
        in_specs=[_bs((T, 128), lambda i: (0, C_F)), _bs((1, 128), lambda i: (0, 0)), _bs((T, 128), lambda i: (0, 0))],
        out_specs=[_bs((T, 128), lambda i: (0, 0)), _bs((1, 128), lambda i: (0, 0))],
        out_shape=[S((T, 128), F32), S((1, 128), F32)], compiler_params=_cp(), name="fgate_bwd")(P, fb, dcum)


def _att_tile(T):
    return _tile(T, (512,))


def _causal_mask(tq):
    return lax.broadcasted_iota(jnp.int32, (tq, tq), 1) <= lax.broadcasted_iota(jnp.int32, (tq, tq), 0)


def _attn_fwd(qkv, cumq, cumk):
    T = qkv.shape[0]
    tq = _att_tile(T)
    nq = T // tq

    def body(q_ref, k_ref, v_ref, cq_ref, ck_ref, o_ref, lse_ref):
        i = pl.program_id(1)
        q = q_ref[...]
        cq = cq_ref[...]

        def block(j, carry, diagonal):
            m_old, l_old, acc = carry
            rows = pl.ds(pl.multiple_of(j * tq, tq), tq)
            s = lax.dot_general(q, k_ref[rows, :], NT, preferred_element_type=F32)
            s = s * ATT_SCALE + cq - ck_ref[j]
            if diagonal:
                s = jnp.where(_causal_mask(tq), s, -jnp.inf)
            m_new = jnp.maximum(m_old, jnp.max(s, axis=1, keepdims=True))
            p = jnp.exp(s - m_new)
            alpha = jnp.exp(m_old - m_new)
            l_new = alpha * l_old + jnp.sum(p, axis=1, keepdims=True)
            acc = alpha * acc + jnp.dot(p.astype(BF16), v_ref[rows, :], preferred_element_type=F32)
            return m_new, l_new, acc

        init = (jnp.full((tq, 1), -jnp.inf, F32), jnp.zeros((tq, 1), F32), jnp.zeros((tq, HD), F32))
        carry = lax.fori_loop(0, i, lambda j, c: block(j, c, False), init)
        m, l, acc = block(i, carry, True)
        o_ref[...] = acc / l
        lse_ref[...] = m + jnp.log(l)

    return pl.pallas_call(
        body, grid=(NH, nq),
        in_specs=[_bs((tq, HD), lambda h, i: (i, h)),
                  _bs((T, HD), lambda h, i: (0, NH + h)),
                  _bs((T, HD), lambda h, i: (0, 2 * NH + h)),
                  _bs((None, tq, 1), lambda h, i: (h, i, 0)),
                  _bs((None, nq, 1, tq), lambda h, i: (h, 0, 0, 0))],
        out_specs=[_bs((tq, HD), lambda h, i: (i, h)), _bs((None, tq, 1), lambda h, i: (h, i, 0))],
        out_shape=[S((T, DA), F32), S((NH, T, 1), F32)],
        compiler_params=_cp(), name="attn_fwd")(qkv, qkv, qkv, cumq, cumk)


def _attn_bwd(qkv, cumq, cumk, lse, o, do):
    T = qkv.shape[0]
    tq = _att_tile(T)
    nq = T // tq

    def body(q_ref, k_ref, v_ref, cq_ref, ck_ref, lse_ref, o_ref, do_ref, dq_ref, dk_ref, dv_ref, dc_ref, dr_ref):
        j = pl.program_id(1)

        @pl.when(j == 0)
        def _():
            dq_ref[...] = jnp.zeros((T, HD), F32)
            dr_ref[...] = jnp.zeros((T, 1), F32)

        k = k_ref[...]
        v = v_ref[...]
        ck = ck_ref[...]

        def block(i, carry, diagonal):
            dk_acc, dv_acc, dc_acc = carry
            rows = pl.ds(pl.multiple_of(i * tq, tq), tq)
            q = q_ref[rows, :]
            do_f = do_ref[rows, :]
            dob = do_f.astype(BF16)
            s = lax.dot_general(q, k, NT, preferred_element_type=F32)
            p = jnp.exp(s * ATT_SCALE + cq_ref[rows, :] - ck - lse_ref[rows, :])
            if diagonal:
                p = jnp.where(_causal_mask(tq), p, 0.0)
            delta = jnp.sum(do_f * o_ref[rows, :], axis=1, keepdims=True)
            dp = lax.dot_general(dob, v, NT, preferred_element_type=F32)
            ds = p * (dp - delta)
            dsb = (ds * ATT_SCALE).astype(BF16)
            dq_ref[rows, :] += jnp.dot(dsb, k, preferred_element_type=F32)
            dr_ref[rows, :] += jnp.sum(ds, axis=1, keepdims=True)
            return (dk_acc + lax.dot_general(dsb, q, TN, preferred_element_type=F32),
                    dv_acc + lax.dot_general(p.astype(BF16), dob, TN, preferred_element_type=F32),
                    dc_acc - jnp.sum(ds, axis=0, keepdims=True))

        init = (jnp.zeros((tq, HD), F32), jnp.zeros((tq, HD), F32), jnp.zeros((1, tq), F32))
        carry = block(j, init, True)
        dk_acc, dv_acc, dc_acc = lax.fori_loop(j + 1, nq, lambda i, c: block(i, c, False), carry)
        dk_ref[...] = dk_acc
        dv_ref[...] = dv_acc
        dc_ref[...] = dc_acc

    def whole(col):
        return _bs((T, HD), lambda h, j: (0, col + h))

    qvec = _bs((None, T, 1), lambda h, j: (h, 0, 0))
    kv_out = _bs((tq, HD), lambda h, j: (j, h))
    return pl.pallas_call(
        body, grid=(NH, nq),
        in_specs=[whole(0), _bs((tq, HD), lambda h, j: (j, NH + h)), _bs((tq, HD), lambda h, j: (j, 2 * NH + h)),
                  qvec, _bs((None, None, 1, tq), lambda h, j: (h, j, 0, 0)), qvec, whole(0), whole(0)],
        out_specs=[whole(0), kv_out, kv_out, _bs((None, 1, tq), lambda h, j: (h, 0, j)), qvec],
        out_shape=[S((T, DA), F32)] * 3 + [S((NH, 1, T), F32), S((NH, T, 1), F32)],
        compiler_params=_cp(), name="attn_bwd")(qkv, qkv, qkv, cumq, cumk, lse, o, do)


def _lru_gates(xr, wa, wx, ba, bx, sp):
    xb = xr.astype(BF16)
    r = jax.nn.sigmoid(jnp.dot(xb, wa, preferred_element_type=F32) + ba)
    ig = jax.nn.sigmoid(jnp.dot(xb, wx, preferred_element_type=F32) + bx)
    log_a = -LRU_C * r * sp
    a = jnp.exp(log_a)
    th = jnp.tanh(log_a)
    om = -2.0 * th / (1.0 - th)
    mult = jnp.sqrt(om)
    return xb, r, ig, a, om, mult


def _lru_xr(lx_ref, cw, cb, start, b, rows):
    lx = lx_ref[rows, :]
    ext = jnp.concatenate([_halo_prev(lx_ref, start, b), lx], axis=0)
    return cw[3:4, :] * lx + cw[2:3, :] * _down(ext, 1) + cw[1:2, :] * _down(ext, 2) + cw[0:1, :] * _down(ext, 3) + cb


def _lru_fwd(P, lcw, vec, wa, wx):
    T = P.shape[0]
    bt = _bt(T)
    nb = T // bt

    def body(g_ref, lx_ref, cw_ref, vec_ref, wa_ref, wx_ref, y_ref, h_ref):
        cw = cw_ref[...]
        vec = vec_ref[...]
        wa = wa_ref[...].astype(BF16)
        wx = wx_ref[...].astype(BF16)
        sp = _softplus(-vec[3:4, :])

        def step(b, carry):
            start = pl.multiple_of(b * bt, bt)
            rows = pl.ds(start, bt)
            xr = _lru_xr(lx_ref, cw, vec[0:1, :], start, b, rows)
            _, _, ig, a, _, mult = _lru_gates(xr, wa, wx, vec[1:2, :], vec[2:3, :], sp)
            u = mult * (ig * xr)
            ac, hc = _scan_fwd(a, u)
            hb = hc + ac * carry
            h_ref[rows, :] = hb
            gel, _ = _gelu_parts(g_ref[rows, :])
            y_ref[rows, :] = gel * hb
            return hb[bt - 1:bt, :]

        lax.fori_loop(0, nb, step, jnp.zeros((1, 128), F32))

    own = _bs((T, 128), lambda c: (0, c))
    return pl.pallas_call(
        body, grid=(DL // 128,),
        in_specs=[_bs((T, 128), lambda c: (0, C_GATE + c)), _bs((T, 128), lambda c: (0, C_LX + c)),
                  _bs((8, 128), lambda c: (0, c)), _bs((8, 128), lambda c: (0, c)),
                  _bs((None, LB, LB), lambda c: (c, 0, 0)), _bs((None, LB, LB), lambda c: (c, 0, 0))],
        out_specs=[own, own], out_shape=[S((T, DL), F32)] * 2, compiler_params=_cp(), name="lru_fwd")(P, P, lcw, vec, wa, wx)


def _lru_bwd(P, lcw, vec, wa, wx, hst, dy):
    T = P.shape[0]
    bt = _bt(T)
    nb = T // bt

    def body(g_ref, lx_ref, cw_ref, vec_ref, wa_ref, wx_ref, h_ref, dy_ref,
             dg_ref, dlx_ref, sm_ref, dwa_ref, dwx_ref, dxr_s):
        cw = cw_ref[...]
        vec = vec_ref[...]
        wa = wa_ref[...].astype(BF16)
        wx = wx_ref[...].astype(BF16)
        lam = vec[3:4, :]
        sp = _softplus(-lam)
        dwa_ref[...] = jnp.zeros((LB, LB), F32)
        dwx_ref[...] = jnp.zeros((LB, LB), F32)
        zero = jnp.zeros((1, 128), F32)

        def step1(i, carry):
            wc, s_cb, s_ba, s_bx, s_sp = carry
            b = nb - 1 - i
            start = pl.multiple_of(b * bt, bt)
            rows = pl.ds(start, bt)
            xr = _lru_xr(lx_ref, cw, vec[0:1, :], start, b, rows)
            xb, r, ig, a, om, mult = _lru_gates(xr, wa, wx, vec[1:2, :], vec[2:3, :], sp)
            hb = h_ref[rows, :]
            dy = dy_ref[rows, :]
            gel, dgel = _gelu_parts(g_ref[rows, :])
            dg_ref[rows, :] = dy * hb * dgel
            dh = dy * gel
            ac, wcum = _scan_bwd(a, a * dh)
            w = wcum + ac * wc
            g = dh + _up(jnp.concatenate([w, jnp.broadcast_to(wc, (8, 128))], axis=0), 1)
            hprev = _down(jnp.concatenate([_halo_prev(h_ref, start, b), hb], axis=0), 1)
            da = g * hprev
            dmult = g * (ig * xr)
            dix = g * mult
            di = dix * xr
            dlog_a = da * a - dmult * ((1.0 - om) / mult)
            dr = dlog_a * (-LRU_C * sp)
            dpr = dr * r * (1.0 - r)
            dpi = di * ig * (1.0 - ig)
            dprb, dpib = dpr.astype(BF16), dpi.astype(BF16)
            dwa_ref[...] += lax.dot_general(xb, dprb, TN, preferred_element_type=F32)
            dwx_ref[...] += lax.dot_general(xb, dpib, TN, preferred_element_type=F32)
            dxr = (dix * ig + lax.dot_general(dprb, wa, NT, preferred_element_type=F32)
                   + lax.dot_general(dpib, wx, NT, preferred_element_type=F32))
            dxr_s[rows, :] = dxr
            return (w[0:1, :], s_cb + jnp.sum(dxr, axis=0, keepdims=True), s_ba + jnp.sum(dpr, axis=0, keepdims=True),
                    s_bx + jnp.sum(dpi, axis=0, keepdims=True), s_sp + jnp.sum(dlog_a * (-LRU_C * r), axis=0, keepdims=True))

        _, s_cb, s_ba, s_bx, s_sp = lax.fori_loop(0, nb, step1, (zero, zero, zero, zero, zero))

        def step2(b, carry):
            t0, t1, t2, t3 = carry
            start = pl.multiple_of(b * bt, bt)
            rows = pl.ds(start, bt)
            dxr = dxr_s[rows, :]
            extn = jnp.concatenate([dxr, _halo_next(dxr_s, start, bt, b, nb)], axis=0)
            dlx_ref[rows, :] = (cw[3:4, :] * dxr + cw[2:3, :] * _up(extn, 1) + cw[1:2, :] * _up(extn, 2)
                                + cw[0:1, :] * _up(extn, 3))
            lx = lx_ref[rows, :]
            ext = jnp.concatenate([_halo_prev(lx_ref, start, b), lx], axis=0)
            return (t0 + jnp.sum(dxr * _down(ext, 3), axis=0, keepdims=True),
                    t1 + jnp.sum(dxr * _down(ext, 2), axis=0, keepdims=True),
                    t2 + jnp.sum(dxr * _down(ext, 1), axis=0, keepdims=True),
                    t3 + jnp.sum(dxr * lx, axis=0, keepdims=True))

        t0, t1, t2, t3 = lax.fori_loop(0, nb, step2, (zero, zero, zero, zero))
        sm_ref[...] = jnp.zeros((16, 128), F32)
        for k, val in enumerate((t0, t1, t2, t3, s_cb, s_ba, s_bx, -s_sp * jax.nn.sigmoid(-lam))):
            sm_ref[k:k + 1, :] = val

    own = _bs((T, 128), lambda c: (0, c))
    wspec = _bs((None, LB, LB), lambda c: (c, 0, 0))
    return pl.pallas_call(
        body, grid=(DL // 128,),
        in_specs=[_bs((T, 128), lambda c: (0, C_GATE + c)), _bs((T, 128), lambda c: (0, C_LX + c)),
                  _bs((8, 128), lambda c: (0, c)), _bs((8, 128), lambda c: (0, c)), wspec, wspec, own, own],
        out_specs=[own, own, _bs((16, 128), lambda c: (0, c)), wspec, wspec],
        out_shape=[S((T, DL), F32)] * 2 + [S((16, DL), F32), S((4, LB, LB), F32), S((4, LB, LB), F32)],
        scratch_shapes=[pltpu.VMEM((T, 128), F32)], compiler_params=_cp(), name="lru_bwd")(P, P, lcw, vec, wa, wx, hst, dy)


_GROUPS = ((0, DC), (DC, DC + DA), (DC + DA, D))


def _gnorm_fwd(yc, ya, yl, gain):
    T = yc.shape[0]
    tb = _tile(T, (512,))

    def body(c_ref, a_ref, l_ref, g_ref, yn_ref, r0_ref, r1_ref, r2_ref):
        for (lo, hi), src, r_ref in zip(_GROUPS, (c_ref, a_ref, l_ref), (r0_ref, r1_ref, r2_ref)):
            yv = src[...]
            r = lax.rsqrt(jnp.mean(yv * yv, axis=1, keepdims=True) + EPS)
            yn_ref[:, lo:hi] = (yv * r * g_ref[:, lo:hi]).astype(BF16)
            r_ref[...] = r

    rs = _bs((tb, 1), lambda i: (i, 0))
    return pl.pallas_call(
        body, grid=(T // tb,),
        in_specs=[_bs((tb, DC), lambda i: (i, 0)), _bs((tb, DA), lambda i: (i, 0)), _bs((tb, DL), lambda i: (i, 0)),
                  _bs((1, D), lambda i: (0, 0))],
        out_specs=[_bs((tb, D), lambda i: (i, 0)), rs, rs, rs],
        out_shape=[S((T, D), BF16)] + [S((T, 1), F32)] * 3, compiler_params=_cp(), name="gnorm_fwd")(yc, ya, yl, gain)


def _gnorm_bwd(dyn, yc, ya, yl, r0, r1, r2, gain):
    T = yc.shape[0]
    tb = _tile(T, (512,))

    def body(d_ref, c_ref, a_ref, l_ref, r0_ref, r1_ref, r2_ref, g_ref, dc_ref, da_ref, dl_ref, dg_ref):
        i = pl.program_id(0)
        for (lo, hi), src, r_ref, dst in zip(_GROUPS, (c_ref, a_ref, l_ref), (r0_ref, r1_ref, r2_ref), (dc_ref, da_ref, dl_ref)):
            r = r_ref[...]
            yhat = src[...] * r
            dy = d_ref[:, lo:hi]
            dyh = dy * g_ref[:, lo:hi]
            m = jnp.mean(dyh * yhat, axis=1, keepdims=True)
            dst[...] = r * (dyh - yhat * m)
            part = jnp.sum(dy * yhat, axis=0, keepdims=True)

            @pl.when(i == 0)
            def _():
                dg_ref[:, lo:hi] = part

            @pl.when(i > 0)
            def _():
                dg_ref[:, lo:hi] += part

    rs = _bs((tb, 1), lambda i: (i, 0))
    specs = [_bs((tb, DC), lambda i: (i, 0)), _bs((tb, DA), lambda i: (i, 0)), _bs((tb, DL), lambda i: (i, 0))]
    return pl.pallas_call(
        body, grid=(T // tb,),
        in_specs=[_bs((tb, D), lambda i: (i, 0))] + specs + [rs, rs, rs, _bs((1, D), lambda i: (0, 0))],
        out_specs=specs + [_bs((1, D), lambda i: (0, 0))],
        out_shape=[S((T, DC), F32), S((T, DA), F32), S((T, DL), F32), S((1, D), F32)],
        compiler_params=_cp(), name="gnorm_bwd")(dyn, yc, ya, yl, r0, r1, r2, gain)


HBM = pl.BlockSpec(memory_space=pltpu.HBM)
N_BIG = 6


def _place():
    x, y, c = lax.axis_index("x"), lax.axis_index("y"), lax.axis_index("c")
    return x, y, c, 2 * x + y


def _peer(x, y, j):
    return x ^ ((j + 1) >> 1), y ^ ((j + 1) & 1)


SEM = pl.BlockSpec(memory_space=pltpu.SEMAPHORE)
ANY = pl.BlockSpec(memory_space=pl.ANY)
VM = pl.BlockSpec(memory_space=pltpu.VMEM)
EFFECT = pltpu.SideEffectType.DATAFLOW_SIDE_EFFECTING
N_AG = N_BIG + 1


def _hbm(a):
    return pltpu.with_memory_space_constraint(a, pltpu.HBM)


AG_ORDER = (0, 1, N_BIG, 2, 3, 4, 5)
AG_FIRST, AG_REST = (0, 1), (N_BIG, 2, 3, 4, 5)


def _ag_copy(src, land, ssem, rsem, t, j, chip):
    x, y, c, _ = _place()
    px, py = _peer(x, y, j)
    if t == N_BIG:
        s_ref, d_ref = src, land.at[chip]
    else:
        rh = src.shape[0] // 2
        half = pl.ds(c * rh, rh)
        s_ref, d_ref = src.at[half], land.at[chip, half]
    return pltpu.make_async_remote_copy(src_ref=s_ref, dst_ref=d_ref, send_sem=ssem.at[3 * t + j], recv_sem=rsem.at[3 * t + j],
                                        device_id=(px, py, c), device_id_type=MESH)


def _ag_start(l, srcs, dep):
    n = N_AG

    def body(*refs):
        src = refs[:n]
        ssem, rsem = refs[2 * n + 1], refs[2 * n + 2]
        land = refs[3 * n + 3:4 * n + 3]
        token = refs[4 * n + 3]
        _, _, _, me = _place()
        for t in AG_ORDER:
            for j in range(3):
                _ag_copy(src[t], land[t], ssem, rsem, t, j, me).start()
        token[...] = jnp.zeros_like(token)

    lands = [lax.empty((4,) + a.shape, a.dtype) for a in srcs]
    dma = pltpu.SemaphoreType.DMA
    outs = pl.pallas_call(
        body, name=f"ag_start_{l}",
        out_shape=(dma((3 * n,)), dma((3 * n,))) + tuple(pltpu.HBM(a.shape, a.dtype) for a in list(srcs) + lands) + (S((8, 128), F32),),
        in_specs=[HBM] * (2 * n) + [ANY], out_specs=(SEM, SEM) + (HBM,) * (2 * n) + (VM,),
        input_output_aliases={i: 2 + i for i in range(2 * n)},
        compiler_params=pltpu.CompilerParams(has_side_effects=EFFECT),
    )(*[_hbm(a) for a in srcs], *[_hbm(a) for a in lands], dep)
    return outs[0], outs[1], outs[2:2 + n], outs[2 + n:2 + 2 * n], outs[-1]


def _ag_wait(name, idx, ssem, rsem, srcs, lands, after):
    n = len(idx)

    def body(*refs):
        src, land = refs[:n], refs[n:2 * n]
        ssem, rsem = refs[2 * n], refs[2 * n + 1]
        x, y, _, _ = _place()
        for p, t in enumerate(idx):
            for j in range(3):
                px, py = _peer(x, y, j)
                cp = _ag_copy(src[p], land[p], ssem, rsem, t, j, 2 * px + py)
                cp.wait_send()
                cp.wait_recv()

    outs = pl.pallas_call(
        body, name=name,
        out_shape=tuple(pltpu.HBM(a.shape, a.dtype) for a in list(srcs) + list(lands)),
        in_specs=[HBM] * (2 * n) + [SEM, SEM, ANY], out_specs=(HBM,) * (2 * n),
        input_output_aliases={i: i for i in range(2 * n)},
        compiler_params=pltpu.CompilerParams(has_side_effects=EFFECT),
    )(*srcs, *lands, ssem, rsem, after)
    return outs[:n], outs[n:]


def _fwd_copies(idx, src, land, ssem, rsem, recv_side):
    x, y, c, me = _place()
    cps = []
    for p, t in enumerate(idx):
        cps.append(pltpu.make_async_remote_copy(src_ref=src[p], dst_ref=land[p].at[me], send_sem=ssem.at[4 * p], recv_sem=rsem.at[4 * p],
                                                device_id=(x, y, 1 - c), device_id_type=MESH))
        if t == N_BIG:
            continue
        rh = src[p].shape[0] // 2
        for j in range(3):
            px, py = _peer(x, y, j)
            part = land[p].at[2 * px + py, pl.ds(((1 - c) if recv_side else c) * rh, rh)]
            cps.append(pltpu.make_async_remote_copy(src_ref=part, dst_ref=part, send_sem=ssem.at[4 * p + 1 + j],
                                                    recv_sem=rsem.at[4 * p + 1 + j], device_id=(x, y, 1 - c), device_id_type=MESH))
    return cps


def _ag_fwd_start(name, idx, srcs, lands, dep):
    n = len(idx)

    def body(*refs):
        src = refs[:n]
        ssem, rsem = refs[2 * n + 1], refs[2 * n + 2]
        land = refs[3 * n + 3:4 * n + 3]
        token = refs[4 * n + 3]
        for cp in _fwd_copies(idx, src, land, ssem, rsem, False):
            cp.start()
        token[...] = jnp.zeros_like(token)

    dma = pltpu.SemaphoreType.DMA
    outs = pl.pallas_call(
        body, name=name,
        out_shape=(dma((4 * n,)), dma((4 * n,))) + tuple(pltpu.HBM(a.shape, a.dtype) for a in list(srcs) + list(lands)) + (S((8, 128), F32),),
        in_specs=[HBM] * (2 * n) + [ANY], out_specs=(SEM, SEM) + (HBM,) * (2 * n) + (VM,),
        input_output_aliases={i: 2 + i for i in range(2 * n)},
        compiler_params=pltpu.CompilerParams(has_side_effects=EFFECT),
    )(*srcs, *lands, dep)
    return outs[0], outs[1], outs[2:2 + n], outs[2 + n:2 + 2 * n], outs[-1]


def _ag_fwd_wait(name, idx, ssem, rsem, srcs, lands, after):
    n = len(idx)

    def body(*refs):
        src, land = refs[:n], refs[n:2 * n]
        ssem, rsem = refs[2 * n], refs[2 * n + 1]
        for mine, theirs in zip(_fwd_copies(idx, src, land, ssem, rsem, False), _fwd_copies(idx, src, land, ssem, rsem, True)):
            mine.wait_send()
            theirs.wait_recv()

    outs = pl.pallas_call(
        body, name=name,
        out_shape=tuple(pltpu.HBM(a.shape, a.dtype) for a in list(srcs) + list(lands)),
        in_specs=[HBM] * (2 * n) + [SEM, SEM, ANY], out_specs=(HBM,) * (2 * n),
        input_output_aliases={i: i for i in range(2 * n)},
        compiler_params=pltpu.CompilerParams(has_side_effects=EFFECT),
    )(*srcs, *lands, ssem, rsem, after)
    return outs[n:]


def _pair_copy(g, land, ssem, rsem, t):
    x, y, c, _ = _place()
    rh = g.shape[1] // 2
    return pltpu.make_async_remote_copy(src_ref=g.at[:, pl.ds((1 - c) * rh, rh), :], dst_ref=land,
                                        send_sem=ssem.at[t], recv_sem=rsem.at[t], device_id=(x, y, 1 - c), device_id_type=MESH)


def _rs_pair(grads):
    n = len(grads)

    def body(*refs):
        g, out = refs[:n], refs[n:2 * n]
        ssem, rsem = refs[2 * n:]
        for t in range(n):
            _pair_copy(g[t], out[t], ssem, rsem, t).start()
        for t in range(n):
            _pair_copy(g[t], out[t], ssem, rsem, t).wait()

    dma = pltpu.SemaphoreType.DMA
    return pl.pallas_call(
        body, in_specs=[HBM] * n, out_specs=[HBM] * n,
        out_shape=[S((4, g.shape[1] // 2, g.shape[2]), g.dtype) for g in grads],
        scratch_shapes=[dma((n,)), dma((n,))], name="rs_pair")(*grads)


def _rs_pair_start(name, grads, dep):
    n = len(grads)

    def body(*refs):
        g = refs[:n]
        ssem, rsem = refs[2 * n + 1], refs[2 * n + 2]
        land = refs[3 * n + 3:4 * n + 3]
        token = refs[4 * n + 3]
        for t in range(n):
            _pair_copy(g[t], land[t], ssem, rsem, t).start()
        token[...] = jnp.zeros_like(token)

    lands = [lax.empty((4, a.shape[1] // 2, a.shape[2]), a.dtype) for a in grads]
    dma = pltpu.SemaphoreType.DMA
    outs = pl.pallas_call(
        body, name=name,
        out_shape=(dma((n,)), dma((n,))) + tuple(pltpu.HBM(a.shape, a.dtype) for a in list(grads) + lands) + (S((8, 128), F32),),
        in_specs=[HBM] * (2 * n) + [ANY], out_specs=(SEM, SEM) + (HBM,) * (2 * n) + (VM,),
        input_output_aliases={i: 2 + i for i in range(2 * n)},
        compiler_params=pltpu.CompilerParams(has_side_effects=EFFECT),
    )(*[_hbm(a) for a in grads], *[_hbm(a) for a in lands], dep)
    return outs[0], outs[1], outs[2:2 + n], outs[2 + n:2 + 2 * n], outs[-1]


def _rs_pair_wait(name, ssem, rsem, grads, lands, after):
    n = len(grads)

    def body(*refs):
        g, land = refs[:n], refs[n:2 * n]
        ssem, rsem = refs[2 * n], refs[2 * n + 1]
        for t in range(n):
            cp = _pair_copy(g[t], land[t], ssem, rsem, t)
            cp.wait_send()
            cp.wait_recv()

    outs = pl.pallas_call(
        body, name=name,
        out_shape=tuple(pltpu.HBM(a.shape, a.dtype) for a in list(grads) + list(lands)),
        in_specs=[HBM] * (2 * n) + [SEM, SEM, ANY], out_specs=(HBM,) * (2 * n),
        input_output_aliases={i: i for i in range(2 * n)},
        compiler_params=pltpu.CompilerParams(has_side_effects=EFFECT),
    )(*grads, *lands, ssem, rsem, after)
    return outs[:n], outs[n:]


def _rs_copy(s, land, ssem, rsem, t, j):
    x, y, c, _ = _place()
    px, py = _peer(x, y, j)
    return pltpu.make_async_remote_copy(src_ref=s[t].at[2 * px + py], dst_ref=land[t].at[j],
                                        send_sem=ssem.at[3 * t + j], recv_sem=rsem.at[3 * t + j], device_id=(px, py, c), device_id_type=MESH)


def _rs_start(name, sums, dep):
    n = len(sums)

    def body(*refs):
        s = refs[:n]
        ssem, rsem = refs[2 * n + 1], refs[2 * n + 2]
        land = refs[3 * n + 3:4 * n + 3]
        token = refs[4 * n + 3]
        for t in range(n):
            for j in range(3):
                _rs_copy(s, land, ssem, rsem, t, j).start()
        token[...] = jnp.zeros_like(token)

    lands = [lax.empty((3,) + a.shape[1:], a.dtype) for a in sums]
    dma = pltpu.SemaphoreType.DMA
    outs = pl.pallas_call(
        body, name=name,
        out_shape=(dma((3 * n,)), dma((3 * n,))) + tuple(pltpu.HBM(a.shape, a.dtype) for a in list(sums) + lands) + (S((8, 128), F32),),
        in_specs=[HBM] * (2 * n) + [ANY], out_specs=(SEM, SEM) + (HBM,) * (2 * n) + (VM,),
        input_output_aliases={i: 2 + i for i in range(2 * n)},
        compiler_params=pltpu.CompilerParams(has_side_effects=EFFECT),
    )(*[_hbm(a) for a in sums], *[_hbm(a) for a in lands], dep)
    return outs[0], outs[1], outs[2:2 + n], outs[2 + n:2 + 2 * n], outs[-1]


def _rs_wait(name, ssem, rsem, sums, lands, after):
    n = len(sums)

    def body(*refs):
        s, land = refs[:n], refs[n:2 * n]
        ssem, rsem = refs[2 * n], refs[2 * n + 1]
        for t in range(n):
            for j in range(3):
                cp = _rs_copy(s, land, ssem, rsem, t, j)
                cp.wait_send()
                cp.wait_recv()

    outs = pl.pallas_call(
        body, name=name,
        out_shape=tuple(pltpu.HBM(a.shape, a.dtype) for a in list(sums) + list(lands)),
        in_specs=[HBM] * (2 * n) + [SEM, SEM, ANY], out_specs=(HBM,) * (2 * n),
        input_output_aliases={i: i for i in range(2 * n)},
        compiler_params=pltpu.CompilerParams(has_side_effects=EFFECT),
    )(*sums, *lands, ssem, rsem, after)
    return outs[n:]


def _rs_join(halves, lo, hi, dep):
    def body(*refs):
        h = refs[:N_BIG]
        out = refs[N_BIG + 1:2 * N_BIG + 1]
        ssem, rsem = refs[2 * N_BIG + 1:]
        x, y, c, _ = _place()

        def cp(t):
            return pltpu.make_async_remote_copy(
                src_ref=h[t].at[pl.ds(lo, hi - lo)], dst_ref=out[t], send_sem=ssem.at[t], recv_sem=rsem.at[t],
                device_id=(x, y, 1 - c), device_id_type=MESH)

        for t in range(N_BIG):
            cp(t).start()
        for t in range(N_BIG):
            cp(t).wait()

    dma = pltpu.SemaphoreType.DMA
    return pl.pallas_call(
        body, in_specs=[HBM] * N_BIG + [ANY], out_specs=[HBM] * N_BIG,
        out_shape=[S((hi - lo,) + h.shape[1:], h.dtype) for h in halves],
        scratch_shapes=[dma((N_BIG,)), dma((N_BIG,))], name="rs_join")(*halves, dep)


def _all_reduce_small(pack, dep):
    R = pack.shape[0]
    rb = _tile(R, (512, 256, 128, 8))

    def body(x_ref, dep_ref, all_ref, sum_ref, send_sems, recv_sems, local_sem):
        x, y, c = lax.axis_index("x"), lax.axis_index("y"), lax.axis_index("c")
        me, sibling = (x, y, c), (x, y, 1 - c)
        chips = [(1 - x, y), (x, 1 - y), (1 - x, 1 - y)]

        def rows(px, py, pc):
            return all_ref.at[pl.ds((4 * px + 2 * py + pc) * R, R), :]

        def copy(k, block, to, src=None):
            return pltpu.make_async_remote_copy(
                src_ref=rows(*block) if src is None else src, dst_ref=rows(*block),
                send_sem=send_sems.at[k], recv_sem=recv_sems.at[k], device_id=to, device_id_type=MESH)

        mine = pltpu.make_async_copy(x_ref, rows(*me), local_sem)
        mine.start()
        first = [copy(0, me, sibling, src=x_ref)]
        first += [copy(1 + j, me, (*chip, c), src=x_ref) for j, chip in enumerate(chips)]
        for cp in first:
            cp.start()
        passed = [copy(4 + j, (*chip, c), sibling) for j, chip in enumerate(chips)]
        for j, chip in enumerate(chips):
            copy(1 + j, (*chip, c), me).wait_recv()
            passed[j].start()
        copy(0, sibling, me).wait_recv()
        for j, chip in enumerate(chips):
            copy(4 + j, (*chip, 1 - c), me).wait_recv()
        for cp in first + passed:
            cp.wait_send()
        mine.wait()

        def step(b, carry):
            off = pl.multiple_of(b * rb, rb)
            acc = all_ref[pl.ds(off, rb), :]
            for k in range(1, 8):
                acc = acc + all_ref[pl.ds(pl.multiple_of(k * R + off, 8), rb), :]
            sum_ref[pl.ds(off, rb), :] = acc
            return carry

        lax.fori_loop(0, R // rb, step, 0)

    vm = pl.BlockSpec(memory_space=pltpu.VMEM)
    dma = pltpu.SemaphoreType.DMA
    _, total = pl.pallas_call(
        body, in_specs=[vm, pl.BlockSpec(memory_space=pl.ANY)], out_specs=[vm, vm],
        out_shape=[S((8 * R, 128), F32), S((R, 128), F32)],
        scratch_shapes=[dma((7,)), dma((7,)), dma],
        compiler_params=_cp(), name="allreduce_small")(pack, dep)
    return total


def _row_tile(rh, cc, tile_bytes=3 * 1024 * 1024 // 2):
    for t in (512, 256, 128, 64, 32, 16):
        if rh % t == 0 and t * cc * 4 <= tile_bytes:
            return t
    return 16


def _my_chip():
    return 2 * lax.axis_index("x") + lax.axis_index("y")


def _pair_sum(g, recv):
    _, r, cc = g.shape
    rh = r // 2
    tb = _row_tile(rh, cc)
    nbh = rh // tb

    def body(g_ref, r_ref, o_ref):
        o_ref[...] = (g_ref[...].astype(F32) + r_ref[...].astype(F32)).astype(BF16)

    def chip(k):
        return (_my_chip() + 1 + k) % 4

    mine = _bs((None, tb, cc), lambda k, i: (chip(k), lax.axis_index("c") * nbh + i, 0))
    plain = _bs((None, tb, cc), lambda k, i: (chip(k), i, 0))
    return pl.pallas_call(body, grid=(3, nbh), in_specs=[mine, plain], out_specs=plain,
                          out_shape=S((4, rh, cc), BF16), compiler_params=_cp(), name="rs_pair_sum")(g, recv)


def _owner_sum(g, recv, ici, acc, l):
    _, r, cc = g.shape
    rh = r // 2
    tb = _row_tile(rh, cc)
    nbh = rh // tb

    def body(g_ref, r_ref, i0_ref, i1_ref, i2_ref, acc_ref, o_ref):
        s = g_ref[...].astype(F32) + r_ref[...].astype(F32)
        o_ref[...] = s + i0_ref[...].astype(F32) + i1_ref[...].astype(F32) + i2_ref[...].astype(F32)

    def slot(j):
        return _bs((None, tb, cc), lambda i: (j, i, 0))

    return pl.pallas_call(
        body, grid=(nbh,),
        in_specs=[_bs((None, tb, cc), lambda i: (_my_chip(), lax.axis_index("c") * nbh + i, 0)),
                  _bs((None, tb, cc), lambda i: (_my_chip(), i, 0)),
                  slot(0), slot(1), slot(2), pl.BlockSpec(memory_space=pl.ANY)],
        out_specs=_bs((None, tb, cc), lambda i: (l, i, 0)),
        out_shape=S(acc.shape, F32), input_output_aliases={5: 0},
        compiler_params=_cp(), name="rs_owner_sum")(g, recv, ici, ici, ici, acc)


def _adam_math(w, g, m, v):
    m = B1 * m + (1.0 - B1) * g
    v = B2 * v + (1.0 - B2) * (g * g)
    m_hat = m / (1.0 - B1 ** STEP)
    v_hat = v / (1.0 - B2 ** STEP)
    delta = -LR * (m_hat / (jnp.sqrt(v_hat) + AEPS) + WD * w)
    return delta, m, v


def _adamw_big(w, g_mine, g_sib, m, v, lo, hi, prev):
    L, r, cc = w.shape
    rh = r // 2
    tb = _row_tile(rh, cc)
    nbh = rh // tb

    def body(w_ref, gm_ref, gs_ref, m_ref, v_ref, *rest):
        go_ref, d_ref, mo_ref, vo_ref = rest[-4:]
        mine = pl.program_id(1) == lax.axis_index("c")
        g = jnp.where(mine, gm_ref[...], gs_ref[...])
        d, m, v = _adam_math(w_ref[...], g, m_ref[...], v_ref[...])
        go_ref[...] = g
        d_ref[...] = d
        mo_ref[...] = m
        vo_ref[...] = v

    def mine_map(l, hf, i):
        c = lax.axis_index("c")
        return (l + lo, jnp.where(hf == c, i, jnp.where(c == 0, nbh - 1, 0)), 0)

    def sib_map(l, hf, i):
        c = lax.axis_index("c")
        return (l, jnp.where(hf != c, i, jnp.where(c == 0, 0, nbh - 1)), 0)

    full = _bs((None, tb, cc), lambda l, hf, i: (l + lo, hf * nbh + i, 0))
    extra = [] if prev is None else list(prev)
    return pl.pallas_call(
        body, grid=(hi - lo, 2, nbh),
        in_specs=[full, _bs((None, tb, cc), mine_map), _bs((None, tb, cc), sib_map), full, full]
        + [pl.BlockSpec(memory_space=pl.ANY)] * len(extra),
        out_specs=[full] * 4, out_shape=[S(w.shape, F32)] * 4,
        input_output_aliases={5 + k: k for k in range(len(extra))},
        compiler_params=_cp(), name="adamw_big")(w, g_mine, g_sib, m, v, *extra)


def _adamw_small(w, g, m, v):
    R = w.shape[0]
    tb = _tile(R, (512, 256, 128, 8))

    def body(w_ref, g_ref, m_ref, v_ref, d_ref, mo_ref, vo_ref):
        d, m, v = _adam_math(w_ref[...], g_ref[...], m_ref[...], v_ref[...])
        d_ref[...] = d
        mo_ref[...] = m
        vo_ref[...] = v

    spec = _bs((tb, 128), lambda i: (i, 0))
    return pl.pallas_call(body, grid=(R // tb,), in_specs=[spec] * 4, out_specs=[spec] * 3,
                          out_shape=[S((R, 128), F32)] * 3, compiler_params=_cp(), name="adamw_small")(w, g, m, v)


def _mix_pad(w):
    return jnp.concatenate([w[:, :4608], w[:, 4616:DIN], w[:, 4608:4616], jnp.zeros((D, PW - DIN), w.dtype)], axis=1)


def _mix_unpad(g):
    return jnp.concatenate([g[:, :4608], g[:, 5632:5640], g[:, 4608:5632]], axis=1)


def _pack(parts):
    flat = jnp.concatenate([p.reshape(-1).astype(F32) for p in parts])
    n = flat.shape[0]
    total = -(-n // (512 * 128)) * (512 * 128)
    return jnp.pad(flat, (0, total - n)).reshape(total // 128, 128)


def _unpack(pack, shapes):
    flat = pack.reshape(-1)
    out, off = [], 0
    for s in shapes:
        n = math.prod(s)
        out.append(flat[off:off + n].reshape(s))
        off += n
    return out


def _ffn_forward(x, gain, win, wout):
    h, rstd = _rms_fwd(x, gain)
    zg, zu, act = _ffn_in(h, win)
    y = _ffn_out(act, wout, x)
    return y, (x, h, rstd, zg, zu, act)


def _ffn_backward(dy, dyb, saved, gain, win, wout, dep):
    x, h, rstd, zg, zu, act = saved
    dz = _ffn_bwd_dz(dyb, wout, zg, zu, dep)
    dwout = _mm_tn("ffn_bwd_dwout", act, dyb, scale=0.5, tm=512, tn=1024)
    dwin = _ffn_bwd_dwin(h, dz)
    dh = _ffn_bwd_dh(dz, win)
    dx, dxb, dgain = _rms_bwd(dh, x, rstd, gain, dy)
    return dx, dxb, dgain, dwin, dwout


def _mixer_forward(x, p):
    T = x.shape[0]
    h, rstd = _rms_fwd(x, p["norm_mix"])
    tm = _tile(T, (1024, 512))
    tn = 1152
    P = _mm("mix_in", h, p["wmix"],
            _bs((tm, D), lambda i, j, k: (i, 0)), _bs((D, tn), lambda i, j, k: (0, j)),
            _bs((tm, tn), lambda i, j, k: (i, j)), S((T, PW), F32), (T // tm, PW // tn, 1), NN, 1, (tm, tn))
    yc = _conv_fwd(P, p["cw"])
    cum = _fgate_fwd(P, p["fb"])
    cumt = cum[:, :NH].T
    tq = _att_tile(T)
    cumq, cumk = cumt.reshape(NH, T, 1), cumt.reshape(NH, T // tq, 1, tq)
    qkv = P[:, C_Q * 128:C_GATE * 128].astype(BF16)
    ya, lse = _attn_fwd(qkv, cumq, cumk)
    yl, hst = _lru_fwd(P, p["lcw"], p["lvec"], p["lru_w_a"], p["lru_w_x"])
    yn, r0, r1, r2 = _gnorm_fwd(yc, ya, yl, p["mix_out_norm"])
    y = _mm("mix_out", yn, p["wo"],
            _bs((tm, D), lambda i, j, k: (i, 0)), _bs((D, 1024), lambda i, j, k: (0, j)),
            _bs((tm, 1024), lambda i, j, k: (i, j)), S((T, D), F32), (T // tm, D // 1024, 1), NN, 1, (tm, 1024),
            res=x, r_spec=_bs((tm, 1024), lambda i, j, k: (i, j)))
    return y, (x, h, rstd, P, qkv, cumq, cumk, lse, yc, ya, yl, hst, yn, r0, r1, r2)


def _mixer_backward(dy, dyb, saved, p, dep):
    x, h, rstd, P, qkv, cumq, cumk, lse, yc, ya, yl, hst, yn, r0, r1, r2 = saved
    T = x.shape[0]
    dyn = _mm_nt_full("mix_bwd_dyn", dyb, p["wo"], 512, dep=dep)
    dwo = _mm_tn("mix_bwd_dwo", yn, dyb, tm=512, tn=1024)
    dyc, dya, dyl, dgn = _gnorm_bwd(dyn, yc, ya, yl, r0, r1, r2, p["mix_out_norm"])
    dcb, dcc, dcv, dcw = _conv_bwd(P, p["cw"], dyc)
    dq, dk, dv, dck, dcq = _attn_bwd(qkv, cumq, cumk, lse, ya, dya)
    dcum = jnp.pad((dck.reshape(NH, T) + dcq.reshape(NH, T)).T, ((0, 0), (0, 128 - NH)))
    df, dfb = _fgate_bwd(P, p["fb"], dcum)
    dgate, dlx, lsm, dwa, dwx = _lru_bwd(P, p["lcw"], p["lvec"], p["lru_w_a"], p["lru_w_x"], hst, dyl)
    dP = jnp.concatenate([dcb, dcc, dcv, dq, dk, dv, dgate, dlx, df], axis=1).astype(BF16)
    tm = _tile(T, (512,))
    dh = _mm("mix_bwd_dh", dP, p["wmix"],
             _bs((tm, PW), lambda j, i, k: (i, 0)), _bs((1024, PW), lambda j, i, k: (j, 0)),
             _bs((tm, 1024), lambda j, i, k: (i, j)), S((T, D), F32), (D // 1024, T // tm, 1), NT, 1, (tm, 1024))
    dwmix = _mm_tn("mix_bwd_dwmix", h, dP, tm=512, tn=1152)
    dx, dxb, dgm = _rms_bwd(dh, x, rstd, p["norm_mix"], dy)
    small = dict(norm_mix=dgm[0], mix_out_norm=dgn[0], conv_w=dcw[:3], fgate_b=dfb[0, :NH], lru_conv_w=lsm[:4],
                 lru_conv_b=lsm[4], lru_b_a=lsm[5], lru_b_x=lsm[6], lru_lambda=lsm[7], lru_w_a=dwa, lru_w_x=dwx)
    return dx, dxb, small, dwmix, dwo


BIG =("ffn1_w_in", "ffn1_w_out", "mix_w_in", "mix_w_out", "ffn2_w_in", "ffn2_w_out")
SMALL = ("norm_ffn1", "norm_mix", "conv_w", "fgate_b", "lru_conv_w", "lru_conv_b", "lru_w_a", "lru_b_a", "lru_w_x",
         "lru_b_x", "lru_lambda", "mix_out_norm", "norm_ffn2", "final_norm")
WEIGHTS = ("norm_ffn1", "ffn1_w_in", "ffn1_w_out", "norm_mix", "mix_w_in", "conv_w", "fgate_b", "lru_conv_w", "lru_conv_b",
           "lru_w_a", "lru_b_a", "lru_w_x", "lru_b_x", "lru_lambda", "mix_out_norm", "mix_w_out", "norm_ffn2", "ffn2_w_in",
           "ffn2_w_out", "final_norm")


def _step(args):
    xx, yy, cc_ = lax.axis_index("x"), lax.axis_index("y"), lax.axis_index("c")
    me = 2 * xx + yy
    x0 = args["x"][0]
    tgt = args["loss_target"][0]
    T = x0.shape[0]
    L = args["norm_ffn1"].shape[0]

    def ag_sources(l):
        small = jnp.concatenate([args["conv_w"][l], args["lru_conv_w"][l], jnp.zeros((1, 128), F32)], axis=0)
        return [args[n][l].astype(BF16) for n in BIG] + [small]

    def layer_params(l, gat, gsm):
        w1i, w1o, wmx, wo, w2i, w2o = gat
        cwl = gsm.transpose(1, 0, 2).reshape(8, 4 * 128)
        return dict(
            w1i=w1i, w1o=w1o.reshape(-1, D), w2i=w2i, w2o=w2o.reshape(-1, D), wo=wo.reshape(D, D),
            wmix=_mix_pad(wmx.transpose(1, 0, 2).reshape(D, DIN)),
            cw=jnp.concatenate([cwl[:3], jnp.zeros((5, DC), F32)], axis=0),
            lcw=jnp.concatenate([cwl[3:7], jnp.zeros((4, DL), F32)], axis=0),
            fb=jnp.pad(args["fgate_b"][l], (0, 128 - NH)).reshape(1, 128),
            lvec=jnp.concatenate([args["lru_conv_b"][l][None], args["lru_b_a"][l][None], args["lru_b_x"][l][None],
                                  args["lru_lambda"][l][None], jnp.zeros((4, DL), F32)], axis=0),
            lru_w_a=args["lru_w_a"][l], lru_w_x=args["lru_w_x"][l],
            norm_ffn1=args["norm_ffn1"][l][None], norm_mix=args["norm_mix"][l][None],
            mix_out_norm=args["mix_out_norm"][l][None], norm_ffn2=args["norm_ffn2"][l][None])

    xs = x0
    saved, layers = [], []
    def ici_done(name, idx, flight, after):
        ssem, rsem, srcs, lands, _ = flight
        s, ld = _ag_wait("ag_wait_" + name, idx, ssem, rsem, [srcs[t] for t in idx], [lands[t] for t in idx], after)
        return _ag_fwd_start("ag_fwd_start_" + name, idx, s, ld, after)

    def gathered(name, idx, fwd, after):
        ssem, rsem, s, ld, _ = fwd
        return _ag_fwd_wait("ag_fwd_wait_" + name, idx, ssem, rsem, s, ld, after)

    every_w = tuple(range(N_AG))
    flight = _ag_start(0, ag_sources(0), x0)
    fwd = None
    for l in range(L):
        if l == 0:
            (w1i,) = gathered("0i", (0,), ici_done("0i", (0,), flight, xs), xs)
        else:
            w1i, w1o, wmx, wo, w2i, w2o, gsm = gathered(str(l), every_w, fwd, xs)
        nxt = _ag_start(l + 1, ag_sources(l + 1), w1i) if l + 1 < L else None
        h1, rstd1 = _rms_fwd(xs, args["norm_ffn1"][l][None])
        zg1, zu1, act1 = _ffn_in(h1, w1i)
        if l == 0:
            (w1o,) = gathered("0o", (1,), ici_done("0o", (1,), flight, act1), act1)
        x1 = _ffn_out(act1, w1o.reshape(-1, D), xs)
        s1 = (xs, h1, rstd1, zg1, zu1, act1)
        if l == 0:
            gsm, wmx, wo, w2i, w2o = gathered("0b", AG_REST, ici_done("0b", AG_REST, flight, x1), x1)
        p = layer_params(l, (w1i, w1o, wmx, wo, w2i, w2o), gsm)
        x2, s2 = _mixer_forward(x1, p)
        h2, rstd2 = _rms_fwd(x2, p["norm_ffn2"])
        zg2, zu2, act2 = _ffn_in(h2, p["w2i"])
        fwd = ici_done(str(l + 1), every_w, nxt, act2) if nxt is not None else None
        x3 = _ffn_out(act2, p["w2o"], x2, dep=None if fwd is None else fwd[4])
        s3 = (x2, h2, rstd2, zg2, zu2, act2)
        saved.append((s1, s2, s3))
        layers.append(p)
        xs = x3
    lpart, dx, dxb, dfinal = _loss_head(xs, args["final_norm"][None], tgt)
    loss = lax.psum(lpart[0, 0], ("x", "y", "c"))

    acc = [None] * N_BIG
    small_grads = [None] * L
    every = tuple(range(N_BIG))
    in_air = []
    pair_flight = None
    dep = lpart

    def to_blocks(t, g):
        if t in (1, 5):
            return g.reshape(4, g.shape[0] // 4, D)
        if t == 2:
            return _mix_unpad(g).reshape(D, 4, DIN // 4).transpose(1, 0, 2)
        if t == 3:
            return g.reshape(4, D // 4, D)
        return g

    def launch(name, l, idx, grads, recv, after):
        sums = [_pair_sum(g, r) for g, r in zip(grads, recv)]
        started = _rs_start("rs_start_" + name, sums, after)
        in_air.append(("rs_wait_" + name, l, idx, grads, recv, started))
        return started[4]

    def land(entry, after):
        name, l, idx, grads, recv, (ssem, rsem, sums, lands, _) = entry
        ici = _rs_wait(name, ssem, rsem, sums, lands, after)
        for t, g, r, i3 in zip(idx, grads, recv, ici):
            if acc[t] is None:
                acc[t] = jnp.zeros((L, g.shape[1] // 2, g.shape[2]), F32)
            acc[t] = _owner_sum(g, r, i3, acc[t], l)

    for l in reversed(range(L)):
        p = layers[l]
        s1, s2, s3 = saved[l]
        dx, dxb, dg2, dw2i, dw2o = _ffn_backward(dx, dxb, s3, p["norm_ffn2"], p["w2i"], p["w2o"], dep)
        if pair_flight is not None:
            lp, (ssem, rsem, g_thru, lands, _) = pair_flight
            g_thru, recv = _rs_pair_wait(f"rs_pair_wait_{lp}", ssem, rsem, g_thru, lands, dx)
            dep = launch(str(lp), lp, every, g_thru, recv, dx)
            pair_flight = None
        if l == 0:
            part = [to_blocks(4, dw2i), to_blocks(5, dw2o)]
            dep = launch("0c", 0, (4, 5), part, _rs_pair(part), dx)
        dx, dxb, sm, dwmix, dwo = _mixer_backward(dx, dxb, s2, p, dep)
        if l == 0:
            part = [to_blocks(2, dwmix), to_blocks(3, dwo)]
            dep = launch("0b", 0, (2, 3), part, _rs_pair(part), dx)
        dx, dxb, dg1, dw1i, dw1o = _ffn_backward(dx, dxb, s1, p["norm_ffn1"], p["w1i"], p["w1o"], dep)
        sm["norm_ffn1"] = dg1[0]
        sm["norm_ffn2"] = dg2[0]
        small_grads[l] = sm
        for entry in [e for e in in_air if e[1] > l]:
            land(entry, dx)
            in_air.remove(entry)
        if l > 0:
            grads = [to_blocks(t, g) for t, g in enumerate((dw1i, dw1o, dwmix, dwo, dw2i, dw2o))]
            pair_flight = (l, _rs_pair_start(f"rs_pair_start_{l}", grads, dx))
            dep = pair_flight[1][4]

    part = [to_blocks(0, dw1i), to_blocks(1, dw1o)]
    dep = launch("0a", 0, (0, 1), part, _rs_pair(part), dx)

    def adamw(k, lo, hi, sib, prev):
        n = BIG[k]
        return _adamw_big(args[n], acc[k], sib[k], args["m_" + n], args["v_" + n], lo, hi, prev)

    res = [None] * N_BIG
    after = dx
    if L > 1:
        sib = _rs_join(acc, 1, L, dep)
        for k in range(N_BIG):
            res[k] = adamw(k, 1, L, sib, None)
        after = res[N_BIG - 1][0]
    full = {n: (dfinal[0] if n == "final_norm" else jnp.stack([small_grads[l][n] for l in range(L)])) for n in SMALL}
    red_pack = _all_reduce_small(_pack([full[n] for n in SMALL]), after)
    for entry in list(in_air):
        land(entry, red_pack)
    sib = _rs_join(acc, 0, 1, dx)
    out = {"loss": loss, "grad_x": dx[None]}
    for k, n in enumerate(BIG):
        outs = adamw(k, 0, 1, sib, res[k])
        out["grad_" + n], out["delta_" + n], out["new_m_" + n], out["new_v_" + n] = outs

    shapes = [full[n].shape for n in SMALL]
    red = dict(zip(SMALL, _unpack(red_pack, shapes)))
    for n in ("conv_w", "lru_conv_w"):
        red[n] = lax.dynamic_slice_in_dim(red[n], me * 128, 128, axis=2)
    oshapes = [args[n].shape for n in SMALL]
    d, m, v = _adamw_small(_pack([args[n] for n in SMALL]), _pack([red[n] for n in SMALL]),
                           _pack([args["m_" + n] for n in SMALL]), _pack([args["v_" + n] for n in SMALL]))
    for n, gg, dd, mm, vv in zip(SMALL, [red[n] for n in SMALL], _unpack(d, oshapes), _unpack(m, oshapes), _unpack(v, oshapes)):
        out["grad_" + n], out["delta_" + n], out["new_m_" + n], out["new_v_" + n] = gg, dd, mm, vv
    return out


def kernel(x, norm_ffn1, ffn1_w_in, ffn1_w_out, norm_mix, mix_w_in, conv_w, fgate_b, lru_conv_w, lru_conv_b, lru_w_a, lru_b_a, lru_w_x, lru_b_x, lru_lambda, mix_out_norm, mix_w_out, norm_ffn2, ffn2_w_in, ffn2_w_out, final_norm, loss_target, m_norm_ffn1, m_ffn1_w_in, m_ffn1_w_out, m_norm_mix, m_mix_w_in, m_conv_w, m_fgate_b, m_lru_conv_w, m_lru_conv_b, m_lru_w_a, m_lru_b_a, m_lru_w_x, m_lru_b_x, m_lru_lambda, m_mix_out_norm, m_mix_w_out, m_norm_ffn2, m_ffn2_w_in, m_ffn2_w_out, m_final_norm, v_norm_ffn1, v_ffn1_w_in, v_ffn1_w_out, v_norm_mix, v_mix_w_in, v_conv_w, v_fgate_b, v_lru_conv_w, v_lru_conv_b, v_lru_w_a, v_lru_b_a, v_lru_w_x, v_lru_b_x, v_lru_lambda, v_mix_out_norm, v_mix_w_out, v_norm_ffn2, v_ffn2_w_in, v_ffn2_w_out, v_final_norm):
    args = dict(locals())
    out = _step(args)
    res = [out["loss"], out["grad_x"]]
    for prefix in ("grad_", "delta_", "new_m_", "new_v_"):
        res += [out[prefix + n] for n in WEIGHTS]
    return tuple(res)
```

```python
import functools
import math

import jax
import jax.numpy as jnp
from jax import lax
from jax.experimental import pallas as pl
from jax.experimental.pallas import tpu as pltpu

F32 = jnp.float32
BF16 = jnp.bfloat16
S = jax.ShapeDtypeStruct
MESH = pl.DeviceIdType.MESH

D = 2048
DC = 512
DA = 1024
NH = 8
HD = 128
DL = 512
LB = 128
DIN = 5640
PW = 5760
C_Q, C_K, C_V = 12, 20, 28
C_GATE, C_LX, C_F = 36, 40, 44
EPS = 1e-6
LRU_C = 8.0
ATT_SCALE = HD ** -0.5
LR, B1, B2, AEPS, WD, STEP = 0.001, 0.9, 0.999, 1e-08, 0.01, 10
VMEM_LIMIT = 56 * 1024 * 1024

NT = (((1,), (1,)), ((), ()))
TN = (((0,), (0,)), ((), ()))
NN = (((1,), (0,)), ((), ()))


def _cp():
    return pltpu.CompilerParams(vmem_limit_bytes=VMEM_LIMIT)


def _bs(shape, fn):
    return pl.BlockSpec(shape, fn)


def _mm(name, a, b, a_spec, b_spec, o_spec, o_shape, grid, dims, nk, acc_tile, scale=1.0, res=None, r_spec=None, dep=None):
    has_res = res is not None
    has_dep = dep is not None

    def body(*refs):
        if has_dep:
            refs = refs[:2 + has_res] + refs[3 + has_res:]
        if has_res:
            a_ref, b_ref, r_ref, o_ref = refs[:4]
            rest = refs[4:]
        else:
            a_ref, b_ref, o_ref = refs[:3]
            rest = refs[3:]
        prod = lax.dot_general(a_ref[...].astype(BF16), b_ref[...].astype(BF16), dims, preferred_element_type=F32)

        def finish(acc):
            if scale != 1.0:
                acc = acc * scale
            if has_res:
                acc = r_ref[...] + acc
            o_ref[...] = acc.astype(o_ref.dtype)

        if nk == 1:
            finish(prod)
        else:
            acc_ref = rest[0]
            k = pl.program_id(2)

            @pl.when(k == 0)
            def _():
                acc_ref[...] = prod

            @pl.when(k > 0)
            def _():
                acc_ref[...] += prod

            @pl.when(k == nk - 1)
            def _():
                finish(acc_ref[...])

    in_specs = [a_spec, b_spec] + ([r_spec] if has_res else []) + ([pl.BlockSpec(memory_space=pl.ANY)] if has_dep else [])
    args = (a, b) + ((res,) if has_res else ()) + ((dep,) if has_dep else ())
    scratch = [pltpu.VMEM(acc_tile, F32)] if nk > 1 else []
    return pl.pallas_call(body, grid=grid, in_specs=in_specs, out_specs=o_spec, out_shape=o_shape,
                          scratch_shapes=scratch, compiler_params=_cp(), name=name)(*args)


def _tile(n, pref):
    for t in pref:
        if n % t == 0:
            return t
    return n


def _rms_fwd(x, gain):
    T = x.shape[0]
    tb = _tile(T, (512,))

    def body(x_ref, g_ref, h_ref, r_ref):
        xv = x_ref[...]
        r = lax.rsqrt(jnp.mean(xv * xv, axis=1, keepdims=True) + EPS)
        h_ref[...] = (xv * r * g_ref[...]).astype(BF16)
        r_ref[...] = r

    return pl.pallas_call(
        body, grid=(T // tb,),
        in_specs=[_bs((tb, D), lambda i: (i, 0)), _bs((1, D), lambda i: (0, 0))],
        out_specs=[_bs((tb, D), lambda i: (i, 0)), _bs((tb, 1), lambda i: (i, 0))],
        out_shape=[S((T, D), BF16), S((T, 1), F32)], compiler_params=_cp(), name="rms_fwd")(x, gain)


def _rms_bwd(dh, x, rstd, gain, dres):
    T = x.shape[0]
    tb = _tile(T, (512,))

    def body(dh_ref, x_ref, r_ref, g_ref, dres_ref, dx_ref, dxb_ref, dg_ref):
        i = pl.program_id(0)
        r = r_ref[...]
        xhat = x_ref[...] * r
        dh = dh_ref[...]
        dxh = dh * g_ref[...]
        m = jnp.mean(dxh * xhat, axis=1, keepdims=True)
        dx = dres_ref[...] + r * (dxh - xhat * m)
        dx_ref[...] = dx
        dxb_ref[...] = dx.astype(BF16)
        part = jnp.sum(dh * xhat, axis=0, keepdims=True)

        @pl.when(i == 0)
        def _():
            dg_ref[...] = part

        @pl.when(i > 0)
        def _():
            dg_ref[...] += part

    row = _bs((tb, D), lambda i: (i, 0))
    return pl.pallas_call(
        body, grid=(T // tb,),
        in_specs=[row, row, _bs((tb, 1), lambda i: (i, 0)), _bs((1, D), lambda i: (0, 0)), row],
        out_specs=[row, row, _bs((1, D), lambda i: (0, 0))],
        out_shape=[S((T, D), F32), S((T, D), BF16), S((1, D), F32)], compiler_params=_cp(), name="rms_bwd")(dh, x, rstd, gain, dres)


def _loss_head(x, gain, tgt):
    T = x.shape[0]
    tb = _tile(T, (512,))

    def body(x_ref, g_ref, t_ref, l_ref, dx_ref, dxb_ref, dg_ref):
        i = pl.program_id(0)
        xv = x_ref[...]
        g = g_ref[...]
        r = lax.rsqrt(jnp.mean(xv * xv, axis=1, keepdims=True) + EPS)
        xhat = xv * r
        e = xhat * g - t_ref[...]
        lpart = 0.5 * jnp.sum(jnp.sum(e * e, axis=1, keepdims=True), axis=0, keepdims=True) * (1.0 / D)
        dy = e * (1.0 / D)
        dxh = dy * g
        m = jnp.mean(dxh * xhat, axis=1, keepdims=True)
        dx = r * (dxh - xhat * m)
        dx_ref[...] = dx
        dxb_ref[...] = dx.astype(BF16)
        gpart = jnp.sum(dy * xhat, axis=0, keepdims=True)
        lrow = jnp.broadcast_to(lpart, (1, 128))

        @pl.when(i == 0)
        def _():
            dg_ref[...] = gpart
            l_ref[...] = lrow

        @pl.when(i > 0)
        def _():
            dg_ref[...] += gpart
            l_ref[...] += lrow

    row = _bs((tb, D), lambda i: (i, 0))
    return pl.pallas_call(
        body, grid=(T // tb,),
        in_specs=[row, _bs((1, D), lambda i: (0, 0)), row],
        out_specs=[_bs((1, 128), lambda i: (0, 0)), row, row, _bs((1, D), lambda i: (0, 0))],
        out_shape=[S((1, 128), F32), S((T, D), F32), S((T, D), BF16), S((1, D), F32)],
        compiler_params=_cp(), name="loss_head")(x, gain, tgt)


def _sigmoid(z):
    return 0.5 * jnp.tanh(0.5 * z) + 0.5


def _ffn_in(h, win):
    T = h.shape[0]
    Fs = win.shape[2]
    F = 2 * Fs
    tn = _tile(Fs, (256, 128))
    nb = Fs // tn
    tm = _tile(T, (1024, 512))

    def body(h_ref, wg_ref, wu_ref, zg_ref, zu_ref, a_ref):
        hv = h_ref[...]
        zg = jnp.dot(hv, wg_ref[...], preferred_element_type=F32)
        zu = jnp.dot(hv, wu_ref[...], preferred_element_type=F32)
        zg_ref[...] = zg.astype(BF16)
        zu_ref[...] = zu.astype(BF16)
        a_ref[...] = (zg * _sigmoid(zg) * zu).astype(BF16)

    col = _bs((tm, tn), lambda i, j: (i, j))
    return pl.pallas_call(
        body, grid=(T // tm, F // tn),
        in_specs=[_bs((tm, D), lambda i, j: (i, 0)),
                  _bs((None, D, tn), lambda i, j: (j // nb, 0, j % nb)),
                  _bs((None, D, tn), lambda i, j: (2 + j // nb, 0, j % nb))],
        out_specs=[col, col, col],
        out_shape=[S((T, F), BF16)] * 3, compiler_params=_cp(), name="ffn_in")(h, win, win)


def _ffn_out(act, wout, x, dep=None):
    T, F = act.shape
    tm = _tile(T, (512,))
    tn = 1024
    return _mm("ffn_out", act, wout,
               _bs((tm, F), lambda j, i, k: (i, 0)), _bs((F, tn), lambda j, i, k: (0, j)),
               _bs((tm, tn), lambda j, i, k: (i, j)), S((T, D), F32), (D // tn, T // tm, 1), NN, 1, (tm, tn),
               scale=0.5, res=x, r_spec=_bs((tm, tn), lambda j, i, k: (i, j)), dep=dep)


def _ffn_bwd_dz(dyb, wout, zg, zu, dep):
    T, F = zg.shape
    tm = _tile(T, (1024, 512))
    tn = _tile(F, (512, 256))

    def body(dy_ref, w_ref, zg_ref, zu_ref, dep_ref, dz_ref):
        da = 0.5 * lax.dot_general(dy_ref[...], w_ref[...], NT, preferred_element_type=F32)
        zg = zg_ref[...].astype(F32)
        zu = zu_ref[...].astype(F32)
        s = _sigmoid(zg)
        dz_ref[0] = (da * zu * (s * (1.0 + zg * (1.0 - s)))).astype(BF16)
        dz_ref[1] = (da * (zg * s)).astype(BF16)

    col = _bs((tm, tn), lambda i, j: (i, j))
    return pl.pallas_call(
        body, grid=(T // tm, F // tn),
        in_specs=[_bs((tm, D), lambda i, j: (i, 0)), _bs((tn, D), lambda i, j: (j, 0)), col, col,
                  pl.BlockSpec(memory_space=pl.ANY)],
        out_specs=_bs((2, tm, tn), lambda i, j: (0, i, j)), out_shape=S((2, T, F), BF16),
        compiler_params=_cp(), name="ffn_bwd_dz")(dyb, wout, zg, zu, dep)


def _ffn_bwd_dh(dz, win):
    _, T, F = dz.shape
    Fs = win.shape[2]
    tk = _tile(Fs, (2816, 1408, 256, 128))
    nkb = Fs // tk
    tm = _tile(T, (1024, 512))
    tn = 512
    nk = 2 * nkb

    def body(dzg_ref, dzu_ref, wg_ref, wu_ref, o_ref, acc_ref):
        k = pl.program_id(2)
        prod = (lax.dot_general(dzg_ref[...], wg_ref[...], NT, preferred_element_type=F32)
                + lax.dot_general(dzu_ref[...], wu_ref[...], NT, preferred_element_type=F32))

        @pl.when(k == 0)
        def _():
            acc_ref[...] = prod

        @pl.when(k > 0)
        def _():
            acc_ref[...] += prod

        @pl.when(k == nk - 1)
        def _():
            o_ref[...] = acc_ref[...]

    def a_spec(half):
        return _bs((None, tm, tk), lambda i, j, k: (half, i, k))

    return pl.pallas_call(
        body, grid=(T // tm, D // tn, nk),
        in_specs=[a_spec(0), a_spec(1),
                  _bs((None, tn, tk), lambda i, j, k: (k // nkb, j, k % nkb)),
                  _bs((None, tn, tk), lambda i, j, k: (2 + k // nkb, j, k % nkb))],
        out_specs=_bs((tm, tn), lambda i, j, k: (i, j)), out_shape=S((T, D), F32),
        scratch_shapes=[pltpu.VMEM((tm, tn), F32)], compiler_params=_cp(), name="ffn_bwd_dh")(dz, dz, win, win)


def _ffn_bwd_dwin(h, dz):
    _, T, F = dz.shape
    Fs = F // 2
    tn = _tile(Fs, (1408, 256, 128))
    nb = Fs // tn
    tm = 512
    return _mm("ffn_bwd_dwin", h, dz,
               _bs((T, tm), lambda i, j, k: (0, i)), _bs((None, T, tn), lambda i, j, k: (j // (2 * nb), 0, j % (2 * nb))),
               _bs((None, tm, tn), lambda i, j, k: (j // nb, i, j % nb)), S((4, D, Fs), BF16),
               (D // tm, 4 * nb, 1), TN, 1, (tm, tn))


def _mm_tn(name, a, b, scale=1.0, tm=512, tn=1024):
    T, M = a.shape
    N = b.shape[1]
    tm = _tile(M, (tm, 512, 256, 128))
    tn = _tile(N, (tn, 1152, 1024, 512, 128))
    return _mm(name, a, b,
               _bs((T, tm), lambda i, j, k: (0, i)), _bs((T, tn), lambda i, j, k: (0, j)),
               _bs((tm, tn), lambda i, j, k: (i, j)), S((M, N), BF16), (M // tm, N // tn, 1), TN, 1, (tm, tn), scale=scale)


def _mm_nt_full(name, a, b, tn, dep=None):
    T, K = a.shape
    N = b.shape[0]
    tm = _tile(T, (1024, 512))
    return _mm(name, a, b,
               _bs((tm, K), lambda i, j, k: (i, 0)), _bs((tn, K), lambda i, j, k: (j, 0)),
               _bs((tm, tn), lambda i, j, k: (i, j)), S((T, N), F32), (T // tm, N // tn, 1), NT, 1, (tm, tn), dep=dep)


def _bt(T):
    return _tile(T, (512,))


def _down(ext, s):
    return pltpu.roll(ext, s, 0)[8:, :]


def _up(ext, s):
    n = ext.shape[0]
    return pltpu.roll(ext, n - s, 0)[: n - 8, :]


def _halo_prev(ref, start, b):
    lo = pl.multiple_of(jnp.maximum(start - 8, 0), 8)
    return ref[pl.ds(lo, 8), :] * (b > 0).astype(F32)


def _halo_next(ref, start, bt, b, nb):
    lo = pl.multiple_of(jnp.minimum(start + bt, (nb - 1) * bt), 8)
    return ref[pl.ds(lo, 8), :] * (b < nb - 1).astype(F32)


def _scan_fwd(A, U):
    n = U.shape[0]
    row = lax.broadcasted_iota(jnp.int32, U.shape, 0)
    d = 1
    while d < n:
        keep = row >= d
        Us = jnp.where(keep, pltpu.roll(U, d, 0), 0.0)
        if A is None:
            U = U + Us
        else:
            As = jnp.where(keep, pltpu.roll(A, d, 0), 1.0)
            U = A * Us + U
            A = A * As
        d *= 2
    return A, U


def _scan_bwd(A, U):
    n = U.shape[0]
    row = lax.broadcasted_iota(jnp.int32, U.shape, 0)
    d = 1
    while d < n:
        keep = row < n - d
        Us = jnp.where(keep, pltpu.roll(U, n - d, 0), 0.0)
        if A is None:
            U = U + Us
        else:
            As = jnp.where(keep, pltpu.roll(A, n - d, 0), 1.0)
            U = A * Us + U
            A = A * As
        d *= 2
    return A, U


def _softplus(z):
    return jnp.maximum(z, 0.0) + jnp.log(1.0 + jnp.exp(-jnp.abs(z)))


def _gelu_parts(g):
    k0 = math.sqrt(2.0 / math.pi)
    t = jnp.tanh(k0 * (g + 0.044715 * g * g * g))
    gel = 0.5 * g * (1.0 + t)
    dgel = 0.5 * (1.0 + t) + 0.5 * g * (1.0 - t * t) * k0 * (1.0 + 3.0 * 0.044715 * g * g)
    return gel, dgel


def _conv_fwd(P, cw):
    T = P.shape[0]
    bt = _bt(T)
    nb = T // bt

    def body(b_ref, c_ref, v_ref, w_ref, y_ref):
        w = w_ref[...]

        def step(b, carry):
            start = pl.multiple_of(b * bt, bt)
            rows = pl.ds(start, bt)
            m = c_ref[rows, :] * v_ref[rows, :]
            ext = jnp.concatenate([_halo_prev(c_ref, start, b) * _halo_prev(v_ref, start, b), m], axis=0)
            z = w[2:3, :] * m + w[1:2, :] * _down(ext, 1) + w[0:1, :] * _down(ext, 2)
            y_ref[rows, :] = b_ref[rows, :] * z
            return carry

        lax.fori_loop(0, nb, step, 0)

    def colspec(off):
        return _bs((T, 128), lambda c: (0, off + c))

    return pl.pallas_call(
        body, grid=(DC // 128,),
        in_specs=[colspec(0), colspec(4), colspec(8), _bs((8, 128), lambda c: (0, c))],
        out_specs=_bs((T, 128), lambda c: (0, c)), out_shape=S((T, DC), F32),
        compiler_params=_cp(), name="conv_fwd")(P, P, P, cw)


def _conv_bwd(P, cw, dy):
    T = P.shape[0]
    bt = _bt(T)
    nb = T // bt

    def body(b_ref, c_ref, v_ref, w_ref, dy_ref, db_ref, dc_ref, dv_ref, dw_ref):
        w = w_ref[...]

        def step(b, carry):
            a0, a1, a2 = carry
            start = pl.multiple_of(b * bt, bt)
            rows = pl.ds(start, bt)
            cb, cc, cv, dy = b_ref[rows, :], c_ref[rows, :], v_ref[rows, :], dy_ref[rows, :]
            m = cc * cv
            ext = jnp.concatenate([_halo_prev(c_ref, start, b) * _halo_prev(v_ref, start, b), m], axis=0)
            m1, m2 = _down(ext, 1), _down(ext, 2)
            z = w[2:3, :] * m + w[1:2, :] * m1 + w[0:1, :] * m2
            db_ref[rows, :] = dy * z
            dz = dy * cb
            extn = jnp.concatenate([dz, _halo_next(dy_ref, start, bt, b, nb) * _halo_next(b_ref, start, bt, b, nb)], axis=0)
            dm = w[2:3, :] * dz + w[1:2, :] * _up(extn, 1) + w[0:1, :] * _up(extn, 2)
            dc_ref[rows, :] = dm * cv
            dv_ref[rows, :] = dm * cc
            return (a0 + jnp.sum(dz * m2, axis=0, keepdims=True),
                    a1 + jnp.sum(dz * m1, axis=0, keepdims=True),
                    a2 + jnp.sum(dz * m, axis=0, keepdims=True))

        zero = jnp.zeros((1, 128), F32)
        a0, a1, a2 = lax.fori_loop(0, nb, step, (zero, zero, zero))
        dw_ref[...] = jnp.zeros((8, 128), F32)
        dw_ref[0:1, :] = a0
        dw_ref[1:2, :] = a1
        dw_ref[2:3, :] = a2

    def colspec(off):
        return _bs((T, 128), lambda c: (0, off + c))

    own = _bs((T, 128), lambda c: (0, c))
    return pl.pallas_call(
        body, grid=(DC // 128,),
        in_specs=[colspec(0), colspec(4), colspec(8), _bs((8, 128), lambda c: (0, c)), own],
        out_specs=[own, own, own, _bs((8, 128), lambda c: (0, c))],
        out_shape=[S((T, DC), F32)] * 3 + [S((8, DC), F32)], compiler_params=_cp(), name="conv_bwd")(P, P, P, cw, dy)


def _fgate_fwd(P, fb):
    T = P.shape[0]
    bt = _bt(T)
    nb = T // bt

    def body(f_ref, b_ref, c_ref):
        bias = b_ref[...]

        def step(b, carry):
            rows = pl.ds(pl.multiple_of(b * bt, bt), bt)
            logf = -_softplus(-(f_ref[rows, :] + bias))
            _, cs = _scan_fwd(None, logf)
            cs = cs + carry
            c_ref[rows, :] = cs
            return cs[bt - 1:bt, :]

        lax.fori_loop(0, nb, step, jnp.zeros((1, 128), F32))

    return pl.pallas_call(
        body, grid=(1,),
        in_specs=[_bs((T, 128), lambda i: (0, C_F)), _bs((1, 128), lambda i: (0, 0))],
        out_specs=_bs((T, 128), lambda i: (0, 0)), out_shape=S((T, 128), F32),
        compiler_params=_cp(), name="fgate_fwd")(P, fb)


def _fgate_bwd(P, fb, dcum):
    T = P.shape[0]
    bt = _bt(T)
    nb = T // bt

    def body(f_ref, b_ref, dc_ref, df_ref, db_ref):
        bias = b_ref[...]

        def step(i, carry):
            run, acc = carry
            b = nb - 1 - i
            rows = pl.ds(pl.multiple_of(b * bt, bt), bt)
            _, rs = _scan_bwd(None, dc_ref[rows, :])
            rs = rs + run
            df = rs * jax.nn.sigmoid(-(f_ref[rows, :] + bias))
            df_ref[rows, :] = df
            return rs[0:1, :], acc + jnp.sum(df, axis=0, keepdims=True)

        zero = jnp.zeros((1, 128), F32)
        _, acc = lax.fori_loop(0, nb, step, (zero, zero))
        db_ref[...] = acc

    return pl.pallas_call(
        body, grid=(1,),
        in_specs=[_bs((T, 128), lambda i: (0, C_F)), _bs((1, 128), lambda i: (0, 0)), _bs((T, 128), lambda i: (0, 0))],
        out_specs=[_bs((T, 128), lambda i: (0, 0)), _bs((1, 128), lambda i: (0, 0))],
        out_shape=[S((T, 128), F32), S((1, 128), F32)], compiler_params=_cp(), name="fgate_bwd")(P, fb, dcum)


def _att_tile(T):
    return _tile(T, (512,))


def _causal_mask(tq):
    return lax.broadcasted_iota(jnp.int32, (tq, tq), 1) <= lax.broadcasted_iota(jnp.int32, (tq, tq), 0)


def _attn_fwd(qkv, cumq, cumk):
    T = qkv.shape[0]
    tq = _att_tile(T)
    nq = T // tq

    def body(q_ref, k_ref, v_ref, cq_ref, ck_ref, o_ref, lse_ref):
        i = pl.program_id(1)
        q = q_ref[...]
        cq = cq_ref[...]

        def block(j, carry, diagonal):
            m_old, l_old, acc = carry
            rows = pl.ds(pl.multiple_of(j * tq, tq), tq)
            s = lax.dot_general(q, k_ref[rows, :], NT, preferred_element_type=F32)
            s = s * ATT_SCALE + cq - ck_ref[j]
            if diagonal:
                s = jnp.where(_causal_mask(tq), s, -jnp.inf)
            m_new = jnp.maximum(m_old, jnp.max(s, axis=1, keepdims=True))
            p = jnp.exp(s - m_new)
            alpha = jnp.exp(m_old - m_new)
            l_new = alpha * l_old + jnp.sum(p, axis=1, keepdims=True)
            acc = alpha * acc + jnp.dot(p.astype(BF16), v_ref[rows, :], preferred_element_type=F32)
            return m_new, l_new, acc

        init = (jnp.full((tq, 1), -jnp.inf, F32), jnp.zeros((tq, 1), F32), jnp.zeros((tq, HD), F32))
        carry = lax.fori_loop(0, i, lambda j, c: block(j, c, False), init)
        m, l, acc = block(i, carry, True)
        o_ref[...] = acc / l
        lse_ref[...] = m + jnp.log(l)

    return pl.pallas_call(
        body, grid=(NH, nq),
        in_specs=[_bs((tq, HD), lambda h, i: (i, h)),
                  _bs((T, HD), lambda h, i: (0, NH + h)),
                  _bs((T, HD), lambda h, i: (0, 2 * NH + h)),
                  _bs((None, tq, 1), lambda h, i: (h, i, 0)),
                  _bs((None, nq, 1, tq), lambda h, i: (h, 0, 0, 0))],
        out_specs=[_bs((tq, HD), lambda h, i: (i, h)), _bs((None, tq, 1), lambda h, i: (h, i, 0))],
        out_shape=[S((T, DA), F32), S((NH, T, 1), F32)],
        compiler_params=_cp(), name="attn_fwd")(qkv, qkv, qkv, cumq, cumk)


def _attn_bwd(qkv, cumq, cumk, lse, o, do):
    T = qkv.shape[0]
    tq = _att_tile(T)
    nq = T // tq

    def body(q_ref, k_ref, v_ref, cq_ref, ck_ref, lse_ref, o_ref, do_ref, dq_ref, dk_ref, dv_ref, dc_ref, dr_ref):
        j = pl.program_id(1)

        @pl.when(j == 0)
        def _():
            dq_ref[...] = jnp.zeros((T, HD), F32)
            dr_ref[...] = jnp.zeros((T, 1), F32)

        k = k_ref[...]
        v = v_ref[...]
        ck = ck_ref[...]

        def block(i, carry, diagonal):
            dk_acc, dv_acc, dc_acc = carry
            rows = pl.ds(pl.multiple_of(i * tq, tq), tq)
            q = q_ref[rows, :]
            do_f = do_ref[rows, :]
            dob = do_f.astype(BF16)
            s = lax.dot_general(q, k, NT, preferred_element_type=F32)
            p = jnp.exp(s * ATT_SCALE + cq_ref[rows, :] - ck - lse_ref[rows, :])
            if diagonal:
                p = jnp.where(_causal_mask(tq), p, 0.0)
            delta = jnp.sum(do_f * o_ref[rows, :], axis=1, keepdims=True)
            dp = lax.dot_general(dob, v, NT, preferred_element_type=F32)
            ds = p * (dp - delta)
            dsb = (ds * ATT_SCALE).astype(BF16)
            dq_ref[rows, :] += jnp.dot(dsb, k, preferred_element_type=F32)
            dr_ref[rows, :] += jnp.sum(ds, axis=1, keepdims=True)
            return (dk_acc + lax.dot_general(dsb, q, TN, preferred_element_type=F32),
                    dv_acc + lax.dot_general(p.astype(BF16), dob, TN, preferred_element_type=F32),
                    dc_acc - jnp.sum(ds, axis=0, keepdims=True))

        init = (jnp.zeros((tq, HD), F32), jnp.zeros((tq, HD), F32), jnp.zeros((1, tq), F32))
        carry = block(j, init, True)
        dk_acc, dv_acc, dc_acc = lax.fori_loop(j + 1, nq, lambda i, c: block(i, c, False), carry)
        dk_ref[...] = dk_acc
        dv_ref[...] = dv_acc
        dc_ref[...] = dc_acc

    def whole(col):
        return _bs((T, HD), lambda h, j: (0, col + h))

    qvec = _bs((None, T, 1), lambda h, j: (h, 0, 0))
    kv_out = _bs((tq, HD), lambda h, j: (j, h))
    return pl.pallas_call(
        body, grid=(NH, nq),
        in_specs=[whole(0), _bs((tq, HD), lambda h, j: (j, NH + h)), _bs((tq, HD), lambda h, j: (j, 2 * NH + h)),
                  qvec, _bs((None, None, 1, tq), lambda h, j: (h, j, 0, 0)), qvec, whole(0), whole(0)],
        out_specs=[whole(0), kv_out, kv_out, _bs((None, 1, tq), lambda h, j: (h, 0, j)), qvec],
        out_shape=[S((T, DA), F32)] * 3 + [S((NH, 1, T), F32), S((NH, T, 1), F32)],
        compiler_params=_cp(), name="attn_bwd")(qkv, qkv, qkv, cumq, cumk, lse, o, do)


def _lru_gates(xr, wa, wx, ba, bx, sp):
    xb = xr.astype(BF16)
    r = jax.nn.sigmoid(jnp.dot(xb, wa, preferred_element_type=F32) + ba)
    ig = jax.nn.sigmoid(jnp.dot(xb, wx, preferred_element_type=F32) + bx)
    log_a = -LRU_C * r * sp
    a = jnp.exp(log_a)
    th = jnp.tanh(log_a)
    om = -2.0 * th / (1.0 - th)
    mult = jnp.sqrt(om)
    return xb, r, ig, a, om, mult


def _lru_xr(lx_ref, cw, cb, start, b, rows):
    lx = lx_ref[rows, :]
    ext = jnp.concatenate([_halo_prev(lx_ref, start, b), lx], axis=0)
    return cw[3:4, :] * lx + cw[2:3, :] * _down(ext, 1) + cw[1:2, :] * _down(ext, 2) + cw[0:1, :] * _down(ext, 3) + cb


def _lru_fwd(P, lcw, vec, wa, wx):
    T = P.shape[0]
    bt = _bt(T)
    nb = T // bt

    def body(g_ref, lx_ref, cw_ref, vec_ref, wa_ref, wx_ref, y_ref, h_ref):
        cw = cw_ref[...]
        vec = vec_ref[...]
        wa = wa_ref[...].astype(BF16)
        wx = wx_ref[...].astype(BF16)
        sp = _softplus(-vec[3:4, :])

        def step(b, carry):
            start = pl.multiple_of(b * bt, bt)
            rows = pl.ds(start, bt)
            xr = _lru_xr(lx_ref, cw, vec[0:1, :], start, b, rows)
            _, _, ig, a, _, mult = _lru_gates(xr, wa, wx, vec[1:2, :], vec[2:3, :], sp)
            u = mult * (ig * xr)
            ac, hc = _scan_fwd(a, u)
            hb = hc + ac * carry
            h_ref[rows, :] = hb
            gel, _ = _gelu_parts(g_ref[rows, :])
            y_ref[rows, :] = gel * hb
            return hb[bt - 1:bt, :]

        lax.fori_loop(0, nb, step, jnp.zeros((1, 128), F32))

    own = _bs((T, 128), lambda c: (0, c))
    return pl.pallas_call(
        body, grid=(DL // 128,),
        in_specs=[_bs((T, 128), lambda c: (0, C_GATE + c)), _bs((T, 128), lambda c: (0, C_LX + c)),
                  _bs((8, 128), lambda c: (0, c)), _bs((8, 128), lambda c: (0, c)),
                  _bs((None, LB, LB), lambda c: (c, 0, 0)), _bs((None, LB, LB), lambda c: (c, 0, 0))],
        out_specs=[own, own], out_shape=[S((T, DL), F32)] * 2, compiler_params=_cp(), name="lru_fwd")(P, P, lcw, vec, wa, wx)


def _lru_bwd(P, lcw, vec, wa, wx, hst, dy):
    T = P.shape[0]
    bt = _bt(T)
    nb = T // bt

    def body(g_ref, lx_ref, cw_ref, vec_ref, wa_ref, wx_ref, h_ref, dy_ref,
             dg_ref, dlx_ref, sm_ref, dwa_ref, dwx_ref, dxr_s):
        cw = cw_ref[...]
        vec = vec_ref[...]
        wa = wa_ref[...].astype(BF16)
        wx = wx_ref[...].astype(BF16)
        lam = vec[3:4, :]
        sp = _softplus(-lam)
        dwa_ref[...] = jnp.zeros((LB, LB), F32)
        dwx_ref[...] = jnp.zeros((LB, LB), F32)
        zero = jnp.zeros((1, 128), F32)

        def step1(i, carry):
            wc, s_cb, s_ba, s_bx, s_sp = carry
            b = nb - 1 - i
            start = pl.multiple_of(b * bt, bt)
            rows = pl.ds(start, bt)
            xr = _lru_xr(lx_ref, cw, vec[0:1, :], start, b, rows)
            xb, r, ig, a, om, mult = _lru_gates(xr, wa, wx, vec[1:2, :], vec[2:3, :], sp)
            hb = h_ref[rows, :]
            dy = dy_ref[rows, :]
            gel, dgel = _gelu_parts(g_ref[rows, :])
            dg_ref[rows, :] = dy * hb * dgel
            dh = dy * gel
            ac, wcum = _scan_bwd(a, a * dh)
            w = wcum + ac * wc
            g = dh + _up(jnp.concatenate([w, jnp.broadcast_to(wc, (8, 128))], axis=0), 1)
            hprev = _down(jnp.concatenate([_halo_prev(h_ref, start, b), hb], axis=0), 1)
            da = g * hprev
            dmult = g * (ig * xr)
            dix = g * mult
            di = dix * xr
            dlog_a = da * a - dmult * ((1.0 - om) / mult)
            dr = dlog_a * (-LRU_C * sp)
            dpr = dr * r * (1.0 - r)
            dpi = di * ig * (1.0 - ig)
            dprb, dpib = dpr.astype(BF16), dpi.astype(BF16)
            dwa_ref[...] += lax.dot_general(xb, dprb, TN, preferred_element_type=F32)
            dwx_ref[...] += lax.dot_general(xb, dpib, TN, preferred_element_type=F32)
            dxr = (dix * ig + lax.dot_general(dprb, wa, NT, preferred_element_type=F32)
                   + lax.dot_general(dpib, wx, NT, preferred_element_type=F32))
            dxr_s[rows, :] = dxr
            return (w[0:1, :], s_cb + jnp.sum(dxr, axis=0, keepdims=True), s_ba + jnp.sum(dpr, axis=0, keepdims=True),
                    s_bx + jnp.sum(dpi, axis=0, keepdims=True), s_sp + jnp.sum(dlog_a * (-LRU_C * r), axis=0, keepdims=True))

        _, s_cb, s_ba, s_bx, s_sp = lax.fori_loop(0, nb, step1, (zero, zero, zero, zero, zero))

        def step2(b, carry):
            t0, t1, t2, t3 = carry
            start = pl.multiple_of(b * bt, bt)
            rows = pl.ds(start, bt)
            dxr = dxr_s[rows, :]
            extn = jnp.concatenate([dxr, _halo_next(dxr_s, start, bt, b, nb)], axis=0)
            dlx_ref[rows, :] = (cw[3:4, :] * dxr + cw[2:3, :] * _up(extn, 1) + cw[1:2, :] * _up(extn, 2)
                                + cw[0:1, :] * _up(extn, 3))
            lx = lx_ref[rows, :]
            ext = jnp.concatenate([_halo_prev(lx_ref, start, b), lx], axis=0)
            return (t0 + jnp.sum(dxr * _down(ext, 3), axis=0, keepdims=True),
                    t1 + jnp.sum(dxr * _down(ext, 2), axis=0, keepdims=True),
                    t2 + jnp.sum(dxr * _down(ext, 1), axis=0, keepdims=True),
                    t3 + jnp.sum(dxr * lx, axis=0, keepdims=True))

        t0, t1, t2, t3 = lax.fori_loop(0, nb, step2, (zero, zero, zero, zero))
        sm_ref[...] = jnp.zeros((16, 128), F32)
        for k, val in enumerate((t0, t1, t2, t3, s_cb, s_ba, s_bx, -s_sp * jax.nn.sigmoid(-lam))):
            sm_ref[k:k + 1, :] = val

    own = _bs((T, 128), lambda c: (0, c))
    wspec = _bs((None, LB, LB), lambda c: (c, 0, 0))
    return pl.pallas_call(
        body, grid=(DL // 128,),
        in_specs=[_bs((T, 128), lambda c: (0, C_GATE + c)), _bs((T, 128), lambda c: (0, C_LX + c)),
                  _bs((8, 128), lambda c: (0, c)), _bs((8, 128), lambda c: (0, c)), wspec, wspec, own, own],
        out_specs=[own, own, _bs((16, 128), lambda c: (0, c)), wspec, wspec],
        out_shape=[S((T, DL), F32)] * 2 + [S((16, DL), F32), S((4, LB, LB), F32), S((4, LB, LB), F32)],
        scratch_shapes=[pltpu.VMEM((T, 128), F32)], compiler_params=_cp(), name="lru_bwd")(P, P, lcw, vec, wa, wx, hst, dy)


_GROUPS = ((0, DC), (DC, DC + DA), (DC + DA, D))


def _gnorm_fwd(yc, ya, yl, gain):
    T = yc.shape[0]
    tb = _tile(T, (512,))

    def body(c_ref, a_ref, l_ref, g_ref, yn_ref, r0_ref, r1_ref, r2_ref):
        for (lo, hi), src, r_ref in zip(_GROUPS, (c_ref, a_ref, l_ref), (r0_ref, r1_ref, r2_ref)):
            yv = src[...]
            r = lax.rsqrt(jnp.mean(yv * yv, axis=1, keepdims=True) + EPS)
            yn_ref[:, lo:hi] = (yv * r * g_ref[:, lo:hi]).astype(BF16)
            r_ref[...] = r

    rs = _bs((tb, 1), lambda i: (i, 0))
    return pl.pallas_call(
        body, grid=(T // tb,),
        in_specs=[_bs((tb, DC), lambda i: (i, 0)), _bs((tb, DA), lambda i: (i, 0)), _bs((tb, DL), lambda i: (i, 0)),
                  _bs((1, D), lambda i: (0, 0))],
        out_specs=[_bs((tb, D), lambda i: (i, 0)), rs, rs, rs],
        out_shape=[S((T, D), BF16)] + [S((T, 1), F32)] * 3, compiler_params=_cp(), name="gnorm_fwd")(yc, ya, yl, gain)


def _gnorm_bwd(dyn, yc, ya, yl, r0, r1, r2, gain):
    T = yc.shape[0]
    tb = _tile(T, (512,))

    def body(d_ref, c_ref, a_ref, l_ref, r0_ref, r1_ref, r2_ref, g_ref, dc_ref, da_ref, dl_ref, dg_ref):
        i = pl.program_id(0)
        for (lo, hi), src, r_ref, dst in zip(_GROUPS, (c_ref, a_ref, l_ref), (r0_ref, r1_ref, r2_ref), (dc_ref, da_ref, dl_ref)):
            r = r_ref[...]
            yhat = src[...] * r
            dy = d_ref[:, lo:hi]
            dyh = dy * g_ref[:, lo:hi]
            m = jnp.mean(dyh * yhat, axis=1, keepdims=True)
            dst[...] = r * (dyh - yhat * m)
            part = jnp.sum(dy * yhat, axis=0, keepdims=True)

            @pl.when(i == 0)
            def _():
                dg_ref[:, lo:hi] = part

            @pl.when(i > 0)
            def _():
                dg_ref[:, lo:hi] += part

    rs = _bs((tb, 1), lambda i: (i, 0))
    specs = [_bs((tb, DC), lambda i: (i, 0)), _bs((tb, DA), lambda i: (i, 0)), _bs((tb, DL), lambda i: (i, 0))]
    return pl.pallas_call(
        body, grid=(T // tb,),
        in_specs=[_bs((tb, D), lambda i: (i, 0))] + specs + [rs, rs, rs, _bs((1, D), lambda i: (0, 0))],
        out_specs=specs + [_bs((1, D), lambda i: (0, 0))],
        out_shape=[S((T, DC), F32), S((T, DA), F32), S((T, DL), F32), S((1, D), F32)],
        compiler_params=_cp(), name="gnorm_bwd")(dyn, yc, ya, yl, r0, r1, r2, gain)


HBM = pl.BlockSpec(memory_space=pltpu.HBM)
N_BIG = 6


def _place():
    x, y, c = lax.axis_index("x"), lax.axis_index("y"), lax.axis_index("c")
    return x, y, c, 2 * x + y


def _peer(x, y, j):
    return x ^ ((j + 1) >> 1), y ^ ((j + 1) & 1)


SEM = pl.BlockSpec(memory_space=pltpu.SEMAPHORE)
ANY = pl.BlockSpec(memory_space=pl.ANY)
VM = pl.BlockSpec(memory_space=pltpu.VMEM)
EFFECT = pltpu.SideEffectType.DATAFLOW_SIDE_EFFECTING
N_AG = N_BIG + 1


def _hbm(a):
    return pltpu.with_memory_space_constraint(a, pltpu.HBM)


AG_ORDER = (0, 1, N_BIG, 2, 3, 4, 5)
AG_REST = (N_BIG, 2, 3, 4, 5)


def _ag_copy(src, land, ssem, rsem, t, j, chip):
    x, y, c, _ = _place()
    px, py = _peer(x, y, j)
    if t == N_BIG:
        s_ref, d_ref = src, land.at[chip]
    else:
        rh = src.shape[0] // 2
        half = pl.ds(c * rh, rh)
        s_ref, d_ref = src.at[half], land.at[chip, half]
    return pltpu.make_async_remote_copy(src_ref=s_ref, dst_ref=d_ref, send_sem=ssem.at[3 * t + j], recv_sem=rsem.at[3 * t + j],
                                        device_id=(px, py, c), device_id_type=MESH)


def _ag_start(l, srcs, dep):
    n = N_AG

    def body(*refs):
        src = refs[:n]
        ssem, rsem = refs[2 * n + 1], refs[2 * n + 2]
        land = refs[3 * n + 3:4 * n + 3]
        token = refs[4 * n + 3]
        _, _, _, me = _place()
        for t in AG_ORDER:
            for j in range(3):
                _ag_copy(src[t], land[t], ssem, rsem, t, j, me).start()
        token[...] = jnp.zeros_like(token)

    lands = [lax.empty((4,) + a.shape, a.dtype) for a in srcs]
    dma = pltpu.SemaphoreType.DMA
    outs = pl.pallas_call(
        body, name=f"ag_start_{l}",
        out_shape=(dma((3 * n,)), dma((3 * n,))) + tuple(pltpu.HBM(a.shape, a.dtype) for a in list(srcs) + lands) + (S((8, 128), F32),),
        in_specs=[HBM] * (2 * n) + [ANY], out_specs=(SEM, SEM) + (HBM,) * (2 * n) + (VM,),
        input_output_aliases={i: 2 + i for i in range(2 * n)},
        compiler_params=pltpu.CompilerParams(has_side_effects=EFFECT),
    )(*[_hbm(a) for a in srcs], *[_hbm(a) for a in lands], dep)
    return outs[0], outs[1], outs[2:2 + n], outs[2 + n:2 + 2 * n], outs[-1]


def _ag_wait(name, idx, ssem, rsem, srcs, lands, after):
    n = len(idx)

    def body(*refs):
        src, land = refs[:n], refs[n:2 * n]
        ssem, rsem = refs[2 * n], refs[2 * n + 1]
        x, y, _, _ = _place()
        for p, t in enumerate(idx):
            for j in range(3):
                px, py = _peer(x, y, j)
                cp = _ag_copy(src[p], land[p], ssem, rsem, t, j, 2 * px + py)
                cp.wait_send()
                cp.wait_recv()

    outs = pl.pallas_call(
        body, name=name,
        out_shape=tuple(pltpu.HBM(a.shape, a.dtype) for a in list(srcs) + list(lands)),
        in_specs=[HBM] * (2 * n) + [SEM, SEM, ANY], out_specs=(HBM,) * (2 * n),
        input_output_aliases={i: i for i in range(2 * n)},
        compiler_params=pltpu.CompilerParams(has_side_effects=EFFECT),
    )(*srcs, *lands, ssem, rsem, after)
    return outs[:n], outs[n:]


def _fwd_copies(idx, src, land, ssem, rsem, recv_side):
    x, y, c, me = _place()
    cps = []
    for p, t in enumerate(idx):
        cps.append(pltpu.make_async_remote_copy(src_ref=src[p], dst_ref=land[p].at[me], send_sem=ssem.at[4 * p], recv_sem=rsem.at[4 * p],
                                                device_id=(x, y, 1 - c), device_id_type=MESH))
        if t == N_BIG:
            continue
        rh = src[p].shape[0] // 2
        for j in range(3):
            px, py = _peer(x, y, j)
            part = land[p].at[2 * px + py, pl.ds(((1 - c) if recv_side else c) * rh, rh)]
            cps.append(pltpu.make_async_remote_copy(src_ref=part, dst_ref=part, send_sem=ssem.at[4 * p + 1 + j],
                                                    recv_sem=rsem.at[4 * p + 1 + j], device_id=(x, y, 1 - c), device_id_type=MESH))
    return cps


def _ag_fwd_start(name, idx, srcs, lands, dep):
    n = len(idx)

    def body(*refs):
        src = refs[:n]
        ssem, rsem = refs[2 * n + 1], refs[2 * n + 2]
        land = refs[3 * n + 3:4 * n + 3]
        token = refs[4 * n + 3]
        for cp in _fwd_copies(idx, src, land, ssem, rsem, False):
            cp.start()
        token[...] = jnp.zeros_like(token)

    dma = pltpu.SemaphoreType.DMA
    outs = pl.pallas_call(
        body, name=name,
        out_shape=(dma((4 * n,)), dma((4 * n,))) + tuple(pltpu.HBM(a.shape, a.dtype) for a in list(srcs) + list(lands)) + (S((8, 128), F32),),
        in_specs=[HBM] * (2 * n) + [ANY], out_specs=(SEM, SEM) + (HBM,) * (2 * n) + (VM,),
        input_output_aliases={i: 2 + i for i in range(2 * n)},
        compiler_params=pltpu.CompilerParams(has_side_effects=EFFECT),
    )(*srcs, *lands, dep)
    return outs[0], outs[1], outs[2:2 + n], outs[2 + n:2 + 2 * n], outs[-1]


def _ag_fwd_wait(name, idx, ssem, rsem, srcs, lands, after):
    n = len(idx)

    def body(*refs):
        src, land = refs[:n], refs[n:2 * n]
        ssem, rsem = refs[2 * n], refs[2 * n + 1]
        for mine, theirs in zip(_fwd_copies(idx, src, land, ssem, rsem, False), _fwd_copies(idx, src, land, ssem, rsem, True)):
            mine.wait_send()
            theirs.wait_recv()

    outs = pl.pallas_call(
        body, name=name,
        out_shape=tuple(pltpu.HBM(a.shape, a.dtype) for a in list(srcs) + list(lands)),
        in_specs=[HBM] * (2 * n) + [SEM, SEM, ANY], out_specs=(HBM,) * (2 * n),
        input_output_aliases={i: i for i in range(2 * n)},
        compiler_params=pltpu.CompilerParams(has_side_effects=EFFECT),
    )(*srcs, *lands, ssem, rsem, after)
    return outs[n:]


def _pair_copy(g, land, ssem, rsem, t):
    x, y, c, _ = _place()
    rh = g.shape[1] // 2
    return pltpu.make_async_remote_copy(src_ref=g.at[:, pl.ds((1 - c) * rh, rh), :], dst_ref=land,
                                        send_sem=ssem.at[t], recv_sem=rsem.at[t], device_id=(x, y, 1 - c), device_id_type=MESH)


def _rs_pair(grads):
    n = len(grads)

    def body(*refs):
        g, out = refs[:n], refs[n:2 * n]
        ssem, rsem = refs[2 * n:]
        for t in range(n):
            _pair_copy(g[t], out[t], ssem, rsem, t).start()
        for t in range(n):
            _pair_copy(g[t], out[t], ssem, rsem, t).wait()

    dma = pltpu.SemaphoreType.DMA
    return pl.pallas_call(
        body, in_specs=[HBM] * n, out_specs=[HBM] * n,
        out_shape=[S((4, g.shape[1] // 2, g.shape[2]), g.dtype) for g in grads],
        scratch_shapes=[dma((n,)), dma((n,))], name="rs_pair")(*grads)


def _rs_pair_start(name, grads, dep):
    n = len(grads)

    def body(*refs):
        g = refs[:n]
        ssem, rsem = refs[2 * n + 1], refs[2 * n + 2]
        land = refs[3 * n + 3:4 * n + 3]
        token = refs[4 * n + 3]
        for t in range(n):
            _pair_copy(g[t], land[t], ssem, rsem, t).start()
        token[...] = jnp.zeros_like(token)

    lands = [lax.empty((4, a.shape[1] // 2, a.shape[2]), a.dtype) for a in grads]
    dma = pltpu.SemaphoreType.DMA
    outs = pl.pallas_call(
        body, name=name,
        out_shape=(dma((n,)), dma((n,))) + tuple(pltpu.HBM(a.shape, a.dtype) for a in list(grads) + lands) + (S((8, 128), F32),),
        in_specs=[HBM] * (2 * n) + [ANY], out_specs=(SEM, SEM) + (HBM,) * (2 * n) + (VM,),
        input_output_aliases={i: 2 + i for i in range(2 * n)},
        compiler_params=pltpu.CompilerParams(has_side_effects=EFFECT),
    )(*[_hbm(a) for a in grads], *[_hbm(a) for a in lands], dep)
    return outs[0], outs[1], outs[2:2 + n], outs[2 + n:2 + 2 * n], outs[-1]


def _rs_pair_wait(name, ssem, rsem, grads, lands, after):
    n = len(grads)

    def body(*refs):
        g, land = refs[:n], refs[n:2 * n]
        ssem, rsem = refs[2 * n], refs[2 * n + 1]
        for t in range(n):
            cp = _pair_copy(g[t], land[t], ssem, rsem, t)
            cp.wait_send()
            cp.wait_recv()

    outs = pl.pallas_call(
        body, name=name,
        out_shape=tuple(pltpu.HBM(a.shape, a.dtype) for a in list(grads) + list(lands)),
        in_specs=[HBM] * (2 * n) + [SEM, SEM, ANY], out_specs=(HBM,) * (2 * n),
        input_output_aliases={i: i for i in range(2 * n)},
        compiler_params=pltpu.CompilerParams(has_side_effects=EFFECT),
    )(*grads, *lands, ssem, rsem, after)
    return outs[:n], outs[n:]


def _rs_copy(s, land, ssem, rsem, t, j):
    x, y, c, _ = _place()
    px, py = _peer(x, y, j)
    return pltpu.make_async_remote_copy(src_ref=s[t].at[2 * px + py], dst_ref=land[t].at[j],
                                        send_sem=ssem.at[3 * t + j], recv_sem=rsem.at[3 * t + j], device_id=(px, py, c), device_id_type=MESH)


def _rs_start(name, sums, dep):
    n = len(sums)

    def body(*refs):
        s = refs[:n]
        ssem, rsem = refs[2 * n + 1], refs[2 * n + 2]
        land = refs[3 * n + 3:4 * n + 3]
        token = refs[4 * n + 3]
        for t in range(n):
            for j in range(3):
                _rs_copy(s, land, ssem, rsem, t, j).start()
        token[...] = jnp.zeros_like(token)

    lands = [lax.empty((3,) + a.shape[1:], a.dtype) for a in sums]
    dma = pltpu.SemaphoreType.DMA
    outs = pl.pallas_call(
        body, name=name,
        out_shape=(dma((3 * n,)), dma((3 * n,))) + tuple(pltpu.HBM(a.shape, a.dtype) for a in list(sums) + lands) + (S((8, 128), F32),),
        in_specs=[HBM] * (2 * n) + [ANY], out_specs=(SEM, SEM) + (HBM,) * (2 * n) + (VM,),
        input_output_aliases={i: 2 + i for i in range(2 * n)},
        compiler_params=pltpu.CompilerParams(has_side_effects=EFFECT),
    )(*[_hbm(a) for a in sums], *[_hbm(a) for a in lands], dep)
    return outs[0], outs[1], outs[2:2 + n], outs[2 + n:2 + 2 * n], outs[-1]


def _rs_wait(name, ssem, rsem, sums, lands, after):
    n = len(sums)

    def body(*refs):
        s, land = refs[:n], refs[n:2 * n]
        ssem, rsem = refs[2 * n], refs[2 * n + 1]
        for t in range(n):
            for j in range(3):
                cp = _rs_copy(s, land, ssem, rsem, t, j)
                cp.wait_send()
                cp.wait_recv()

    outs = pl.pallas_call(
        body, name=name,
        out_shape=tuple(pltpu.HBM(a.shape, a.dtype) for a in list(sums) + list(lands)),
        in_specs=[HBM] * (2 * n) + [SEM, SEM, ANY], out_specs=(HBM,) * (2 * n),
        input_output_aliases={i: i for i in range(2 * n)},
        compiler_params=pltpu.CompilerParams(has_side_effects=EFFECT),
    )(*sums, *lands, ssem, rsem, after)
    return outs[n:]


def _rs_join(halves, lo, hi, dep):
    def body(*refs):
        h = refs[:N_BIG]
        out = refs[N_BIG + 1:2 * N_BIG + 1]
        ssem, rsem = refs[2 * N_BIG + 1:]
        x, y, c, _ = _place()

        def cp(t):
            return pltpu.make_async_remote_copy(
                src_ref=h[t].at[pl.ds(lo, hi - lo)], dst_ref=out[t], send_sem=ssem.at[t], recv_sem=rsem.at[t],
                device_id=(x, y, 1 - c), device_id_type=MESH)

        for t in range(N_BIG):
            cp(t).start()
        for t in range(N_BIG):
            cp(t).wait()

    dma = pltpu.SemaphoreType.DMA
    return pl.pallas_call(
        body, in_specs=[HBM] * N_BIG + [ANY], out_specs=[HBM] * N_BIG,
        out_shape=[S((hi - lo,) + h.shape[1:], h.dtype) for h in halves],
        scratch_shapes=[dma((N_BIG,)), dma((N_BIG,))], name="rs_join")(*halves, dep)


def _all_reduce_small(pack, dep):
    R = pack.shape[0]
    rb = _tile(R, (512, 256, 128, 8))

    def body(x_ref, dep_ref, all_ref, sum_ref, send_sems, recv_sems, local_sem):
        x, y, c = lax.axis_index("x"), lax.axis_index("y"), lax.axis_index("c")
        me, sibling = (x, y, c), (x, y, 1 - c)
        chips = [(1 - x, y), (x, 1 - y), (1 - x, 1 - y)]

        def rows(px, py, pc):
            return all_ref.at[pl.ds((4 * px + 2 * py + pc) * R, R), :]

        def copy(k, block, to, src=None):
            return pltpu.make_async_remote_copy(
                src_ref=rows(*block) if src is None else src, dst_ref=rows(*block),
                send_sem=send_sems.at[k], recv_sem=recv_sems.at[k], device_id=to, device_id_type=MESH)

        mine = pltpu.make_async_copy(x_ref, rows(*me), local_sem)
        mine.start()
        first = [copy(0, me, sibling, src=x_ref)]
        first += [copy(1 + j, me, (*chip, c), src=x_ref) for j, chip in enumerate(chips)]
        for cp in first:
            cp.start()
        passed = [copy(4 + j, (*chip, c), sibling) for j, chip in enumerate(chips)]
        for j, chip in enumerate(chips):
            copy(1 + j, (*chip, c), me).wait_recv()
            passed[j].start()
        copy(0, sibling, me).wait_recv()
        for j, chip in enumerate(chips):
            copy(4 + j, (*chip, 1 - c), me).wait_recv()
        for cp in first + passed:
            cp.wait_send()
        mine.wait()

        def step(b, carry):
            off = pl.multiple_of(b * rb, rb)
            acc = all_ref[pl.ds(off, rb), :]
            for k in range(1, 8):
                acc = acc + all_ref[pl.ds(pl.multiple_of(k * R + off, 8), rb), :]
            sum_ref[pl.ds(off, rb), :] = acc
            return carry

        lax.fori_loop(0, R // rb, step, 0)

    vm = pl.BlockSpec(memory_space=pltpu.VMEM)
    dma = pltpu.SemaphoreType.DMA
    _, total = pl.pallas_call(
        body, in_specs=[vm, pl.BlockSpec(memory_space=pl.ANY)], out_specs=[vm, vm],
        out_shape=[S((8 * R, 128), F32), S((R, 128), F32)],
        scratch_shapes=[dma((7,)), dma((7,)), dma],
        compiler_params=_cp(), name="allreduce_small")(pack, dep)
    return total


def _row_tile(rh, cc, tile_bytes=3 * 1024 * 1024 // 2):
    for t in (512, 256, 128, 64, 32, 16):
        if rh % t == 0 and t * cc * 4 <= tile_bytes:
            return t
    return 16


def _my_chip():
    return 2 * lax.axis_index("x") + lax.axis_index("y")


def _pair_sum(g, recv):
    _, r, cc = g.shape
    rh = r // 2
    tb = _row_tile(rh, cc)
    nbh = rh // tb

    def body(g_ref, r_ref, o_ref):
        o_ref[...] = (g_ref[...].astype(F32) + r_ref[...].astype(F32)).astype(BF16)

    def chip(k):
        return (_my_chip() + 1 + k) % 4

    mine = _bs((None, tb, cc), lambda k, i: (chip(k), lax.axis_index("c") * nbh + i, 0))
    plain = _bs((None, tb, cc), lambda k, i: (chip(k), i, 0))
    return pl.pallas_call(body, grid=(3, nbh), in_specs=[mine, plain], out_specs=plain,
                          out_shape=S((4, rh, cc), BF16), compiler_params=_cp(), name="rs_pair_sum")(g, recv)


def _owner_sum(g, recv, ici, acc, l):
    _, r, cc = g.shape
    rh = r // 2
    tb = _row_tile(rh, cc)
    nbh = rh // tb

    def body(g_ref, r_ref, i0_ref, i1_ref, i2_ref, acc_ref, o_ref):
        s = g_ref[...].astype(F32) + r_ref[...].astype(F32)
        o_ref[...] = s + i0_ref[...].astype(F32) + i1_ref[...].astype(F32) + i2_ref[...].astype(F32)

    def slot(j):
        return _bs((None, tb, cc), lambda i: (j, i, 0))

    return pl.pallas_call(
        body, grid=(nbh,),
        in_specs=[_bs((None, tb, cc), lambda i: (_my_chip(), lax.axis_index("c") * nbh + i, 0)),
                  _bs((None, tb, cc), lambda i: (_my_chip(), i, 0)),
                  slot(0), slot(1), slot(2), pl.BlockSpec(memory_space=pl.ANY)],
        out_specs=_bs((None, tb, cc), lambda i: (l, i, 0)),
        out_shape=S(acc.shape, F32), input_output_aliases={5: 0},
        compiler_params=_cp(), name="rs_owner_sum")(g, recv, ici, ici, ici, acc)


def _adam_math(w, g, m, v):
    m = B1 * m + (1.0 - B1) * g
    v = B2 * v + (1.0 - B2) * (g * g)
    m_hat = m / (1.0 - B1 ** STEP)
    v_hat = v / (1.0 - B2 ** STEP)
    delta = -LR * (m_hat / (jnp.sqrt(v_hat) + AEPS) + WD * w)
    return delta, m, v


def _adamw_big(w, g_mine, g_sib, m, v, lo, hi, prev):
    L, r, cc = w.shape
    rh = r // 2
    tb = _row_tile(rh, cc)
    nbh = rh // tb

    def body(w_ref, gm_ref, gs_ref, m_ref, v_ref, *rest):
        go_ref, d_ref, mo_ref, vo_ref = rest[-4:]
        mine = pl.program_id(1) == lax.axis_index("c")
        g = jnp.where(mine, gm_ref[...], gs_ref[...])
        d, m, v = _adam_math(w_ref[...], g, m_ref[...], v_ref[...])
        go_ref[...] = g
        d_ref[...] = d
        mo_ref[...] = m
        vo_ref[...] = v

    def mine_map(l, hf, i):
        c = lax.axis_index("c")
        return (l + lo, jnp.where(hf == c, i, jnp.where(c == 0, nbh - 1, 0)), 0)

    def sib_map(l, hf, i):
        c = lax.axis_index("c")
        return (l, jnp.where(hf != c, i, jnp.where(c == 0, 0, nbh - 1)), 0)

    full = _bs((None, tb, cc), lambda l, hf, i: (l + lo, hf * nbh + i, 0))
    extra = [] if prev is None else list(prev)
    return pl.pallas_call(
        body, grid=(hi - lo, 2, nbh),
        in_specs=[full, _bs((None, tb, cc), mine_map), _bs((None, tb, cc), sib_map), full, full]
        + [pl.BlockSpec(memory_space=pl.ANY)] * len(extra),
        out_specs=[full] * 4, out_shape=[S(w.shape, F32)] * 4,
        input_output_aliases={5 + k: k for k in range(len(extra))},
        compiler_params=_cp(), name="adamw_big")(w, g_mine, g_sib, m, v, *extra)


def _adamw_small(w, g, m, v):
    R = w.shape[0]
    tb = _tile(R, (512, 256, 128, 8))

    def body(w_ref, g_ref, m_ref, v_ref, d_ref, mo_ref, vo_ref):
        d, m, v = _adam_math(w_ref[...], g_ref[...], m_ref[...], v_ref[...])
        d_ref[...] = d
        mo_ref[...] = m
        vo_ref[...] = v

    spec = _bs((tb, 128), lambda i: (i, 0))
    return pl.pallas_call(body, grid=(R // tb,), in_specs=[spec] * 4, out_specs=[spec] * 3,
                          out_shape=[S((R, 128), F32)] * 3, compiler_params=_cp(), name="adamw_small")(w, g, m, v)


def _mix_pad(w):
    return jnp.concatenate([w[:, :4608], w[:, 4616:DIN], w[:, 4608:4616], jnp.zeros((D, PW - DIN), w.dtype)], axis=1)


def _mix_unpad(g):
    return jnp.concatenate([g[:, :4608], g[:, 5632:5640], g[:, 4608:5632]], axis=1)


def _pack(parts):
    flat = jnp.concatenate([p.reshape(-1).astype(F32) for p in parts])
    n = flat.shape[0]
    total = -(-n // (512 * 128)) * (512 * 128)
    return jnp.pad(flat, (0, total - n)).reshape(total // 128, 128)


def _unpack(pack, shapes):
    flat = pack.reshape(-1)
    out, off = [], 0
    for s in shapes:
        n = math.prod(s)
        out.append(flat[off:off + n].reshape(s))
        off += n
    return out


def _ffn_backward(dy, dyb, saved, gain, win, wout, dep):
    x, h, rstd, zg, zu, act = saved
    dz = _ffn_bwd_dz(dyb, wout, zg, zu, dep)
    dwout = _mm_tn("ffn_bwd_dwout", act, dyb, scale=0.5, tm=512, tn=1024)
    dwin = _ffn_bwd_dwin(h, dz)
    dh = _ffn_bwd_dh(dz, win)
    dx, dxb, dgain = _rms_bwd(dh, x, rstd, gain, dy)
    return dx, dxb, dgain, dwin, dwout


def _mixer_forward(x, p):
    T = x.shape[0]
    h, rstd = _rms_fwd(x, p["norm_mix"])
    tm = _tile(T, (1024, 512))
    tn = 1152
    P = _mm("mix_in", h, p["wmix"],
            _bs((tm, D), lambda i, j, k: (i, 0)), _bs((D, tn), lambda i, j, k: (0, j)),
            _bs((tm, tn), lambda i, j, k: (i, j)), S((T, PW), F32), (T // tm, PW // tn, 1), NN, 1, (tm, tn))
    yc = _conv_fwd(P, p["cw"])
    cum = _fgate_fwd(P, p["fb"])
    cumt = cum[:, :NH].T
    tq = _att_tile(T)
    cumq, cumk = cumt.reshape(NH, T, 1), cumt.reshape(NH, T // tq, 1, tq)
    qkv = P[:, C_Q * 128:C_GATE * 128].astype(BF16)
    ya, lse = _attn_fwd(qkv, cumq, cumk)
    yl, hst = _lru_fwd(P, p["lcw"], p["lvec"], p["lru_w_a"], p["lru_w_x"])
    yn, r0, r1, r2 = _gnorm_fwd(yc, ya, yl, p["mix_out_norm"])
    y = _mm("mix_out", yn, p["wo"],
            _bs((tm, D), lambda i, j, k: (i, 0)), _bs((D, 1024), lambda i, j, k: (0, j)),
            _bs((tm, 1024), lambda i, j, k: (i, j)), S((T, D), F32), (T // tm, D // 1024, 1), NN, 1, (tm, 1024),
            res=x, r_spec=_bs((tm, 1024), lambda i, j, k: (i, j)))
    return y, (x, h, rstd, P, qkv, cumq, cumk, lse, yc, ya, yl, hst, yn, r0, r1, r2)


def _mixer_backward(dy, dyb, saved, p, dep):
    x, h, rstd, P, qkv, cumq, cumk, lse, yc, ya, yl, hst, yn, r0, r1, r2 = saved
    T = x.shape[0]
    dyn = _mm_nt_full("mix_bwd_dyn", dyb, p["wo"], 512, dep=dep)
    dwo = _mm_tn("mix_bwd_dwo", yn, dyb, tm=512, tn=1024)
    dyc, dya, dyl, dgn = _gnorm_bwd(dyn, yc, ya, yl, r0, r1, r2, p["mix_out_norm"])
    dcb, dcc, dcv, dcw = _conv_bwd(P, p["cw"], dyc)
    dq, dk, dv, dck, dcq = _attn_bwd(qkv, cumq, cumk, lse, ya, dya)
    dcum = jnp.pad((dck.reshape(NH, T) + dcq.reshape(NH, T)).T, ((0, 0), (0, 128 - NH)))
    df, dfb = _fgate_bwd(P, p["fb"], dcum)
    dgate, dlx, lsm, dwa, dwx = _lru_bwd(P, p["lcw"], p["lvec"], p["lru_w_a"], p["lru_w_x"], hst, dyl)
    dP = jnp.concatenate([dcb, dcc, dcv, dq, dk, dv, dgate, dlx, df], axis=1).astype(BF16)
    tm = _tile(T, (512,))
    dh = _mm("mix_bwd_dh", dP, p["wmix"],
             _bs((tm, PW), lambda j, i, k: (i, 0)), _bs((1024, PW), lambda j, i, k: (j, 0)),
             _bs((tm, 1024), lambda j, i, k: (i, j)), S((T, D), F32), (D // 1024, T // tm, 1), NT, 1, (tm, 1024))
    dwmix = _mm_tn("mix_bwd_dwmix", h, dP, tm=512, tn=1152)
    dx, dxb, dgm = _rms_bwd(dh, x, rstd, p["norm_mix"], dy)
    small = dict(norm_mix=dgm[0], mix_out_norm=dgn[0], conv_w=dcw[:3], fgate_b=dfb[0, :NH], lru_conv_w=lsm[:4],
                 lru_conv_b=lsm[4], lru_b_a=lsm[5], lru_b_x=lsm[6], lru_lambda=lsm[7], lru_w_a=dwa, lru_w_x=dwx)
    return dx, dxb, small, dwmix, dwo


BIG =("ffn1_w_in", "ffn1_w_out", "mix_w_in", "mix_w_out", "ffn2_w_in", "ffn2_w_out")
SMALL = ("norm_ffn1", "norm_mix", "conv_w", "fgate_b", "lru_conv_w", "lru_conv_b", "lru_w_a", "lru_b_a", "lru_w_x",
         "lru_b_x", "lru_lambda", "mix_out_norm", "norm_ffn2", "final_norm")
WEIGHTS = ("norm_ffn1", "ffn1_w_in", "ffn1_w_out", "norm_mix", "mix_w_in", "conv_w", "fgate_b", "lru_conv_w", "lru_conv_b",
           "lru_w_a", "lru_b_a", "lru_w_x", "lru_b_x", "lru_lambda", "mix_out_norm", "mix_w_out", "norm_ffn2", "ffn2_w_in",
           "ffn2_w_out", "final_norm")


def _step(args):
    xx, yy, cc_ = lax.axis_index("x"), lax.axis_index("y"), lax.axis_index("c")
    me = 2 * xx + yy
    x0 = args["x"][0]
    tgt = args["loss_target"][0]
    T = x0.shape[0]
    L = args["norm_ffn1"].shape[0]

    def ag_sources(l):
        small = jnp.concatenate([args["conv_w"][l], args["lru_conv_w"][l], jnp.zeros((1, 128), F32)], axis=0)
        return [args[n][l].astype(BF16) for n in BIG] + [small]

    def layer_params(l, gat, gsm):
        w1i, w1o, wmx, wo, w2i, w2o = gat
        cwl = gsm.transpose(1, 0, 2).reshape(8, 4 * 128)
        return dict(
            w1i=w1i, w1o=w1o.reshape(-1, D), w2i=w2i, w2o=w2o.reshape(-1, D), wo=wo.reshape(D, D),
            wmix=_mix_pad(wmx.transpose(1, 0, 2).reshape(D, DIN)),
            cw=jnp.concatenate([cwl[:3], jnp.zeros((5, DC), F32)], axis=0),
            lcw=jnp.concatenate([cwl[3:7], jnp.zeros((4, DL), F32)], axis=0),
            fb=jnp.pad(args["fgate_b"][l], (0, 128 - NH)).reshape(1, 128),
            lvec=jnp.concatenate([args["lru_conv_b"][l][None], args["lru_b_a"][l][None], args["lru_b_x"][l][None],
                                  args["lru_lambda"][l][None], jnp.zeros((4, DL), F32)], axis=0),
            lru_w_a=args["lru_w_a"][l], lru_w_x=args["lru_w_x"][l],
            norm_ffn1=args["norm_ffn1"][l][None], norm_mix=args["norm_mix"][l][None],
            mix_out_norm=args["mix_out_norm"][l][None], norm_ffn2=args["norm_ffn2"][l][None])

    xs = x0
    saved, layers = [], []
    def ici_done(name, idx, flight, after):
        ssem, rsem, srcs, lands, _ = flight
        s, ld = _ag_wait("ag_wait_" + name, idx, ssem, rsem, [srcs[t] for t in idx], [lands[t] for t in idx], after)
        return _ag_fwd_start("ag_fwd_start_" + name, idx, s, ld, after)

    def gathered(name, idx, fwd, after):
        ssem, rsem, s, ld, _ = fwd
        return _ag_fwd_wait("ag_fwd_wait_" + name, idx, ssem, rsem, s, ld, after)

    every_w = tuple(range(N_AG))
    flight = _ag_start(0, ag_sources(0), x0)
    fwd = None
    for l in range(L):
        if l == 0:
            (w1i,) = gathered("0i", (0,), ici_done("0i", (0,), flight, xs), xs)
        else:
            w1i, w1o, wmx, wo, w2i, w2o, gsm = gathered(str(l), every_w, fwd, xs)
        nxt = _ag_start(l + 1, ag_sources(l + 1), w1i) if l + 1 < L else None
        h1, rstd1 = _rms_fwd(xs, args["norm_ffn1"][l][None])
        zg1, zu1, act1 = _ffn_in(h1, w1i)
        if l == 0:
            (w1o,) = gathered("0o", (1,), ici_done("0o", (1,), flight, act1), act1)
        x1 = _ffn_out(act1, w1o.reshape(-1, D), xs)
        s1 = (xs, h1, rstd1, zg1, zu1, act1)
        if l == 0:
            gsm, wmx, wo, w2i, w2o = gathered("0b", AG_REST, ici_done("0b", AG_REST, flight, x1), x1)
        p = layer_params(l, (w1i, w1o, wmx, wo, w2i, w2o), gsm)
        x2, s2 = _mixer_forward(x1, p)
        h2, rstd2 = _rms_fwd(x2, p["norm_ffn2"])
        zg2, zu2, act2 = _ffn_in(h2, p["w2i"])
        fwd = ici_done(str(l + 1), every_w, nxt, act2) if nxt is not None else None
        x3 = _ffn_out(act2, p["w2o"], x2, dep=None if fwd is None else fwd[4])
        s3 = (x2, h2, rstd2, zg2, zu2, act2)
        saved.append((s1, s2, s3))
        layers.append(p)
        xs = x3
    lpart, dx, dxb, dfinal = _loss_head(xs, args["final_norm"][None], tgt)
    loss = lax.psum(lpart[0, 0], ("x", "y", "c"))

    acc = [None] * N_BIG
    small_grads = [None] * L
    every = tuple(range(N_BIG))
    in_air = []
    pair_flight = None
    dep = lpart

    def to_blocks(t, g):
        if t in (1, 5):
            return g.reshape(4, g.shape[0] // 4, D)
        if t == 2:
            return _mix_unpad(g).reshape(D, 4, DIN // 4).transpose(1, 0, 2)
        if t == 3:
            return g.reshape(4, D // 4, D)
        return g

    def launch(name, l, idx, grads, recv, after):
        sums = [_pair_sum(g, r) for g, r in zip(grads, recv)]
        started = _rs_start("rs_start_" + name, sums, after)
        in_air.append(("rs_wait_" + name, l, idx, grads, recv, started))
        return started[4]

    def land(entry, after):
        name, l, idx, grads, recv, (ssem, rsem, sums, lands, _) = entry
        ici = _rs_wait(name, ssem, rsem, sums, lands, after)
        for t, g, r, i3 in zip(idx, grads, recv, ici):
            if acc[t] is None:
                acc[t] = jnp.zeros((L, g.shape[1] // 2, g.shape[2]), F32)
            acc[t] = _owner_sum(g, r, i3, acc[t], l)

    for l in reversed(range(L)):
        p = layers[l]
        s1, s2, s3 = saved[l]
        dx, dxb, dg2, dw2i, dw2o = _ffn_backward(dx, dxb, s3, p["norm_ffn2"], p["w2i"], p["w2o"], dep)
        if pair_flight is not None:
            lp, (ssem, rsem, g_thru, lands, _) = pair_flight
            g_thru, recv = _rs_pair_wait(f"rs_pair_wait_{lp}", ssem, rsem, g_thru, lands, dx)
            dep = launch(str(lp), lp, every, g_thru, recv, dx)
            pair_flight = None
        if l == 0:
            part = [to_blocks(4, dw2i), to_blocks(5, dw2o)]
            dep = launch("0c", 0, (4, 5), part, _rs_pair(part), dx)
        dx, dxb, sm, dwmix, dwo = _mixer_backward(dx, dxb, s2, p, dep)
        if l == 0:
            part = [to_blocks(2, dwmix), to_blocks(3, dwo)]
            dep = launch("0b", 0, (2, 3), part, _rs_pair(part), dx)
        dx, dxb, dg1, dw1i, dw1o = _ffn_backward(dx, dxb, s1, p["norm_ffn1"], p["w1i"], p["w1o"], dep)
        sm["norm_ffn1"] = dg1[0]
        sm["norm_ffn2"] = dg2[0]
        small_grads[l] = sm
        for entry in [e for e in in_air if e[1] > l]:
            land(entry, dx)
            in_air.remove(entry)
        if l > 0:
            grads = [to_blocks(t, g) for t, g in enumerate((dw1i, dw1o, dwmix, dwo, dw2i, dw2o))]
            pair_flight = (l, _rs_pair_start(f"rs_pair_start_{l}", grads, dx))
            dep = pair_flight[1][4]

    part = [to_blocks(0, dw1i), to_blocks(1, dw1o)]
    dep = launch("0a", 0, (0, 1), part, _rs_pair(part), dx)

    def adamw(k, lo, hi, sib, prev):
        n = BIG[k]
        return _adamw_big(args[n], acc[k], sib[k], args["m_" + n], args["v_" + n], lo, hi, prev)

    res = [None] * N_BIG
    after = dx
    if L > 1:
        sib = _rs_join(acc, 1, L, dep)
        for k in range(N_BIG):
            res[k] = adamw(k, 1, L, sib, None)
        after = res[N_BIG - 1][0]
    full = {n: (dfinal[0] if n == "final_norm" else jnp.stack([small_grads[l][n] for l in range(L)])) for n in SMALL}
    red_pack = _all_reduce_small(_pack([full[n] for n in SMALL]), after)
    for entry in list(in_air):
        land(entry, red_pack)
    sib = _rs_join(acc, 0, 1, dx)
    out = {"loss": loss, "grad_x": dx[None]}
    for k, n in enumerate(BIG):
        outs = adamw(k, 0, 1, sib, res[k])
        out["grad_" + n], out["delta_" + n], out["new_m_" + n], out["new_v_" + n] = outs

    shapes = [full[n].shape for n in SMALL]
    red = dict(zip(SMALL, _unpack(red_pack, shapes)))
    for n in ("conv_w", "lru_conv_w"):
        red[n] = lax.dynamic_slice_in_dim(red[n], me * 128, 128, axis=2)
    oshapes = [args[n].shape for n in SMALL]
    d, m, v = _adamw_small(_pack([args[n] for n in SMALL]), _pack([red[n] for n in SMALL]),
                           _pack([args["m_" + n] for n in SMALL]), _pack([args["v_" + n] for n in SMALL]))
    for n, gg, dd, mm, vv in zip(SMALL, [red[n] for n in SMALL], _unpack(d, oshapes), _unpack(m, oshapes), _unpack(v, oshapes)):
        out["grad_" + n], out["delta_" + n], out["new_m_" + n], out["new_v_" + n] = gg, dd, mm, vv
    return out


def kernel(x, norm_ffn1, ffn1_w_in, ffn1_w_out, norm_mix, mix_w_in, conv_w, fgate_b, lru_conv_w, lru_conv_b, lru_w_a, lru_b_a, lru_w_x, lru_b_x, lru_lambda, mix_out_norm, mix_w_out, norm_ffn2, ffn2_w_in, ffn2_w_out, final_norm, loss_target, m_norm_ffn1, m_ffn1_w_in, m_ffn1_w_out, m_norm_mix, m_mix_w_in, m_conv_w, m_fgate_b, m_lru_conv_w, m_lru_conv_b, m_lru_w_a, m_lru_b_a, m_lru_w_x, m_lru_b_x, m_lru_lambda, m_mix_out_norm, m_mix_w_out, m_norm_ffn2, m_ffn2_w_in, m_ffn2_w_out, m_final_norm, v_norm_ffn1, v_ffn1_w_in, v_ffn1_w_out, v_norm_mix, v_mix_w_in, v_conv_w, v_fgate_b, v_lru_conv_w, v_lru_conv_b, v_lru_w_a, v_lru_b_a, v_lru_w_x, v_lru_b_x, v_lru_lambda, v_mix_out_norm, v_mix_w_out, v_norm_ffn2, v_ffn2_w_in, v_ffn2_w_out, v_final_norm):
    args = dict(locals())
    out = _step(args)
    res = [out["loss"], out["grad_x"]]
    for prefix in ("grad_", "delta_", "new_m_", "new_v_"):
        res += [out[prefix + n] for n in WEIGHTS]
    return tuple(res)
```

```python
import functools
import math

import jax
import jax.numpy as jnp
from jax import lax
from jax.experimental import pallas as pl
from jax.experimental.pallas import tpu as pltpu

F32 = jnp.float32
BF16 = jnp.bfloat16
S = jax.ShapeDtypeStruct
MESH = pl.DeviceIdType.MESH

D = 2048
DC = 512
DA = 1024
NH = 8
HD = 128
DL = 512
LB = 128
DIN = 5640
PW = 5760
C_Q, C_K, C_V = 12, 20, 28
C_GATE, C_LX, C_F = 36, 40, 44
EPS = 1e-6
LRU_C = 8.0
ATT_SCALE = HD ** -0.5
LR, B1, B2, AEPS, WD, STEP = 0.001, 0.9, 0.999, 1e-08, 0.01, 10
VMEM_LIMIT = 56 * 1024 * 1024

NT = (((1,), (1,)), ((), ()))
TN = (((0,), (0,)), ((), ()))
NN = (((1,), (0,)), ((), ()))


def _cp():
    return pltpu.CompilerParams(vmem_limit_bytes=VMEM_LIMIT)


def _bs(shape, fn):
    return pl.BlockSpec(shape, fn)


def _mm(name, a, b, a_spec, b_spec, o_spec, o_shape, grid, dims, nk, acc_tile, scale=1.0, res=None, r_spec=None, dep=None):
    has_res = res is not None
    has_dep = dep is not None

    def body(*refs):
        if has_dep:
            refs = refs[:2 + has_res] + refs[3 + has_res:]
        if has_res:
            a_ref, b_ref, r_ref, o_ref = refs[:4]
            rest = refs[4:]
        else:
            a_ref, b_ref, o_ref = refs[:3]
            rest = refs[3:]
        prod = lax.dot_general(a_ref[...].astype(BF16), b_ref[...].astype(BF16), dims, preferred_element_type=F32)

        def finish(acc):
            if scale != 1.0:
                acc = acc * scale
            if has_res:
                acc = r_ref[...] + acc
            o_ref[...] = acc.astype(o_ref.dtype)

        if nk == 1:
            finish(prod)
        else:
            acc_ref = rest[0]
            k = pl.program_id(2)

            @pl.when(k == 0)
            def _():
                acc_ref[...] = prod

            @pl.when(k > 0)
            def _():
                acc_ref[...] += prod

            @pl.when(k == nk - 1)
            def _():
                finish(acc_ref[...])

    in_specs = [a_spec, b_spec] + ([r_spec] if has_res else []) + ([pl.BlockSpec(memory_space=pl.ANY)] if has_dep else [])
    args = (a, b) + ((res,) if has_res else ()) + ((dep,) if has_dep else ())
    scratch = [pltpu.VMEM(acc_tile, F32)] if nk > 1 else []
    return pl.pallas_call(body, grid=grid, in_specs=in_specs, out_specs=o_spec, out_shape=o_shape,
                          scratch_shapes=scratch, compiler_params=_cp(), name=name)(*args)


def _tile(n, pref):
    for t in pref:
        if n % t == 0:
            return t
    return n


def _rms_fwd(x, gain):
    T = x.shape[0]
    tb = _tile(T, (512,))

    def body(x_ref, g_ref, h_ref, r_ref):
        xv = x_ref[...]
        r = lax.rsqrt(jnp.mean(xv * xv, axis=1, keepdims=True) + EPS)
        h_ref[...] = (xv * r * g_ref[...]).astype(BF16)
        r_ref[...] = r

    return pl.pallas_call(
        body, grid=(T // tb,),
        in_specs=[_bs((tb, D), lambda i: (i, 0)), _bs((1, D), lambda i: (0, 0))],
        out_specs=[_bs((tb, D), lambda i: (i, 0)), _bs((tb, 1), lambda i: (i, 0))],
        out_shape=[S((T, D), BF16), S((T, 1), F32)], compiler_params=_cp(), name="rms_fwd")(x, gain)


def _rms_bwd(dh, x, rstd, gain, dres):
    T = x.shape[0]
    tb = _tile(T, (512,))

    def body(dh_ref, x_ref, r_ref, g_ref, dres_ref, dx_ref, dxb_ref, dg_ref):
        i = pl.program_id(0)
        r = r_ref[...]
        xhat = x_ref[...] * r
        dh = dh_ref[...]
        dxh = dh * g_ref[...]
        m = jnp.mean(dxh * xhat, axis=1, keepdims=True)
        dx = dres_ref[...] + r * (dxh - xhat * m)
        dx_ref[...] = dx
        dxb_ref[...] = dx.astype(BF16)
        part = jnp.sum(dh * xhat, axis=0, keepdims=True)

        @pl.when(i == 0)
        def _():
            dg_ref[...] = part

        @pl.when(i > 0)
        def _():
            dg_ref[...] += part

    row = _bs((tb, D), lambda i: (i, 0))
    return pl.pallas_call(
        body, grid=(T // tb,),
        in_specs=[row, row, _bs((tb, 1), lambda i: (i, 0)), _bs((1, D), lambda i: (0, 0)), row],
        out_specs=[row, row, _bs((1, D), lambda i: (0, 0))],
        out_shape=[S((T, D), F32), S((T, D), BF16), S((1, D), F32)], compiler_params=_cp(), name="rms_bwd")(dh, x, rstd, gain, dres)


def _loss_head(x, gain, tgt):
    T = x.shape[0]
    tb = _tile(T, (512,))

    def body(x_ref, g_ref, t_ref, l_ref, dx_ref, dxb_ref, dg_ref):
        i = pl.program_id(0)
        xv = x_ref[...]
        g = g_ref[...]
        r = lax.rsqrt(jnp.mean(xv * xv, axis=1, keepdims=True) + EPS)
        xhat = xv * r
        e = xhat * g - t_ref[...]
        lpart = 0.5 * jnp.sum(jnp.sum(e * e, axis=1, keepdims=True), axis=0, keepdims=True) * (1.0 / D)
        dy = e * (1.0 / D)
        dxh = dy * g
        m = jnp.mean(dxh * xhat, axis=1, keepdims=True)
        dx = r * (dxh - xhat * m)
        dx_ref[...] = dx
        dxb_ref[...] = dx.astype(BF16)
        gpart = jnp.sum(dy * xhat, axis=0, keepdims=True)
        lrow = jnp.broadcast_to(lpart, (1, 128))

        @pl.when(i == 0)
        def _():
            dg_ref[...] = gpart
            l_ref[...] = lrow

        @pl.when(i > 0)
        def _():
            dg_ref[...] += gpart
            l_ref[...] += lrow

    row = _bs((tb, D), lambda i: (i, 0))
    return pl.pallas_call(
        body, grid=(T // tb,),
        in_specs=[row, _bs((1, D), lambda i: (0, 0)), row],
        out_specs=[_bs((1, 128), lambda i: (0, 0)), row, row, _bs((1, D), lambda i: (0, 0))],
        out_shape=[S((1, 128), F32), S((T, D), F32), S((T, D), BF16), S((1, D), F32)],
        compiler_params=_cp(), name="loss_head")(x, gain, tgt)


def _sigmoid(z):
    return 0.5 * jnp.tanh(0.5 * z) + 0.5


def _ffn_in(h, win):
    T = h.shape[0]
    Fs = win.shape[2]
    F = 2 * Fs
    tn = _tile(Fs, (256, 128))
    nb = Fs // tn
    tm = _tile(T, (1024, 512))

    def body(h_ref, wg_ref, wu_ref, zg_ref, zu_ref, a_ref):
        hv = h_ref[...]
        zg = jnp.dot(hv, wg_ref[...], preferred_element_type=F32)
        zu = jnp.dot(hv, wu_ref[...], preferred_element_type=F32)
        zg_ref[...] = zg.astype(BF16)
        zu_ref[...] = zu.astype(BF16)
        a_ref[...] = (zg * _sigmoid(zg) * zu).astype(BF16)

    col = _bs((tm, tn), lambda i, j: (i, j))
    return pl.pallas_call(
        body, grid=(T // tm, F // tn),
        in_specs=[_bs((tm, D), lambda i, j: (i, 0)),
                  _bs((None, D, tn), lambda i, j: (j // nb, 0, j % nb)),
                  _bs((None, D, tn), lambda i, j: (2 + j // nb, 0, j % nb))],
        out_specs=[col, col, col],
        out_shape=[S((T, F), BF16)] * 3, compiler_params=_cp(), name="ffn_in")(h, win, win)


def _ffn_out(act, wout, x, dep=None):
    T, F = act.shape
    tm = _tile(T, (512,))
    tn = 1024
    return _mm("ffn_out", act, wout,
               _bs((tm, F), lambda j, i, k: (i, 0)), _bs((F, tn), lambda j, i, k: (0, j)),
               _bs((tm, tn), lambda j, i, k: (i, j)), S((T, D), F32), (D // tn, T // tm, 1), NN, 1, (tm, tn),
               scale=0.5, res=x, r_spec=_bs((tm, tn), lambda j, i, k: (i, j)), dep=dep)


def _ffn_bwd_dz(dyb, wout, zg, zu, dep):
    T, F = zg.shape
    tm = _tile(T, (1024, 512))
    tn = _tile(F, (512, 256))

    def body(dy_ref, w_ref, zg_ref, zu_ref, dep_ref, dz_ref):
        da = 0.5 * lax.dot_general(dy_ref[...], w_ref[...], NT, preferred_element_type=F32)
        zg = zg_ref[...].astype(F32)
        zu = zu_ref[...].astype(F32)
        s = _sigmoid(zg)
        dz_ref[0] = (da * zu * (s * (1.0 + zg * (1.0 - s)))).astype(BF16)
        dz_ref[1] = (da * (zg * s)).astype(BF16)

    col = _bs((tm, tn), lambda i, j: (i, j))
    return pl.pallas_call(
        body, grid=(T // tm, F // tn),
        in_specs=[_bs((tm, D), lambda i, j: (i, 0)), _bs((tn, D), lambda i, j: (j, 0)), col, col,
                  pl.BlockSpec(memory_space=pl.ANY)],
        out_specs=_bs((2, tm, tn), lambda i, j: (0, i, j)), out_shape=S((2, T, F), BF16),
        compiler_params=_cp(), name="ffn_bwd_dz")(dyb, wout, zg, zu, dep)


def _ffn_bwd_dh(dz, win):
    _, T, F = dz.shape
    Fs = win.shape[2]
    tk = _tile(Fs, (2816, 1408, 256, 128))
    nkb = Fs // tk
    tm = _tile(T, (1024, 512))
    tn = 512
    nk = 2 * nkb

    def body(dzg_ref, dzu_ref, wg_ref, wu_ref, o_ref, acc_ref):
        k = pl.program_id(2)
        prod = (lax.dot_general(dzg_ref[...], wg_ref[...], NT, preferred_element_type=F32)
                + lax.dot_general(dzu_ref[...], wu_ref[...], NT, preferred_element_type=F32))

        @pl.when(k == 0)
        def _():
            acc_ref[...] = prod

        @pl.when(k > 0)
        def _():
            acc_ref[...] += prod

        @pl.when(k == nk - 1)
        def _():
            o_ref[...] = acc_ref[...]

    def a_spec(half):
        return _bs((None, tm, tk), lambda i, j, k: (half, i, k))

    return pl.pallas_call(
        body, grid=(T // tm, D // tn, nk),
        in_specs=[a_spec(0), a_spec(1),
                  _bs((None, tn, tk), lambda i, j, k: (k // nkb, j, k % nkb)),
                  _bs((None, tn, tk), lambda i, j, k: (2 + k // nkb, j, k % nkb))],
        out_specs=_bs((tm, tn), lambda i, j, k: (i, j)), out_shape=S((T, D), F32),
        scratch_shapes=[pltpu.VMEM((tm, tn), F32)], compiler_params=_cp(), name="ffn_bwd_dh")(dz, dz, win, win)


def _ffn_bwd_dwin(h, dz):
    _, T, F = dz.shape
    Fs = F // 2
    tn = _tile(Fs, (1408, 256, 128))
    nb = Fs // tn
    tm = 512
    return _mm("ffn_bwd_dwin", h, dz,
               _bs((T, tm), lambda i, j, k: (0, i)), _bs((None, T, tn), lambda i, j, k: (j // (2 * nb), 0, j % (2 * nb))),
               _bs((None, tm, tn), lambda i, j, k: (j // nb, i, j % nb)), S((4, D, Fs), BF16),
               (D // tm, 4 * nb, 1), TN, 1, (tm, tn))


def _mm_tn(name, a, b, scale=1.0, tm=512, tn=1024):
    T, M = a.shape
    N = b.shape[1]
    tm = _tile(M, (tm, 512, 256, 128))
    tn = _tile(N, (tn, 1152, 1024, 512, 128))
    return _mm(name, a, b,
               _bs((T, tm), lambda i, j, k: (0, i)), _bs((T, tn), lambda i, j, k: (0, j)),
               _bs((tm, tn), lambda i, j, k: (i, j)), S((M, N), BF16), (M // tm, N // tn, 1), TN, 1, (tm, tn), scale=scale)


def _mm_nt_full(name, a, b, tn, dep=None):
    T, K = a.shape
    N = b.shape[0]
    tm = _tile(T, (1024, 512))
    return _mm(name, a, b,
               _bs((tm, K), lambda i, j, k: (i, 0)), _bs((tn, K), lambda i, j, k: (j, 0)),
               _bs((tm, tn), lambda i, j, k: (i, j)), S((T, N), F32), (T // tm, N // tn, 1), NT, 1, (tm, tn), dep=dep)


def _bt(T):
    return _tile(T, (512,))


def _down(ext, s):
    return pltpu.roll(ext, s, 0)[8:, :]


def _up(ext, s):
    n = ext.shape[0]
    return pltpu.roll(ext, n - s, 0)[: n - 8, :]


def _halo_prev(ref, start, b):
    lo = pl.multiple_of(jnp.maximum(start - 8, 0), 8)
    return ref[pl.ds(lo, 8), :] * (b > 0).astype(F32)


def _halo_next(ref, start, bt, b, nb):
    lo = pl.multiple_of(jnp.minimum(start + bt, (nb - 1) * bt), 8)
    return ref[pl.ds(lo, 8), :] * (b < nb - 1).astype(F32)


def _scan_fwd(A, U):
    n = U.shape[0]
    row = lax.broadcasted_iota(jnp.int32, U.shape, 0)
    d = 1
    while d < n:
        keep = row >= d
        Us = jnp.where(keep, pltpu.roll(U, d, 0), 0.0)
        if A is None:
            U = U + Us
        else:
            As = jnp.where(keep, pltpu.roll(A, d, 0), 1.0)
            U = A * Us + U
            A = A * As
        d *= 2
    return A, U


def _scan_bwd(A, U):
    n = U.shape[0]
    row = lax.broadcasted_iota(jnp.int32, U.shape, 0)
    d = 1
    while d < n:
        keep = row < n - d
        Us = jnp.where(keep, pltpu.roll(U, n - d, 0), 0.0)
        if A is None:
            U = U + Us
        else:
            As = jnp.where(keep, pltpu.roll(A, n - d, 0), 1.0)
            U = A * Us + U
            A = A * As
        d *= 2
    return A, U


def _softplus(z):
    return jnp.maximum(z, 0.0) + jnp.log(1.0 + jnp.exp(-jnp.abs(z)))


def _gelu_parts(g):
    k0 = math.sqrt(2.0 / math.pi)
    t = jnp.tanh(k0 * (g + 0.044715 * g * g * g))
    gel = 0.5 * g * (1.0 + t)
    dgel = 0.5 * (1.0 + t) + 0.5 * g * (1.0 - t * t) * k0 * (1.0 + 3.0 * 0.044715 * g * g)
    return gel, dgel


def _conv_fwd(P, cw):
    T = P.shape[0]
    bt = _bt(T)
    nb = T // bt

    def body(b_ref, c_ref, v_ref, w_ref, y_ref):
        w = w_ref[...]

        def step(b, carry):
            start = pl.multiple_of(b * bt, bt)
            rows = pl.ds(start, bt)
            m = c_ref[rows, :] * v_ref[rows, :]
            ext = jnp.concatenate([_halo_prev(c_ref, start, b) * _halo_prev(v_ref, start, b), m], axis=0)
            z = w[2:3, :] * m + w[1:2, :] * _down(ext, 1) + w[0:1, :] * _down(ext, 2)
            y_ref[rows, :] = b_ref[rows, :] * z
            return carry

        lax.fori_loop(0, nb, step, 0)

    def colspec(off):
        return _bs((T, 128), lambda c: (0, off + c))

    return pl.pallas_call(
        body, grid=(DC // 128,),
        in_specs=[colspec(0), colspec(4), colspec(8), _bs((8, 128), lambda c: (0, c))],
        out_specs=_bs((T, 128), lambda c: (0, c)), out_shape=S((T, DC), F32),
        compiler_params=_cp(), name="conv_fwd")(P, P, P, cw)


def _conv_bwd(P, cw, dy):
    T = P.shape[0]
    bt = _bt(T)
    nb = T // bt

    def body(b_ref, c_ref, v_ref, w_ref, dy_ref, db_ref, dc_ref, dv_ref, dw_ref):
        w = w_ref[...]

        def step(b, carry):
            a0, a1, a2 = carry
            start = pl.multiple_of(b * bt, bt)
            rows = pl.ds(start, bt)
            cb, cc, cv, dy = b_ref[rows, :], c_ref[rows, :], v_ref[rows, :], dy_ref[rows, :]
            m = cc * cv
            ext = jnp.concatenate([_halo_prev(c_ref, start, b) * _halo_prev(v_ref, start, b), m], axis=0)
            m1, m2 = _down(ext, 1), _down(ext, 2)
            z = w[2:3, :] * m + w[1:2, :] * m1 + w[0:1, :] * m2
            db_ref[rows, :] = dy * z
            dz = dy * cb
            extn = jnp.concatenate([dz, _halo_next(dy_ref, start, bt, b, nb) * _halo_next(b_ref, start, bt, b, nb)], axis=0)
            dm = w[2:3, :] * dz + w[1:2, :] * _up(extn, 1) + w[0:1, :] * _up(extn, 2)
            dc_ref[rows, :] = dm * cv
            dv_ref[rows, :] = dm * cc
            return (a0 + jnp.sum(dz * m2, axis=0, keepdims=True),
                    a1 + jnp.sum(dz * m1, axis=0, keepdims=True),
                    a2 + jnp.sum(dz * m, axis=0, keepdims=True))

        zero = jnp.zeros((1, 128), F32)
        a0, a1, a2 = lax.fori_loop(0, nb, step, (zero, zero, zero))
        dw_ref[...] = jnp.zeros((8, 128), F32)
        dw_ref[0:1, :] = a0
        dw_ref[1:2, :] = a1
        dw_ref[2:3, :] = a2

    def colspec(off):
        return _bs((T, 128), lambda c: (0, off + c))

    own = _bs((T, 128), lambda c: (0, c))
    return pl.pallas_call(
        body, grid=(DC // 128,),
        in_specs=[colspec(0), colspec(4), colspec(8), _bs((8, 128), lambda c: (0, c)), own],
        out_specs=[own, own, own, _bs((8, 128), lambda c: (0, c))],
        out_shape=[S((T, DC), F32)] * 3 + [S((8, DC), F32)], compiler_params=_cp(), name="conv_bwd")(P, P, P, cw, dy)


def _fgate_fwd(P, fb):
    T = P.shape[0]
    bt = _bt(T)
    nb = T // bt

    def body(f_ref, b_ref, c_ref):
        bias = b_ref[...]

        def step(b, carry):
            rows = pl.ds(pl.multiple_of(b * bt, bt), bt)
            logf = -_softplus(-(f_ref[rows, :] + bias))
            _, cs = _scan_fwd(None, logf)
            cs = cs + carry
            c_ref[rows, :] = cs
            return cs[bt - 1:bt, :]

        lax.fori_loop(0, nb, step, jnp.zeros((1, 128), F32))

    return pl.pallas_call(
        body, grid=(1,),
        in_specs=[_bs((T, 128), lambda i: (0, C_F)), _bs((1, 128), lambda i: (0, 0))],
        out_specs=_bs((T, 128), lambda i: (0, 0)), out_shape=S((T, 128), F32),
        compiler_params=_cp(), name="fgate_fwd")(P, fb)


def _fgate_bwd(P, fb, dcum):
    T = P.shape[0]
    bt = _bt(T)
    nb = T // bt

    def body(f_ref, b_ref, dc_ref, df_ref, db_ref):
        bias = b_ref[...]

        def step(i, carry):
            run, acc = carry
            b = nb - 1 - i
            rows = pl.ds(pl.multiple_of(b * bt, bt), bt)
            _, rs = _scan_bwd(None, dc_ref[rows, :])
            rs = rs + run
            df = rs * jax.nn.sigmoid(-(f_ref[rows, :] + bias))
            df_ref[rows, :] = df
            return rs[0:1, :], acc + jnp.sum(df, axis=0, keepdims=True)

        zero = jnp.zeros((1, 128), F32)
        _, acc = lax.fori_loop(0, nb, step, (zero, zero))
        db_ref[...] = acc

    return pl.pallas_call(
        body, grid=(1,),
        in_specs=[_bs((T, 128), lambda i: (0, C_F)), _bs((1, 128), lambda i: (0, 0)), _bs((T, 128), lambda i: (0, 0))],
        out_specs=[_bs((T, 128), lambda i: (0, 0)), _bs((1, 128), lambda i: (0, 0))],
        out_shape=[S((T, 128), F32), S((1, 128), F32)], compiler_params=_cp(), name="fgate_bwd")(P, fb, dcum)


def _att_tile(T):
    return _tile(T, (512,))


def _causal_mask(tq):
    return lax.broadcasted_iota(jnp.int32, (tq, tq), 1) <= lax.broadcasted_iota(jnp.int32, (tq, tq), 0)


def _attn_fwd(qkv, cumq, cumk):
    T = qkv.shape[0]
    tq = _att_tile(T)
    nq = T // tq

    def body(q_ref, k_ref, v_ref, cq_ref, ck_ref, o_ref, lse_ref):
        i = pl.program_id(1)
        q = q_ref[...]
        cq = cq_ref[...]

        def block(j, carry, diagonal):
            m_old, l_old, acc = carry
            rows = pl.ds(pl.multiple_of(j * tq, tq), tq)
            s = lax.dot_general(q, k_ref[rows, :], NT, preferred_element_type=F32)
            s = s * ATT_SCALE + cq - ck_ref[j]
            if diagonal:
                s = jnp.where(_causal_mask(tq), s, -jnp.inf)
            m_new = jnp.maximum(m_old, jnp.max(s, axis=1, keepdims=True))
            p = jnp.exp(s - m_new)
            alpha = jnp.exp(m_old - m_new)
            l_new = alpha * l_old + jnp.sum(p, axis=1, keepdims=True)
            acc = alpha * acc + jnp.dot(p.astype(BF16), v_ref[rows, :], preferred_element_type=F32)
            return m_new, l_new, acc

        init = (jnp.full((tq, 1), -jnp.inf, F32), jnp.zeros((tq, 1), F32), jnp.zeros((tq, HD), F32))
        carry = lax.fori_loop(0, i, lambda j, c: block(j, c, False), init)
        m, l, acc = block(i, carry, True)
        o_ref[...] = acc / l
        lse_ref[...] = m + jnp.log(l)

    return pl.pallas_call(
        body, grid=(NH, nq),
        in_specs=[_bs((tq, HD), lambda h, i: (i, h)),
                  _bs((T, HD), lambda h, i: (0, NH + h)),
                  _bs((T, HD), lambda h, i: (0, 2 * NH + h)),
                  _bs((None, tq, 1), lambda h, i: (h, i, 0)),
                  _bs((None, nq, 1, tq), lambda h, i: (h, 0, 0, 0))],
        out_specs=[_bs((tq, HD), lambda h, i: (i, h)), _bs((None, tq, 1), lambda h, i: (h, i, 0))],
        out_shape=[S((T, DA), F32), S((NH, T, 1), F32)],
        compiler_params=_cp(), name="attn_fwd")(qkv, qkv, qkv, cumq, cumk)


def _attn_bwd(qkv, cumq, cumk, lse, o, do):
    T = qkv.shape[0]
    tq = _att_tile(T)
    nq = T // tq

    def body(q_ref, k_ref, v_ref, cq_ref, ck_ref, lse_ref, o_ref, do_ref, dq_ref, dk_ref, dv_ref, dc_ref, dr_ref):
        j = pl.program_id(1)

        @pl.when(j == 0)
        def _():
            dq_ref[...] = jnp.zeros((T, HD), F32)
            dr_ref[...] = jnp.zeros((T, 1), F32)

        k = k_ref[...]
        v = v_ref[...]
        ck = ck_ref[...]

        def block(i, carry, diagonal):
            dk_acc, dv_acc, dc_acc = carry
            rows = pl.ds(pl.multiple_of(i * tq, tq), tq)
            q = q_ref[rows, :]
            do_f = do_ref[rows, :]
            dob = do_f.astype(BF16)
            s = lax.dot_general(q, k, NT, preferred_element_type=F32)
            p = jnp.exp(s * ATT_SCALE + cq_ref[rows, :] - ck - lse_ref[rows, :])
            if diagonal:
                p = jnp.where(_causal_mask(tq), p, 0.0)
            delta = jnp.sum(do_f * o_ref[rows, :], axis=1, keepdims=True)
            dp = lax.dot_general(dob, v, NT, preferred_element_type=F32)
            ds = p * (dp - delta)
            dsb = (ds * ATT_SCALE).astype(BF16)
            dq_ref[rows, :] += jnp.dot(dsb, k, preferred_element_type=F32)
            dr_ref[rows, :] += jnp.sum(ds, axis=1, keepdims=True)
            return (dk_acc + lax.dot_general(dsb, q, TN, preferred_element_type=F32),
                    dv_acc + lax.dot_general(p.astype(BF16), dob, TN, preferred_element_type=F32),
                    dc_acc - jnp.sum(ds, axis=0, keepdims=True))

        init = (jnp.zeros((tq, HD), F32), jnp.zeros((tq, HD), F32), jnp.zeros((1, tq), F32))
        carry = block(j, init, True)
        dk_acc, dv_acc, dc_acc = lax.fori_loop(j + 1, nq, lambda i, c: block(i, c, False), carry)
        dk_ref[...] = dk_acc
        dv_ref[...] = dv_acc
        dc_ref[...] = dc_acc

    def whole(col):
        return _bs((T, HD), lambda h, j: (0, col + h))

    qvec = _bs((None, T, 1), lambda h, j: (h, 0, 0))
    kv_out = _bs((tq, HD), lambda h, j: (j, h))
    return pl.pallas_call(
        body, grid=(NH, nq),
        in_specs=[whole(0), _bs((tq, HD), lambda h, j: (j, NH + h)), _bs((tq, HD), lambda h, j: (j, 2 * NH + h)),
                  qvec, _bs((None, None, 1, tq), lambda h, j: (h, j, 0, 0)), qvec, whole(0), whole(0)],
        out_specs=[whole(0), kv_out, kv_out, _bs((None, 1, tq), lambda h, j: (h, 0, j)), qvec],
        out_shape=[S((T, DA), F32)] * 3 + [S((NH, 1, T), F32), S((NH, T, 1), F32)],
        compiler_params=_cp(), name="attn_bwd")(qkv, qkv, qkv, cumq, cumk, lse, o, do)


def _lru_gates(xr, wa, wx, ba, bx, sp):
    xb = xr.astype(BF16)
    r = jax.nn.sigmoid(jnp.dot(xb, wa, preferred_element_type=F32) + ba)
    ig = jax.nn.sigmoid(jnp.dot(xb, wx, preferred_element_type=F32) + bx)
    log_a = -LRU_C * r * sp
    a = jnp.exp(log_a)
    th = jnp.tanh(log_a)
    om = -2.0 * th / (1.0 - th)
    mult = jnp.sqrt(om)
    return xb, r, ig, a, om, mult


def _lru_xr(lx_ref, cw, cb, start, b, rows):
    lx = lx_ref[rows, :]
    ext = jnp.concatenate([_halo_prev(lx_ref, start, b), lx], axis=0)
    return cw[3:4, :] * lx + cw[2:3, :] * _down(ext, 1) + cw[1:2, :] * _down(ext, 2) + cw[0:1, :] * _down(ext, 3) + cb


def _lru_fwd(P, lcw, vec, wa, wx):
    T = P.shape[0]
    bt = _bt(T)
    nb = T // bt

    def body(g_ref, lx_ref, cw_ref, vec_ref, wa_ref, wx_ref, y_ref, h_ref):
        cw = cw_ref[...]
        vec = vec_ref[...]
        wa = wa_ref[...].astype(BF16)
        wx = wx_ref[...].astype(BF16)
        sp = _softplus(-vec[3:4, :])

        def step(b, carry):
            start = pl.multiple_of(b * bt, bt)
            rows = pl.ds(start, bt)
            xr = _lru_xr(lx_ref, cw, vec[0:1, :], start, b, rows)
            _, _, ig, a, _, mult = _lru_gates(xr, wa, wx, vec[1:2, :], vec[2:3, :], sp)
            u = mult * (ig * xr)
            ac, hc = _scan_fwd(a, u)
            hb = hc + ac * carry
            h_ref[rows, :] = hb
            gel, _ = _gelu_parts(g_ref[rows, :])
            y_ref[rows, :] = gel * hb
            return hb[bt - 1:bt, :]

        lax.fori_loop(0, nb, step, jnp.zeros((1, 128), F32))

    own = _bs((T, 128), lambda c: (0, c))
    return pl.pallas_call(
        body, grid=(DL // 128,),
        in_specs=[_bs((T, 128), lambda c: (0, C_GATE + c)), _bs((T, 128), lambda c: (0, C_LX + c)),
                  _bs((8, 128), lambda c: (0, c)), _bs((8, 128), lambda c: (0, c)),
                  _bs((None, LB, LB), lambda c: (c, 0, 0)), _bs((None, LB, LB), lambda c: (c, 0, 0))],
        out_specs=[own, own], out_shape=[S((T, DL), F32)] * 2, compiler_params=_cp(), name="lru_fwd")(P, P, lcw, vec, wa, wx)


def _lru_bwd(P, lcw, vec, wa, wx, hst, dy):
    T = P.shape[0]
    bt = _bt(T)
    nb = T // bt

    def body(g_ref, lx_ref, cw_ref, vec_ref, wa_ref, wx_ref, h_ref, dy_ref,
             dg_ref, dlx_ref, sm_ref, dwa_ref, dwx_ref, dxr_s):
        cw = cw_ref[...]
        vec = vec_ref[...]
        wa = wa_ref[...].astype(BF16)
        wx = wx_ref[...].astype(BF16)
        lam = vec[3:4, :]
        sp = _softplus(-lam)
        dwa_ref[...] = jnp.zeros((LB, LB), F32)
        dwx_ref[...] = jnp.zeros((LB, LB), F32)
        zero = jnp.zeros((1, 128), F32)

        def step1(i, carry):
            wc, s_cb, s_ba, s_bx, s_sp = carry
            b = nb - 1 - i
            start = pl.multiple_of(b * bt, bt)
            rows = pl.ds(start, bt)
            xr = _lru_xr(lx_ref, cw, vec[0:1, :], start, b, rows)
            xb, r, ig, a, om, mult = _lru_gates(xr, wa, wx, vec[1:2, :], vec[2:3, :], sp)
            hb = h_ref[rows, :]
            dy = dy_ref[rows, :]
            gel, dgel = _gelu_parts(g_ref[rows, :])
            dg_ref[rows, :] = dy * hb * dgel
            dh = dy * gel
            ac, wcum = _scan_bwd(a, a * dh)
            w = wcum + ac * wc
            g = dh + _up(jnp.concatenate([w, jnp.broadcast_to(wc, (8, 128))], axis=0), 1)
            hprev = _down(jnp.concatenate([_halo_prev(h_ref, start, b), hb], axis=0), 1)
            da = g * hprev
            dmult = g * (ig * xr)
            dix = g * mult
            di = dix * xr
            dlog_a = da * a - dmult * ((1.0 - om) / mult)
            dr = dlog_a * (-LRU_C * sp)
            dpr = dr * r * (1.0 - r)
            dpi = di * ig * (1.0 - ig)
            dprb, dpib = dpr.astype(BF16), dpi.astype(BF16)
            dwa_ref[...] += lax.dot_general(xb, dprb, TN, preferred_element_type=F32)
            dwx_ref[...] += lax.dot_general(xb, dpib, TN, preferred_element_type=F32)
            dxr = (dix * ig + lax.dot_general(dprb, wa, NT, preferred_element_type=F32)
                   + lax.dot_general(dpib, wx, NT, preferred_element_type=F32))
            dxr_s[rows, :] = dxr
            return (w[0:1, :], s_cb + jnp.sum(dxr, axis=0, keepdims=True), s_ba + jnp.sum(dpr, axis=0, keepdims=True),
                    s_bx + jnp.sum(dpi, axis=0, keepdims=True), s_sp + jnp.sum(dlog_a * (-LRU_C * r), axis=0, keepdims=True))

        _, s_cb, s_ba, s_bx, s_sp = lax.fori_loop(0, nb, step1, (zero, zero, zero, zero, zero))

        def step2(b, carry):
            t0, t1, t2, t3 = carry
            start = pl.multiple_of(b * bt, bt)
            rows = pl.ds(start, bt)
            dxr = dxr_s[rows, :]
            extn = jnp.concatenate([dxr, _halo_next(dxr_s, start, bt, b, nb)], axis=0)
            dlx_ref[rows, :] = (cw[3:4, :] * dxr + cw[2:3, :] * _up(extn, 1) + cw[1:2, :] * _up(extn, 2)
                                + cw[0:1, :] * _up(extn, 3))
            lx = lx_ref[rows, :]
            ext = jnp.concatenate([_halo_prev(lx_ref, start, b), lx], axis=0)
            return (t0 + jnp.sum(dxr * _down(ext, 3), axis=0, keepdims=True),
                    t1 + jnp.sum(dxr * _down(ext, 2), axis=0, keepdims=True),
                    t2 + jnp.sum(dxr * _down(ext, 1), axis=0, keepdims=True),
                    t3 + jnp.sum(dxr * lx, axis=0, keepdims=True))

        t0, t1, t2, t3 = lax.fori_loop(0, nb, step2, (zero, zero, zero, zero))
        sm_ref[...] = jnp.zeros((16, 128), F32)
        for k, val in enumerate((t0, t1, t2, t3, s_cb, s_ba, s_bx, -s_sp * jax.nn.sigmoid(-lam))):
            sm_ref[k:k + 1, :] = val

    own = _bs((T, 128), lambda c: (0, c))
    wspec = _bs((None, LB, LB), lambda c: (c, 0, 0))
    return pl.pallas_call(
        body, grid=(DL // 128,),
        in_specs=[_bs((T, 128), lambda c: (0, C_GATE + c)), _bs((T, 128), lambda c: (0, C_LX + c)),
                  _bs((8, 128), lambda c: (0, c)), _bs((8, 128), lambda c: (0, c)), wspec, wspec, own, own],
        out_specs=[own, own, _bs((16, 128), lambda c: (0, c)), wspec, wspec],
        out_shape=[S((T, DL), F32)] * 2 + [S((16, DL), F32), S((4, LB, LB), F32), S((4, LB, LB), F32)],
        scratch_shapes=[pltpu.VMEM((T, 128), F32)], compiler_params=_cp(), name="lru_bwd")(P, P, lcw, vec, wa, wx, hst, dy)


_GROUPS = ((0, DC), (DC, DC + DA), (DC + DA, D))


def _gnorm_fwd(yc, ya, yl, gain):
    T = yc.shape[0]
    tb = _tile(T, (512,))

    def body(c_ref, a_ref, l_ref, g_ref, yn_ref, r0_ref, r1_ref, r2_ref):
        for (lo, hi), src, r_ref in zip(_GROUPS, (c_ref, a_ref, l_ref), (r0_ref, r1_ref, r2_ref)):
            yv = src[...]
            r = lax.rsqrt(jnp.mean(yv * yv, axis=1, keepdims=True) + EPS)
            yn_ref[:, lo:hi] = (yv * r * g_ref[:, lo:hi]).astype(BF16)
            r_ref[...] = r

    rs = _bs((tb, 1), lambda i: (i, 0))
    return pl.pallas_call(
        body, grid=(T // tb,),
        in_specs=[_bs((tb, DC), lambda i: (i, 0)), _bs((tb, DA), lambda i: (i, 0)), _bs((tb, DL), lambda i: (i, 0)),
                  _bs((1, D), lambda i: (0, 0))],
        out_specs=[_bs((tb, D), lambda i: (i, 0)), rs, rs, rs],
        out_shape=[S((T, D), BF16)] + [S((T, 1), F32)] * 3, compiler_params=_cp(), name="gnorm_fwd")(yc, ya, yl, gain)


def _gnorm_bwd(dyn, yc, ya, yl, r0, r1, r2, gain):
    T = yc.shape[0]
    tb = _tile(T, (512,))

    def body(d_ref, c_ref, a_ref, l_ref, r0_ref, r1_ref, r2_ref, g_ref, dc_ref, da_ref, dl_ref, dg_ref):
        i = pl.program_id(0)
        for (lo, hi), src, r_ref, dst in zip(_GROUPS, (c_ref, a_ref, l_ref), (r0_ref, r1_ref, r2_ref), (dc_ref, da_ref, dl_ref)):
            r = r_ref[...]
            yhat = src[...] * r
            dy = d_ref[:, lo:hi]
            dyh = dy * g_ref[:, lo:hi]
            m = jnp.mean(dyh * yhat, axis=1, keepdims=True)
            dst[...] = r * (dyh - yhat * m)
            part = jnp.sum(dy * yhat, axis=0, keepdims=True)

            @pl.when(i == 0)
            def _():
                dg_ref[:, lo:hi] = part

            @pl.when(i > 0)
            def _():
                dg_ref[:, lo:hi] += part

    rs = _bs((tb, 1), lambda i: (i, 0))
    specs = [_bs((tb, DC), lambda i: (i, 0)), _bs((tb, DA), lambda i: (i, 0)), _bs((tb, DL), lambda i: (i, 0))]
    return pl.pallas_call(
        body, grid=(T // tb,),
        in_specs=[_bs((tb, D), lambda i: (i, 0))] + specs + [rs, rs, rs, _bs((1, D), lambda i: (0, 0))],
        out_specs=specs + [_bs((1, D), lambda i: (0, 0))],
        out_shape=[S((T, DC), F32), S((T, DA), F32), S((T, DL), F32), S((1, D), F32)],
        compiler_params=_cp(), name="gnorm_bwd")(dyn, yc, ya, yl, r0, r1, r2, gain)


HBM = pl.BlockSpec(memory_space=pltpu.HBM)
N_BIG = 6


def _place():
    x, y, c = lax.axis_index("x"), lax.axis_index("y"), lax.axis_index("c")
    return x, y, c, 2 * x + y


def _peer(x, y, j):
    return x ^ ((j + 1) >> 1), y ^ ((j + 1) & 1)


SEM = pl.BlockSpec(memory_space=pltpu.SEMAPHORE)
ANY = pl.BlockSpec(memory_space=pl.ANY)
VM = pl.BlockSpec(memory_space=pltpu.VMEM)
EFFECT = pltpu.SideEffectType.DATAFLOW_SIDE_EFFECTING
N_AG = N_BIG + 1


def _hbm(a):
    return pltpu.with_memory_space_constraint(a, pltpu.HBM)


AG_ORDER = (0, 1, N_BIG, 2, 3, 4, 5)
AG_REST = (N_BIG, 2, 3, 4, 5)


def _ag_copy(src, land, ssem, rsem, t, j, chip):
    x, y, c, _ = _place()
    px, py = _peer(x, y, j)
    if t == N_BIG:
        s_ref, d_ref = src, land.at[chip]
    else:
        rh = src.shape[0] // 2
        half = pl.ds(c * rh, rh)
        s_ref, d_ref = src.at[half], land.at[chip, half]
    return pltpu.make_async_remote_copy(src_ref=s_ref, dst_ref=d_ref, send_sem=ssem.at[3 * t + j], recv_sem=rsem.at[3 * t + j],
                                        device_id=(px, py, c), device_id_type=MESH)


def _ag_start(l, srcs, dep):
    n = N_AG

    def body(*refs):
        src = refs[:n]
        ssem, rsem = refs[2 * n + 1], refs[2 * n + 2]
        land = refs[3 * n + 3:4 * n + 3]
        token = refs[4 * n + 3]
        _, _, _, me = _place()
        for t in AG_ORDER:
            for j in range(3):
                _ag_copy(src[t], land[t], ssem, rsem, t, j, me).start()
        token[...] = jnp.zeros_like(token)

    lands = [lax.empty((4,) + a.shape, a.dtype) for a in srcs]
    dma = pltpu.SemaphoreType.DMA
    outs = pl.pallas_call(
        body, name=f"ag_start_{l}",
        out_shape=(dma((3 * n,)), dma((3 * n,))) + tuple(pltpu.HBM(a.shape, a.dtype) for a in list(srcs) + lands) + (S((8, 128), F32),),
        in_specs=[HBM] * (2 * n) + [ANY], out_specs=(SEM, SEM) + (HBM,) * (2 * n) + (VM,),
        input_output_aliases={i: 2 + i for i in range(2 * n)},
        compiler_params=pltpu.CompilerParams(has_side_effects=EFFECT),
    )(*[_hbm(a) for a in srcs], *[_hbm(a) for a in lands], dep)
    return outs[0], outs[1], outs[2:2 + n], outs[2 + n:2 + 2 * n], outs[-1]


def _ag_wait(name, idx, ssem, rsem, srcs, lands, after):
    n = len(idx)

    def body(*refs):
        src, land = refs[:n], refs[n:2 * n]
        ssem, rsem = refs[2 * n], refs[2 * n + 1]
        x, y, _, _ = _place()
        for p, t in enumerate(idx):
            for j in range(3):
                px, py = _peer(x, y, j)
                cp = _ag_copy(src[p], land[p], ssem, rsem, t, j, 2 * px + py)
                cp.wait_send()
                cp.wait_recv()

    outs = pl.pallas_call(
        body, name=name,
        out_shape=tuple(pltpu.HBM(a.shape, a.dtype) for a in list(srcs) + list(lands)),
        in_specs=[HBM] * (2 * n) + [SEM, SEM, ANY], out_specs=(HBM,) * (2 * n),
        input_output_aliases={i: i for i in range(2 * n)},
        compiler_params=pltpu.CompilerParams(has_side_effects=EFFECT),
    )(*srcs, *lands, ssem, rsem, after)
    return outs[:n], outs[n:]


def _fwd_copies(idx, src, land, ssem, rsem, recv_side):
    x, y, c, me = _place()
    cps = []
    for p, t in enumerate(idx):
        cps.append(pltpu.make_async_remote_copy(src_ref=src[p], dst_ref=land[p].at[me], send_sem=ssem.at[4 * p], recv_sem=rsem.at[4 * p],
                                                device_id=(x, y, 1 - c), device_id_type=MESH))
        if t == N_BIG:
            continue
        rh = src[p].shape[0] // 2
        for j in range(3):
            px, py = _peer(x, y, j)
            part = land[p].at[2 * px + py, pl.ds(((1 - c) if recv_side else c) * rh, rh)]
            cps.append(pltpu.make_async_remote_copy(src_ref=part, dst_ref=part, send_sem=ssem.at[4 * p + 1 + j],
                                                    recv_sem=rsem.at[4 * p + 1 + j], device_id=(x, y, 1 - c), device_id_type=MESH))
    return cps


def _ag_fwd_start(name, idx, srcs, lands, dep):
    n = len(idx)

    def body(*refs):
        src = refs[:n]
        ssem, rsem = refs[2 * n + 1], refs[2 * n + 2]
        land = refs[3 * n + 3:4 * n + 3]
        token = refs[4 * n + 3]
        for cp in _fwd_copies(idx, src, land, ssem, rsem, False):
            cp.start()
        token[...] = jnp.zeros_like(token)

    dma = pltpu.SemaphoreType.DMA
    outs = pl.pallas_call(
        body, name=name,
        out_shape=(dma((4 * n,)), dma((4 * n,))) + tuple(pltpu.HBM(a.shape, a.dtype) for a in list(srcs) + list(lands)) + (S((8, 128), F32),),
        in_specs=[HBM] * (2 * n) + [ANY], out_specs=(SEM, SEM) + (HBM,) * (2 * n) + (VM,),
        input_output_aliases={i: 2 + i for i in range(2 * n)},
        compiler_params=pltpu.CompilerParams(has_side_effects=EFFECT),
    )(*srcs, *lands, dep)
    return outs[0], outs[1], outs[2:2 + n], outs[2 + n:2 + 2 * n], outs[-1]


def _ag_fwd_wait(name, idx, ssem, rsem, srcs, lands, after):
    n = len(idx)

    def body(*refs):
        src, land = refs[:n], refs[n:2 * n]
        ssem, rsem = refs[2 * n], refs[2 * n + 1]
        for mine, theirs in zip(_fwd_copies(idx, src, land, ssem, rsem, False), _fwd_copies(idx, src, land, ssem, rsem, True)):
            mine.wait_send()
            theirs.wait_recv()

    outs = pl.pallas_call(
        body, name=name,
        out_shape=tuple(pltpu.HBM(a.shape, a.dtype) for a in list(srcs) + list(lands)),
        in_specs=[HBM] * (2 * n) + [SEM, SEM, ANY], out_specs=(HBM,) * (2 * n),
        input_output_aliases={i: i for i in range(2 * n)},
        compiler_params=pltpu.CompilerParams(has_side_effects=EFFECT),
    )(*srcs, *lands, ssem, rsem, after)
    return outs[n:]


def _pair_copy(g, land, ssem, rsem, t):
    x, y, c, _ = _place()
    rh = g.shape[1] // 2
    return pltpu.make_async_remote_copy(src_ref=g.at[:, pl.ds((1 - c) * rh, rh), :], dst_ref=land,
                                        send_sem=ssem.at[t], recv_sem=rsem.at[t], device_id=(x, y, 1 - c), device_id_type=MESH)


def _rs_pair(grads):
    n = len(grads)

    def body(*refs):
        g, out = refs[:n], refs[n:2 * n]
        ssem, rsem = refs[2 * n:]
        for t in range(n):
            _pair_copy(g[t], out[t], ssem, rsem, t).start()
        for t in range(n):
            _pair_copy(g[t], out[t], ssem, rsem, t).wait()

    dma = pltpu.SemaphoreType.DMA
    return pl.pallas_call(
        body, in_specs=[HBM] * n, out_specs=[HBM] * n,
        out_shape=[S((4, g.shape[1] // 2, g.shape[2]), g.dtype) for g in grads],
        scratch_shapes=[dma((n,)), dma((n,))], name="rs_pair")(*grads)


def _rs_pair_start(name, grads, dep):
    n = len(grads)

    def body(*refs):
        g = refs[:n]
        ssem, rsem = refs[2 * n + 1], refs[2 * n + 2]
        land = refs[3 * n + 3:4 * n + 3]
        token = refs[4 * n + 3]
        for t in range(n):
            _pair_copy(g[t], land[t], ssem, rsem, t).start()
        token[...] = jnp.zeros_like(token)

    lands = [lax.empty((4, a.shape[1] // 2, a.shape[2]), a.dtype) for a in grads]
    dma = pltpu.SemaphoreType.DMA
    outs = pl.pallas_call(
        body, name=name,
        out_shape=(dma((n,)), dma((n,))) + tuple(pltpu.HBM(a.shape, a.dtype) for a in list(grads) + lands) + (S((8, 128), F32),),
        in_specs=[HBM] * (2 * n) + [ANY], out_specs=(SEM, SEM) + (HBM,) * (2 * n) + (VM,),
        input_output_aliases={i: 2 + i for i in range(2 * n)},
        compiler_params=pltpu.CompilerParams(has_side_effects=EFFECT),
    )(*[_hbm(a) for a in grads], *[_hbm(a) for a in lands], dep)
    return outs[0], outs[1], outs[2:2 + n], outs[2 + n:2 + 2 * n], outs[-1]


def _rs_pair_wait(name, ssem, rsem, grads, lands, after):
    n = len(grads)

    def body(*refs):
        g, land = refs[:n], refs[n:2 * n]
        ssem, rsem = refs[2 * n], refs[2 * n + 1]
        for t in range(n):
            cp = _pair_copy(g[t], land[t], ssem, rsem, t)
            cp.wait_send()
            cp.wait_recv()

    outs = pl.pallas_call(
        body, name=name,
        out_shape=tuple(pltpu.HBM(a.shape, a.dtype) for a in list(grads) + list(lands)),
        in_specs=[HBM] * (2 * n) + [SEM, SEM, ANY], out_specs=(HBM,) * (2 * n),
        input_output_aliases={i: i for i in range(2 * n)},
        compiler_params=pltpu.CompilerParams(has_side_effects=EFFECT),
    )(*grads, *lands, ssem, rsem, after)
    return outs[:n], outs[n:]


def _rs_copy(s, land, ssem, rsem, t, j):
    x, y, c, _ = _place()
    px, py = _peer(x, y, j)
    return pltpu.make_async_remote_copy(src_ref=s[t].at[2 * px + py], dst_ref=land[t].at[j],
                                        send_sem=ssem.at[3 * t + j], recv_sem=rsem.at[3 * t + j], device_id=(px, py, c), device_id_type=MESH)


def _rs_start(name, sums, dep):
    n = len(sums)

    def body(*refs):
        s = refs[:n]
        ssem, rsem = refs[2 * n + 1], refs[2 * n + 2]
        land = refs[3 * n + 3:4 * n + 3]
        token = refs[4 * n + 3]
        for t in range(n):
            for j in range(3):
                _rs_copy(s, land, ssem, rsem, t, j).start()
        token[...] = jnp.zeros_like(token)

    lands = [lax.empty((3,) + a.shape[1:], a.dtype) for a in sums]
    dma = pltpu.SemaphoreType.DMA
    outs = pl.pallas_call(
        body, name=name,
        out_shape=(dma((3 * n,)), dma((3 * n,))) + tuple(pltpu.HBM(a.shape, a.dtype) for a in list(sums) + lands) + (S((8, 128), F32),),
        in_specs=[HBM] * (2 * n) + [ANY], out_specs=(SEM, SEM) + (HBM,) * (2 * n) + (VM,),
        input_output_aliases={i: 2 + i for i in range(2 * n)},
        compiler_params=pltpu.CompilerParams(has_side_effects=EFFECT),
    )(*[_hbm(a) for a in sums], *[_hbm(a) for a in lands], dep)
    return outs[0], outs[1], outs[2:2 + n], outs[2 + n:2 + 2 * n], outs[-1]


def _rs_wait(name, ssem, rsem, sums, lands, after):
    n = len(sums)

    def body(*refs):
        s, land = refs[:n], refs[n:2 * n]
        ssem, rsem = refs[2 * n], refs[2 * n + 1]
        for t in range(n):
            for j in range(3):
                cp = _rs_copy(s, land, ssem, rsem, t, j)
                cp.wait_send()
                cp.wait_recv()

    outs = pl.pallas_call(
        body, name=name,
        out_shape=tuple(pltpu.HBM(a.shape, a.dtype) for a in list(sums) + list(lands)),
        in_specs=[HBM] * (2 * n) + [SEM, SEM, ANY], out_specs=(HBM,) * (2 * n),
        input_output_aliases={i: i for i in range(2 * n)},
        compiler_params=pltpu.CompilerParams(has_side_effects=EFFECT),
    )(*sums, *lands, ssem, rsem, after)
    return outs[n:]


def _rs_join(halves, lo, hi, dep):
    def body(*refs):
        h = refs[:N_BIG]
        out = refs[N_BIG + 1:2 * N_BIG + 1]
        ssem, rsem = refs[2 * N_BIG + 1:]
        x, y, c, _ = _place()

        def cp(t):
            return pltpu.make_async_remote_copy(
                src_ref=h[t].at[pl.ds(lo, hi - lo)], dst_ref=out[t], send_sem=ssem.at[t], recv_sem=rsem.at[t],
                device_id=(x, y, 1 - c), device_id_type=MESH)

        for t in range(N_BIG):
            cp(t).start()
        for t in range(N_BIG):
            cp(t).wait()

    dma = pltpu.SemaphoreType.DMA
    return pl.pallas_call(
        body, in_specs=[HBM] * N_BIG + [ANY], out_specs=[HBM] * N_BIG,
        out_shape=[S((hi - lo,) + h.shape[1:], h.dtype) for h in halves],
        scratch_shapes=[dma((N_BIG,)), dma((N_BIG,))], name="rs_join")(*halves, dep)


def _all_reduce_small(pack, dep):
    R = pack.shape[0]
    rb = _tile(R, (512, 256, 128, 8))

    def body(x_ref, dep_ref, all_ref, sum_ref, send_sems, recv_sems, local_sem):
        x, y, c = lax.axis_index("x"), lax.axis_index("y"), lax.axis_index("c")
        me, sibling = (x, y, c), (x, y, 1 - c)
        chips = [(1 - x, y), (x, 1 - y), (1 - x, 1 - y)]

        def rows(px, py, pc):
            return all_ref.at[pl.ds((4 * px + 2 * py + pc) * R, R), :]

        def copy(k, block, to, src=None):
            return pltpu.make_async_remote_copy(
                src_ref=rows(*block) if src is None else src, dst_ref=rows(*block),
                send_sem=send_sems.at[k], recv_sem=recv_sems.at[k], device_id=to, device_id_type=MESH)

        mine = pltpu.make_async_copy(x_ref, rows(*me), local_sem)
        mine.start()
        first = [copy(0, me, sibling, src=x_ref)]
        first += [copy(1 + j, me, (*chip, c), src=x_ref) for j, chip in enumerate(chips)]
        for cp in first:
            cp.start()
        passed = [copy(4 + j, (*chip, c), sibling) for j, chip in enumerate(chips)]
        for j, chip in enumerate(chips):
            copy(1 + j, (*chip, c), me).wait_recv()
            passed[j].start()
        copy(0, sibling, me).wait_recv()
        for j, chip in enumerate(chips):
            copy(4 + j, (*chip, 1 - c), me).wait_recv()
        for cp in first + passed:
            cp.wait_send()
        mine.wait()

        def step(b, carry):
            off = pl.multiple_of(b * rb, rb)
            acc = all_ref[pl.ds(off, rb), :]
            for k in range(1, 8):
                acc = acc + all_ref[pl.ds(pl.multiple_of(k * R + off, 8), rb), :]
            sum_ref[pl.ds(off, rb), :] = acc
            return carry

        lax.fori_loop(0, R // rb, step, 0)

    vm = pl.BlockSpec(memory_space=pltpu.VMEM)
    dma = pltpu.SemaphoreType.DMA
    _, total = pl.pallas_call(
        body, in_specs=[vm, pl.BlockSpec(memory_space=pl.ANY)], out_specs=[vm, vm],
        out_shape=[S((8 * R, 128), F32), S((R, 128), F32)],
        scratch_shapes=[dma((7,)), dma((7,)), dma],
        compiler_params=_cp(), name="allreduce_small")(pack, dep)
    return total


def _row_tile(rh, cc, tile_bytes=3 * 1024 * 1024 // 2):
    for t in (1024, 704, 512, 352, 256, 176, 128, 64, 32, 16):
        if rh % t == 0 and t * cc * 4 <= tile_bytes:
            return t
    return 16


def _my_chip():
    return 2 * lax.axis_index("x") + lax.axis_index("y")


def _pair_sum(g, recv):
    _, r, cc = g.shape
    rh = r // 2
    tb = _row_tile(rh, cc, 6 * 1024 * 1024)
    nbh = rh // tb

    def body(g_ref, r_ref, o_ref):
        o_ref[...] = (g_ref[...].astype(F32) + r_ref[...].astype(F32)).astype(BF16)

    def chip(k):
        return (_my_chip() + 1 + k) % 4

    mine = _bs((None, tb, cc), lambda k, i: (chip(k), lax.axis_index("c") * nbh + i, 0))
    plain = _bs((None, tb, cc), lambda k, i: (chip(k), i, 0))
    return pl.pallas_call(body, grid=(3, nbh), in_specs=[mine, plain], out_specs=plain,
                          out_shape=S((4, rh, cc), BF16), compiler_params=_cp(), name="rs_pair_sum")(g, recv)


def _owner_sum(g, recv, ici, acc, l):
    _, r, cc = g.shape
    rh = r // 2
    tb = _row_tile(rh, cc, 3 * 1024 * 1024)
    nbh = rh // tb

    def body(g_ref, r_ref, i0_ref, i1_ref, i2_ref, acc_ref, o_ref):
        s = g_ref[...].astype(F32) + r_ref[...].astype(F32)
        o_ref[...] = s + i0_ref[...].astype(F32) + i1_ref[...].astype(F32) + i2_ref[...].astype(F32)

    def slot(j):
        return _bs((None, tb, cc), lambda i: (j, i, 0))

    return pl.pallas_call(
        body, grid=(nbh,),
        in_specs=[_bs((None, tb, cc), lambda i: (_my_chip(), lax.axis_index("c") * nbh + i, 0)),
                  _bs((None, tb, cc), lambda i: (_my_chip(), i, 0)),
                  slot(0), slot(1), slot(2), pl.BlockSpec(memory_space=pl.ANY)],
        out_specs=_bs((None, tb, cc), lambda i: (l, i, 0)),
        out_shape=S(acc.shape, F32), input_output_aliases={5: 0},
        compiler_params=_cp(), name="rs_owner_sum")(g, recv, ici, ici, ici, acc)


def _adam_math(w, g, m, v):
    m = B1 * m + (1.0 - B1) * g
    v = B2 * v + (1.0 - B2) * (g * g)
    m_hat = m / (1.0 - B1 ** STEP)
    v_hat = v / (1.0 - B2 ** STEP)
    delta = -LR * (m_hat / (jnp.sqrt(v_hat) + AEPS) + WD * w)
    return delta, m, v


def _adamw_big(w, g_mine, g_sib, m, v, lo, hi, prev):
    L, r, cc = w.shape
    rh = r // 2
    tb = _row_tile(rh, cc)
    nbh = rh // tb

    def body(w_ref, gm_ref, gs_ref, m_ref, v_ref, *rest):
        go_ref, d_ref, mo_ref, vo_ref = rest[-4:]
        mine = pl.program_id(1) == lax.axis_index("c")
        g = jnp.where(mine, gm_ref[...], gs_ref[...])
        d, m, v = _adam_math(w_ref[...], g, m_ref[...], v_ref[...])
        go_ref[...] = g
        d_ref[...] = d
        mo_ref[...] = m
        vo_ref[...] = v

    def mine_map(l, hf, i):
        c = lax.axis_index("c")
        return (l + lo, jnp.where(hf == c, i, jnp.where(c == 0, nbh - 1, 0)), 0)

    def sib_map(l, hf, i):
        c = lax.axis_index("c")
        return (l, jnp.where(hf != c, i, jnp.where(c == 0, 0, nbh - 1)), 0)

    full = _bs((None, tb, cc), lambda l, hf, i: (l + lo, hf * nbh + i, 0))
    extra = [] if prev is None else list(prev)
    return pl.pallas_call(
        body, grid=(hi - lo, 2, nbh),
        in_specs=[full, _bs((None, tb, cc), mine_map), _bs((None, tb, cc), sib_map), full, full]
        + [pl.BlockSpec(memory_space=pl.ANY)] * len(extra),
        out_specs=[full] * 4, out_shape=[S(w.shape, F32)] * 4,
        input_output_aliases={5 + k: k for k in range(len(extra))},
        compiler_params=_cp(), name="adamw_big")(w, g_mine, g_sib, m, v, *extra)


def _adamw_small(w, g, m, v):
    R = w.shape[0]
    tb = _tile(R, (512, 256, 128, 8))

    def body(w_ref, g_ref, m_ref, v_ref, d_ref, mo_ref, vo_ref):
        d, m, v = _adam_math(w_ref[...], g_ref[...], m_ref[...], v_ref[...])
        d_ref[...] = d
        mo_ref[...] = m
        vo_ref[...] = v

    spec = _bs((tb, 128), lambda i: (i, 0))
    return pl.pallas_call(body, grid=(R // tb,), in_specs=[spec] * 4, out_specs=[spec] * 3,
                          out_shape=[S((R, 128), F32)] * 3, compiler_params=_cp(), name="adamw_small")(w, g, m, v)


def _mix_pad(w):
    return jnp.concatenate([w[:, :4608], w[:, 4616:DIN], w[:, 4608:4616], jnp.zeros((D, PW - DIN), w.dtype)], axis=1)


def _mix_unpad(g):
    return jnp.concatenate([g[:, :4608], g[:, 5632:5640], g[:, 4608:5632]], axis=1)


def _pack(parts):
    flat = jnp.concatenate([p.reshape(-1).astype(F32) for p in parts])
    n = flat.shape[0]
    total = -(-n // (512 * 128)) * (512 * 128)
    return jnp.pad(flat, (0, total - n)).reshape(total // 128, 128)


def _unpack(pack, shapes):
    flat = pack.reshape(-1)
    out, off = [], 0
    for s in shapes:
        n = math.prod(s)
        out.append(flat[off:off + n].reshape(s))
        off += n
    return out


def _ffn_backward(dy, dyb, saved, gain, win, wout, dep):
    x, h, rstd, zg, zu, act = saved
    dz = _ffn_bwd_dz(dyb, wout, zg, zu, dep)
    dwout = _mm_tn("ffn_bwd_dwout", act, dyb, scale=0.5, tm=512, tn=1024)
    dwin = _ffn_bwd_dwin(h, dz)
    dh = _ffn_bwd_dh(dz, win)
    dx, dxb, dgain = _rms_bwd(dh, x, rstd, gain, dy)
    return dx, dxb, dgain, dwin, dwout


def _mixer_forward(x, p):
    T = x.shape[0]
    h, rstd = _rms_fwd(x, p["norm_mix"])
    tm = _tile(T, (1024, 512))
    tn = 1152
    P = _mm("mix_in", h, p["wmix"],
            _bs((tm, D), lambda i, j, k: (i, 0)), _bs((D, tn), lambda i, j, k: (0, j)),
            _bs((tm, tn), lambda i, j, k: (i, j)), S((T, PW), F32), (T // tm, PW // tn, 1), NN, 1, (tm, tn))
    yc = _conv_fwd(P, p["cw"])
    cum = _fgate_fwd(P, p["fb"])
    cumt = cum[:, :NH].T
    tq = _att_tile(T)
    cumq, cumk = cumt.reshape(NH, T, 1), cumt.reshape(NH, T // tq, 1, tq)
    qkv = P[:, C_Q * 128:C_GATE * 128].astype(BF16)
    ya, lse = _attn_fwd(qkv, cumq, cumk)
    yl, hst = _lru_fwd(P, p["lcw"], p["lvec"], p["lru_w_a"], p["lru_w_x"])
    yn, r0, r1, r2 = _gnorm_fwd(yc, ya, yl, p["mix_out_norm"])
    y = _mm("mix_out", yn, p["wo"],
            _bs((tm, D), lambda i, j, k: (i, 0)), _bs((D, 1024), lambda i, j, k: (0, j)),
            _bs((tm, 1024), lambda i, j, k: (i, j)), S((T, D), F32), (T // tm, D // 1024, 1), NN, 1, (tm, 1024),
            res=x, r_spec=_bs((tm, 1024), lambda i, j, k: (i, j)))
    return y, (x, h, rstd, P, qkv, cumq, cumk, lse, yc, ya, yl, hst, yn, r0, r1, r2)


def _mixer_backward(dy, dyb, saved, p, dep):
    x, h, rstd, P, qkv, cumq, cumk, lse, yc, ya, yl, hst, yn, r0, r1, r2 = saved
    T = x.shape[0]
    dyn = _mm_nt_full("mix_bwd_dyn", dyb, p["wo"], 512, dep=dep)
    dwo = _mm_tn("mix_bwd_dwo", yn, dyb, tm=512, tn=1024)
    dyc, dya, dyl, dgn = _gnorm_bwd(dyn, yc, ya, yl, r0, r1, r2, p["mix_out_norm"])
    dcb, dcc, dcv, dcw = _conv_bwd(P, p["cw"], dyc)
    dq, dk, dv, dck, dcq = _attn_bwd(qkv, cumq, cumk, lse, ya, dya)
    dcum = jnp.pad((dck.reshape(NH, T) + dcq.reshape(NH, T)).T, ((0, 0), (0, 128 - NH)))
    df, dfb = _fgate_bwd(P, p["fb"], dcum)
    dgate, dlx, lsm, dwa, dwx = _lru_bwd(P, p["lcw"], p["lvec"], p["lru_w_a"], p["lru_w_x"], hst, dyl)
    dP = jnp.concatenate([dcb, dcc, dcv, dq, dk, dv, dgate, dlx, df], axis=1).astype(BF16)
    tm = _tile(T, (512,))
    dh = _mm("mix_bwd_dh", dP, p["wmix"],
             _bs((tm, PW), lambda j, i, k: (i, 0)), _bs((1024, PW), lambda j, i, k: (j, 0)),
             _bs((tm, 1024), lambda j, i, k: (i, j)), S((T, D), F32), (D // 1024, T // tm, 1), NT, 1, (tm, 1024))
    dwmix = _mm_tn("mix_bwd_dwmix", h, dP, tm=512, tn=1152)
    dx, dxb, dgm = _rms_bwd(dh, x, rstd, p["norm_mix"], dy)
    small = dict(norm_mix=dgm[0], mix_out_norm=dgn[0], conv_w=dcw[:3], fgate_b=dfb[0, :NH], lru_conv_w=lsm[:4],
                 lru_conv_b=lsm[4], lru_b_a=lsm[5], lru_b_x=lsm[6], lru_lambda=lsm[7], lru_w_a=dwa, lru_w_x=dwx)
    return dx, dxb, small, dwmix, dwo


BIG =("ffn1_w_in", "ffn1_w_out", "mix_w_in", "mix_w_out", "ffn2_w_in", "ffn2_w_out")
SMALL = ("norm_ffn1", "norm_mix", "conv_w", "fgate_b", "lru_conv_w", "lru_conv_b", "lru_w_a", "lru_b_a", "lru_w_x",
         "lru_b_x", "lru_lambda", "mix_out_norm", "norm_ffn2", "final_norm")
WEIGHTS = ("norm_ffn1", "ffn1_w_in", "ffn1_w_out", "norm_mix", "mix_w_in", "conv_w", "fgate_b", "lru_conv_w", "lru_conv_b",
           "lru_w_a", "lru_b_a", "lru_w_x", "lru_b_x", "lru_lambda", "mix_out_norm", "mix_w_out", "norm_ffn2", "ffn2_w_in",
           "ffn2_w_out", "final_norm")


def _step(args):
    xx, yy, cc_ = lax.axis_index("x"), lax.axis_index("y"), lax.axis_index("c")
    me = 2 * xx + yy
    x0 = args["x"][0]
    tgt = args["loss_target"][0]
    T = x0.shape[0]
    L = args["norm_ffn1"].shape[0]

    def ag_sources(l):
        small = jnp.concatenate([args["conv_w"][l], args["lru_conv_w"][l], jnp.zeros((1, 128), F32)], axis=0)
        return [args[n][l].astype(BF16) for n in BIG] + [small]

    def layer_params(l, gat, gsm):
        w1i, w1o, wmx, wo, w2i, w2o = gat
        cwl = gsm.transpose(1, 0, 2).reshape(8, 4 * 128)
        return dict(
            w1i=w1i, w1o=w1o.reshape(-1, D), w2i=w2i, w2o=w2o.reshape(-1, D), wo=wo.reshape(D, D),
            wmix=_mix_pad(wmx.transpose(1, 0, 2).reshape(D, DIN)),
            cw=jnp.concatenate([cwl[:3], jnp.zeros((5, DC), F32)], axis=0),
            lcw=jnp.concatenate([cwl[3:7], jnp.zeros((4, DL), F32)], axis=0),
            fb=jnp.pad(args["fgate_b"][l], (0, 128 - NH)).reshape(1, 128),
            lvec=jnp.concatenate([args["lru_conv_b"][l][None], args["lru_b_a"][l][None], args["lru_b_x"][l][None],
                                  args["lru_lambda"][l][None], jnp.zeros((4, DL), F32)], axis=0),
            lru_w_a=args["lru_w_a"][l], lru_w_x=args["lru_w_x"][l],
            norm_ffn1=args["norm_ffn1"][l][None], norm_mix=args["norm_mix"][l][None],
            mix_out_norm=args["mix_out_norm"][l][None], norm_ffn2=args["norm_ffn2"][l][None])

    xs = x0
    saved, layers = [], []
    def ici_done(name, idx, flight, after):
        ssem, rsem, srcs, lands, _ = flight
        s, ld = _ag_wait("ag_wait_" + name, idx, ssem, rsem, [srcs[t] for t in idx], [lands[t] for t in idx], after)
        return _ag_fwd_start("ag_fwd_start_" + name, idx, s, ld, after)

    def gathered(name, idx, fwd, after):
        ssem, rsem, s, ld, _ = fwd
        return _ag_fwd_wait("ag_fwd_wait_" + name, idx, ssem, rsem, s, ld, after)

    every_w = tuple(range(N_AG))
    flight = _ag_start(0, ag_sources(0), x0)
    fwd = None
    for l in range(L):
        if l == 0:
            (w1i,) = gathered("0i", (0,), ici_done("0i", (0,), flight, xs), xs)
        else:
            w1i, w1o, wmx, wo, w2i, w2o, gsm = gathered(str(l), every_w, fwd, xs)
        nxt = _ag_start(l + 1, ag_sources(l + 1), w1i) if l + 1 < L else None
        h1, rstd1 = _rms_fwd(xs, args["norm_ffn1"][l][None])
        zg1, zu1, act1 = _ffn_in(h1, w1i)
        if l == 0:
            (w1o,) = gathered("0o", (1,), ici_done("0o", (1,), flight, act1), act1)
        x1 = _ffn_out(act1, w1o.reshape(-1, D), xs)
        s1 = (xs, h1, rstd1, zg1, zu1, act1)
        if l == 0:
            gsm, wmx, wo, w2i, w2o = gathered("0b", AG_REST, ici_done("0b", AG_REST, flight, x1), x1)
        p = layer_params(l, (w1i, w1o, wmx, wo, w2i, w2o), gsm)
        x2, s2 = _mixer_forward(x1, p)
        h2, rstd2 = _rms_fwd(x2, p["norm_ffn2"])
        zg2, zu2, act2 = _ffn_in(h2, p["w2i"])
        fwd = ici_done(str(l + 1), every_w, nxt, act2) if nxt is not None else None
        x3 = _ffn_out(act2, p["w2o"], x2, dep=None if fwd is None else fwd[4])
        s3 = (x2, h2, rstd2, zg2, zu2, act2)
        saved.append((s1, s2, s3))
        layers.append(p)
        xs = x3
    lpart, dx, dxb, dfinal = _loss_head(xs, args["final_norm"][None], tgt)
    loss = lax.psum(lpart[0, 0], ("x", "y", "c"))

    acc = [None] * N_BIG
    small_grads = [None] * L
    every = tuple(range(N_BIG))
    in_air = []
    pair_flight = None
    dep = lpart

    def to_blocks(t, g):
        if t in (1, 5):
            return g.reshape(4, g.shape[0] // 4, D)
        if t == 2:
            return _mix_unpad(g).reshape(D, 4, DIN // 4).transpose(1, 0, 2)
        if t == 3:
            return g.reshape(4, D // 4, D)
        return g

    def launch(name, l, idx, grads, recv, after):
        sums = [_pair_sum(g, r) for g, r in zip(grads, recv)]
        started = _rs_start("rs_start_" + name, sums, after)
        in_air.append(("rs_wait_" + name, l, idx, grads, recv, started))
        return started[4]

    def land(entry, after):
        name, l, idx, grads, recv, (ssem, rsem, sums, lands, _) = entry
        ici = _rs_wait(name, ssem, rsem, sums, lands, after)
        for t, g, r, i3 in zip(idx, grads, recv, ici):
            if acc[t] is None:
                acc[t] = jnp.zeros((L, g.shape[1] // 2, g.shape[2]), F32)
            acc[t] = _owner_sum(g, r, i3, acc[t], l)

    for l in reversed(range(L)):
        p = layers[l]
        s1, s2, s3 = saved[l]
        dx, dxb, dg2, dw2i, dw2o = _ffn_backward(dx, dxb, s3, p["norm_ffn2"], p["w2i"], p["w2o"], dep)
        if pair_flight is not None:
            lp, (ssem, rsem, g_thru, lands, _) = pair_flight
            g_thru, recv = _rs_pair_wait(f"rs_pair_wait_{lp}", ssem, rsem, g_thru, lands, dx)
            dep = launch(str(lp), lp, every, g_thru, recv, dx)
            pair_flight = None
        if l == 0:
            part = [to_blocks(4, dw2i), to_blocks(5, dw2o)]
            dep = launch("0c", 0, (4, 5), part, _rs_pair(part), dx)
        dx, dxb, sm, dwmix, dwo = _mixer_backward(dx, dxb, s2, p, dep)
        if l == 0:
            part = [to_blocks(2, dwmix), to_blocks(3, dwo)]
            dep = launch("0b", 0, (2, 3), part, _rs_pair(part), dx)
        dx, dxb, dg1, dw1i, dw1o = _ffn_backward(dx, dxb, s1, p["norm_ffn1"], p["w1i"], p["w1o"], dep)
        sm["norm_ffn1"] = dg1[0]
        sm["norm_ffn2"] = dg2[0]
        small_grads[l] = sm
        for entry in [e for e in in_air if e[1] > l]:
            land(entry, dx)
            in_air.remove(entry)
        if l > 0:
            grads = [to_blocks(t, g) for t, g in enumerate((dw1i, dw1o, dwmix, dwo, dw2i, dw2o))]
            pair_flight = (l, _rs_pair_start(f"rs_pair_start_{l}", grads, dx))
            dep = pair_flight[1][4]

    part = [to_blocks(0, dw1i), to_blocks(1, dw1o)]
    dep = launch("0a", 0, (0, 1), part, _rs_pair(part), dx)

    def adamw(k, lo, hi, sib, prev):
        n = BIG[k]
        return _adamw_big(args[n], acc[k], sib[k], args["m_" + n], args["v_" + n], lo, hi, prev)

    res = [None] * N_BIG
    after = dx
    if L > 1:
        sib = _rs_join(acc, 1, L, dep)
        for k in range(N_BIG):
            res[k] = adamw(k, 1, L, sib, None)
        after = res[N_BIG - 1][0]
    full = {n: (dfinal[0] if n == "final_norm" else jnp.stack([small_grads[l][n] for l in range(L)])) for n in SMALL}
    red_pack = _all_reduce_small(_pack([full[n] for n in SMALL]), after)
    for entry in list(in_air):
        land(entry, red_pack)
    sib = _rs_join(acc, 0, 1, dx)
    out = {"loss": loss, "grad_x": dx[None]}
    for k, n in enumerate(BIG):
        outs = adamw(k, 0, 1, sib, res[k])
        out["grad_" + n], out["delta_" + n], out["new_m_" + n], out["new_v_" + n] = outs

    shapes = [full[n].shape for n in SMALL]
    red = dict(zip(SMALL, _unpack(red_pack, shapes)))
    for n in ("conv_w", "lru_conv_w"):
        red[n] = lax.dynamic_slice_in_dim(red[n], me * 128, 128, axis=2)
    oshapes = [args[n].shape for n in SMALL]
    d, m, v = _adamw_small(_pack([args[n] for n in SMALL]), _pack([red[n] for n in SMALL]),
                           _pack([args["m_" + n] for n in SMALL]), _pack([args["v_" + n] for n in SMALL]))
    for n, gg, dd, mm, vv in zip(SMALL, [red[n] for n in SMALL], _unpack(d, oshapes), _unpack(m, oshapes), _unpack(v, oshapes)):
        out["grad_" + n], out["delta_" + n], out["new_m_" + n], out["new_v_" + n] = gg, dd, mm, vv
    return out


def kernel(x, norm_ffn1, ffn1_w_in, ffn1_w_out, norm_mix, mix_w_in, conv_w, fgate_b, lru_conv_w, lru_conv_b, lru_w_a, lru_b_a, lru_w_x, lru_b_x, lru_lambda, mix_out_norm, mix_w_out, norm_ffn2, ffn2_w_in, ffn2_w_out, final_norm, loss_target, m_norm_ffn1, m_ffn1_w_in, m_ffn1_w_out, m_norm_mix, m_mix_w_in, m_conv_w, m_fgate_b, m_lru_conv_w, m_lru_conv_b, m_lru_w_a, m_lru_b_a, m_lru_w_x, m_lru_b_x, m_lru_lambda, m_mix_out_norm, m_mix_w_out, m_norm_ffn2, m_ffn2_w_in, m_ffn2_w_out, m_final_norm, v_norm_ffn1, v_ffn1_w_in, v_ffn1_w_out, v_norm_mix, v_mix_w_in, v_conv_w, v_fgate_b, v_lru_conv_w, v_lru_conv_b, v_lru_w_a, v_lru_b_a, v_lru_w_x, v_lru_b_x, v_lru_lambda, v_mix_out_norm, v_mix_w_out, v_norm_ffn2, v_ffn2_w_in, v_ffn2_w_out, v_final_norm):
    args = dict(locals())
    out = _step(args)
    res = [out["loss"], out["grad_x"]]
    for prefix in ("grad_", "delta_", "new_m_", "new_v_"):
        res += [out[prefix + n] for n in WEIGHTS]
    return tuple(res)
```

```python
import functools
import math

import jax
import jax.numpy as jnp
from jax import lax
from jax.experimental import pallas as pl
from jax.experimental.pallas import tpu as pltpu

F32 = jnp.float32
BF16 = jnp.bfloat16
S = jax.ShapeDtypeStruct
MESH = pl.DeviceIdType.MESH

D = 2048
DC = 512
DA = 1024
NH = 8
HD = 128
DL = 512
LB = 128
DIN = 5640
PW = 5760
C_Q, C_K, C_V = 12, 20, 28
C_GATE, C_LX, C_F = 36, 40, 44
EPS = 1e-6
LRU_C = 8.0
ATT_SCALE = HD ** -0.5
LR, B1, B2, AEPS, WD, STEP = 0.001, 0.9, 0.999, 1e-08, 0.01, 10
VMEM_LIMIT = 56 * 1024 * 1024

NT = (((1,), (1,)), ((), ()))
TN = (((0,), (0,)), ((), ()))
NN = (((1,), (0,)), ((), ()))


def _cp():
    return pltpu.CompilerParams(vmem_limit_bytes=VMEM_LIMIT)


def _bs(shape, fn):
    return pl.BlockSpec(shape, fn)


def _mm(name, a, b, a_spec, b_spec, o_spec, o_shape, grid, dims, nk, acc_tile, scale=1.0, res=None, r_spec=None, dep=None):
    has_res = res is not None
    has_dep = dep is not None

    def body(*refs):
        if has_dep:
            refs = refs[:2 + has_res] + refs[3 + has_res:]
        if has_res:
            a_ref, b_ref, r_ref, o_ref = refs[:4]
            rest = refs[4:]
        else:
            a_ref, b_ref, o_ref = refs[:3]
            rest = refs[3:]
        prod = lax.dot_general(a_ref[...].astype(BF16), b_ref[...].astype(BF16), dims, preferred_element_type=F32)

        def finish(acc):
            if scale != 1.0:
                acc = acc * scale
            if has_res:
                acc = r_ref[...] + acc
            o_ref[...] = acc.astype(o_ref.dtype)

        if nk == 1:
            finish(prod)
        else:
            acc_ref = rest[0]
            k = pl.program_id(2)

            @pl.when(k == 0)
            def _():
                acc_ref[...] = prod

            @pl.when(k > 0)
            def _():
                acc_ref[...] += prod

            @pl.when(k == nk - 1)
            def _():
                finish(acc_ref[...])

    in_specs = [a_spec, b_spec] + ([r_spec] if has_res else []) + ([pl.BlockSpec(memory_space=pl.ANY)] if has_dep else [])
    args = (a, b) + ((res,) if has_res else ()) + ((dep,) if has_dep else ())
    scratch = [pltpu.VMEM(acc_tile, F32)] if nk > 1 else []
    return pl.pallas_call(body, grid=grid, in_specs=in_specs, out_specs=o_spec, out_shape=o_shape,
                          scratch_shapes=scratch, compiler_params=_cp(), name=name)(*args)


def _tile(n, pref):
    for t in pref:
        if n % t == 0:
            return t
    return n


def _rms_fwd(x, gain):
    T = x.shape[0]
    tb = _tile(T, (512,))

    def body(x_ref, g_ref, h_ref, r_ref):
        xv = x_ref[...]
        r = lax.rsqrt(jnp.mean(xv * xv, axis=1, keepdims=True) + EPS)
        h_ref[...] = (xv * r * g_ref[...]).astype(BF16)
        r_ref[...] = r

    return pl.pallas_call(
        body, grid=(T // tb,),
        in_specs=[_bs((tb, D), lambda i: (i, 0)), _bs((1, D), lambda i: (0, 0))],
        out_specs=[_bs((tb, D), lambda i: (i, 0)), _bs((tb, 1), lambda i: (i, 0))],
        out_shape=[S((T, D), BF16), S((T, 1), F32)], compiler_params=_cp(), name="rms_fwd")(x, gain)


def _rms_bwd(dh, x, rstd, gain, dres):
    T = x.shape[0]
    tb = _tile(T, (512,))

    def body(dh_ref, x_ref, r_ref, g_ref, dres_ref, dx_ref, dxb_ref, dg_ref):
        i = pl.program_id(0)
        r = r_ref[...]
        xhat = x_ref[...] * r
        dh = dh_ref[...]
        dxh = dh * g_ref[...]
        m = jnp.mean(dxh * xhat, axis=1, keepdims=True)
        dx = dres_ref[...] + r * (dxh - xhat * m)
        dx_ref[...] = dx
        dxb_ref[...] = dx.astype(BF16)
        part = jnp.sum(dh * xhat, axis=0, keepdims=True)

        @pl.when(i == 0)
        def _():
            dg_ref[...] = part

        @pl.when(i > 0)
        def _():
            dg_ref[...] += part

    row = _bs((tb, D), lambda i: (i, 0))
    return pl.pallas_call(
        body, grid=(T // tb,),
        in_specs=[row, row, _bs((tb, 1), lambda i: (i, 0)), _bs((1, D), lambda i: (0, 0)), row],
        out_specs=[row, row, _bs((1, D), lambda i: (0, 0))],
        out_shape=[S((T, D), F32), S((T, D), BF16), S((1, D), F32)], compiler_params=_cp(), name="rms_bwd")(dh, x, rstd, gain, dres)


def _loss_head(x, gain, tgt):
    T = x.shape[0]
    tb = _tile(T, (512,))

    def body(x_ref, g_ref, t_ref, l_ref, dx_ref, dxb_ref, dg_ref):
        i = pl.program_id(0)
        xv = x_ref[...]
        g = g_ref[...]
        r = lax.rsqrt(jnp.mean(xv * xv, axis=1, keepdims=True) + EPS)
        xhat = xv * r
        e = xhat * g - t_ref[...]
        lpart = 0.5 * jnp.sum(jnp.sum(e * e, axis=1, keepdims=True), axis=0, keepdims=True) * (1.0 / D)
        dy = e * (1.0 / D)
        dxh = dy * g
        m = jnp.mean(dxh * xhat, axis=1, keepdims=True)
        dx = r * (dxh - xhat * m)
        dx_ref[...] = dx
        dxb_ref[...] = dx.astype(BF16)
        gpart = jnp.sum(dy * xhat, axis=0, keepdims=True)
        lrow = jnp.broadcast_to(lpart, (1, 128))

        @pl.when(i == 0)
        def _():
            dg_ref[...] = gpart
            l_ref[...] = lrow

        @pl.when(i > 0)
        def _():
            dg_ref[...] += gpart
            l_ref[...] += lrow

    row = _bs((tb, D), lambda i: (i, 0))
    return pl.pallas_call(
        body, grid=(T // tb,),
        in_specs=[row, _bs((1, D), lambda i: (0, 0)), row],
        out_specs=[_bs((1, 128), lambda i: (0, 0)), row, row, _bs((1, D), lambda i: (0, 0))],
        out_shape=[S((1, 128), F32), S((T, D), F32), S((T, D), BF16), S((1, D), F32)],
        compiler_params=_cp(), name="loss_head")(x, gain, tgt)


def _sigmoid(z):
    return 0.5 * jnp.tanh(0.5 * z) + 0.5


def _ffn_in(h, win):
    T = h.shape[0]
    Fs = win.shape[2]
    F = 2 * Fs
    tn = _tile(Fs, (256, 128))
    nb = Fs // tn
    tm = _tile(T, (1024, 512))

    def body(h_ref, wg_ref, wu_ref, zg_ref, zu_ref, a_ref):
        hv = h_ref[...]
        zg = jnp.dot(hv, wg_ref[...], preferred_element_type=F32)
        zu = jnp.dot(hv, wu_ref[...], preferred_element_type=F32)
        zg_ref[...] = zg.astype(BF16)
        zu_ref[...] = zu.astype(BF16)
        a_ref[...] = (zg * _sigmoid(zg) * zu).astype(BF16)

    col = _bs((tm, tn), lambda i, j: (i, j))
    return pl.pallas_call(
        body, grid=(T // tm, F // tn),
        in_specs=[_bs((tm, D), lambda i, j: (i, 0)),
                  _bs((None, D, tn), lambda i, j: (j // nb, 0, j % nb)),
                  _bs((None, D, tn), lambda i, j: (2 + j // nb, 0, j % nb))],
        out_specs=[col, col, col],
        out_shape=[S((T, F), BF16)] * 3, compiler_params=_cp(), name="ffn_in")(h, win, win)


def _ffn_out(act, wout, x, dep=None):
    T, F = act.shape
    tm = _tile(T, (512,))
    tn = 1024
    return _mm("ffn_out", act, wout,
               _bs((tm, F), lambda j, i, k: (i, 0)), _bs((F, tn), lambda j, i, k: (0, j)),
               _bs((tm, tn), lambda j, i, k: (i, j)), S((T, D), F32), (D // tn, T // tm, 1), NN, 1, (tm, tn),
               scale=0.5, res=x, r_spec=_bs((tm, tn), lambda j, i, k: (i, j)), dep=dep)


def _ffn_bwd_dz(dyb, wout, zg, zu, dep):
    T, F = zg.shape
    tm = _tile(T, (1024, 512))
    tn = _tile(F, (512, 256))

    def body(dy_ref, w_ref, zg_ref, zu_ref, dep_ref, dz_ref):
        da = 0.5 * lax.dot_general(dy_ref[...], w_ref[...], NT, preferred_element_type=F32)
        zg = zg_ref[...].astype(F32)
        zu = zu_ref[...].astype(F32)
        s = _sigmoid(zg)
        dz_ref[0] = (da * zu * (s * (1.0 + zg * (1.0 - s)))).astype(BF16)
        dz_ref[1] = (da * (zg * s)).astype(BF16)

    col = _bs((tm, tn), lambda i, j: (i, j))
    return pl.pallas_call(
        body, grid=(T // tm, F // tn),
        in_specs=[_bs((tm, D), lambda i, j: (i, 0)), _bs((tn, D), lambda i, j: (j, 0)), col, col,
                  pl.BlockSpec(memory_space=pl.ANY)],
        out_specs=_bs((2, tm, tn), lambda i, j: (0, i, j)), out_shape=S((2, T, F), BF16),
        compiler_params=_cp(), name="ffn_bwd_dz")(dyb, wout, zg, zu, dep)


def _ffn_bwd_dh(dz, win):
    _, T, F = dz.shape
    Fs = win.shape[2]
    tk = _tile(Fs, (2816, 1408, 256, 128))
    nkb = Fs // tk
    tm = _tile(T, (1024, 512))
    tn = 512
    nk = 2 * nkb

    def body(dzg_ref, dzu_ref, wg_ref, wu_ref, o_ref, acc_ref):
        k = pl.program_id(2)
        prod = (lax.dot_general(dzg_ref[...], wg_ref[...], NT, preferred_element_type=F32)
                + lax.dot_general(dzu_ref[...], wu_ref[...], NT, preferred_element_type=F32))

        @pl.when(k == 0)
        def _():
            acc_ref[...] = prod

        @pl.when(k > 0)
        def _():
            acc_ref[...] += prod

        @pl.when(k == nk - 1)
        def _():
            o_ref[...] = acc_ref[...]

    def a_spec(half):
        return _bs((None, tm, tk), lambda i, j, k: (half, i, k))

    return pl.pallas_call(
        body, grid=(T // tm, D // tn, nk),
        in_specs=[a_spec(0), a_spec(1),
                  _bs((None, tn, tk), lambda i, j, k: (k // nkb, j, k % nkb)),
                  _bs((None, tn, tk), lambda i, j, k: (2 + k // nkb, j, k % nkb))],
        out_specs=_bs((tm, tn), lambda i, j, k: (i, j)), out_shape=S((T, D), F32),
        scratch_shapes=[pltpu.VMEM((tm, tn), F32)], compiler_params=_cp(), name="ffn_bwd_dh")(dz, dz, win, win)


def _ffn_bwd_dwin(h, dz):
    _, T, F = dz.shape
    Fs = F // 2
    tn = _tile(Fs, (1408, 256, 128))
    nb = Fs // tn
    tm = 512
    return _mm("ffn_bwd_dwin", h, dz,
               _bs((T, tm), lambda i, j, k: (0, i)), _bs((None, T, tn), lambda i, j, k: (j // (2 * nb), 0, j % (2 * nb))),
               _bs((None, tm, tn), lambda i, j, k: (j // nb, i, j % nb)), S((4, D, Fs), BF16),
               (D // tm, 4 * nb, 1), TN, 1, (tm, tn))


def _mm_tn(name, a, b, scale=1.0, tm=512, tn=1024):
    T, M = a.shape
    N = b.shape[1]
    tm = _tile(M, (tm, 512, 256, 128))
    tn = _tile(N, (tn, 1152, 1024, 512, 128))
    return _mm(name, a, b,
               _bs((T, tm), lambda i, j, k: (0, i)), _bs((T, tn), lambda i, j, k: (0, j)),
               _bs((tm, tn), lambda i, j, k: (i, j)), S((M, N), BF16), (M // tm, N // tn, 1), TN, 1, (tm, tn), scale=scale)


def _mm_nt_full(name, a, b, tn, dep=None):
    T, K = a.shape
    N = b.shape[0]
    tm = _tile(T, (1024, 512))
    return _mm(name, a, b,
               _bs((tm, K), lambda i, j, k: (i, 0)), _bs((tn, K), lambda i, j, k: (j, 0)),
               _bs((tm, tn), lambda i, j, k: (i, j)), S((T, N), F32), (T // tm, N // tn, 1), NT, 1, (tm, tn), dep=dep)


def _bt(T):
    return _tile(T, (512,))


def _down(ext, s):
    return pltpu.roll(ext, s, 0)[8:, :]


def _up(ext, s):
    n = ext.shape[0]
    return pltpu.roll(ext, n - s, 0)[: n - 8, :]


def _halo_prev(ref, start, b):
    lo = pl.multiple_of(jnp.maximum(start - 8, 0), 8)
    return ref[pl.ds(lo, 8), :] * (b > 0).astype(F32)


def _halo_next(ref, start, bt, b, nb):
    lo = pl.multiple_of(jnp.minimum(start + bt, (nb - 1) * bt), 8)
    return ref[pl.ds(lo, 8), :] * (b < nb - 1).astype(F32)


def _scan_fwd(A, U):
    n = U.shape[0]
    row = lax.broadcasted_iota(jnp.int32, U.shape, 0)
    d = 1
    while d < n:
        keep = row >= d
        Us = jnp.where(keep, pltpu.roll(U, d, 0), 0.0)
        if A is None:
            U = U + Us
        else:
            As = jnp.where(keep, pltpu.roll(A, d, 0), 1.0)
            U = A * Us + U
            A = A * As
        d *= 2
    return A, U


def _scan_bwd(A, U):
    n = U.shape[0]
    row = lax.broadcasted_iota(jnp.int32, U.shape, 0)
    d = 1
    while d < n:
        keep = row < n - d
        Us = jnp.where(keep, pltpu.roll(U, n - d, 0), 0.0)
        if A is None:
            U = U + Us
        else:
            As = jnp.where(keep, pltpu.roll(A, n - d, 0), 1.0)
            U = A * Us + U
            A = A * As
        d *= 2
    return A, U


def _softplus(z):
    return jnp.maximum(z, 0.0) + jnp.log(1.0 + jnp.exp(-jnp.abs(z)))


def _gelu_parts(g):
    k0 = math.sqrt(2.0 / math.pi)
    t = jnp.tanh(k0 * (g + 0.044715 * g * g * g))
    gel = 0.5 * g * (1.0 + t)
    dgel = 0.5 * (1.0 + t) + 0.5 * g * (1.0 - t * t) * k0 * (1.0 + 3.0 * 0.044715 * g * g)
    return gel, dgel


def _conv_fwd(P, cw):
    T = P.shape[0]
    bt = _bt(T)
    nb = T // bt

    def body(b_ref, c_ref, v_ref, w_ref, y_ref):
        w = w_ref[...]

        def step(b, carry):
            start = pl.multiple_of(b * bt, bt)
            rows = pl.ds(start, bt)
            m = c_ref[rows, :] * v_ref[rows, :]
            ext = jnp.concatenate([_halo_prev(c_ref, start, b) * _halo_prev(v_ref, start, b), m], axis=0)
            z = w[2:3, :] * m + w[1:2, :] * _down(ext, 1) + w[0:1, :] * _down(ext, 2)
            y_ref[rows, :] = b_ref[rows, :] * z
            return carry

        lax.fori_loop(0, nb, step, 0)

    def colspec(off):
        return _bs((T, 128), lambda c: (0, off + c))

    return pl.pallas_call(
        body, grid=(DC // 128,),
        in_specs=[colspec(0), colspec(4), colspec(8), _bs((8, 128), lambda c: (0, c))],
        out_specs=_bs((T, 128), lambda c: (0, c)), out_shape=S((T, DC), F32),
        compiler_params=_cp(), name="conv_fwd")(P, P, P, cw)


def _conv_bwd(P, cw, dy):
    T = P.shape[0]
    bt = _bt(T)
    nb = T // bt

    def body(b_ref, c_ref, v_ref, w_ref, dy_ref, db_ref, dc_ref, dv_ref, dw_ref):
        w = w_ref[...]

        def step(b, carry):
            a0, a1, a2 = carry
            start = pl.multiple_of(b * bt, bt)
            rows = pl.ds(start, bt)
            cb, cc, cv, dy = b_ref[rows, :], c_ref[rows, :], v_ref[rows, :], dy_ref[rows, :]
            m = cc * cv
            ext = jnp.concatenate([_halo_prev(c_ref, start, b) * _halo_prev(v_ref, start, b), m], axis=0)
            m1, m2 = _down(ext, 1), _down(ext, 2)
            z = w[2:3, :] * m + w[1:2, :] * m1 + w[0:1, :] * m2
            db_ref[rows, :] = dy * z
            dz = dy * cb
            extn = jnp.concatenate([dz, _halo_next(dy_ref, start, bt, b, nb) * _halo_next(b_ref, start, bt, b, nb)], axis=0)
            dm = w[2:3, :] * dz + w[1:2, :] * _up(extn, 1) + w[0:1, :] * _up(extn, 2)
            dc_ref[rows, :] = dm * cv
            dv_ref[rows, :] = dm * cc
            return (a0 + jnp.sum(dz * m2, axis=0, keepdims=True),
                    a1 + jnp.sum(dz * m1, axis=0, keepdims=True),
                    a2 + jnp.sum(dz * m, axis=0, keepdims=True))

        zero = jnp.zeros((1, 128), F32)
        a0, a1, a2 = lax.fori_loop(0, nb, step, (zero, zero, zero))
        dw_ref[...] = jnp.zeros((8, 128), F32)
        dw_ref[0:1, :] = a0
        dw_ref[1:2, :] = a1
        dw_ref[2:3, :] = a2

    def colspec(off):
        return _bs((T, 128), lambda c: (0, off + c))

    own = _bs((T, 128), lambda c: (0, c))
    return pl.pallas_call(
        body, grid=(DC // 128,),
        in_specs=[colspec(0), colspec(4), colspec(8), _bs((8, 128), lambda c: (0, c)), own],
        out_specs=[own, own, own, _bs((8, 128), lambda c: (0, c))],
        out_shape=[S((T, DC), F32)] * 3 + [S((8, DC), F32)], compiler_params=_cp(), name="conv_bwd")(P, P, P, cw, dy)


def _fgate_fwd(P, fb):
    T = P.shape[0]
    bt = _bt(T)
    nb = T // bt

    def body(f_ref, b_ref, c_ref):
        bias = b_ref[...]

        def step(b, carry):
            rows = pl.ds(pl.multiple_of(b * bt, bt), bt)
            logf = -_softplus(-(f_ref[rows, :] + bias))
            _, cs = _scan_fwd(None, logf)
            cs = cs + carry
            c_ref[rows, :] = cs
            return cs[bt - 1:bt, :]

        lax.fori_loop(0, nb, step, jnp.zeros((1, 128), F32))

    return pl.pallas_call(
        body, grid=(1,),
        in_specs=[_bs((T, 128), lambda i: (0, C_F)), _bs((1, 128), lambda i: (0, 0))],
        out_specs=_bs((T, 128), lambda i: (0, 0)), out_shape=S((T, 128), F32),
        compiler_params=_cp(), name="fgate_fwd")(P, fb)


def _fgate_bwd(P, fb, dcum):
    T = P.shape[0]
    bt = _bt(T)
    nb = T // bt

    def body(f_ref, b_ref, dc_ref, df_ref, db_ref):
        bias = b_ref[...]

        def step(i, carry):
            run, acc = carry
            b = nb - 1 - i
            rows = pl.ds(pl.multiple_of(b * bt, bt), bt)
            _, rs = _scan_bwd(None, dc_ref[rows, :])
            rs = rs + run
            df = rs * jax.nn.sigmoid(-(f_ref[rows, :] + bias))
            df_ref[rows, :] = df
            return rs[0:1, :], acc + jnp.sum(df, axis=0, keepdims=True)

        zero = jnp.zeros((1, 128), F32)
        _, acc = lax.fori_loop(0, nb, step, (zero, zero))
        db_ref[...] = acc

    return pl.pallas_call(
        body, grid=(1,),
        in_specs=[_bs((T, 128), lambda i: (0, C_F)), _bs((1, 128), lambda i: (0, 0)), _bs((T, 128), lambda i: (0, 0))],
        out_specs=[_bs((T, 128), lambda i: (0, 0)), _bs((1, 128), lambda i: (0, 0))],
        out_shape=[S((T, 128), F32), S((1, 128), F32)], compiler_params=_cp(), name="fgate_bwd")(P, fb, dcum)


def _att_tile(T):
    return _tile(T, (512,))


def _causal_mask(tq):
    return lax.broadcasted_iota(jnp.int32, (tq, tq), 1) <= lax.broadcasted_iota(jnp.int32, (tq, tq), 0)


def _attn_fwd(qkv, cumq, cumk):
    T = qkv.shape[0]
    tq = _att_tile(T)
    nq = T // tq

    def body(q_ref, k_ref, v_ref, cq_ref, ck_ref, o_ref, lse_ref):
        i = pl.program_id(1)
        q = q_ref[...]
        cq = cq_ref[...]

        def block(j, carry, diagonal):
            m_old, l_old, acc = carry
            rows = pl.ds(pl.multiple_of(j * tq, tq), tq)
            s = lax.dot_general(q, k_ref[rows, :], NT, preferred_element_type=F32)
            s = s * ATT_SCALE + cq - ck_ref[j]
            if diagonal:
                s = jnp.where(_causal_mask(tq), s, -jnp.inf)
            m_new = jnp.maximum(m_old, jnp.max(s, axis=1, keepdims=True))
            p = jnp.exp(s - m_new)
            alpha = jnp.exp(m_old - m_new)
            l_new = alpha * l_old + jnp.sum(p, axis=1, keepdims=True)
            acc = alpha * acc + jnp.dot(p.astype(BF16), v_ref[rows, :], preferred_element_type=F32)
            return m_new, l_new, acc

        init = (jnp.full((tq, 1), -jnp.inf, F32), jnp.zeros((tq, 1), F32), jnp.zeros((tq, HD), F32))
        carry = lax.fori_loop(0, i, lambda j, c: block(j, c, False), init)
        m, l, acc = block(i, carry, True)
        o_ref[...] = acc / l
        lse_ref[...] = m + jnp.log(l)

    return pl.pallas_call(
        body, grid=(NH, nq),
        in_specs=[_bs((tq, HD), lambda h, i: (i, h)),
                  _bs((T, HD), lambda h, i: (0, NH + h)),
                  _bs((T, HD), lambda h, i: (0, 2 * NH + h)),
                  _bs((None, tq, 1), lambda h, i: (h, i, 0)),
                  _bs((None, nq, 1, tq), lambda h, i: (h, 0, 0, 0))],
        out_specs=[_bs((tq, HD), lambda h, i: (i, h)), _bs((None, tq, 1), lambda h, i: (h, i, 0))],
        out_shape=[S((T, DA), F32), S((NH, T, 1), F32)],
        compiler_params=_cp(), name="attn_fwd")(qkv, qkv, qkv, cumq, cumk)


def _attn_bwd(qkv, cumq, cumk, lse, o, do):
    T = qkv.shape[0]
    tq = _att_tile(T)
    nq = T // tq

    def body(q_ref, k_ref, v_ref, cq_ref, ck_ref, lse_ref, o_ref, do_ref, dq_ref, dk_ref, dv_ref, dc_ref, dr_ref):
        j = pl.program_id(1)

        @pl.when(j == 0)
        def _():
            dq_ref[...] = jnp.zeros((T, HD), F32)
            dr_ref[...] = jnp.zeros((T, 1), F32)

        k = k_ref[...]
        v = v_ref[...]
        ck = ck_ref[...]

        def block(i, carry, diagonal):
            dk_acc, dv_acc, dc_acc = carry
            rows = pl.ds(pl.multiple_of(i * tq, tq), tq)
            q = q_ref[rows, :]
            do_f = do_ref[rows, :]
            dob = do_f.astype(BF16)
            s = lax.dot_general(q, k, NT, preferred_element_type=F32)
            p = jnp.exp(s * ATT_SCALE + cq_ref[rows, :] - ck - lse_ref[rows, :])
            if diagonal:
                p = jnp.where(_causal_mask(tq), p, 0.0)
            delta = jnp.sum(do_f * o_ref[rows, :], axis=1, keepdims=True)
            dp = lax.dot_general(dob, v, NT, preferred_element_type=F32)
            ds = p * (dp - delta)
            dsb = (ds * ATT_SCALE).astype(BF16)
            dq_ref[rows, :] += jnp.dot(dsb, k, preferred_element_type=F32)
            dr_ref[rows, :] += jnp.sum(ds, axis=1, keepdims=True)
            return (dk_acc + lax.dot_general(dsb, q, TN, preferred_element_type=F32),
                    dv_acc + lax.dot_general(p.astype(BF16), dob, TN, preferred_element_type=F32),
                    dc_acc - jnp.sum(ds, axis=0, keepdims=True))

        init = (jnp.zeros((tq, HD), F32), jnp.zeros((tq, HD), F32), jnp.zeros((1, tq), F32))
        carry = block(j, init, True)
        dk_acc, dv_acc, dc_acc = lax.fori_loop(j + 1, nq, lambda i, c: block(i, c, False), carry)
        dk_ref[...] = dk_acc
        dv_ref[...] = dv_acc
        dc_ref[...] = dc_acc

    def whole(col):
        return _bs((T, HD), lambda h, j: (0, col + h))

    qvec = _bs((None, T, 1), lambda h, j: (h, 0, 0))
    kv_out = _bs((tq, HD), lambda h, j: (j, h))
    return pl.pallas_call(
        body, grid=(NH, nq),
        in_specs=[whole(0), _bs((tq, HD), lambda h, j: (j, NH + h)), _bs((tq, HD), lambda h, j: (j, 2 * NH + h)),
                  qvec, _bs((None, None, 1, tq), lambda h, j: (h, j, 0, 0)), qvec, whole(0), whole(0)],
        out_specs=[whole(0), kv_out, kv_out, _bs((None, 1, tq), lambda h, j: (h, 0, j)), qvec],
        out_shape=[S((T, DA), F32)] * 3 + [S((NH, 1, T), F32), S((NH, T, 1), F32)],
        compiler_params=_cp(), name="attn_bwd")(qkv, qkv, qkv, cumq, cumk, lse, o, do)


def _lru_gates(xr, wa, wx, ba, bx, sp):
    xb = xr.astype(BF16)
    r = jax.nn.sigmoid(jnp.dot(xb, wa, preferred_element_type=F32) + ba)
    ig = jax.nn.sigmoid(jnp.dot(xb, wx, preferred_element_type=F32) + bx)
    log_a = -LRU_C * r * sp
    a = jnp.exp(log_a)
    th = jnp.tanh(log_a)
    om = -2.0 * th / (1.0 - th)
    mult = jnp.sqrt(om)
    return xb, r, ig, a, om, mult


def _lru_xr(lx_ref, cw, cb, start, b, rows):
    lx = lx_ref[rows, :]
    ext = jnp.concatenate([_halo_prev(lx_ref, start, b), lx], axis=0)
    return cw[3:4, :] * lx + cw[2:3, :] * _down(ext, 1) + cw[1:2, :] * _down(ext, 2) + cw[0:1, :] * _down(ext, 3) + cb


def _lru_fwd(P, lcw, vec, wa, wx):
    T = P.shape[0]
    bt = _bt(T)
    nb = T // bt

    def body(g_ref, lx_ref, cw_ref, vec_ref, wa_ref, wx_ref, y_ref, h_ref):
        cw = cw_ref[...]
        vec = vec_ref[...]
        wa = wa_ref[...].astype(BF16)
        wx = wx_ref[...].astype(BF16)
        sp = _softplus(-vec[3:4, :])

        def step(b, carry):
            start = pl.multiple_of(b * bt, bt)
            rows = pl.ds(start, bt)
            xr = _lru_xr(lx_ref, cw, vec[0:1, :], start, b, rows)
            _, _, ig, a, _, mult = _lru_gates(xr, wa, wx, vec[1:2, :], vec[2:3, :], sp)
            u = mult * (ig * xr)
            ac, hc = _scan_fwd(a, u)
            hb = hc + ac * carry
            h_ref[rows, :] = hb
            gel, _ = _gelu_parts(g_ref[rows, :])
            y_ref[rows, :] = gel * hb
            return hb[bt - 1:bt, :]

        lax.fori_loop(0, nb, step, jnp.zeros((1, 128), F32))

    own = _bs((T, 128), lambda c: (0, c))
    return pl.pallas_call(
        body, grid=(DL // 128,),
        in_specs=[_bs((T, 128), lambda c: (0, C_GATE + c)), _bs((T, 128), lambda c: (0, C_LX + c)),
                  _bs((8, 128), lambda c: (0, c)), _bs((8, 128), lambda c: (0, c)),
                  _bs((None, LB, LB), lambda c: (c, 0, 0)), _bs((None, LB, LB), lambda c: (c, 0, 0))],
        out_specs=[own, own], out_shape=[S((T, DL), F32)] * 2, compiler_params=_cp(), name="lru_fwd")(P, P, lcw, vec, wa, wx)


def _lru_bwd(P, lcw, vec, wa, wx, hst, dy):
    T = P.shape[0]
    bt = _bt(T)
    nb = T // bt

    def body(g_ref, lx_ref, cw_ref, vec_ref, wa_ref, wx_ref, h_ref, dy_ref,
             dg_ref, dlx_ref, sm_ref, dwa_ref, dwx_ref, dxr_s):
        cw = cw_ref[...]
        vec = vec_ref[...]
        wa = wa_ref[...].astype(BF16)
        wx = wx_ref[...].astype(BF16)
        lam = vec[3:4, :]
        sp = _softplus(-lam)
        dwa_ref[...] = jnp.zeros((LB, LB), F32)
        dwx_ref[...] = jnp.zeros((LB, LB), F32)
        zero = jnp.zeros((1, 128), F32)

        def step1(i, carry):
            wc, s_cb, s_ba, s_bx, s_sp = carry
            b = nb - 1 - i
            start = pl.multiple_of(b * bt, bt)
            rows = pl.ds(start, bt)
            xr = _lru_xr(lx_ref, cw, vec[0:1, :], start, b, rows)
            xb, r, ig, a, om, mult = _lru_gates(xr, wa, wx, vec[1:2, :], vec[2:3, :], sp)
            hb = h_ref[rows, :]
            dy = dy_ref[rows, :]
            gel, dgel = _gelu_parts(g_ref[rows, :])
            dg_ref[rows, :] = dy * hb * dgel
            dh = dy * gel
            ac, wcum = _scan_bwd(a, a * dh)
            w = wcum + ac * wc
            g = dh + _up(jnp.concatenate([w, jnp.broadcast_to(wc, (8, 128))], axis=0), 1)
            hprev = _down(jnp.concatenate([_halo_prev(h_ref, start, b), hb], axis=0), 1)
            da = g * hprev
            dmult = g * (ig * xr)
            dix = g * mult
            di = dix * xr
            dlog_a = da * a - dmult * ((1.0 - om) / mult)
            dr = dlog_a * (-LRU_C * sp)
            dpr = dr * r * (1.0 - r)
            dpi = di * ig * (1.0 - ig)
            dprb, dpib = dpr.astype(BF16), dpi.astype(BF16)
            dwa_ref[...] += lax.dot_general(xb, dprb, TN, preferred_element_type=F32)
            dwx_ref[...] += lax.dot_general(xb, dpib, TN, preferred_element_type=F32)
            dxr = (dix * ig + lax.dot_general(dprb, wa, NT, preferred_element_type=F32)
                   + lax.dot_general(dpib, wx, NT, preferred_element_type=F32))
            dxr_s[rows, :] = dxr
            return (w[0:1, :], s_cb + jnp.sum(dxr, axis=0, keepdims=True), s_ba + jnp.sum(dpr, axis=0, keepdims=True),
                    s_bx + jnp.sum(dpi, axis=0, keepdims=True), s_sp + jnp.sum(dlog_a * (-LRU_C * r), axis=0, keepdims=True))

        _, s_cb, s_ba, s_bx, s_sp = lax.fori_loop(0, nb, step1, (zero, zero, zero, zero, zero))

        def step2(b, carry):
            t0, t1, t2, t3 = carry
            start = pl.multiple_of(b * bt, bt)
            rows = pl.ds(start, bt)
            dxr = dxr_s[rows, :]
            extn = jnp.concatenate([dxr, _halo_next(dxr_s, start, bt, b, nb)], axis=0)
            dlx_ref[rows, :] = (cw[3:4, :] * dxr + cw[2:3, :] * _up(extn, 1) + cw[1:2, :] * _up(extn, 2)
                                + cw[0:1, :] * _up(extn, 3))
            lx = lx_ref[rows, :]
            ext = jnp.concatenate([_halo_prev(lx_ref, start, b), lx], axis=0)
            return (t0 + jnp.sum(dxr * _down(ext, 3), axis=0, keepdims=True),
                    t1 + jnp.sum(dxr * _down(ext, 2), axis=0, keepdims=True),
                    t2 + jnp.sum(dxr * _down(ext, 1), axis=0, keepdims=True),
                    t3 + jnp.sum(dxr * lx, axis=0, keepdims=True))

        t0, t1, t2, t3 = lax.fori_loop(0, nb, step2, (zero, zero, zero, zero))
        sm_ref[...] = jnp.zeros((16, 128), F32)
        for k, val in enumerate((t0, t1, t2, t3, s_cb, s_ba, s_bx, -s_sp * jax.nn.sigmoid(-lam))):
            sm_ref[k:k + 1, :] = val

    own = _bs((T, 128), lambda c: (0, c))
    wspec = _bs((None, LB, LB), lambda c: (c, 0, 0))
    return pl.pallas_call(
        body, grid=(DL // 128,),
        in_specs=[_bs((T, 128), lambda c: (0, C_GATE + c)), _bs((T, 128), lambda c: (0, C_LX + c)),
                  _bs((8, 128), lambda c: (0, c)), _bs((8, 128), lambda c: (0, c)), wspec, wspec, own, own],
        out_specs=[own, own, _bs((16, 128), lambda c: (0, c)), wspec, wspec],
        out_shape=[S((T, DL), F32)] * 2 + [S((16, DL), F32), S((4, LB, LB), F32), S((4, LB, LB), F32)],
        scratch_shapes=[pltpu.VMEM((T, 128), F32)], compiler_params=_cp(), name="lru_bwd")(P, P, lcw, vec, wa, wx, hst, dy)


_GROUPS = ((0, DC), (DC, DC + DA), (DC + DA, D))


def _gnorm_fwd(yc, ya, yl, gain):
    T = yc.shape[0]
    tb = _tile(T, (512,))

    def body(c_ref, a_ref, l_ref, g_ref, yn_ref, r0_ref, r1_ref, r2_ref):
        for (lo, hi), src, r_ref in zip(_GROUPS, (c_ref, a_ref, l_ref), (r0_ref, r1_ref, r2_ref)):
            yv = src[...]
            r = lax.rsqrt(jnp.mean(yv * yv, axis=1, keepdims=True) + EPS)
            yn_ref[:, lo:hi] = (yv * r * g_ref[:, lo:hi]).astype(BF16)
            r_ref[...] = r

    rs = _bs((tb, 1), lambda i: (i, 0))
    return pl.pallas_call(
        body, grid=(T // tb,),
        in_specs=[_bs((tb, DC), lambda i: (i, 0)), _bs((tb, DA), lambda i: (i, 0)), _bs((tb, DL), lambda i: (i, 0)),
                  _bs((1, D), lambda i: (0, 0))],
        out_specs=[_bs((tb, D), lambda i: (i, 0)), rs, rs, rs],
        out_shape=[S((T, D), BF16)] + [S((T, 1), F32)] * 3, compiler_params=_cp(), name="gnorm_fwd")(yc, ya, yl, gain)


def _gnorm_bwd(dyn, yc, ya, yl, r0, r1, r2, gain):
    T = yc.shape[0]
    tb = _tile(T, (512,))

    def body(d_ref, c_ref, a_ref, l_ref, r0_ref, r1_ref, r2_ref, g_ref, dc_ref, da_ref, dl_ref, dg_ref):
        i = pl.program_id(0)
        for (lo, hi), src, r_ref, dst in zip(_GROUPS, (c_ref, a_ref, l_ref), (r0_ref, r1_ref, r2_ref), (dc_ref, da_ref, dl_ref)):
            r = r_ref[...]
            yhat = src[...] * r
            dy = d_ref[:, lo:hi]
            dyh = dy * g_ref[:, lo:hi]
            m = jnp.mean(dyh * yhat, axis=1, keepdims=True)
            dst[...] = r * (dyh - yhat * m)
            part = jnp.sum(dy * yhat, axis=0, keepdims=True)

            @pl.when(i == 0)
            def _():
                dg_ref[:, lo:hi] = part

            @pl.when(i > 0)
            def _():
                dg_ref[:, lo:hi] += part

    rs = _bs((tb, 1), lambda i: (i, 0))
    specs = [_bs((tb, DC), lambda i: (i, 0)), _bs((tb, DA), lambda i: (i, 0)), _bs((tb, DL), lambda i: (i, 0))]
    return pl.pallas_call(
        body, grid=(T // tb,),
        in_specs=[_bs((tb, D), lambda i: (i, 0))] + specs + [rs, rs, rs, _bs((1, D), lambda i: (0, 0))],
        out_specs=specs + [_bs((1, D), lambda i: (0, 0))],
        out_shape=[S((T, DC), F32), S((T, DA), F32), S((T, DL), F32), S((1, D), F32)],
        compiler_params=_cp(), name="gnorm_bwd")(dyn, yc, ya, yl, r0, r1, r2, gain)


HBM = pl.BlockSpec(memory_space=pltpu.HBM)
N_BIG = 6


def _place():
    x, y, c = lax.axis_index("x"), lax.axis_index("y"), lax.axis_index("c")
    return x, y, c, 2 * x + y


def _peer(x, y, j):
    return x ^ ((j + 1) >> 1), y ^ ((j + 1) & 1)


SEM = pl.BlockSpec(memory_space=pltpu.SEMAPHORE)
ANY = pl.BlockSpec(memory_space=pl.ANY)
VM = pl.BlockSpec(memory_space=pltpu.VMEM)
EFFECT = pltpu.SideEffectType.DATAFLOW_SIDE_EFFECTING
N_AG = N_BIG + 1


def _hbm(a):
    return pltpu.with_memory_space_constraint(a, pltpu.HBM)


AG_ORDER = (0, 1, N_BIG, 2, 3, 4, 5)
AG_MIX, AG_FFN2 = (N_BIG, 2, 3), (4, 5)


def _ag_copy(src, land, ssem, rsem, t, j, chip):
    x, y, c, _ = _place()
    px, py = _peer(x, y, j)
    if t == N_BIG:
        s_ref, d_ref = src, land.at[chip]
    else:
        rh = src.shape[0] // 2
        half = pl.ds(c * rh, rh)
        s_ref, d_ref = src.at[half], land.at[chip, half]
    return pltpu.make_async_remote_copy(src_ref=s_ref, dst_ref=d_ref, send_sem=ssem.at[3 * t + j], recv_sem=rsem.at[3 * t + j],
                                        device_id=(px, py, c), device_id_type=MESH)


def _ag_start(l, srcs, dep):
    n = N_AG

    def body(*refs):
        src = refs[:n]
        ssem, rsem = refs[2 * n + 1], refs[2 * n + 2]
        land = refs[3 * n + 3:4 * n + 3]
        token = refs[4 * n + 3]
        _, _, _, me = _place()
        for t in AG_ORDER:
            for j in range(3):
                _ag_copy(src[t], land[t], ssem, rsem, t, j, me).start()
        token[...] = jnp.zeros_like(token)

    lands = [lax.empty((4,) + a.shape, a.dtype) for a in srcs]
    dma = pltpu.SemaphoreType.DMA
    outs = pl.pallas_call(
        body, name=f"ag_start_{l}",
        out_shape=(dma((3 * n,)), dma((3 * n,))) + tuple(pltpu.HBM(a.shape, a.dtype) for a in list(srcs) + lands) + (S((8, 128), F32),),
        in_specs=[HBM] * (2 * n) + [ANY], out_specs=(SEM, SEM) + (HBM,) * (2 * n) + (VM,),
        input_output_aliases={i: 2 + i for i in range(2 * n)},
        compiler_params=pltpu.CompilerParams(has_side_effects=EFFECT),
    )(*[_hbm(a) for a in srcs], *[_hbm(a) for a in lands], dep)
    return outs[0], outs[1], outs[2:2 + n], outs[2 + n:2 + 2 * n], outs[-1]


def _ag_wait(name, idx, ssem, rsem, srcs, lands, after):
    n = len(idx)

    def body(*refs):
        src, land = refs[:n], refs[n:2 * n]
        ssem, rsem = refs[2 * n], refs[2 * n + 1]
        x, y, _, _ = _place()
        for p, t in enumerate(idx):
            for j in range(3):
                px, py = _peer(x, y, j)
                cp = _ag_copy(src[p], land[p], ssem, rsem, t, j, 2 * px + py)
                cp.wait_send()
                cp.wait_recv()

    outs = pl.pallas_call(
        body, name=name,
        out_shape=tuple(pltpu.HBM(a.shape, a.dtype) for a in list(srcs) + list(lands)),
        in_specs=[HBM] * (2 * n) + [SEM, SEM, ANY], out_specs=(HBM,) * (2 * n),
        input_output_aliases={i: i for i in range(2 * n)},
        compiler_params=pltpu.CompilerParams(has_side_effects=EFFECT),
    )(*srcs, *lands, ssem, rsem, after)
    return outs[:n], outs[n:]


def _fwd_copies(idx, src, land, ssem, rsem, recv_side):
    x, y, c, me = _place()
    cps = []
    for p, t in enumerate(idx):
        cps.append(pltpu.make_async_remote_copy(src_ref=src[p], dst_ref=land[p].at[me], send_sem=ssem.at[4 * p], recv_sem=rsem.at[4 * p],
                                                device_id=(x, y, 1 - c), device_id_type=MESH))
        if t == N_BIG:
            continue
        rh = src[p].shape[0] // 2
        for j in range(3):
            px, py = _peer(x, y, j)
            part = land[p].at[2 * px + py, pl.ds(((1 - c) if recv_side else c) * rh, rh)]
            cps.append(pltpu.make_async_remote_copy(src_ref=part, dst_ref=part, send_sem=ssem.at[4 * p + 1 + j],
                                                    recv_sem=rsem.at[4 * p + 1 + j], device_id=(x, y, 1 - c), device_id_type=MESH))
    return cps


def _ag_fwd_start(name, idx, srcs, lands, dep):
    n = len(idx)

    def body(*refs):
        src = refs[:n]
        ssem, rsem = refs[2 * n + 1], refs[2 * n + 2]
        land = refs[3 * n + 3:4 * n + 3]
        token = refs[4 * n + 3]
        for cp in _fwd_copies(idx, src, land, ssem, rsem, False):
            cp.start()
        token[...] = jnp.zeros_like(token)

    dma = pltpu.SemaphoreType.DMA
    outs = pl.pallas_call(
        body, name=name,
        out_shape=(dma((4 * n,)), dma((4 * n,))) + tuple(pltpu.HBM(a.shape, a.dtype) for a in list(srcs) + list(lands)) + (S((8, 128), F32),),
        in_specs=[HBM] * (2 * n) + [ANY], out_specs=(SEM, SEM) + (HBM,) * (2 * n) + (VM,),
        input_output_aliases={i: 2 + i for i in range(2 * n)},
        compiler_params=pltpu.CompilerParams(has_side_effects=EFFECT),
    )(*srcs, *lands, dep)
    return outs[0], outs[1], outs[2:2 + n], outs[2 + n:2 + 2 * n], outs[-1]


def _ag_fwd_wait(name, idx, ssem, rsem, srcs, lands, after):
    n = len(idx)

    def body(*refs):
        src, land = refs[:n], refs[n:2 * n]
        ssem, rsem = refs[2 * n], refs[2 * n + 1]
        for mine, theirs in zip(_fwd_copies(idx, src, land, ssem, rsem, False), _fwd_copies(idx, src, land, ssem, rsem, True)):
            mine.wait_send()
            theirs.wait_recv()

    outs = pl.pallas_call(
        body, name=name,
        out_shape=tuple(pltpu.HBM(a.shape, a.dtype) for a in list(srcs) + list(lands)),
        in_specs=[HBM] * (2 * n) + [SEM, SEM, ANY], out_specs=(HBM,) * (2 * n),
        input_output_aliases={i: i for i in range(2 * n)},
        compiler_params=pltpu.CompilerParams(has_side_effects=EFFECT),
    )(*srcs, *lands, ssem, rsem, after)
    return outs[n:]


def _pair_copy(g, land, ssem, rsem, t):
    x, y, c, _ = _place()
    rh = g.shape[1] // 2
    return pltpu.make_async_remote_copy(src_ref=g.at[:, pl.ds((1 - c) * rh, rh), :], dst_ref=land,
                                        send_sem=ssem.at[t], recv_sem=rsem.at[t], device_id=(x, y, 1 - c), device_id_type=MESH)


def _rs_pair(grads):
    n = len(grads)

    def body(*refs):
        g, out = refs[:n], refs[n:2 * n]
        ssem, rsem = refs[2 * n:]
        for t in range(n):
            _pair_copy(g[t], out[t], ssem, rsem, t).start()
        for t in range(n):
            _pair_copy(g[t], out[t], ssem, rsem, t).wait()

    dma = pltpu.SemaphoreType.DMA
    return pl.pallas_call(
        body, in_specs=[HBM] * n, out_specs=[HBM] * n,
        out_shape=[S((4, g.shape[1] // 2, g.shape[2]), g.dtype) for g in grads],
        scratch_shapes=[dma((n,)), dma((n,))], name="rs_pair")(*grads)


def _rs_pair_start(name, grads, dep):
    n = len(grads)

    def body(*refs):
        g = refs[:n]
        ssem, rsem = refs[2 * n + 1], refs[2 * n + 2]
        land = refs[3 * n + 3:4 * n + 3]
        token = refs[4 * n + 3]
        for t in range(n):
            _pair_copy(g[t], land[t], ssem, rsem, t).start()
        token[...] = jnp.zeros_like(token)

    lands = [lax.empty((4, a.shape[1] // 2, a.shape[2]), a.dtype) for a in grads]
    dma = pltpu.SemaphoreType.DMA
    outs = pl.pallas_call(
        body, name=name,
        out_shape=(dma((n,)), dma((n,))) + tuple(pltpu.HBM(a.shape, a.dtype) for a in list(grads) + lands) + (S((8, 128), F32),),
        in_specs=[HBM] * (2 * n) + [ANY], out_specs=(SEM, SEM) + (HBM,) * (2 * n) + (VM,),
        input_output_aliases={i: 2 + i for i in range(2 * n)},
        compiler_params=pltpu.CompilerParams(has_side_effects=EFFECT),
    )(*[_hbm(a) for a in grads], *[_hbm(a) for a in lands], dep)
    return outs[0], outs[1], outs[2:2 + n], outs[2 + n:2 + 2 * n], outs[-1]


def _rs_pair_wait(name, ssem, rsem, grads, lands, after):
    n = len(grads)

    def body(*refs):
        g, land = refs[:n], refs[n:2 * n]
        ssem, rsem = refs[2 * n], refs[2 * n + 1]
        for t in range(n):
            cp = _pair_copy(g[t], land[t], ssem, rsem, t)
            cp.wait_send()
            cp.wait_recv()

    outs = pl.pallas_call(
        body, name=name,
        out_shape=tuple(pltpu.HBM(a.shape, a.dtype) for a in list(grads) + list(lands)),
        in_specs=[HBM] * (2 * n) + [SEM, SEM, ANY], out_specs=(HBM,) * (2 * n),
        input_output_aliases={i: i for i in range(2 * n)},
        compiler_params=pltpu.CompilerParams(has_side_effects=EFFECT),
    )(*grads, *lands, ssem, rsem, after)
    return outs[:n], outs[n:]


def _rs_copy(s, land, ssem, rsem, t, j):
    x, y, c, _ = _place()
    px, py = _peer(x, y, j)
    return pltpu.make_async_remote_copy(src_ref=s[t].at[2 * px + py], dst_ref=land[t].at[j],
                                        send_sem=ssem.at[3 * t + j], recv_sem=rsem.at[3 * t + j], device_id=(px, py, c), device_id_type=MESH)


def _rs_start(name, sums, dep):
    n = len(sums)

    def body(*refs):
        s = refs[:n]
        ssem, rsem = refs[2 * n + 1], refs[2 * n + 2]
        land = refs[3 * n + 3:4 * n + 3]
        token = refs[4 * n + 3]
        for t in range(n):
            for j in range(3):
                _rs_copy(s, land, ssem, rsem, t, j).start()
        token[...] = jnp.zeros_like(token)

    lands = [lax.empty((3,) + a.shape[1:], a.dtype) for a in sums]
    dma = pltpu.SemaphoreType.DMA
    outs = pl.pallas_call(
        body, name=name,
        out_shape=(dma((3 * n,)), dma((3 * n,))) + tuple(pltpu.HBM(a.shape, a.dtype) for a in list(sums) + lands) + (S((8, 128), F32),),
        in_specs=[HBM] * (2 * n) + [ANY], out_specs=(SEM, SEM) + (HBM,) * (2 * n) + (VM,),
        input_output_aliases={i: 2 + i for i in range(2 * n)},
        compiler_params=pltpu.CompilerParams(has_side_effects=EFFECT),
    )(*[_hbm(a) for a in sums], *[_hbm(a) for a in lands], dep)
    return outs[0], outs[1], outs[2:2 + n], outs[2 + n:2 + 2 * n], outs[-1]


def _rs_wait(name, ssem, rsem, sums, lands, after):
    n = len(sums)

    def body(*refs):
        s, land = refs[:n], refs[n:2 * n]
        ssem, rsem = refs[2 * n], refs[2 * n + 1]
        for t in range(n):
            for j in range(3):
                cp = _rs_copy(s, land, ssem, rsem, t, j)
                cp.wait_send()
                cp.wait_recv()

    outs = pl.pallas_call(
        body, name=name,
        out_shape=tuple(pltpu.HBM(a.shape, a.dtype) for a in list(sums) + list(lands)),
        in_specs=[HBM] * (2 * n) + [SEM, SEM, ANY], out_specs=(HBM,) * (2 * n),
        input_output_aliases={i: i for i in range(2 * n)},
        compiler_params=pltpu.CompilerParams(has_side_effects=EFFECT),
    )(*sums, *lands, ssem, rsem, after)
    return outs[n:]


def _rs_join(halves, lo, hi, dep):
    def body(*refs):
        h = refs[:N_BIG]
        out = refs[N_BIG + 1:2 * N_BIG + 1]
        ssem, rsem = refs[2 * N_BIG + 1:]
        x, y, c, _ = _place()

        def cp(t):
            return pltpu.make_async_remote_copy(
                src_ref=h[t].at[pl.ds(lo, hi - lo)], dst_ref=out[t], send_sem=ssem.at[t], recv_sem=rsem.at[t],
                device_id=(x, y, 1 - c), device_id_type=MESH)

        for t in range(N_BIG):
            cp(t).start()
        for t in range(N_BIG):
            cp(t).wait()

    dma = pltpu.SemaphoreType.DMA
    return pl.pallas_call(
        body, in_specs=[HBM] * N_BIG + [ANY], out_specs=[HBM] * N_BIG,
        out_shape=[S((hi - lo,) + h.shape[1:], h.dtype) for h in halves],
        scratch_shapes=[dma((N_BIG,)), dma((N_BIG,))], name="rs_join")(*halves, dep)


def _all_reduce_small(pack, dep):
    R = pack.shape[0]
    rb = _tile(R, (512, 256, 128, 8))

    def body(x_ref, dep_ref, all_ref, sum_ref, send_sems, recv_sems, local_sem):
        x, y, c = lax.axis_index("x"), lax.axis_index("y"), lax.axis_index("c")
        me, sibling = (x, y, c), (x, y, 1 - c)
        chips = [(1 - x, y), (x, 1 - y), (1 - x, 1 - y)]

        def rows(px, py, pc):
            return all_ref.at[pl.ds((4 * px + 2 * py + pc) * R, R), :]

        def copy(k, block, to, src=None):
            return pltpu.make_async_remote_copy(
                src_ref=rows(*block) if src is None else src, dst_ref=rows(*block),
                send_sem=send_sems.at[k], recv_sem=recv_sems.at[k], device_id=to, device_id_type=MESH)

        mine = pltpu.make_async_copy(x_ref, rows(*me), local_sem)
        mine.start()
        first = [copy(0, me, sibling, src=x_ref)]
        first += [copy(1 + j, me, (*chip, c), src=x_ref) for j, chip in enumerate(chips)]
        for cp in first:
            cp.start()
        passed = [copy(4 + j, (*chip, c), sibling) for j, chip in enumerate(chips)]
        for j, chip in enumerate(chips):
            copy(1 + j, (*chip, c), me).wait_recv()
            passed[j].start()
        copy(0, sibling, me).wait_recv()
        for j, chip in enumerate(chips):
            copy(4 + j, (*chip, 1 - c), me).wait_recv()
        for cp in first + passed:
            cp.wait_send()
        mine.wait()

        def step(b, carry):
            off = pl.multiple_of(b * rb, rb)
            acc = all_ref[pl.ds(off, rb), :]
            for k in range(1, 8):
                acc = acc + all_ref[pl.ds(pl.multiple_of(k * R + off, 8), rb), :]
            sum_ref[pl.ds(off, rb), :] = acc
            return carry

        lax.fori_loop(0, R // rb, step, 0)

    vm = pl.BlockSpec(memory_space=pltpu.VMEM)
    dma = pltpu.SemaphoreType.DMA
    _, total = pl.pallas_call(
        body, in_specs=[vm, pl.BlockSpec(memory_space=pl.ANY)], out_specs=[vm, vm],
        out_shape=[S((8 * R, 128), F32), S((R, 128), F32)],
        scratch_shapes=[dma((7,)), dma((7,)), dma],
        compiler_params=_cp(), name="allreduce_small")(pack, dep)
    return total


def _row_tile(rh, cc, tile_bytes=3 * 1024 * 1024 // 2):
    for t in (1024, 704, 512, 352, 256, 176, 128, 64, 32, 16):
        if rh % t == 0 and t * cc * 4 <= tile_bytes:
            return t
    return 16


def _my_chip():
    return 2 * lax.axis_index("x") + lax.axis_index("y")


def _pair_sum(g, recv):
    _, r, cc = g.shape
    rh = r // 2
    tb = _row_tile(rh, cc, 6 * 1024 * 1024)
    nbh = rh // tb

    def body(g_ref, r_ref, o_ref):
        o_ref[...] = (g_ref[...].astype(F32) + r_ref[...].astype(F32)).astype(BF16)

    def chip(k):
        return (_my_chip() + 1 + k) % 4

    mine = _bs((None, tb, cc), lambda k, i: (chip(k), lax.axis_index("c") * nbh + i, 0))
    plain = _bs((None, tb, cc), lambda k, i: (chip(k), i, 0))
    return pl.pallas_call(body, grid=(3, nbh), in_specs=[mine, plain], out_specs=plain,
                          out_shape=S((4, rh, cc), BF16), compiler_params=_cp(), name="rs_pair_sum")(g, recv)


def _owner_sum(g, recv, ici, acc, l):
    _, r, cc = g.shape
    rh = r // 2
    tb = _row_tile(rh, cc, 3 * 1024 * 1024)
    nbh = rh // tb

    def body(g_ref, r_ref, i0_ref, i1_ref, i2_ref, acc_ref, o_ref):
        s = g_ref[...].astype(F32) + r_ref[...].astype(F32)
        o_ref[...] = s + i0_ref[...].astype(F32) + i1_ref[...].astype(F32) + i2_ref[...].astype(F32)

    def slot(j):
        return _bs((None, tb, cc), lambda i: (j, i, 0))

    return pl.pallas_call(
        body, grid=(nbh,),
        in_specs=[_bs((None, tb, cc), lambda i: (_my_chip(), lax.axis_index("c") * nbh + i, 0)),
                  _bs((None, tb, cc), lambda i: (_my_chip(), i, 0)),
                  slot(0), slot(1), slot(2), pl.BlockSpec(memory_space=pl.ANY)],
        out_specs=_bs((None, tb, cc), lambda i: (l, i, 0)),
        out_shape=S(acc.shape, F32), input_output_aliases={5: 0},
        compiler_params=_cp(), name="rs_owner_sum")(g, recv, ici, ici, ici, acc)


def _adam_math(w, g, m, v):
    m = B1 * m + (1.0 - B1) * g
    v = B2 * v + (1.0 - B2) * (g * g)
    m_hat = m / (1.0 - B1 ** STEP)
    v_hat = v / (1.0 - B2 ** STEP)
    delta = -LR * (m_hat / (jnp.sqrt(v_hat) + AEPS) + WD * w)
    return delta, m, v


def _adamw_big(w, g_mine, g_sib, m, v, lo, hi, prev):
    L, r, cc = w.shape
    rh = r // 2
    tb = _row_tile(rh, cc)
    nbh = rh // tb

    def body(w_ref, gm_ref, gs_ref, m_ref, v_ref, *rest):
        go_ref, d_ref, mo_ref, vo_ref = rest[-4:]
        mine = pl.program_id(1) == lax.axis_index("c")
        g = jnp.where(mine, gm_ref[...], gs_ref[...])
        d, m, v = _adam_math(w_ref[...], g, m_ref[...], v_ref[...])
        go_ref[...] = g
        d_ref[...] = d
        mo_ref[...] = m
        vo_ref[...] = v

    def mine_map(l, hf, i):
        c = lax.axis_index("c")
        return (l + lo, jnp.where(hf == c, i, jnp.where(c == 0, nbh - 1, 0)), 0)

    def sib_map(l, hf, i):
        c = lax.axis_index("c")
        return (l, jnp.where(hf != c, i, jnp.where(c == 0, 0, nbh - 1)), 0)

    full = _bs((None, tb, cc), lambda l, hf, i: (l + lo, hf * nbh + i, 0))
    extra = [] if prev is None else list(prev)
    return pl.pallas_call(
        body, grid=(hi - lo, 2, nbh),
        in_specs=[full, _bs((None, tb, cc), mine_map), _bs((None, tb, cc), sib_map), full, full]
        + [pl.BlockSpec(memory_space=pl.ANY)] * len(extra),
        out_specs=[full] * 4, out_shape=[S(w.shape, F32)] * 4,
        input_output_aliases={5 + k: k for k in range(len(extra))},
        compiler_params=_cp(), name="adamw_big")(w, g_mine, g_sib, m, v, *extra)


def _adamw_small(w, g, m, v):
    R = w.shape[0]
    tb = _tile(R, (512, 256, 128, 8))

    def body(w_ref, g_ref, m_ref, v_ref, d_ref, mo_ref, vo_ref):
        d, m, v = _adam_math(w_ref[...], g_ref[...], m_ref[...], v_ref[...])
        d_ref[...] = d
        mo_ref[...] = m
        vo_ref[...] = v

    spec = _bs((tb, 128), lambda i: (i, 0))
    return pl.pallas_call(body, grid=(R // tb,), in_specs=[spec] * 4, out_specs=[spec] * 3,
                          out_shape=[S((R, 128), F32)] * 3, compiler_params=_cp(), name="adamw_small")(w, g, m, v)


def _mix_pad(w):
    return jnp.concatenate([w[:, :4608], w[:, 4616:DIN], w[:, 4608:4616], jnp.zeros((D, PW - DIN), w.dtype)], axis=1)


def _mix_unpad(g):
    return jnp.concatenate([g[:, :4608], g[:, 5632:5640], g[:, 4608:5632]], axis=1)


def _pack(parts):
    flat = jnp.concatenate([p.reshape(-1).astype(F32) for p in parts])
    n = flat.shape[0]
    total = -(-n // (512 * 128)) * (512 * 128)
    return jnp.pad(flat, (0, total - n)).reshape(total // 128, 128)


def _unpack(pack, shapes):
    flat = pack.reshape(-1)
    out, off = [], 0
    for s in shapes:
        n = math.prod(s)
        out.append(flat[off:off + n].reshape(s))
        off += n
    return out


def _ffn_backward(dy, dyb, saved, gain, win, wout, dep):
    x, h, rstd, zg, zu, act = saved
    dz = _ffn_bwd_dz(dyb, wout, zg, zu, dep)
    dwout = _mm_tn("ffn_bwd_dwout", act, dyb, scale=0.5, tm=512, tn=1024)
    dwin = _ffn_bwd_dwin(h, dz)
    dh = _ffn_bwd_dh(dz, win)
    dx, dxb, dgain = _rms_bwd(dh, x, rstd, gain, dy)
    return dx, dxb, dgain, dwin, dwout


def _mixer_forward(x, p):
    T = x.shape[0]
    h, rstd = _rms_fwd(x, p["norm_mix"])
    tm = _tile(T, (1024, 512))
    tn = 1152
    P = _mm("mix_in", h, p["wmix"],
            _bs((tm, D), lambda i, j, k: (i, 0)), _bs((D, tn), lambda i, j, k: (0, j)),
            _bs((tm, tn), lambda i, j, k: (i, j)), S((T, PW), F32), (T // tm, PW // tn, 1), NN, 1, (tm, tn))
    yc = _conv_fwd(P, p["cw"])
    cum = _fgate_fwd(P, p["fb"])
    cumt = cum[:, :NH].T
    tq = _att_tile(T)
    cumq, cumk = cumt.reshape(NH, T, 1), cumt.reshape(NH, T // tq, 1, tq)
    qkv = P[:, C_Q * 128:C_GATE * 128].astype(BF16)
    ya, lse = _attn_fwd(qkv, cumq, cumk)
    yl, hst = _lru_fwd(P, p["lcw"], p["lvec"], p["lru_w_a"], p["lru_w_x"])
    yn, r0, r1, r2 = _gnorm_fwd(yc, ya, yl, p["mix_out_norm"])
    y = _mm("mix_out", yn, p["wo"],
            _bs((tm, D), lambda i, j, k: (i, 0)), _bs((D, 1024), lambda i, j, k: (0, j)),
            _bs((tm, 1024), lambda i, j, k: (i, j)), S((T, D), F32), (T // tm, D // 1024, 1), NN, 1, (tm, 1024),
            res=x, r_spec=_bs((tm, 1024), lambda i, j, k: (i, j)))
    return y, (x, h, rstd, P, qkv, cumq, cumk, lse, yc, ya, yl, hst, yn, r0, r1, r2)


def _mixer_backward(dy, dyb, saved, p, dep):
    x, h, rstd, P, qkv, cumq, cumk, lse, yc, ya, yl, hst, yn, r0, r1, r2 = saved
    T = x.shape[0]
    dyn = _mm_nt_full("mix_bwd_dyn", dyb, p["wo"], 512, dep=dep)
    dwo = _mm_tn("mix_bwd_dwo", yn, dyb, tm=512, tn=1024)
    dyc, dya, dyl, dgn = _gnorm_bwd(dyn, yc, ya, yl, r0, r1, r2, p["mix_out_norm"])
    dcb, dcc, dcv, dcw = _conv_bwd(P, p["cw"], dyc)
    dq, dk, dv, dck, dcq = _attn_bwd(qkv, cumq, cumk, lse, ya, dya)
    dcum = jnp.pad((dck.reshape(NH, T) + dcq.reshape(NH, T)).T, ((0, 0), (0, 128 - NH)))
    df, dfb = _fgate_bwd(P, p["fb"], dcum)
    dgate, dlx, lsm, dwa, dwx = _lru_bwd(P, p["lcw"], p["lvec"], p["lru_w_a"], p["lru_w_x"], hst, dyl)
    dP = jnp.concatenate([dcb, dcc, dcv, dq, dk, dv, dgate, dlx, df], axis=1).astype(BF16)
    tm = _tile(T, (512,))
    dh = _mm("mix_bwd_dh", dP, p["wmix"],
             _bs((tm, PW), lambda j, i, k: (i, 0)), _bs((1024, PW), lambda j, i, k: (j, 0)),
             _bs((tm, 1024), lambda j, i, k: (i, j)), S((T, D), F32), (D // 1024, T // tm, 1), NT, 1, (tm, 1024))
    dwmix = _mm_tn("mix_bwd_dwmix", h, dP, tm=512, tn=1152)
    dx, dxb, dgm = _rms_bwd(dh, x, rstd, p["norm_mix"], dy)
    small = dict(norm_mix=dgm[0], mix_out_norm=dgn[0], conv_w=dcw[:3], fgate_b=dfb[0, :NH], lru_conv_w=lsm[:4],
                 lru_conv_b=lsm[4], lru_b_a=lsm[5], lru_b_x=lsm[6], lru_lambda=lsm[7], lru_w_a=dwa, lru_w_x=dwx)
    return dx, dxb, small, dwmix, dwo


BIG =("ffn1_w_in", "ffn1_w_out", "mix_w_in", "mix_w_out", "ffn2_w_in", "ffn2_w_out")
SMALL = ("norm_ffn1", "norm_mix", "conv_w", "fgate_b", "lru_conv_w", "lru_conv_b", "lru_w_a", "lru_b_a", "lru_w_x",
         "lru_b_x", "lru_lambda", "mix_out_norm", "norm_ffn2", "final_norm")
WEIGHTS = ("norm_ffn1", "ffn1_w_in", "ffn1_w_out", "norm_mix", "mix_w_in", "conv_w", "fgate_b", "lru_conv_w", "lru_conv_b",
           "lru_w_a", "lru_b_a", "lru_w_x", "lru_b_x", "lru_lambda", "mix_out_norm", "mix_w_out", "norm_ffn2", "ffn2_w_in",
           "ffn2_w_out", "final_norm")


def _step(args):
    xx, yy, cc_ = lax.axis_index("x"), lax.axis_index("y"), lax.axis_index("c")
    me = 2 * xx + yy
    x0 = args["x"][0]
    tgt = args["loss_target"][0]
    T = x0.shape[0]
    L = args["norm_ffn1"].shape[0]

    def ag_sources(l):
        small = jnp.concatenate([args["conv_w"][l], args["lru_conv_w"][l], jnp.zeros((1, 128), F32)], axis=0)
        return [args[n][l].astype(BF16) for n in BIG] + [small]

    def layer_params(l, gat, gsm):
        w1i, w1o, wmx, wo, w2i, w2o = gat
        cwl = gsm.transpose(1, 0, 2).reshape(8, 4 * 128)
        return dict(
            w1i=w1i, w1o=w1o.reshape(-1, D), w2i=w2i, w2o=w2o.reshape(-1, D), wo=wo.reshape(D, D),
            wmix=_mix_pad(wmx.transpose(1, 0, 2).reshape(D, DIN)),
            cw=jnp.concatenate([cwl[:3], jnp.zeros((5, DC), F32)], axis=0),
            lcw=jnp.concatenate([cwl[3:7], jnp.zeros((4, DL), F32)], axis=0),
            fb=jnp.pad(args["fgate_b"][l], (0, 128 - NH)).reshape(1, 128),
            lvec=jnp.concatenate([args["lru_conv_b"][l][None], args["lru_b_a"][l][None], args["lru_b_x"][l][None],
                                  args["lru_lambda"][l][None], jnp.zeros((4, DL), F32)], axis=0),
            lru_w_a=args["lru_w_a"][l], lru_w_x=args["lru_w_x"][l],
            norm_ffn1=args["norm_ffn1"][l][None], norm_mix=args["norm_mix"][l][None],
            mix_out_norm=args["mix_out_norm"][l][None], norm_ffn2=args["norm_ffn2"][l][None])

    xs = x0
    saved, layers = [], []
    def ici_done(name, idx, flight, after):
        ssem, rsem, srcs, lands, _ = flight
        s, ld = _ag_wait("ag_wait_" + name, idx, ssem, rsem, [srcs[t] for t in idx], [lands[t] for t in idx], after)
        return _ag_fwd_start("ag_fwd_start_" + name, idx, s, ld, after)

    def gathered(name, idx, fwd, after):
        ssem, rsem, s, ld, _ = fwd
        return _ag_fwd_wait("ag_fwd_wait_" + name, idx, ssem, rsem, s, ld, after)

    every_w = tuple(range(N_AG))
    flight = _ag_start(0, ag_sources(0), x0)
    fwd = None
    for l in range(L):
        if l == 0:
            (w1i,) = gathered("0i", (0,), ici_done("0i", (0,), flight, xs), xs)
        else:
            w1i, w1o, wmx, wo, w2i, w2o, gsm = gathered(str(l), every_w, fwd, xs)
        nxt = _ag_start(l + 1, ag_sources(l + 1), w1i) if l + 1 < L else None
        h1, rstd1 = _rms_fwd(xs, args["norm_ffn1"][l][None])
        zg1, zu1, act1 = _ffn_in(h1, w1i)
        if l == 0:
            (w1o,) = gathered("0o", (1,), ici_done("0o", (1,), flight, act1), act1)
        x1 = _ffn_out(act1, w1o.reshape(-1, D), xs)
        s1 = (xs, h1, rstd1, zg1, zu1, act1)
        if l == 0:
            gsm, wmx, wo = gathered("0m", AG_MIX, ici_done("0m", AG_MIX, flight, x1), x1)
            w2i = w2o = w1o
        p = layer_params(l, (w1i, w1o, wmx, wo, w2i, w2o), gsm)
        x2, s2 = _mixer_forward(x1, p)
        if l == 0:
            w2i, w2o = gathered("0f", AG_FFN2, ici_done("0f", AG_FFN2, flight, x2), x2)
            p["w2i"], p["w2o"] = w2i, w2o.reshape(-1, D)
        h2, rstd2 = _rms_fwd(x2, p["norm_ffn2"])
        zg2, zu2, act2 = _ffn_in(h2, p["w2i"])
        fwd = ici_done(str(l + 1), every_w, nxt, act2) if nxt is not None else None
        x3 = _ffn_out(act2, p["w2o"], x2, dep=None if fwd is None else fwd[4])
        s3 = (x2, h2, rstd2, zg2, zu2, act2)
        saved.append((s1, s2, s3))
        layers.append(p)
        xs = x3
    lpart, dx, dxb, dfinal = _loss_head(xs, args["final_norm"][None], tgt)
    loss = lax.psum(lpart[0, 0], ("x", "y", "c"))

    acc = [None] * N_BIG
    small_grads = [None] * L
    every = tuple(range(N_BIG))
    in_air = []
    pair_flight = None
    dep = lpart

    def to_blocks(t, g):
        if t in (1, 5):
            return g.reshape(4, g.shape[0] // 4, D)
        if t == 2:
            return _mix_unpad(g).reshape(D, 4, DIN // 4).transpose(1, 0, 2)
        if t == 3:
            return g.reshape(4, D // 4, D)
        return g

    def launch(name, l, idx, grads, recv, after):
        sums = [_pair_sum(g, r) for g, r in zip(grads, recv)]
        started = _rs_start("rs_start_" + name, sums, after)
        in_air.append(("rs_wait_" + name, l, idx, grads, recv, started))
        return started[4]

    def land(entry, after):
        name, l, idx, grads, recv, (ssem, rsem, sums, lands, _) = entry
        ici = _rs_wait(name, ssem, rsem, sums, lands, after)
        for t, g, r, i3 in zip(idx, grads, recv, ici):
            if acc[t] is None:
                acc[t] = jnp.zeros((L, g.shape[1] // 2, g.shape[2]), F32)
            acc[t] = _owner_sum(g, r, i3, acc[t], l)

    for l in reversed(range(L)):
        p = layers[l]
        s1, s2, s3 = saved[l]
        dx, dxb, dg2, dw2i, dw2o = _ffn_backward(dx, dxb, s3, p["norm_ffn2"], p["w2i"], p["w2o"], dep)
        if pair_flight is not None:
            lp, (ssem, rsem, g_thru, lands, _) = pair_flight
            g_thru, recv = _rs_pair_wait(f"rs_pair_wait_{lp}", ssem, rsem, g_thru, lands, dx)
            dep = launch(str(lp), lp, every, g_thru, recv, dx)
            pair_flight = None
        if l == 0:
            part = [to_blocks(4, dw2i), to_blocks(5, dw2o)]
            dep = launch("0c", 0, (4, 5), part, _rs_pair(part), dx)
        dx, dxb, sm, dwmix, dwo = _mixer_backward(dx, dxb, s2, p, dep)
        if l == 0:
            part = [to_blocks(2, dwmix), to_blocks(3, dwo)]
            dep = launch("0b", 0, (2, 3), part, _rs_pair(part), dx)
        dx, dxb, dg1, dw1i, dw1o = _ffn_backward(dx, dxb, s1, p["norm_ffn1"], p["w1i"], p["w1o"], dep)
        sm["norm_ffn1"] = dg1[0]
        sm["norm_ffn2"] = dg2[0]
        small_grads[l] = sm
        for entry in [e for e in in_air if e[1] > l]:
            land(entry, dx)
            in_air.remove(entry)
        if l > 0:
            grads = [to_blocks(t, g) for t, g in enumerate((dw1i, dw1o, dwmix, dwo, dw2i, dw2o))]
            pair_flight = (l, _rs_pair_start(f"rs_pair_start_{l}", grads, dx))
            dep = pair_flight[1][4]

    part = [to_blocks(0, dw1i), to_blocks(1, dw1o)]
    dep = launch("0a", 0, (0, 1), part, _rs_pair(part), dx)

    def adamw(k, lo, hi, sib, prev):
        n = BIG[k]
        return _adamw_big(args[n], acc[k], sib[k], args["m_" + n], args["v_" + n], lo, hi, prev)

    res = [None] * N_BIG
    after = dx
    if L > 1:
        sib = _rs_join(acc, 1, L, dep)
        for k in range(N_BIG):
            res[k] = adamw(k, 1, L, sib, None)
        after = res[N_BIG - 1][0]
    full = {n: (dfinal[0] if n == "final_norm" else jnp.stack([small_grads[l][n] for l in range(L)])) for n in SMALL}
    red_pack = _all_reduce_small(_pack([full[n] for n in SMALL]), after)
    for entry in list(in_air):
        land(entry, red_pack)
    sib = _rs_join(acc, 0, 1, dx)
    out = {"loss": loss, "grad_x": dx[None]}
    for k, n in enumerate(BIG):
        outs = adamw(k, 0, 1, sib, res[k])
        out["grad_" + n], out["delta_" + n], out["new_m_" + n], out["new_v_" + n] = outs

    shapes = [full[n].shape for n in SMALL]
    red = dict(zip(SMALL, _unpack(red_pack, shapes)))
    for n in ("conv_w", "lru_conv_w"):
        red[n] = lax.dynamic_slice_in_dim(red[n], me * 128, 128, axis=2)
    oshapes = [args[n].shape for n in SMALL]
    d, m, v = _adamw_small(_pack([args[n] for n in SMALL]), _pack([red[n] for n in SMALL]),
                           _pack([args["m_" + n] for n in SMALL]), _pack([args["v_" + n] for n in SMALL]))
    for n, gg, dd, mm, vv in zip(SMALL, [red[n] for n in SMALL], _unpack(d, oshapes), _unpack(m, oshapes), _unpack(v, oshapes)):
        out["grad_" + n], out["delta_" + n], out["new_m_" + n], out["new_v_" + n] = gg, dd, mm, vv
    return out


def kernel(x, norm_ffn1, ffn1_w_in, ffn1_w_out, norm_mix, mix_w_in, conv_w, fgate_b, lru_conv_w, lru_conv_b, lru_w_a, lru_b_a, lru_w_x, lru_b_x, lru_lambda, mix_out_norm, mix_w_out, norm_ffn2, ffn2_w_in, ffn2_w_out, final_norm, loss_target, m_norm_ffn1, m_ffn1_w_in, m_ffn1_w_out, m_norm_mix, m_mix_w_in, m_conv_w, m_fgate_b, m_lru_conv_w, m_lru_conv_b, m_lru_w_a, m_lru_b_a, m_lru_w_x, m_lru_b_x, m_lru_lambda, m_mix_out_norm, m_mix_w_out, m_norm_ffn2, m_ffn2_w_in, m_ffn2_w_out, m_final_norm, v_norm_ffn1, v_ffn1_w_in, v_ffn1_w_out, v_norm_mix, v_mix_w_in, v_conv_w, v_fgate_b, v_lru_conv_w, v_lru_conv_b, v_lru_w_a, v_lru_b_a, v_lru_w_x, v_lru_b_x, v_lru_lambda, v_mix_out_norm, v_mix_w_out, v_norm_ffn2, v_ffn2_w_in, v_ffn2_w_out, v_final_norm):
    args = dict(locals())
    out = _step(args)
    res = [out["loss"], out["grad_x"]]
    for prefix in ("grad_", "delta_", "new_m_", "new_v_"):
        res += [out[prefix + n] for n in WEIGHTS]
    return tuple(res)
```

```python
import functools
import math

import jax
import jax.numpy as jnp
from jax import lax
from jax.experimental import pallas as pl
from jax.experimental.pallas import tpu as pltpu

F32 = jnp.float32
BF16 = jnp.bfloat16
S = jax.ShapeDtypeStruct
MESH = pl.DeviceIdType.MESH

D = 2048
DC = 512
DA = 1024
NH = 8
HD = 128
DL = 512
LB = 128
DIN = 5640
PW = 5760
C_Q, C_K, C_V = 12, 20, 28
C_GATE, C_LX, C_F = 36, 40, 44
EPS = 1e-6
LRU_C = 8.0
ATT_SCALE = HD ** -0.5
LR, B1, B2, AEPS, WD, STEP = 0.001, 0.9, 0.999, 1e-08, 0.01, 10
VMEM_LIMIT = 56 * 1024 * 1024

NT = (((1,), (1,)), ((), ()))
TN = (((0,), (0,)), ((), ()))
NN = (((1,), (0,)), ((), ()))


def _cp():
    return pltpu.CompilerParams(vmem_limit_bytes=VMEM_LIMIT)


def _bs(shape, fn):
    return pl.BlockSpec(shape, fn)


def _mm(name, a, b, a_spec, b_spec, o_spec, o_shape, grid, dims, nk, acc_tile, scale=1.0, res=None, r_spec=None, dep=None):
    has_res = res is not None
    has_dep = dep is not None

    def body(*refs):
        if has_dep:
            refs = refs[:2 + has_res] + refs[3 + has_res:]
        if has_res:
            a_ref, b_ref, r_ref, o_ref = refs[:4]
            rest = refs[4:]
        else:
            a_ref, b_ref, o_ref = refs[:3]
            rest = refs[3:]
        prod = lax.dot_general(a_ref[...].astype(BF16), b_ref[...].astype(BF16), dims, preferred_element_type=F32)

        def finish(acc):
            if scale != 1.0:
                acc = acc * scale
            if has_res:
                acc = r_ref[...] + acc
            o_ref[...] = acc.astype(o_ref.dtype)

        if nk == 1:
            finish(prod)
        else:
            acc_ref = rest[0]
            k = pl.program_id(2)

            @pl.when(k == 0)
            def _():
                acc_ref[...] = prod

            @pl.when(k > 0)
            def _():
                acc_ref[...] += prod

            @pl.when(k == nk - 1)
            def _():
                finish(acc_ref[...])

    in_specs = [a_spec, b_spec] + ([r_spec] if has_res else []) + ([pl.BlockSpec(memory_space=pl.ANY)] if has_dep else [])
    args = (a, b) + ((res,) if has_res else ()) + ((dep,) if has_dep else ())
    scratch = [pltpu.VMEM(acc_tile, F32)] if nk > 1 else []
    return pl.pallas_call(body, grid=grid, in_specs=in_specs, out_specs=o_spec, out_shape=o_shape,
                          scratch_shapes=scratch, compiler_params=_cp(), name=name)(*args)


def _tile(n, pref):
    for t in pref:
        if n % t == 0:
            return t
    return n


def _rms_fwd(x, gain):
    T = x.shape[0]
    tb = _tile(T, (512,))

    def body(x_ref, g_ref, h_ref, r_ref):
        xv = x_ref[...]
        r = lax.rsqrt(jnp.mean(xv * xv, axis=1, keepdims=True) + EPS)
        h_ref[...] = (xv * r * g_ref[...]).astype(BF16)
        r_ref[...] = r

    return pl.pallas_call(
        body, grid=(T // tb,),
        in_specs=[_bs((tb, D), lambda i: (i, 0)), _bs((1, D), lambda i: (0, 0))],
        out_specs=[_bs((tb, D), lambda i: (i, 0)), _bs((tb, 1), lambda i: (i, 0))],
        out_shape=[S((T, D), BF16), S((T, 1), F32)], compiler_params=_cp(), name="rms_fwd")(x, gain)


def _rms_bwd(dh, x, rstd, gain, dres):
    T = x.shape[0]
    tb = _tile(T, (512,))

    def body(dh_ref, x_ref, r_ref, g_ref, dres_ref, dx_ref, dxb_ref, dg_ref):
        i = pl.program_id(0)
        r = r_ref[...]
        xhat = x_ref[...] * r
        dh = dh_ref[...]
        dxh = dh * g_ref[...]
        m = jnp.mean(dxh * xhat, axis=1, keepdims=True)
        dx = dres_ref[...] + r * (dxh - xhat * m)
        dx_ref[...] = dx
        dxb_ref[...] = dx.astype(BF16)
        part = jnp.sum(dh * xhat, axis=0, keepdims=True)

        @pl.when(i == 0)
        def _():
            dg_ref[...] = part

        @pl.when(i > 0)
        def _():
            dg_ref[...] += part

    row = _bs((tb, D), lambda i: (i, 0))
    return pl.pallas_call(
        body, grid=(T // tb,),
        in_specs=[row, row, _bs((tb, 1), lambda i: (i, 0)), _bs((1, D), lambda i: (0, 0)), row],
        out_specs=[row, row, _bs((1, D), lambda i: (0, 0))],
        out_shape=[S((T, D), F32), S((T, D), BF16), S((1, D), F32)], compiler_params=_cp(), name="rms_bwd")(dh, x, rstd, gain, dres)


def _loss_head(x, gain, tgt):
    T = x.shape[0]
    tb = _tile(T, (512,))

    def body(x_ref, g_ref, t_ref, l_ref, dx_ref, dxb_ref, dg_ref):
        i = pl.program_id(0)
        xv = x_ref[...]
        g = g_ref[...]
        r = lax.rsqrt(jnp.mean(xv * xv, axis=1, keepdims=True) + EPS)
        xhat = xv * r
        e = xhat * g - t_ref[...]
        lpart = 0.5 * jnp.sum(jnp.sum(e * e, axis=1, keepdims=True), axis=0, keepdims=True) * (1.0 / D)
        dy = e * (1.0 / D)
        dxh = dy * g
        m = jnp.mean(dxh * xhat, axis=1, keepdims=True)
        dx = r * (dxh - xhat * m)
        dx_ref[...] = dx
        dxb_ref[...] = dx.astype(BF16)
        gpart = jnp.sum(dy * xhat, axis=0, keepdims=True)
        lrow = jnp.broadcast_to(lpart, (1, 128))

        @pl.when(i == 0)
        def _():
            dg_ref[...] = gpart
            l_ref[...] = lrow

        @pl.when(i > 0)
        def _():
            dg_ref[...] += gpart
            l_ref[...] += lrow

    row = _bs((tb, D), lambda i: (i, 0))
    return pl.pallas_call(
        body, grid=(T // tb,),
        in_specs=[row, _bs((1, D), lambda i: (0, 0)), row],
        out_specs=[_bs((1, 128), lambda i: (0, 0)), row, row, _bs((1, D), lambda i: (0, 0))],
        out_shape=[S((1, 128), F32), S((T, D), F32), S((T, D), BF16), S((1, D), F32)],
        compiler_params=_cp(), name="loss_head")(x, gain, tgt)


def _sigmoid(z):
    return 0.5 * jnp.tanh(0.5 * z) + 0.5


def _ffn_in(h, win):
    T = h.shape[0]
    Fs = win.shape[2]
    F = 2 * Fs
    tn = _tile(Fs, (256, 128))
    nb = Fs // tn
    tm = _tile(T, (1024, 512))

    def body(h_ref, wg_ref, wu_ref, zg_ref, zu_ref, a_ref):
        hv = h_ref[...]
        zg = jnp.dot(hv, wg_ref[...], preferred_element_type=F32)
        zu = jnp.dot(hv, wu_ref[...], preferred_element_type=F32)
        zg_ref[...] = zg.astype(BF16)
        zu_ref[...] = zu.astype(BF16)
        a_ref[...] = (zg * _sigmoid(zg) * zu).astype(BF16)

    col = _bs((tm, tn), lambda i, j: (i, j))
    return pl.pallas_call(
        body, grid=(T // tm, F // tn),
        in_specs=[_bs((tm, D), lambda i, j: (i, 0)),
                  _bs((None, D, tn), lambda i, j: (j // nb, 0, j % nb)),
                  _bs((None, D, tn), lambda i, j: (2 + j // nb, 0, j % nb))],
        out_specs=[col, col, col],
        out_shape=[S((T, F), BF16)] * 3, compiler_params=_cp(), name="ffn_in")(h, win, win)


def _ffn_out(act, wout, x, dep=None):
    T, F = act.shape
    tm = _tile(T, (512,))
    tn = 1024
    return _mm("ffn_out", act, wout,
               _bs((tm, F), lambda j, i, k: (i, 0)), _bs((F, tn), lambda j, i, k: (0, j)),
               _bs((tm, tn), lambda j, i, k: (i, j)), S((T, D), F32), (D // tn, T // tm, 1), NN, 1, (tm, tn),
               scale=0.5, res=x, r_spec=_bs((tm, tn), lambda j, i, k: (i, j)), dep=dep)


def _ffn_bwd_dz(dyb, wout, zg, zu, dep):
    T, F = zg.shape
    tm = _tile(T, (1024, 512))
    tn = _tile(F, (512, 256))

    def body(dy_ref, w_ref, zg_ref, zu_ref, dep_ref, dz_ref):
        da = 0.5 * lax.dot_general(dy_ref[...], w_ref[...], NT, preferred_element_type=F32)
        zg = zg_ref[...].astype(F32)
        zu = zu_ref[...].astype(F32)
        s = _sigmoid(zg)
        dz_ref[0] = (da * zu * (s * (1.0 + zg * (1.0 - s)))).astype(BF16)
        dz_ref[1] = (da * (zg * s)).astype(BF16)

    col = _bs((tm, tn), lambda i, j: (i, j))
    return pl.pallas_call(
        body, grid=(T // tm, F // tn),
        in_specs=[_bs((tm, D), lambda i, j: (i, 0)), _bs((tn, D), lambda i, j: (j, 0)), col, col,
                  pl.BlockSpec(memory_space=pl.ANY)],
        out_specs=_bs((2, tm, tn), lambda i, j: (0, i, j)), out_shape=S((2, T, F), BF16),
        compiler_params=_cp(), name="ffn_bwd_dz")(dyb, wout, zg, zu, dep)


def _ffn_bwd_dh(dz, win):
    _, T, F = dz.shape
    Fs = win.shape[2]
    tk = _tile(Fs, (2816, 1408, 256, 128))
    nkb = Fs // tk
    tm = _tile(T, (1024, 512))
    tn = 512
    nk = 2 * nkb

    def body(dzg_ref, dzu_ref, wg_ref, wu_ref, o_ref, acc_ref):
        k = pl.program_id(2)
        prod = (lax.dot_general(dzg_ref[...], wg_ref[...], NT, preferred_element_type=F32)
                + lax.dot_general(dzu_ref[...], wu_ref[...], NT, preferred_element_type=F32))

        @pl.when(k == 0)
        def _():
            acc_ref[...] = prod

        @pl.when(k > 0)
        def _():
            acc_ref[...] += prod

        @pl.when(k == nk - 1)
        def _():
            o_ref[...] = acc_ref[...]

    def a_spec(half):
        return _bs((None, tm, tk), lambda i, j, k: (half, i, k))

    return pl.pallas_call(
        body, grid=(T // tm, D // tn, nk),
        in_specs=[a_spec(0), a_spec(1),
                  _bs((None, tn, tk), lambda i, j, k: (k // nkb, j, k % nkb)),
                  _bs((None, tn, tk), lambda i, j, k: (2 + k // nkb, j, k % nkb))],
        out_specs=_bs((tm, tn), lambda i, j, k: (i, j)), out_shape=S((T, D), F32),
        scratch_shapes=[pltpu.VMEM((tm, tn), F32)], compiler_params=_cp(), name="ffn_bwd_dh")(dz, dz, win, win)


def _ffn_bwd_dwin(h, dz):
    _, T, F = dz.shape
    Fs = F // 2
    tn = _tile(Fs, (1408, 256, 128))
    nb = Fs // tn
    tm = 512
    return _mm("ffn_bwd_dwin", h, dz,
               _bs((T, tm), lambda i, j, k: (0, i)), _bs((None, T, tn), lambda i, j, k: (j // (2 * nb), 0, j % (2 * nb))),
               _bs((None, tm, tn), lambda i, j, k: (j // nb, i, j % nb)), S((4, D, Fs), BF16),
               (D // tm, 4 * nb, 1), TN, 1, (tm, tn))


def _mm_tn(name, a, b, scale=1.0, tm=512, tn=1024):
    T, M = a.shape
    N = b.shape[1]
    tm = _tile(M, (tm, 512, 256, 128))
    tn = _tile(N, (tn, 1152, 1024, 512, 128))
    return _mm(name, a, b,
               _bs((T, tm), lambda i, j, k: (0, i)), _bs((T, tn), lambda i, j, k: (0, j)),
               _bs((tm, tn), lambda i, j, k: (i, j)), S((M, N), BF16), (M // tm, N // tn, 1), TN, 1, (tm, tn), scale=scale)


def _mm_nt_full(name, a, b, tn, dep=None):
    T, K = a.shape
    N = b.shape[0]
    tm = _tile(T, (1024, 512))
    return _mm(name, a, b,
               _bs((tm, K), lambda i, j, k: (i, 0)), _bs((tn, K), lambda i, j, k: (j, 0)),
               _bs((tm, tn), lambda i, j, k: (i, j)), S((T, N), F32), (T // tm, N // tn, 1), NT, 1, (tm, tn), dep=dep)


def _bt(T):
    return _tile(T, (512,))


def _down(ext, s):
    return pltpu.roll(ext, s, 0)[8:, :]


def _up(ext, s):
    n = ext.shape[0]
    return pltpu.roll(ext, n - s, 0)[: n - 8, :]


def _halo_prev(ref, start, b):
    lo = pl.multiple_of(jnp.maximum(start - 8, 0), 8)
    return ref[pl.ds(lo, 8), :] * (b > 0).astype(F32)


def _halo_next(ref, start, bt, b, nb):
    lo = pl.multiple_of(jnp.minimum(start + bt, (nb - 1) * bt), 8)
    return ref[pl.ds(lo, 8), :] * (b < nb - 1).astype(F32)


def _scan_fwd(A, U):
    n = U.shape[0]
    row = lax.broadcasted_iota(jnp.int32, U.shape, 0)
    d = 1
    while d < n:
        keep = row >= d
        Us = jnp.where(keep, pltpu.roll(U, d, 0), 0.0)
        if A is None:
            U = U + Us
        else:
            As = jnp.where(keep, pltpu.roll(A, d, 0), 1.0)
            U = A * Us + U
            A = A * As
        d *= 2
    return A, U


def _scan_bwd(A, U):
    n = U.shape[0]
    row = lax.broadcasted_iota(jnp.int32, U.shape, 0)
    d = 1
    while d < n:
        keep = row < n - d
        Us = jnp.where(keep, pltpu.roll(U, n - d, 0), 0.0)
        if A is None:
            U = U + Us
        else:
            As = jnp.where(keep, pltpu.roll(A, n - d, 0), 1.0)
            U = A * Us + U
            A = A * As
        d *= 2
    return A, U


def _softplus(z):
    return jnp.maximum(z, 0.0) + jnp.log(1.0 + jnp.exp(-jnp.abs(z)))


def _gelu_parts(g):
    k0 = math.sqrt(2.0 / math.pi)
    t = jnp.tanh(k0 * (g + 0.044715 * g * g * g))
    gel = 0.5 * g * (1.0 + t)
    dgel = 0.5 * (1.0 + t) + 0.5 * g * (1.0 - t * t) * k0 * (1.0 + 3.0 * 0.044715 * g * g)
    return gel, dgel


def _conv_fwd(P, cw):
    T = P.shape[0]
    bt = _bt(T)
    nb = T // bt

    def body(b_ref, c_ref, v_ref, w_ref, y_ref):
        w = w_ref[...]

        def step(b, carry):
            start = pl.multiple_of(b * bt, bt)
            rows = pl.ds(start, bt)
            m = c_ref[rows, :] * v_ref[rows, :]
            ext = jnp.concatenate([_halo_prev(c_ref, start, b) * _halo_prev(v_ref, start, b), m], axis=0)
            z = w[2:3, :] * m + w[1:2, :] * _down(ext, 1) + w[0:1, :] * _down(ext, 2)
            y_ref[rows, :] = b_ref[rows, :] * z
            return carry

        lax.fori_loop(0, nb, step, 0)

    def colspec(off):
        return _bs((T, 128), lambda c: (0, off + c))

    return pl.pallas_call(
        body, grid=(DC // 128,),
        in_specs=[colspec(0), colspec(4), colspec(8), _bs((8, 128), lambda c: (0, c))],
        out_specs=_bs((T, 128), lambda c: (0, c)), out_shape=S((T, DC), F32),
        compiler_params=_cp(), name="conv_fwd")(P, P, P, cw)


def _conv_bwd(P, cw, dy):
    T = P.shape[0]
    bt = _bt(T)
    nb = T // bt

    def body(b_ref, c_ref, v_ref, w_ref, dy_ref, db_ref, dc_ref, dv_ref, dw_ref):
        w = w_ref[...]

        def step(b, carry):
            a0, a1, a2 = carry
            start = pl.multiple_of(b * bt, bt)
            rows = pl.ds(start, bt)
            cb, cc, cv, dy = b_ref[rows, :], c_ref[rows, :], v_ref[rows, :], dy_ref[rows, :]
            m = cc * cv
            ext = jnp.concatenate([_halo_prev(c_ref, start, b) * _halo_prev(v_ref, start, b), m], axis=0)
            m1, m2 = _down(ext, 1), _down(ext, 2)
            z = w[2:3, :] * m + w[1:2, :] * m1 + w[0:1, :] * m2
            db_ref[rows, :] = dy * z
            dz = dy * cb
            extn = jnp.concatenate([dz, _halo_next(dy_ref, start, bt, b, nb) * _halo_next(b_ref, start, bt, b, nb)], axis=0)
            dm = w[2:3, :] * dz + w[1:2, :] * _up(extn, 1) + w[0:1, :] * _up(extn, 2)
            dc_ref[rows, :] = dm * cv
            dv_ref[rows, :] = dm * cc
            return (a0 + jnp.sum(dz * m2, axis=0, keepdims=True),
                    a1 + jnp.sum(dz * m1, axis=0, keepdims=True),
                    a2 + jnp.sum(dz * m, axis=0, keepdims=True))

        zero = jnp.zeros((1, 128), F32)
        a0, a1, a2 = lax.fori_loop(0, nb, step, (zero, zero, zero))
        dw_ref[...] = jnp.zeros((8, 128), F32)
        dw_ref[0:1, :] = a0
        dw_ref[1:2, :] = a1
        dw_ref[2:3, :] = a2

    def colspec(off):
        return _bs((T, 128), lambda c: (0, off + c))

    own = _bs((T, 128), lambda c: (0, c))
    return pl.pallas_call(
        body, grid=(DC // 128,),
        in_specs=[colspec(0), colspec(4), colspec(8), _bs((8, 128), lambda c: (0, c)), own],
        out_specs=[own, own, own, _bs((8, 128), lambda c: (0, c))],
        out_shape=[S((T, DC), F32)] * 3 + [S((8, DC), F32)], compiler_params=_cp(), name="conv_bwd")(P, P, P, cw, dy)


def _fgate_fwd(P, fb):
    T = P.shape[0]
    bt = _bt(T)
    nb = T // bt

    def body(f_ref, b_ref, c_ref):
        bias = b_ref[...]

        def step(b, carry):
            rows = pl.ds(pl.multiple_of(b * bt, bt), bt)
            logf = -_softplus(-(f_ref[rows, :] + bias))
            _, cs = _scan_fwd(None, logf)
            cs = cs + carry
            c_ref[rows, :] = cs
            return cs[bt - 1:bt, :]

        lax.fori_loop(0, nb, step, jnp.zeros((1, 128), F32))

    return pl.pallas_call(
        body, grid=(1,),
        in_specs=[_bs((T, 128), lambda i: (0, C_F)), _bs((1, 128), lambda i: (0, 0))],
        out_specs=_bs((T, 128), lambda i: (0, 0)), out_shape=S((T, 128), F32),
        compiler_params=_cp(), name="fgate_fwd")(P, fb)


def _fgate_bwd(P, fb, dcum):
    T = P.shape[0]
    bt = _bt(T)
    nb = T // bt

    def body(f_ref, b_ref, dc_ref, df_ref, db_ref):
        bias = b_ref[...]

        def step(i, carry):
            run, acc = carry
            b = nb - 1 - i
            rows = pl.ds(pl.multiple_of(b * bt, bt), bt)
            _, rs = _scan_bwd(None, dc_ref[rows, :])
            rs = rs + run
            df = rs * jax.nn.sigmoid(-(f_ref[rows, :] + bias))
            df_ref[rows, :] = df
            return rs[0:1, :], acc + jnp.sum(df, axis=0, keepdims=True)

        zero = jnp.zeros((1, 128), F32)
        _, acc = lax.fori_loop(0, nb, step, (zero, zero))
        db_ref[...] = acc

    return pl.pallas_call(
        body, grid=(1,),
        in_specs=[_bs((T, 128), lambda i: (0, C_F)), _bs((1, 128), lambda i: (0, 0)), _bs((T, 128), lambda i: (0, 0))],
        out_specs=[_bs((T, 128), lambda i: (0, 0)), _bs((1, 128), lambda i: (0, 0))],
        out_shape=[S((T, 128), F32), S((1, 128), F32)], compiler_params=_cp(), name="fgate_bwd")(P, fb, dcum)


def _att_tile(T):
    return _tile(T, (512,))


def _causal_mask(tq):
    return lax.broadcasted_iota(jnp.int32, (tq, tq), 1) <= lax.broadcasted_iota(jnp.int32, (tq, tq), 0)


def _attn_fwd(qkv, cumq, cumk):
    T = qkv.shape[0]
    tq = _att_tile(T)
    nq = T // tq

    def body(q_ref, k_ref, v_ref, cq_ref, ck_ref, o_ref, lse_ref):
        i = pl.program_id(1)
        q = q_ref[...]
        cq = cq_ref[...]

        def block(j, carry, diagonal):
            m_old, l_old, acc = carry
            rows = pl.ds(pl.multiple_of(j * tq, tq), tq)
            s = lax.dot_general(q, k_ref[rows, :], NT, preferred_element_type=F32)
            s = s * ATT_SCALE - ck_ref[j]
            if diagonal:
                s = jnp.where(_causal_mask(tq), s, -jnp.inf)
            m_new = jnp.maximum(m_old, jnp.max(s, axis=1, keepdims=True))
            p = jnp.exp(s - m_new)
            alpha = jnp.exp(m_old - m_new)
            l_new = alpha * l_old + jnp.sum(p, axis=1, keepdims=True)
            acc = alpha * acc + jnp.dot(p.astype(BF16), v_ref[rows, :], preferred_element_type=F32)
            return m_new, l_new, acc

        init = (jnp.full((tq, 1), -jnp.inf, F32), jnp.zeros((tq, 1), F32), jnp.zeros((tq, HD), F32))
        carry = lax.fori_loop(0, i, lambda j, c: block(j, c, False), init)
        m, l, acc = block(i, carry, True)
        o_ref[...] = acc / l
        lse_ref[...] = m + cq + jnp.log(l)

    return pl.pallas_call(
        body, grid=(NH, nq),
        in_specs=[_bs((tq, HD), lambda h, i: (i, h)),
                  _bs((T, HD), lambda h, i: (0, NH + h)),
                  _bs((T, HD), lambda h, i: (0, 2 * NH + h)),
                  _bs((None, tq, 1), lambda h, i: (h, i, 0)),
                  _bs((None, nq, 1, tq), lambda h, i: (h, 0, 0, 0))],
        out_specs=[_bs((tq, HD), lambda h, i: (i, h)), _bs((None, tq, 1), lambda h, i: (h, i, 0))],
        out_shape=[S((T, DA), F32), S((NH, T, 1), F32)],
        compiler_params=_cp(), name="attn_fwd")(qkv, qkv, qkv, cumq, cumk)


def _attn_bwd(qkv, cumq, cumk, lse, o, do):
    T = qkv.shape[0]
    tq = _att_tile(T)
    nq = T // tq

    def body(q_ref, k_ref, v_ref, cq_ref, ck_ref, lse_ref, o_ref, do_ref, dq_ref, dk_ref, dv_ref, dc_ref, dr_ref):
        j = pl.program_id(1)

        @pl.when(j == 0)
        def _():
            dq_ref[...] = jnp.zeros((T, HD), F32)
            dr_ref[...] = jnp.zeros((T, 1), F32)

        k = k_ref[...]
        v = v_ref[...]
        ck = ck_ref[...]

        def block(i, carry, diagonal):
            dk_acc, dv_acc, dc_acc = carry
            rows = pl.ds(pl.multiple_of(i * tq, tq), tq)
            q = q_ref[rows, :]
            do_f = do_ref[rows, :]
            dob = do_f.astype(BF16)
            s = lax.dot_general(q, k, NT, preferred_element_type=F32)
            p = jnp.exp(s * ATT_SCALE + (cq_ref[rows, :] - lse_ref[rows, :]) - ck)
            if diagonal:
                p = jnp.where(_causal_mask(tq), p, 0.0)
            delta = jnp.sum(do_f * o_ref[rows, :], axis=1, keepdims=True)
            dp = lax.dot_general(dob, v, NT, preferred_element_type=F32)
            ds = p * (dp - delta)
            dsb = (ds * ATT_SCALE).astype(BF16)
            dq_ref[rows, :] += jnp.dot(dsb, k, preferred_element_type=F32)
            dr_ref[rows, :] += jnp.sum(ds, axis=1, keepdims=True)
            return (dk_acc + lax.dot_general(dsb, q, TN, preferred_element_type=F32),
                    dv_acc + lax.dot_general(p.astype(BF16), dob, TN, preferred_element_type=F32),
                    dc_acc - jnp.sum(ds, axis=0, keepdims=True))

        init = (jnp.zeros((tq, HD), F32), jnp.zeros((tq, HD), F32), jnp.zeros((1, tq), F32))
        carry = block(j, init, True)
        dk_acc, dv_acc, dc_acc = lax.fori_loop(j + 1, nq, lambda i, c: block(i, c, False), carry)
        dk_ref[...] = dk_acc
        dv_ref[...] = dv_acc
        dc_ref[...] = dc_acc

    def whole(col):
        return _bs((T, HD), lambda h, j: (0, col + h))

    qvec = _bs((None, T, 1), lambda h, j: (h, 0, 0))
    kv_out = _bs((tq, HD), lambda h, j: (j, h))
    return pl.pallas_call(
        body, grid=(NH, nq),
        in_specs=[whole(0), _bs((tq, HD), lambda h, j: (j, NH + h)), _bs((tq, HD), lambda h, j: (j, 2 * NH + h)),
                  qvec, _bs((None, None, 1, tq), lambda h, j: (h, j, 0, 0)), qvec, whole(0), whole(0)],
        out_specs=[whole(0), kv_out, kv_out, _bs((None, 1, tq), lambda h, j: (h, 0, j)), qvec],
        out_shape=[S((T, DA), F32)] * 3 + [S((NH, 1, T), F32), S((NH, T, 1), F32)],
        compiler_params=_cp(), name="attn_bwd")(qkv, qkv, qkv, cumq, cumk, lse, o, do)


def _lru_gates(xr, wa, wx, ba, bx, sp):
    xb = xr.astype(BF16)
    r = jax.nn.sigmoid(jnp.dot(xb, wa, preferred_element_type=F32) + ba)
    ig = jax.nn.sigmoid(jnp.dot(xb, wx, preferred_element_type=F32) + bx)
    log_a = -LRU_C * r * sp
    a = jnp.exp(log_a)
    th = jnp.tanh(log_a)
    om = -2.0 * th / (1.0 - th)
    mult = jnp.sqrt(om)
    return xb, r, ig, a, om, mult


def _lru_xr(lx_ref, cw, cb, start, b, rows):
    lx = lx_ref[rows, :]
    ext = jnp.concatenate([_halo_prev(lx_ref, start, b), lx], axis=0)
    return cw[3:4, :] * lx + cw[2:3, :] * _down(ext, 1) + cw[1:2, :] * _down(ext, 2) + cw[0:1, :] * _down(ext, 3) + cb


def _lru_fwd(P, lcw, vec, wa, wx):
    T = P.shape[0]
    bt = _bt(T)
    nb = T // bt

    def body(g_ref, lx_ref, cw_ref, vec_ref, wa_ref, wx_ref, y_ref, h_ref):
        cw = cw_ref[...]
        vec = vec_ref[...]
        wa = wa_ref[...].astype(BF16)
        wx = wx_ref[...].astype(BF16)
        sp = _softplus(-vec[3:4, :])

        def step(b, carry):
            start = pl.multiple_of(b * bt, bt)
            rows = pl.ds(start, bt)
            xr = _lru_xr(lx_ref, cw, vec[0:1, :], start, b, rows)
            _, _, ig, a, _, mult = _lru_gates(xr, wa, wx, vec[1:2, :], vec[2:3, :], sp)
            u = mult * (ig * xr)
            ac, hc = _scan_fwd(a, u)
            hb = hc + ac * carry
            h_ref[rows, :] = hb
            gel, _ = _gelu_parts(g_ref[rows, :])
            y_ref[rows, :] = gel * hb
            return hb[bt - 1:bt, :]

        lax.fori_loop(0, nb, step, jnp.zeros((1, 128), F32))

    own = _bs((T, 128), lambda c: (0, c))
    return pl.pallas_call(
        body, grid=(DL // 128,),
        in_specs=[_bs((T, 128), lambda c: (0, C_GATE + c)), _bs((T, 128), lambda c: (0, C_LX + c)),
                  _bs((8, 128), lambda c: (0, c)), _bs((8, 128), lambda c: (0, c)),
                  _bs((None, LB, LB), lambda c: (c, 0, 0)), _bs((None, LB, LB), lambda c: (c, 0, 0))],
        out_specs=[own, own], out_shape=[S((T, DL), F32)] * 2, compiler_params=_cp(), name="lru_fwd")(P, P, lcw, vec, wa, wx)


def _lru_bwd(P, lcw, vec, wa, wx, hst, dy):
    T = P.shape[0]
    bt = _bt(T)
    nb = T // bt

    def body(g_ref, lx_ref, cw_ref, vec_ref, wa_ref, wx_ref, h_ref, dy_ref,
             dg_ref, dlx_ref, sm_ref, dwa_ref, dwx_ref, dxr_s):
        cw = cw_ref[...]
        vec = vec_ref[...]
        wa = wa_ref[...].astype(BF16)
        wx = wx_ref[...].astype(BF16)
        lam = vec[3:4, :]
        sp = _softplus(-lam)
        dwa_ref[...] = jnp.zeros((LB, LB), F32)
        dwx_ref[...] = jnp.zeros((LB, LB), F32)
        zero = jnp.zeros((1, 128), F32)

        def step1(i, carry):
            wc, s_cb, s_ba, s_bx, s_sp = carry
            b = nb - 1 - i
            start = pl.multiple_of(b * bt, bt)
            rows = pl.ds(start, bt)
            xr = _lru_xr(lx_ref, cw, vec[0:1, :], start, b, rows)
            xb, r, ig, a, om, mult = _lru_gates(xr, wa, wx, vec[1:2, :], vec[2:3, :], sp)
            hb = h_ref[rows, :]
            dy = dy_ref[rows, :]
            gel, dgel = _gelu_parts(g_ref[rows, :])
            dg_ref[rows, :] = dy * hb * dgel
            dh = dy * gel
            ac, wcum = _scan_bwd(a, a * dh)
            w = wcum + ac * wc
            g = dh + _up(jnp.concatenate([w, jnp.broadcast_to(wc, (8, 128))], axis=0), 1)
            hprev = _down(jnp.concatenate([_halo_prev(h_ref, start, b), hb], axis=0), 1)
            da = g * hprev
            dmult = g * (ig * xr)
            dix = g * mult
            di = dix * xr
            dlog_a = da * a - dmult * ((1.0 - om) / mult)
            dr = dlog_a * (-LRU_C * sp)
            dpr = dr * r * (1.0 - r)
            dpi = di * ig * (1.0 - ig)
            dprb, dpib = dpr.astype(BF16), dpi.astype(BF16)
            dwa_ref[...] += lax.dot_general(xb, dprb, TN, preferred_element_type=F32)
            dwx_ref[...] += lax.dot_general(xb, dpib, TN, preferred_element_type=F32)
            dxr = (dix * ig + lax.dot_general(dprb, wa, NT, preferred_element_type=F32)
                   + lax.dot_general(dpib, wx, NT, preferred_element_type=F32))
            dxr_s[rows, :] = dxr
            return (w[0:1, :], s_cb + jnp.sum(dxr, axis=0, keepdims=True), s_ba + jnp.sum(dpr, axis=0, keepdims=True),
                    s_bx + jnp.sum(dpi, axis=0, keepdims=True), s_sp + jnp.sum(dlog_a * (-LRU_C * r), axis=0, keepdims=True))

        _, s_cb, s_ba, s_bx, s_sp = lax.fori_loop(0, nb, step1, (zero, zero, zero, zero, zero))

        def step2(b, carry):
            t0, t1, t2, t3 = carry
            start = pl.multiple_of(b * bt, bt)
            rows = pl.ds(start, bt)
            dxr = dxr_s[rows, :]
            extn = jnp.concatenate([dxr, _halo_next(dxr_s, start, bt, b, nb)], axis=0)
            dlx_ref[rows, :] = (cw[3:4, :] * dxr + cw[2:3, :] * _up(extn, 1) + cw[1:2, :] * _up(extn, 2)
                                + cw[0:1, :] * _up(extn, 3))
            lx = lx_ref[rows, :]
            ext = jnp.concatenate([_halo_prev(lx_ref, start, b), lx], axis=0)
            return (t0 + jnp.sum(dxr * _down(ext, 3), axis=0, keepdims=True),
                    t1 + jnp.sum(dxr * _down(ext, 2), axis=0, keepdims=True),
                    t2 + jnp.sum(dxr * _down(ext, 1), axis=0, keepdims=True),
                    t3 + jnp.sum(dxr * lx, axis=0, keepdims=True))

        t0, t1, t2, t3 = lax.fori_loop(0, nb, step2, (zero, zero, zero, zero))
        sm_ref[...] = jnp.zeros((16, 128), F32)
        for k, val in enumerate((t0, t1, t2, t3, s_cb, s_ba, s_bx, -s_sp * jax.nn.sigmoid(-lam))):
            sm_ref[k:k + 1, :] = val

    own = _bs((T, 128), lambda c: (0, c))
    wspec = _bs((None, LB, LB), lambda c: (c, 0, 0))
    return pl.pallas_call(
        body, grid=(DL // 128,),
        in_specs=[_bs((T, 128), lambda c: (0, C_GATE + c)), _bs((T, 128), lambda c: (0, C_LX + c)),
                  _bs((8, 128), lambda c: (0, c)), _bs((8, 128), lambda c: (0, c)), wspec, wspec, own, own],
        out_specs=[own, own, _bs((16, 128), lambda c: (0, c)), wspec, wspec],
        out_shape=[S((T, DL), F32)] * 2 + [S((16, DL), F32), S((4, LB, LB), F32), S((4, LB, LB), F32)],
        scratch_shapes=[pltpu.VMEM((T, 128), F32)], compiler_params=_cp(), name="lru_bwd")(P, P, lcw, vec, wa, wx, hst, dy)


_GROUPS = ((0, DC), (DC, DC + DA), (DC + DA, D))


def _gnorm_fwd(yc, ya, yl, gain):
    T = yc.shape[0]
    tb = _tile(T, (512,))

    def body(c_ref, a_ref, l_ref, g_ref, yn_ref, r0_ref, r1_ref, r2_ref):
        for (lo, hi), src, r_ref in zip(_GROUPS, (c_ref, a_ref, l_ref), (r0_ref, r1_ref, r2_ref)):
            yv = src[...]
            r = lax.rsqrt(jnp.mean(yv * yv, axis=1, keepdims=True) + EPS)
            yn_ref[:, lo:hi] = (yv * r * g_ref[:, lo:hi]).astype(BF16)
            r_ref[...] = r

    rs = _bs((tb, 1), lambda i: (i, 0))
    return pl.pallas_call(
        body, grid=(T // tb,),
        in_specs=[_bs((tb, DC), lambda i: (i, 0)), _bs((tb, DA), lambda i: (i, 0)), _bs((tb, DL), lambda i: (i, 0)),
                  _bs((1, D), lambda i: (0, 0))],
        out_specs=[_bs((tb, D), lambda i: (i, 0)), rs, rs, rs],
        out_shape=[S((T, D), BF16)] + [S((T, 1), F32)] * 3, compiler_params=_cp(), name="gnorm_fwd")(yc, ya, yl, gain)


def _gnorm_bwd(dyn, yc, ya, yl, r0, r1, r2, gain):
    T = yc.shape[0]
    tb = _tile(T, (512,))

    def body(d_ref, c_ref, a_ref, l_ref, r0_ref, r1_ref, r2_ref, g_ref, dc_ref, da_ref, dl_ref, dg_ref):
        i = pl.program_id(0)
        for (lo, hi), src, r_ref, dst in zip(_GROUPS, (c_ref, a_ref, l_ref), (r0_ref, r1_ref, r2_ref), (dc_ref, da_ref, dl_ref)):
            r = r_ref[...]
            yhat = src[...] * r
            dy = d_ref[:, lo:hi]
            dyh = dy * g_ref[:, lo:hi]
            m = jnp.mean(dyh * yhat, axis=1, keepdims=True)
            dst[...] = r * (dyh - yhat * m)
            part = jnp.sum(dy * yhat, axis=0, keepdims=True)

            @pl.when(i == 0)
            def _():
                dg_ref[:, lo:hi] = part

            @pl.when(i > 0)
            def _():
                dg_ref[:, lo:hi] += part

    rs = _bs((tb, 1), lambda i: (i, 0))
    specs = [_bs((tb, DC), lambda i: (i, 0)), _bs((tb, DA), lambda i: (i, 0)), _bs((tb, DL), lambda i: (i, 0))]
    return pl.pallas_call(
        body, grid=(T // tb,),
        in_specs=[_bs((tb, D), lambda i: (i, 0))] + specs + [rs, rs, rs, _bs((1, D), lambda i: (0, 0))],
        out_specs=specs + [_bs((1, D), lambda i: (0, 0))],
        out_shape=[S((T, DC), F32), S((T, DA), F32), S((T, DL), F32), S((1, D), F32)],
        compiler_params=_cp(), name="gnorm_bwd")(dyn, yc, ya, yl, r0, r1, r2, gain)


HBM = pl.BlockSpec(memory_space=pltpu.HBM)
N_BIG = 6


def _place():
    x, y, c = lax.axis_index("x"), lax.axis_index("y"), lax.axis_index("c")
    return x, y, c, 2 * x + y


def _peer(x, y, j):
    return x ^ ((j + 1) >> 1), y ^ ((j + 1) & 1)


SEM = pl.BlockSpec(memory_space=pltpu.SEMAPHORE)
ANY = pl.BlockSpec(memory_space=pl.ANY)
VM = pl.BlockSpec(memory_space=pltpu.VMEM)
EFFECT = pltpu.SideEffectType.DATAFLOW_SIDE_EFFECTING
N_AG = N_BIG + 1


def _hbm(a):
    return pltpu.with_memory_space_constraint(a, pltpu.HBM)


AG_ORDER = (0, 1, N_BIG, 2, 3, 4, 5)
AG_MIX, AG_FFN2 = (N_BIG, 2, 3), (4, 5)


def _ag_copy(src, land, ssem, rsem, t, j, chip):
    x, y, c, _ = _place()
    px, py = _peer(x, y, j)
    if t == N_BIG:
        s_ref, d_ref = src, land.at[chip]
    else:
        rh = src.shape[0] // 2
        half = pl.ds(c * rh, rh)
        s_ref, d_ref = src.at[half], land.at[chip, half]
    return pltpu.make_async_remote_copy(src_ref=s_ref, dst_ref=d_ref, send_sem=ssem.at[3 * t + j], recv_sem=rsem.at[3 * t + j],
                                        device_id=(px, py, c), device_id_type=MESH)


def _ag_start(l, srcs, dep):
    n = N_AG

    def body(*refs):
        src = refs[:n]
        ssem, rsem = refs[2 * n + 1], refs[2 * n + 2]
        land = refs[3 * n + 3:4 * n + 3]
        token = refs[4 * n + 3]
        _, _, _, me = _place()
        for t in AG_ORDER:
            for j in range(3):
                _ag_copy(src[t], land[t], ssem, rsem, t, j, me).start()
        token[...] = jnp.zeros_like(token)

    lands = [lax.empty((4,) + a.shape, a.dtype) for a in srcs]
    dma = pltpu.SemaphoreType.DMA
    outs = pl.pallas_call(
        body, name=f"ag_start_{l}",
        out_shape=(dma((3 * n,)), dma((3 * n,))) + tuple(pltpu.HBM(a.shape, a.dtype) for a in list(srcs) + lands) + (S((8, 128), F32),),
        in_specs=[HBM] * (2 * n) + [ANY], out_specs=(SEM, SEM) + (HBM,) * (2 * n) + (VM,),
        input_output_aliases={i: 2 + i for i in range(2 * n)},
        compiler_params=pltpu.CompilerParams(has_side_effects=EFFECT),
    )(*[_hbm(a) for a in srcs], *[_hbm(a) for a in lands], dep)
    return outs[0], outs[1], outs[2:2 + n], outs[2 + n:2 + 2 * n], outs[-1]


def _ag_wait(name, idx, ssem, rsem, srcs, lands, after):
    n = len(idx)

    def body(*refs):
        src, land = refs[:n], refs[n:2 * n]
        ssem, rsem = refs[2 * n], refs[2 * n + 1]
        x, y, _, _ = _place()
        for p, t in enumerate(idx):
            for j in range(3):
                px, py = _peer(x, y, j)
                cp = _ag_copy(src[p], land[p], ssem, rsem, t, j, 2 * px + py)
                cp.wait_send()
                cp.wait_recv()

    outs = pl.pallas_call(
        body, name=name,
        out_shape=tuple(pltpu.HBM(a.shape, a.dtype) for a in list(srcs) + list(lands)),
        in_specs=[HBM] * (2 * n) + [SEM, SEM, ANY], out_specs=(HBM,) * (2 * n),
        input_output_aliases={i: i for i in range(2 * n)},
        compiler_params=pltpu.CompilerParams(has_side_effects=EFFECT),
    )(*srcs, *lands, ssem, rsem, after)
    return outs[:n], outs[n:]


def _fwd_copies(idx, src, land, ssem, rsem, recv_side):
    x, y, c, me = _place()
    cps = []
    for p, t in enumerate(idx):
        cps.append(pltpu.make_async_remote_copy(src_ref=src[p], dst_ref=land[p].at[me], send_sem=ssem.at[4 * p], recv_sem=rsem.at[4 * p],
                                                device_id=(x, y, 1 - c), device_id_type=MESH))
        if t == N_BIG:
            continue
        rh = src[p].shape[0] // 2
        for j in range(3):
            px, py = _peer(x, y, j)
            part = land[p].at[2 * px + py, pl.ds(((1 - c) if recv_side else c) * rh, rh)]
            cps.append(pltpu.make_async_remote_copy(src_ref=part, dst_ref=part, send_sem=ssem.at[4 * p + 1 + j],
                                                    recv_sem=rsem.at[4 * p + 1 + j], device_id=(x, y, 1 - c), device_id_type=MESH))
    return cps


def _ag_fwd_start(name, idx, srcs, lands, dep):
    n = len(idx)

    def body(*refs):
        src = refs[:n]
        ssem, rsem = refs[2 * n + 1], refs[2 * n + 2]
        land = refs[3 * n + 3:4 * n + 3]
        token = refs[4 * n + 3]
        for cp in _fwd_copies(idx, src, land, ssem, rsem, False):
            cp.start()
        token[...] = jnp.zeros_like(token)

    dma = pltpu.SemaphoreType.DMA
    outs = pl.pallas_call(
        body, name=name,
        out_shape=(dma((4 * n,)), dma((4 * n,))) + tuple(pltpu.HBM(a.shape, a.dtype) for a in list(srcs) + list(lands)) + (S((8, 128), F32),),
        in_specs=[HBM] * (2 * n) + [ANY], out_specs=(SEM, SEM) + (HBM,) * (2 * n) + (VM,),
        input_output_aliases={i: 2 + i for i in range(2 * n)},
        compiler_params=pltpu.CompilerParams(has_side_effects=EFFECT),
    )(*srcs, *lands, dep)
    return outs[0], outs[1], outs[2:2 + n], outs[2 + n:2 + 2 * n], outs[-1]


def _ag_fwd_wait(name, idx, ssem, rsem, srcs, lands, after):
    n = len(idx)

    def body(*refs):
        src, land = refs[:n], refs[n:2 * n]
        ssem, rsem = refs[2 * n], refs[2 * n + 1]
        for mine, theirs in zip(_fwd_copies(idx, src, land, ssem, rsem, False), _fwd_copies(idx, src, land, ssem, rsem, True)):
            mine.wait_send()
            theirs.wait_recv()

    outs = pl.pallas_call(
        body, name=name,
        out_shape=tuple(pltpu.HBM(a.shape, a.dtype) for a in list(srcs) + list(lands)),
        in_specs=[HBM] * (2 * n) + [SEM, SEM, ANY], out_specs=(HBM,) * (2 * n),
        input_output_aliases={i: i for i in range(2 * n)},
        compiler_params=pltpu.CompilerParams(has_side_effects=EFFECT),
    )(*srcs, *lands, ssem, rsem, after)
    return outs[n:]


def _pair_copy(g, land, ssem, rsem, t):
    x, y, c, _ = _place()
    rh = g.shape[1] // 2
    return pltpu.make_async_remote_copy(src_ref=g.at[:, pl.ds((1 - c) * rh, rh), :], dst_ref=land,
                                        send_sem=ssem.at[t], recv_sem=rsem.at[t], device_id=(x, y, 1 - c), device_id_type=MESH)


def _rs_pair(grads):
    n = len(grads)

    def body(*refs):
        g, out = refs[:n], refs[n:2 * n]
        ssem, rsem = refs[2 * n:]
        for t in range(n):
            _pair_copy(g[t], out[t], ssem, rsem, t).start()
        for t in range(n):
            _pair_copy(g[t], out[t], ssem, rsem, t).wait()

    dma = pltpu.SemaphoreType.DMA
    return pl.pallas_call(
        body, in_specs=[HBM] * n, out_specs=[HBM] * n,
        out_shape=[S((4, g.shape[1] // 2, g.shape[2]), g.dtype) for g in grads],
        scratch_shapes=[dma((n,)), dma((n,))], name="rs_pair")(*grads)


def _rs_pair_start(name, grads, dep):
    n = len(grads)

    def body(*refs):
        g = refs[:n]
        ssem, rsem = refs[2 * n + 1], refs[2 * n + 2]
        land = refs[3 * n + 3:4 * n + 3]
        token = refs[4 * n + 3]
        for t in range(n):
            _pair_copy(g[t], land[t], ssem, rsem, t).start()
        token[...] = jnp.zeros_like(token)

    lands = [lax.empty((4, a.shape[1] // 2, a.shape[2]), a.dtype) for a in grads]
    dma = pltpu.SemaphoreType.DMA
    outs = pl.pallas_call(
        body, name=name,
        out_shape=(dma((n,)), dma((n,))) + tuple(pltpu.HBM(a.shape, a.dtype) for a in list(grads) + lands) + (S((8, 128), F32),),
        in_specs=[HBM] * (2 * n) + [ANY], out_specs=(SEM, SEM) + (HBM,) * (2 * n) + (VM,),
        input_output_aliases={i: 2 + i for i in range(2 * n)},
        compiler_params=pltpu.CompilerParams(has_side_effects=EFFECT),
    )(*[_hbm(a) for a in grads], *[_hbm(a) for a in lands], dep)
    return outs[0], outs[1], outs[2:2 + n], outs[2 + n:2 + 2 * n], outs[-1]


def _rs_pair_wait(name, ssem, rsem, grads, lands, after):
    n = len(grads)

    def body(*refs):
        g, land = refs[:n], refs[n:2 * n]
        ssem, rsem = refs[2 * n], refs[2 * n + 1]
        for t in range(n):
            cp = _pair_copy(g[t], land[t], ssem, rsem, t)
            cp.wait_send()
            cp.wait_recv()

    outs = pl.pallas_call(
        body, name=name,
        out_shape=tuple(pltpu.HBM(a.shape, a.dtype) for a in list(grads) + list(lands)),
        in_specs=[HBM] * (2 * n) + [SEM, SEM, ANY], out_specs=(HBM,) * (2 * n),
        input_output_aliases={i: i for i in range(2 * n)},
        compiler_params=pltpu.CompilerParams(has_side_effects=EFFECT),
    )(*grads, *lands, ssem, rsem, after)
    return outs[:n], outs[n:]


def _rs_copy(s, land, ssem, rsem, t, j):
    x, y, c, _ = _place()
    px, py = _peer(x, y, j)
    return pltpu.make_async_remote_copy(src_ref=s[t].at[2 * px + py], dst_ref=land[t].at[j],
                                        send_sem=ssem.at[3 * t + j], recv_sem=rsem.at[3 * t + j], device_id=(px, py, c), device_id_type=MESH)


def _rs_start(name, sums, dep):
    n = len(sums)

    def body(*refs):
        s = refs[:n]
        ssem, rsem = refs[2 * n + 1], refs[2 * n + 2]
        land = refs[3 * n + 3:4 * n + 3]
        token = refs[4 * n + 3]
        for t in range(n):
            for j in range(3):
                _rs_copy(s, land, ssem, rsem, t, j).start()
        token[...] = jnp.zeros_like(token)

    lands = [lax.empty((3,) + a.shape[1:], a.dtype) for a in sums]
    dma = pltpu.SemaphoreType.DMA
    outs = pl.pallas_call(
        body, name=name,
        out_shape=(dma((3 * n,)), dma((3 * n,))) + tuple(pltpu.HBM(a.shape, a.dtype) for a in list(sums) + lands) + (S((8, 128), F32),),
        in_specs=[HBM] * (2 * n) + [ANY], out_specs=(SEM, SEM) + (HBM,) * (2 * n) + (VM,),
        input_output_aliases={i: 2 + i for i in range(2 * n)},
        compiler_params=pltpu.CompilerParams(has_side_effects=EFFECT),
    )(*[_hbm(a) for a in sums], *[_hbm(a) for a in lands], dep)
    return outs[0], outs[1], outs[2:2 + n], outs[2 + n:2 + 2 * n], outs[-1]


def _rs_wait(name, ssem, rsem, sums, lands, after):
    n = len(sums)

    def body(*refs):
        s, land = refs[:n], refs[n:2 * n]
        ssem, rsem = refs[2 * n], refs[2 * n + 1]
        for t in range(n):
            for j in range(3):
                cp = _rs_copy(s, land, ssem, rsem, t, j)
                cp.wait_send()
                cp.wait_recv()

    outs = pl.pallas_call(
        body, name=name,
        out_shape=tuple(pltpu.HBM(a.shape, a.dtype) for a in list(sums) + list(lands)),
        in_specs=[HBM] * (2 * n) + [SEM, SEM, ANY], out_specs=(HBM,) * (2 * n),
        input_output_aliases={i: i for i in range(2 * n)},
        compiler_params=pltpu.CompilerParams(has_side_effects=EFFECT),
    )(*sums, *lands, ssem, rsem, after)
    return outs[n:]


def _rs_join(halves, lo, hi, dep):
    def body(*refs):
        h = refs[:N_BIG]
        out = refs[N_BIG + 1:2 * N_BIG + 1]
        ssem, rsem = refs[2 * N_BIG + 1:]
        x, y, c, _ = _place()

        def cp(t):
            return pltpu.make_async_remote_copy(
                src_ref=h[t].at[pl.ds(lo, hi - lo)], dst_ref=out[t], send_sem=ssem.at[t], recv_sem=rsem.at[t],
                device_id=(x, y, 1 - c), device_id_type=MESH)

        for t in range(N_BIG):
            cp(t).start()
        for t in range(N_BIG):
            cp(t).wait()

    dma = pltpu.SemaphoreType.DMA
    return pl.pallas_call(
        body, in_specs=[HBM] * N_BIG + [ANY], out_specs=[HBM] * N_BIG,
        out_shape=[S((hi - lo,) + h.shape[1:], h.dtype) for h in halves],
        scratch_shapes=[dma((N_BIG,)), dma((N_BIG,))], name="rs_join")(*halves, dep)


def _all_reduce_small(pack, dep):
    R = pack.shape[0]
    rb = _tile(R, (512, 256, 128, 8))

    def body(x_ref, dep_ref, all_ref, sum_ref, send_sems, recv_sems, local_sem):
        x, y, c = lax.axis_index("x"), lax.axis_index("y"), lax.axis_index("c")
        me, sibling = (x, y, c), (x, y, 1 - c)
        chips = [(1 - x, y), (x, 1 - y), (1 - x, 1 - y)]

        def rows(px, py, pc):
            return all_ref.at[pl.ds((4 * px + 2 * py + pc) * R, R), :]

        def copy(k, block, to, src=None):
            return pltpu.make_async_remote_copy(
                src_ref=rows(*block) if src is None else src, dst_ref=rows(*block),
                send_sem=send_sems.at[k], recv_sem=recv_sems.at[k], device_id=to, device_id_type=MESH)

        mine = pltpu.make_async_copy(x_ref, rows(*me), local_sem)
        mine.start()
        first = [copy(0, me, sibling, src=x_ref)]
        first += [copy(1 + j, me, (*chip, c), src=x_ref) for j, chip in enumerate(chips)]
        for cp in first:
            cp.start()
        passed = [copy(4 + j, (*chip, c), sibling) for j, chip in enumerate(chips)]
        for j, chip in enumerate(chips):
            copy(1 + j, (*chip, c), me).wait_recv()
            passed[j].start()
        copy(0, sibling, me).wait_recv()
        for j, chip in enumerate(chips):
            copy(4 + j, (*chip, 1 - c), me).wait_recv()
        for cp in first + passed:
            cp.wait_send()
        mine.wait()

        def step(b, carry):
            off = pl.multiple_of(b * rb, rb)
            acc = all_ref[pl.ds(off, rb), :]
            for k in range(1, 8):
                acc = acc + all_ref[pl.ds(pl.multiple_of(k * R + off, 8), rb), :]
            sum_ref[pl.ds(off, rb), :] = acc
            return carry

        lax.fori_loop(0, R // rb, step, 0)

    vm = pl.BlockSpec(memory_space=pltpu.VMEM)
    dma = pltpu.SemaphoreType.DMA
    _, total = pl.pallas_call(
        body, in_specs=[vm, pl.BlockSpec(memory_space=pl.ANY)], out_specs=[vm, vm],
        out_shape=[S((8 * R, 128), F32), S((R, 128), F32)],
        scratch_shapes=[dma((7,)), dma((7,)), dma],
        compiler_params=_cp(), name="allreduce_small")(pack, dep)
    return total


def _row_tile(rh, cc, tile_bytes=3 * 1024 * 1024 // 2):
    for t in (1024, 704, 512, 352, 256, 176, 128, 64, 32, 16):
        if rh % t == 0 and t * cc * 4 <= tile_bytes:
            return t
    return 16


def _my_chip():
    return 2 * lax.axis_index("x") + lax.axis_index("y")


def _pair_sum(g, recv):
    _, r, cc = g.shape
    rh = r // 2
    tb = _row_tile(rh, cc, 6 * 1024 * 1024)
    nbh = rh // tb

    def body(g_ref, r_ref, o_ref):
        o_ref[...] = (g_ref[...].astype(F32) + r_ref[...].astype(F32)).astype(BF16)

    def chip(k):
        return (_my_chip() + 1 + k) % 4

    mine = _bs((None, tb, cc), lambda k, i: (chip(k), lax.axis_index("c") * nbh + i, 0))
    plain = _bs((None, tb, cc), lambda k, i: (chip(k), i, 0))
    return pl.pallas_call(body, grid=(3, nbh), in_specs=[mine, plain], out_specs=plain,
                          out_shape=S((4, rh, cc), BF16), compiler_params=_cp(), name="rs_pair_sum")(g, recv)


def _owner_sum(g, recv, ici, acc, l):
    _, r, cc = g.shape
    rh = r // 2
    tb = _row_tile(rh, cc, 3 * 1024 * 1024)
    nbh = rh // tb

    def body(g_ref, r_ref, i0_ref, i1_ref, i2_ref, acc_ref, o_ref):
        s = g_ref[...].astype(F32) + r_ref[...].astype(F32)
        o_ref[...] = s + i0_ref[...].astype(F32) + i1_ref[...].astype(F32) + i2_ref[...].astype(F32)

    def slot(j):
        return _bs((None, tb, cc), lambda i: (j, i, 0))

    return pl.pallas_call(
        body, grid=(nbh,),
        in_specs=[_bs((None, tb, cc), lambda i: (_my_chip(), lax.axis_index("c") * nbh + i, 0)),
                  _bs((None, tb, cc), lambda i: (_my_chip(), i, 0)),
                  slot(0), slot(1), slot(2), pl.BlockSpec(memory_space=pl.ANY)],
        out_specs=_bs((None, tb, cc), lambda i: (l, i, 0)),
        out_shape=S(acc.shape, F32), input_output_aliases={5: 0},
        compiler_params=_cp(), name="rs_owner_sum")(g, recv, ici, ici, ici, acc)


def _adam_math(w, g, m, v):
    m = B1 * m + (1.0 - B1) * g
    v = B2 * v + (1.0 - B2) * (g * g)
    m_hat = m / (1.0 - B1 ** STEP)
    v_hat = v / (1.0 - B2 ** STEP)
    delta = -LR * (m_hat / (jnp.sqrt(v_hat) + AEPS) + WD * w)
    return delta, m, v


def _adamw_big(w, g_mine, g_sib, m, v, lo, hi, prev):
    L, r, cc = w.shape
    rh = r // 2
    tb = _row_tile(rh, cc)
    nbh = rh // tb

    def body(w_ref, gm_ref, gs_ref, m_ref, v_ref, *rest):
        go_ref, d_ref, mo_ref, vo_ref = rest[-4:]
        mine = pl.program_id(1) == lax.axis_index("c")
        g = jnp.where(mine, gm_ref[...], gs_ref[...])
        d, m, v = _adam_math(w_ref[...], g, m_ref[...], v_ref[...])
        go_ref[...] = g
        d_ref[...] = d
        mo_ref[...] = m
        vo_ref[...] = v

    def mine_map(l, hf, i):
        c = lax.axis_index("c")
        return (l + lo, jnp.where(hf == c, i, jnp.where(c == 0, nbh - 1, 0)), 0)

    def sib_map(l, hf, i):
        c = lax.axis_index("c")
        return (l, jnp.where(hf != c, i, jnp.where(c == 0, 0, nbh - 1)), 0)

    full = _bs((None, tb, cc), lambda l, hf, i: (l + lo, hf * nbh + i, 0))
    extra = [] if prev is None else list(prev)
    return pl.pallas_call(
        body, grid=(hi - lo, 2, nbh),
        in_specs=[full, _bs((None, tb, cc), mine_map), _bs((None, tb, cc), sib_map), full, full]
        + [pl.BlockSpec(memory_space=pl.ANY)] * len(extra),
        out_specs=[full] * 4, out_shape=[S(w.shape, F32)] * 4,
        input_output_aliases={5 + k: k for k in range(len(extra))},
        compiler_params=_cp(), name="adamw_big")(w, g_mine, g_sib, m, v, *extra)


def _adamw_small(w, g, m, v):
    R = w.shape[0]
    tb = _tile(R, (512, 256, 128, 8))

    def body(w_ref, g_ref, m_ref, v_ref, d_ref, mo_ref, vo_ref):
        d, m, v = _adam_math(w_ref[...], g_ref[...], m_ref[...], v_ref[...])
        d_ref[...] = d
        mo_ref[...] = m
        vo_ref[...] = v

    spec = _bs((tb, 128), lambda i: (i, 0))
    return pl.pallas_call(body, grid=(R // tb,), in_specs=[spec] * 4, out_specs=[spec] * 3,
                          out_shape=[S((R, 128), F32)] * 3, compiler_params=_cp(), name="adamw_small")(w, g, m, v)


def _mix_pad(w):
    return jnp.concatenate([w[:, :4608], w[:, 4616:DIN], w[:, 4608:4616], jnp.zeros((D, PW - DIN), w.dtype)], axis=1)


def _mix_unpad(g):
    return jnp.concatenate([g[:, :4608], g[:, 5632:5640], g[:, 4608:5632]], axis=1)


def _pack(parts):
    flat = jnp.concatenate([p.reshape(-1).astype(F32) for p in parts])
    n = flat.shape[0]
    total = -(-n // (512 * 128)) * (512 * 128)
    return jnp.pad(flat, (0, total - n)).reshape(total // 128, 128)


def _unpack(pack, shapes):
    flat = pack.reshape(-1)
    out, off = [], 0
    for s in shapes:
        n = math.prod(s)
        out.append(flat[off:off + n].reshape(s))
        off += n
    return out


def _ffn_backward(dy, dyb, saved, gain, win, wout, dep):
    x, h, rstd, zg, zu, act = saved
    dz = _ffn_bwd_dz(dyb, wout, zg, zu, dep)
    dwout = _mm_tn("ffn_bwd_dwout", act, dyb, scale=0.5, tm=512, tn=1024)
    dwin = _ffn_bwd_dwin(h, dz)
    dh = _ffn_bwd_dh(dz, win)
    dx, dxb, dgain = _rms_bwd(dh, x, rstd, gain, dy)
    return dx, dxb, dgain, dwin, dwout


def _mixer_forward(x, p):
    T = x.shape[0]
    h, rstd = _rms_fwd(x, p["norm_mix"])
    tm = _tile(T, (1024, 512))
    tn = 1152
    P = _mm("mix_in", h, p["wmix"],
            _bs((tm, D), lambda i, j, k: (i, 0)), _bs((D, tn), lambda i, j, k: (0, j)),
            _bs((tm, tn), lambda i, j, k: (i, j)), S((T, PW), F32), (T // tm, PW // tn, 1), NN, 1, (tm, tn))
    yc = _conv_fwd(P, p["cw"])
    cum = _fgate_fwd(P, p["fb"])
    cumt = cum[:, :NH].T
    tq = _att_tile(T)
    cumq, cumk = cumt.reshape(NH, T, 1), cumt.reshape(NH, T // tq, 1, tq)
    qkv = P[:, C_Q * 128:C_GATE * 128].astype(BF16)
    ya, lse = _attn_fwd(qkv, cumq, cumk)
    yl, hst = _lru_fwd(P, p["lcw"], p["lvec"], p["lru_w_a"], p["lru_w_x"])
    yn, r0, r1, r2 = _gnorm_fwd(yc, ya, yl, p["mix_out_norm"])
    y = _mm("mix_out", yn, p["wo"],
            _bs((tm, D), lambda i, j, k: (i, 0)), _bs((D, 1024), lambda i, j, k: (0, j)),
            _bs((tm, 1024), lambda i, j, k: (i, j)), S((T, D), F32), (T // tm, D // 1024, 1), NN, 1, (tm, 1024),
            res=x, r_spec=_bs((tm, 1024), lambda i, j, k: (i, j)))
    return y, (x, h, rstd, P, qkv, cumq, cumk, lse, yc, ya, yl, hst, yn, r0, r1, r2)


def _mixer_backward(dy, dyb, saved, p, dep):
    x, h, rstd, P, qkv, cumq, cumk, lse, yc, ya, yl, hst, yn, r0, r1, r2 = saved
    T = x.shape[0]
    dyn = _mm_nt_full("mix_bwd_dyn", dyb, p["wo"], 512, dep=dep)
    dwo = _mm_tn("mix_bwd_dwo", yn, dyb, tm=512, tn=1024)
    dyc, dya, dyl, dgn = _gnorm_bwd(dyn, yc, ya, yl, r0, r1, r2, p["mix_out_norm"])
    dcb, dcc, dcv, dcw = _conv_bwd(P, p["cw"], dyc)
    dq, dk, dv, dck, dcq = _attn_bwd(qkv, cumq, cumk, lse, ya, dya)
    dcum = jnp.pad((dck.reshape(NH, T) + dcq.reshape(NH, T)).T, ((0, 0), (0, 128 - NH)))
    df, dfb = _fgate_bwd(P, p["fb"], dcum)
    dgate, dlx, lsm, dwa, dwx = _lru_bwd(P, p["lcw"], p["lvec"], p["lru_w_a"], p["lru_w_x"], hst, dyl)
    dP = jnp.concatenate([dcb, dcc, dcv, dq, dk, dv, dgate, dlx, df], axis=1).astype(BF16)
    tm = _tile(T, (512,))
    dh = _mm("mix_bwd_dh", dP, p["wmix"],
             _bs((tm, PW), lambda j, i, k: (i, 0)), _bs((1024, PW), lambda j, i, k: (j, 0)),
             _bs((tm, 1024), lambda j, i, k: (i, j)), S((T, D), F32), (D // 1024, T // tm, 1), NT, 1, (tm, 1024))
    dwmix = _mm_tn("mix_bwd_dwmix", h, dP, tm=512, tn=1152)
    dx, dxb, dgm = _rms_bwd(dh, x, rstd, p["norm_mix"], dy)
    small = dict(norm_mix=dgm[0], mix_out_norm=dgn[0], conv_w=dcw[:3], fgate_b=dfb[0, :NH], lru_conv_w=lsm[:4],
                 lru_conv_b=lsm[4], lru_b_a=lsm[5], lru_b_x=lsm[6], lru_lambda=lsm[7], lru_w_a=dwa, lru_w_x=dwx)
    return dx, dxb, small, dwmix, dwo


BIG =("ffn1_w_in", "ffn1_w_out", "mix_w_in", "mix_w_out", "ffn2_w_in", "ffn2_w_out")
SMALL = ("norm_ffn1", "norm_mix", "conv_w", "fgate_b", "lru_conv_w", "lru_conv_b", "lru_w_a", "lru_b_a", "lru_w_x",
         "lru_b_x", "lru_lambda", "mix_out_norm", "norm_ffn2", "final_norm")
WEIGHTS = ("norm_ffn1", "ffn1_w_in", "ffn1_w_out", "norm_mix", "mix_w_in", "conv_w", "fgate_b", "lru_conv_w", "lru_conv_b",
           "lru_w_a", "lru_b_a", "lru_w_x", "lru_b_x", "lru_lambda", "mix_out_norm", "mix_w_out", "norm_ffn2", "ffn2_w_in",
           "ffn2_w_out", "final_norm")


def _step(args):
    xx, yy, cc_ = lax.axis_index("x"), lax.axis_index("y"), lax.axis_index("c")
    me = 2 * xx + yy
    x0 = args["x"][0]
    tgt = args["loss_target"][0]
    T = x0.shape[0]
    L = args["norm_ffn1"].shape[0]

    def ag_sources(l):
        small = jnp.concatenate([args["conv_w"][l], args["lru_conv_w"][l], jnp.zeros((1, 128), F32)], axis=0)
        return [args[n][l].astype(BF16) for n in BIG] + [small]

    def layer_params(l, gat, gsm):
        w1i, w1o, wmx, wo, w2i, w2o = gat
        cwl = gsm.transpose(1, 0, 2).reshape(8, 4 * 128)
        return dict(
            w1i=w1i, w1o=w1o.reshape(-1, D), w2i=w2i, w2o=w2o.reshape(-1, D), wo=wo.reshape(D, D),
            wmix=_mix_pad(wmx.transpose(1, 0, 2).reshape(D, DIN)),
            cw=jnp.concatenate([cwl[:3], jnp.zeros((5, DC), F32)], axis=0),
            lcw=jnp.concatenate([cwl[3:7], jnp.zeros((4, DL), F32)], axis=0),
            fb=jnp.pad(args["fgate_b"][l], (0, 128 - NH)).reshape(1, 128),
            lvec=jnp.concatenate([args["lru_conv_b"][l][None], args["lru_b_a"][l][None], args["lru_b_x"][l][None],
                                  args["lru_lambda"][l][None], jnp.zeros((4, DL), F32)], axis=0),
            lru_w_a=args["lru_w_a"][l], lru_w_x=args["lru_w_x"][l],
            norm_ffn1=args["norm_ffn1"][l][None], norm_mix=args["norm_mix"][l][None],
            mix_out_norm=args["mix_out_norm"][l][None], norm_ffn2=args["norm_ffn2"][l][None])

    xs = x0
    saved, layers = [], []
    def ici_done(name, idx, flight, after):
        ssem, rsem, srcs, lands, _ = flight
        s, ld = _ag_wait("ag_wait_" + name, idx, ssem, rsem, [srcs[t] for t in idx], [lands[t] for t in idx], after)
        return _ag_fwd_start("ag_fwd_start_" + name, idx, s, ld, after)

    def gathered(name, idx, fwd, after):
        ssem, rsem, s, ld, _ = fwd
        return _ag_fwd_wait("ag_fwd_wait_" + name, idx, ssem, rsem, s, ld, after)

    every_w = tuple(range(N_AG))
    flight = _ag_start(0, ag_sources(0), x0)
    fwd = None
    for l in range(L):
        if l == 0:
            (w1i,) = gathered("0i", (0,), ici_done("0i", (0,), flight, xs), xs)
        else:
            w1i, w1o, wmx, wo, w2i, w2o, gsm = gathered(str(l), every_w, fwd, xs)
        nxt = _ag_start(l + 1, ag_sources(l + 1), w1i) if l + 1 < L else None
        h1, rstd1 = _rms_fwd(xs, args["norm_ffn1"][l][None])
        zg1, zu1, act1 = _ffn_in(h1, w1i)
        if l == 0:
            (w1o,) = gathered("0o", (1,), ici_done("0o", (1,), flight, act1), act1)
        x1 = _ffn_out(act1, w1o.reshape(-1, D), xs)
        s1 = (xs, h1, rstd1, zg1, zu1, act1)
        if l == 0:
            gsm, wmx, wo = gathered("0m", AG_MIX, ici_done("0m", AG_MIX, flight, x1), x1)
            w2i = w2o = w1o
        p = layer_params(l, (w1i, w1o, wmx, wo, w2i, w2o), gsm)
        x2, s2 = _mixer_forward(x1, p)
        if l == 0:
            w2i, w2o = gathered("0f", AG_FFN2, ici_done("0f", AG_FFN2, flight, x2), x2)
            p["w2i"], p["w2o"] = w2i, w2o.reshape(-1, D)
        h2, rstd2 = _rms_fwd(x2, p["norm_ffn2"])
        zg2, zu2, act2 = _ffn_in(h2, p["w2i"])
        fwd = ici_done(str(l + 1), every_w, nxt, act2) if nxt is not None else None
        x3 = _ffn_out(act2, p["w2o"], x2, dep=None if fwd is None else fwd[4])
        s3 = (x2, h2, rstd2, zg2, zu2, act2)
        saved.append((s1, s2, s3))
        layers.append(p)
        xs = x3
    lpart, dx, dxb, dfinal = _loss_head(xs, args["final_norm"][None], tgt)
    loss = lax.psum(lpart[0, 0], ("x", "y", "c"))

    acc = [None] * N_BIG
    small_grads = [None] * L
    every = tuple(range(N_BIG))
    in_air = []
    pair_flight = None
    dep = lpart

    def to_blocks(t, g):
        if t in (1, 5):
            return g.reshape(4, g.shape[0] // 4, D)
        if t == 2:
            return _mix_unpad(g).reshape(D, 4, DIN // 4).transpose(1, 0, 2)
        if t == 3:
            return g.reshape(4, D // 4, D)
        return g

    def launch(name, l, idx, grads, recv, after):
        sums = [_pair_sum(g, r) for g, r in zip(grads, recv)]
        started = _rs_start("rs_start_" + name, sums, after)
        in_air.append(("rs_wait_" + name, l, idx, grads, recv, started))
        return started[4]

    def land(entry, after):
        name, l, idx, grads, recv, (ssem, rsem, sums, lands, _) = entry
        ici = _rs_wait(name, ssem, rsem, sums, lands, after)
        for t, g, r, i3 in zip(idx, grads, recv, ici):
            if acc[t] is None:
                acc[t] = jnp.zeros((L, g.shape[1] // 2, g.shape[2]), F32)
            acc[t] = _owner_sum(g, r, i3, acc[t], l)

    for l in reversed(range(L)):
        p = layers[l]
        s1, s2, s3 = saved[l]
        dx, dxb, dg2, dw2i, dw2o = _ffn_backward(dx, dxb, s3, p["norm_ffn2"], p["w2i"], p["w2o"], dep)
        if pair_flight is not None:
            lp, (ssem, rsem, g_thru, lands, _) = pair_flight
            g_thru, recv = _rs_pair_wait(f"rs_pair_wait_{lp}", ssem, rsem, g_thru, lands, dx)
            dep = launch(str(lp), lp, every, g_thru, recv, dx)
            pair_flight = None
        if l == 0:
            part = [to_blocks(4, dw2i), to_blocks(5, dw2o)]
            dep = launch("0c", 0, (4, 5), part, _rs_pair(part), dx)
        dx, dxb, sm, dwmix, dwo = _mixer_backward(dx, dxb, s2, p, dep)
        if l == 0:
            part = [to_blocks(2, dwmix), to_blocks(3, dwo)]
            dep = launch("0b", 0, (2, 3), part, _rs_pair(part), dx)
        dx, dxb, dg1, dw1i, dw1o = _ffn_backward(dx, dxb, s1, p["norm_ffn1"], p["w1i"], p["w1o"], dep)
        sm["norm_ffn1"] = dg1[0]
        sm["norm_ffn2"] = dg2[0]
        small_grads[l] = sm
        for entry in [e for e in in_air if e[1] > l]:
            land(entry, dx)
            in_air.remove(entry)
        if l > 0:
            grads = [to_blocks(t, g) for t, g in enumerate((dw1i, dw1o, dwmix, dwo, dw2i, dw2o))]
            pair_flight = (l, _rs_pair_start(f"rs_pair_start_{l}", grads, dx))
            dep = pair_flight[1][4]

    part = [to_blocks(0, dw1i), to_blocks(1, dw1o)]
    dep = launch("0a", 0, (0, 1), part, _rs_pair(part), dx)

    def adamw(k, lo, hi, sib, prev):
        n = BIG[k]
        return _adamw_big(args[n], acc[k], sib[k], args["m_" + n], args["v_" + n], lo, hi, prev)

    res = [None] * N_BIG
    after = dx
    if L > 1:
        sib = _rs_join(acc, 1, L, dep)
        for k in range(N_BIG):
            res[k] = adamw(k, 1, L, sib, None)
        after = res[N_BIG - 1][0]
    full = {n: (dfinal[0] if n == "final_norm" else jnp.stack([small_grads[l][n] for l in range(L)])) for n in SMALL}
    red_pack = _all_reduce_small(_pack([full[n] for n in SMALL]), after)
    for entry in list(in_air):
        land(entry, red_pack)
    sib = _rs_join(acc, 0, 1, dx)
    out = {"loss": loss, "grad_x": dx[None]}
    for k, n in enumerate(BIG):
        outs = adamw(k, 0, 1, sib, res[k])
        out["grad_" + n], out["delta_" + n], out["new_m_" + n], out["new_v_" + n] = outs

    shapes = [full[n].shape for n in SMALL]
    red = dict(zip(SMALL, _unpack(red_pack, shapes)))
    for n in ("conv_w", "lru_conv_w"):
        red[n] = lax.dynamic_slice_in_dim(red[n], me * 128, 128, axis=2)
    oshapes = [args[n].shape for n in SMALL]
    d, m, v = _adamw_small(_pack([args[n] for n in SMALL]), _pack([red[n] for n in SMALL]),
                           _pack([args["m_" + n] for n in SMALL]), _pack([args["v_" + n] for n in SMALL]))
    for n, gg, dd, mm, vv in zip(SMALL, [red[n] for n in SMALL], _unpack(d, oshapes), _unpack(m, oshapes), _unpack(v, oshapes)):
        out["grad_" + n], out["delta_" + n], out["new_m_" + n], out["new_v_" + n] = gg, dd, mm, vv
    return out


def kernel(x, norm_ffn1, ffn1_w_in, ffn1_w_out, norm_mix, mix_w_in, conv_w, fgate_b, lru_conv_w, lru_conv_b, lru_w_a, lru_b_a, lru_w_x, lru_b_x, lru_lambda, mix_out_norm, mix_w_out, norm_ffn2, ffn2_w_in, ffn2_w_out, final_norm, loss_target, m_norm_ffn1, m_ffn1_w_in, m_ffn1_w_out, m_norm_mix, m_mix_w_in, m_conv_w, m_fgate_b, m_lru_conv_w, m_lru_conv_b, m_lru_w_a, m_lru_b_a, m_lru_w_x, m_lru_b_x, m_lru_lambda, m_mix_out_norm, m_mix_w_out, m_norm_ffn2, m_ffn2_w_in, m_ffn2_w_out, m_final_norm, v_norm_ffn1, v_ffn1_w_in, v_ffn1_w_out, v_norm_mix, v_mix_w_in, v_conv_w, v_fgate_b, v_lru_conv_w, v_lru_conv_b, v_lru_w_a, v_lru_b_a, v_lru_w_x, v_lru_b_x, v_lru_lambda, v_mix_out_norm, v_mix_w_out, v_norm_ffn2, v_ffn2_w_in, v_ffn2_w_out, v_final_norm):
    args = dict(locals())
    out = _step(args)
    res = [out["loss"], out["grad_x"]]
    for prefix in ("grad_", "delta_", "new_m_", "new_v_"):
        res += [out[prefix + n] for n in WEIGHTS]
    return tuple(res)
```
